```python
import math
import jax, jax.numpy as jnp
from jax import lax
import numpy as np

D_MODEL = 2048
BATCH = 1
SEQ = 16384
DEPTH = 1

N_META = 16
MIX_WIDTH = D_MODEL
ATT_WIDTH = MIX_WIDTH // 2
SSM_WIDTH = MIX_WIDTH - ATT_WIDTH
ATT_HEAD_DIM = 64
ATT_V_DIM = 2 * ATT_HEAD_DIM
ATT_HEADS = ATT_WIDTH // ATT_V_DIM
QK_WIDTH = ATT_HEADS * 2 * ATT_HEAD_DIM
IN_WIDTH = 2 * QK_WIDTH + ATT_WIDTH + SSM_WIDTH
SSM_GROUP = 16
SSM_GROUPS = SSM_WIDTH // SSM_GROUP
SSM_STATE = 64
N_BUCKETS = 32
MAX_DISTANCE = 128
Q_BLOCK = 128
N_EXPERTS = 64
TOP_K = 8
N_EXPERT_GROUPS = 8
TOPK_GROUPS = 4
EXPERT_HIDDEN = 512
SHARED_HIDDEN = 512
ROUTED_SCALE = 2.5
MOE_BLOCK = 128
EPS = 1e-6

kernel_name = "hymba_diffattn_s5_moe_encoder"


def rms_norm(x, gain):
    xf = x.astype(jnp.float32)
    y = xf * lax.rsqrt(jnp.mean(xf * xf, axis=-1, keepdims=True) + EPS)
    return (y * gain.astype(jnp.float32)).astype(x.dtype)


def t5_bucket(rel):
    half = N_BUCKETS // 2
    exact = half // 2
    side = jnp.where(rel > 0, half, 0)
    n = jnp.abs(rel)
    nf = jnp.maximum(n, 1).astype(jnp.float32)
    large = exact + (jnp.log(nf / exact) / math.log(MAX_DISTANCE / exact) * (half - exact)).astype(jnp.int32)
    large = jnp.minimum(large, half - 1)
    return side + jnp.where(n < exact, n, large)


def diff_attention(q, k, v, rel_bias, lam, lambda_init, subln):
    b, l = q.shape[0], q.shape[1]
    n_qb = -(-l // Q_BLOCK)
    l_pad = n_qb * Q_BLOCK
    qf = jnp.pad(q.astype(jnp.float32), ((0, 0), (0, l_pad - l), (0, 0), (0, 0), (0, 0)))
    qf = qf.reshape(b, n_qb, Q_BLOCK, ATT_HEADS, 2, ATT_HEAD_DIM).transpose(1, 4, 0, 3, 2, 5)
    kf = k.astype(jnp.float32).transpose(3, 0, 2, 1, 4)
    vf = v.astype(jnp.float32).transpose(0, 2, 1, 3)
    kpos = jnp.arange(l, dtype=jnp.int32)
    scale = ATT_HEAD_DIM ** -0.5
    table = rel_bias.astype(jnp.float32)

    def block(args):
        qb, start = args
        qpos = start + jnp.arange(Q_BLOCK, dtype=jnp.int32)
        bucket = t5_bucket(kpos[None, :] - qpos[:, None])
        bias = jnp.transpose(table[bucket], (2, 0, 1))
        s = jnp.einsum('cbhqd,cbhkd->cbhqk', qb, kf) * scale + bias
        p = jax.nn.softmax(s, axis=-1)
        a = p[0] - lam * p[1]
        return jnp.einsum('bhqk,bhkd->bhqd', a, vf)

    starts = jnp.arange(n_qb, dtype=jnp.int32) * Q_BLOCK
    out = lax.map(block, (qf, starts))
    out = out.transpose(1, 0, 3, 2, 4).reshape(b, l_pad, ATT_HEADS, ATT_V_DIM)[:, :l]
    out = rms_norm(out, subln) * (1.0 - lambda_init)
    return out.reshape(b, l, ATT_WIDTH)


def _complex_affine_combine(e1, e2):
    a1r, a1i, b1r, b1i = e1
    a2r, a2i, b2r, b2i = e2
    return (a2r * a1r - a2i * a1i,
            a2r * a1i + a2i * a1r,
            a2r * b1r - a2i * b1i + b2r,
            a2r * b1i + a2i * b1r + b2i)


def s5_scan(u, a_re, a_im, log_step, b_re, b_im, c_re, c_im, reverse):
    f32 = jnp.float32
    a_re = a_re.astype(f32)
    a_im = a_im.astype(f32)
    dt = jnp.exp(log_step.astype(f32))[:, None]
    decay = jnp.exp(a_re * dt)
    ab_re = decay * jnp.cos(a_im * dt)
    ab_im = decay * jnp.sin(a_im * dt)
    den = a_re * a_re + a_im * a_im
    zr = ab_re - 1.0
    f_re = (zr * a_re + ab_im * a_im) / den
    f_im = (ab_im * a_re - zr * a_im) / den
    b_re = b_re.astype(f32)
    b_im = b_im.astype(f32)
    bb_re = f_re[..., None] * b_re - f_im[..., None] * b_im
    bb_im = f_re[..., None] * b_im + f_im[..., None] * b_re
    bu_re = jnp.einsum('blgp,gnp->blgn', u, bb_re)
    bu_im = jnp.einsum('blgp,gnp->blgn', u, bb_im)
    a_full_re = jnp.broadcast_to(ab_re, bu_re.shape)
    a_full_im = jnp.broadcast_to(ab_im, bu_re.shape)
    _, _, s_re, s_im = lax.associative_scan(
        _complex_affine_combine, (a_full_re, a_full_im, bu_re, bu_im), axis=1, reverse=reverse)
    return (jnp.einsum('blgn,gpn->blgp', s_re, c_re.astype(f32))
            - jnp.einsum('blgn,gpn->blgp', s_im, c_im.astype(f32)))


def s5_mixer(u, a_re, a_im, log_step, b_re, b_im, c_re, c_im, d_skip, w_glu, b_glu):
    b, l = u.shape[0], u.shape[1]
    uf = u.astype(jnp.float32).reshape(b, l, SSM_GROUPS, SSM_GROUP)
    y = d_skip.astype(jnp.float32).reshape(SSM_GROUPS, SSM_GROUP) * uf
    y = y + s5_scan(uf, a_re[0], a_im[0], log_step[0], b_re[0], b_im[0], c_re[0], c_im[0], False)
    y = y + s5_scan(uf, a_re[1], a_im[1], log_step[1], b_re[1], b_im[1], c_re[1], c_im[1], True)
    y = jax.nn.gelu(y.reshape(b, l, SSM_WIDTH))
    gate = jax.nn.sigmoid(y @ w_glu.astype(jnp.float32) + b_glu.astype(jnp.float32))
    return (y * gate).astype(u.dtype)


def swiglu(x, w_gate, w_up, w_down):
    return (jax.nn.silu(x @ w_gate) * (x @ w_up)) @ w_down


def routed_moe(xn, router_w, router_bias, w_gate, w_up, w_down):
    b, l, d = xn.shape
    t = b * l
    xt = xn.reshape(t, d)
    scores = jax.nn.sigmoid(xt.astype(jnp.float32) @ router_w.astype(jnp.float32))
    choice = scores + router_bias.astype(jnp.float32)
    per_group = N_EXPERTS // N_EXPERT_GROUPS
    grp_score = lax.top_k(choice.reshape(t, N_EXPERT_GROUPS, per_group), 2)[0].sum(-1)
    _, top_groups = lax.top_k(grp_score, TOPK_GROUPS)
    gmask = jnp.any(top_groups[..., None] == jnp.arange(N_EXPERT_GROUPS), axis=1)
    emask = jnp.repeat(gmask, per_group, axis=1)
    _, idx = lax.top_k(jnp.where(emask, choice, -jnp.inf), TOP_K)
    wts = jnp.take_along_axis(scores, idx, axis=1)
    wts = wts / jnp.sum(wts, axis=-1, keepdims=True) * ROUTED_SCALE

    n_assign = t * TOP_K
    flat_e = idx.reshape(-1)
    flat_tok = jnp.repeat(jnp.arange(t, dtype=jnp.int32), TOP_K)
    flat_w = wts.reshape(-1)
    order = jnp.argsort(flat_e)
    se = flat_e[order]
    counts = jnp.bincount(flat_e, length=N_EXPERTS)
    starts = jnp.cumsum(counts) - counts
    padded = (counts + MOE_BLOCK - 1) // MOE_BLOCK * MOE_BLOCK
    pad_ends = jnp.cumsum(padded)
    pad_starts = pad_ends - padded
    dest = pad_starts[se] + jnp.arange(n_assign) - starts[se]
    n_blocks = -(-n_assign // MOE_BLOCK) + N_EXPERTS
    n_slots = n_blocks * MOE_BLOCK
    slot_tok = jnp.full((n_slots,), t, jnp.int32).at[dest].set(flat_tok[order])
    slot_w = jnp.zeros((n_slots,), jnp.float32).at[dest].set(flat_w[order])
    block_e = jnp.minimum(
        jnp.searchsorted(pad_ends, jnp.arange(n_blocks) * MOE_BLOCK, side='right'), N_EXPERTS - 1)

    x_pad = jnp.concatenate([xt, jnp.zeros((1, d), xt.dtype)], axis=0)

    def step(acc, inp):
        tok, wt, e = inp
        xb = x_pad[tok]
        yb = swiglu(xb, w_gate[e], w_up[e], w_down[e]).astype(jnp.float32) * wt[:, None]
        return acc.at[tok].add(yb), None

    acc0 = jnp.zeros((t + 1, d), jnp.float32)
    acc, _ = lax.scan(step, acc0, (slot_tok.reshape(n_blocks, MOE_BLOCK),
                                   slot_w.reshape(n_blocks, MOE_BLOCK), block_e))
    return acc[:t].reshape(b, l, d).astype(xn.dtype)


def setup_inputs(seed: int = 0) -> dict:
    key = jax.random.key(seed)
    ks = jax.random.split(key, 40)
    f32 = jnp.float32

    def nrm(k, shape, scale):
        return jax.random.normal(k, shape, f32) * scale

    n_idx = jnp.arange(SSM_STATE, dtype=f32)
    return {
        "x": nrm(ks[0], (BATCH, SEQ, D_MODEL), 1.0),
        "meta_tokens": nrm(ks[1], (N_META, D_MODEL), 1.0),
        "rel_bias": nrm(ks[2], (N_BUCKETS, ATT_HEADS), 0.1),
        "norm_mix": 1.0 + nrm(ks[3], (DEPTH, D_MODEL), 0.02),
        "w_in": nrm(ks[4], (DEPTH, D_MODEL, IN_WIDTH), D_MODEL ** -0.5),
        "q_norm": 1.0 + nrm(ks[5], (DEPTH, ATT_HEAD_DIM), 0.02),
        "k_norm": 1.0 + nrm(ks[6], (DEPTH, ATT_HEAD_DIM), 0.02),
        "lam_q1": nrm(ks[7], (DEPTH, ATT_HEAD_DIM), 0.1),
        "lam_k1": nrm(ks[8], (DEPTH, ATT_HEAD_DIM), 0.1),
        "lam_q2": nrm(ks[9], (DEPTH, ATT_HEAD_DIM), 0.1),
        "lam_k2": nrm(ks[10], (DEPTH, ATT_HEAD_DIM), 0.1),
        "subln": 1.0 + nrm(ks[11], (DEPTH, ATT_V_DIM), 0.02),
        "ssm_a_re": -0.5 + nrm(ks[12], (DEPTH, 2, SSM_GROUPS, SSM_STATE), 0.01),
        "ssm_a_im": math.pi * n_idx + nrm(ks[13], (DEPTH, 2, SSM_GROUPS, SSM_STATE), 0.01),
        "ssm_log_step": jax.random.uniform(ks[14], (DEPTH, 2, SSM_GROUPS), f32,
                                           math.log(1e-3), math.log(1e-1)),
        "ssm_b_re": nrm(ks[15], (DEPTH, 2, SSM_GROUPS, SSM_STATE, SSM_GROUP), (2 * SSM_GROUP) ** -0.5),
        "ssm_b_im": nrm(ks[16], (DEPTH, 2, SSM_GROUPS, SSM_STATE, SSM_GROUP), (2 * SSM_GROUP) ** -0.5),
        "ssm_c_re": nrm(ks[17], (DEPTH, 2, SSM_GROUPS, SSM_GROUP, SSM_STATE), SSM_STATE ** -0.5),
        "ssm_c_im": nrm(ks[18], (DEPTH, 2, SSM_GROUPS, SSM_GROUP, SSM_STATE), SSM_STATE ** -0.5),
        "ssm_d": nrm(ks[19], (DEPTH, SSM_WIDTH), 0.5),
        "w_glu": nrm(ks[20], (DEPTH, SSM_WIDTH, SSM_WIDTH), SSM_WIDTH ** -0.5),
        "b_glu": nrm(ks[21], (DEPTH, SSM_WIDTH), 0.01),
        "ssm_norm": 1.0 + nrm(ks[22], (DEPTH, SSM_WIDTH), 0.02),
        "w_out": nrm(ks[23], (DEPTH, MIX_WIDTH, D_MODEL), MIX_WIDTH ** -0.5),
        "norm_ffn": 1.0 + nrm(ks[24], (DEPTH, D_MODEL), 0.02),
        "router_w": nrm(ks[25], (DEPTH, D_MODEL, N_EXPERTS), D_MODEL ** -0.5),
        "router_bias": nrm(ks[26], (DEPTH, N_EXPERTS), 0.01),
        "w_gate": nrm(ks[27], (DEPTH, N_EXPERTS, D_MODEL, EXPERT_HIDDEN), D_MODEL ** -0.5),
        "w_up": nrm(ks[28], (DEPTH, N_EXPERTS, D_MODEL, EXPERT_HIDDEN), D_MODEL ** -0.5),
        "w_down": nrm(ks[29], (DEPTH, N_EXPERTS, EXPERT_HIDDEN, D_MODEL), EXPERT_HIDDEN ** -0.5),
        "shared_gate": nrm(ks[30], (DEPTH, D_MODEL, SHARED_HIDDEN), D_MODEL ** -0.5),
        "shared_up": nrm(ks[31], (DEPTH, D_MODEL, SHARED_HIDDEN), D_MODEL ** -0.5),
        "shared_down": nrm(ks[32], (DEPTH, SHARED_HIDDEN, D_MODEL), SHARED_HIDDEN ** -0.5),
    }


def reference(x, meta_tokens, rel_bias, norm_mix, w_in, q_norm, k_norm, lam_q1, lam_k1,
              lam_q2, lam_k2, subln, ssm_a_re, ssm_a_im, ssm_log_step, ssm_b_re, ssm_b_im,
              ssm_c_re, ssm_c_im, ssm_d, w_glu, b_glu, ssm_norm, w_out, norm_ffn, router_w,
              router_bias, w_gate, w_up, w_down, shared_gate, shared_up, shared_down):
    b = x.shape[0]
    meta = jnp.broadcast_to(meta_tokens.astype(x.dtype)[None], (b, N_META, D_MODEL))
    h = jnp.concatenate([meta, x], axis=1)
    l = h.shape[1]
    for i in range(DEPTH):
        hn = rms_norm(h, norm_mix[i])
        proj = hn @ w_in[i]
        q = proj[..., :QK_WIDTH].reshape(b, l, ATT_HEADS, 2, ATT_HEAD_DIM)
        k = proj[..., QK_WIDTH:2 * QK_WIDTH].reshape(b, l, ATT_HEADS, 2, ATT_HEAD_DIM)
        v = proj[..., 2 * QK_WIDTH:2 * QK_WIDTH + ATT_WIDTH].reshape(b, l, ATT_HEADS, ATT_V_DIM)
        u = proj[..., 2 * QK_WIDTH + ATT_WIDTH:]
        q = rms_norm(q, q_norm[i])
        k = rms_norm(k, k_norm[i])
        lambda_init = 0.8 - 0.6 * math.exp(-0.3 * i)
        lam = (jnp.exp(jnp.sum(lam_q1[i].astype(jnp.float32) * lam_k1[i].astype(jnp.float32)))
               - jnp.exp(jnp.sum(lam_q2[i].astype(jnp.float32) * lam_k2[i].astype(jnp.float32)))
               + lambda_init)
        att = diff_attention(q, k, v, rel_bias, lam, lambda_init, subln[i]).astype(h.dtype)
        ssm = s5_mixer(u, ssm_a_re[i], ssm_a_im[i], ssm_log_step[i], ssm_b_re[i], ssm_b_im[i],
                       ssm_c_re[i], ssm_c_im[i], ssm_d[i], w_glu[i], b_glu[i])
        ssm = rms_norm(ssm, ssm_norm[i]).astype(h.dtype)
        h = h + jnp.concatenate([att, ssm], axis=-1) @ w_out[i]
        hn = rms_norm(h, norm_ffn[i])
        h = (h + swiglu(hn, shared_gate[i], shared_up[i], shared_down[i])
             + routed_moe(hn, router_w[i], router_bias[i], w_gate[i], w_up[i], w_down[i]))
    return h[:, N_META:, :]
```

```python
import functools
import math

import jax
import jax.numpy as jnp
from jax import lax
from jax.experimental import pallas as pl
from jax.experimental.pallas import tpu as pltpu

F32 = jnp.float32
BF16 = jnp.bfloat16

D_MODEL = 2048
N_META = 16
ATT_WIDTH = 1024
SSM_WIDTH = 1024
HEAD_DIM = 64
V_DIM = 128
HEADS = 8
QK_WIDTH = 1024
IN_WIDTH = 4096
SSM_GROUP = 16
SSM_GROUPS = 64
SSM_STATE = 64
N_BUCKETS = 32
MAX_DISTANCE = 128
N_EXPERTS = 64
TOP_K = 8
N_EXPERT_GROUPS = 8
TOPK_GROUPS = 4
EXPERT_HIDDEN = 512
SHARED_HIDDEN = 512
ROUTED_SCALE = 2.5
EPS = 1e-6
LAMBDA_INIT = 0.8 - 0.6 * math.exp(-0.3 * 0)

ROW_BLOCK = 512
ATT_BLOCK = 512
S5_CHUNK = 16
S5_GROUPS_PER_STEP = 8
MOE_ROW_BLOCK = 512
MIX_ROW_BLOCK = 256
NEG_BIG = -1e30
VMEM_LIMIT = 56 * 1024 * 1024


def _resident(shape, index_map):
    return pl.BlockSpec(shape, index_map, pipeline_mode=pl.Buffered(1))


def _inproj_body(x_ref, meta_ref, g_ref, w_ref, seg_ref, qg_ref, kg_ref, o_ref, *, n_xblk):
    i = pl.program_id(0)

    def run(src_ref):
        xv = src_ref[...]
        ms = jnp.mean(xv * xv, axis=-1, keepdims=True)
        hn = (xv * lax.rsqrt(ms + EPS) * g_ref[...]).astype(BF16)
        for s in range(4):
            ps = jnp.dot(hn, w_ref[:, s * 1024:(s + 1) * 1024], preferred_element_type=F32)
            if s < 2:
                gain = qg_ref if s == 0 else kg_ref
                msq = jnp.dot((ps * ps).astype(BF16), seg_ref[...], preferred_element_type=F32)
                ps = ps * lax.rsqrt(msq + EPS) * gain[...]
            o_ref[:, s * 1024:(s + 1) * 1024] = ps.astype(BF16)

    @pl.when(i < n_xblk)
    def _():
        run(x_ref)

    @pl.when(i == n_xblk)
    def _():
        run(meta_ref)


def _inproj(x2, meta_pad, gain, w_bf, seg, qg, kg):
    seq = x2.shape[0]
    n_xblk = seq // ROW_BLOCK
    rows = seq + ROW_BLOCK
    return pl.pallas_call(
        functools.partial(_inproj_body, n_xblk=n_xblk),
        grid=(n_xblk + 1,),
        in_specs=[
            pl.BlockSpec((ROW_BLOCK, D_MODEL), lambda i: (jnp.minimum(i, n_xblk - 1), 0)),
            _resident((ROW_BLOCK, D_MODEL), lambda i: (0, 0)),
            _resident((1, D_MODEL), lambda i: (0, 0)),
            _resident((D_MODEL, IN_WIDTH), lambda i: (0, 0)),
            _resident((QK_WIDTH, QK_WIDTH), lambda i: (0, 0)),
            _resident((1, QK_WIDTH), lambda i: (0, 0)),
            _resident((1, QK_WIDTH), lambda i: (0, 0)),
        ],
        out_specs=pl.BlockSpec((ROW_BLOCK, IN_WIDTH), lambda i: (i, 0)),
        out_shape=jax.ShapeDtypeStruct((rows, IN_WIDTH), BF16),
        compiler_params=pltpu.CompilerParams(
            dimension_semantics=("arbitrary",), vmem_limit_bytes=VMEM_LIMIT),
        name="inproj",
    )(x2, meta_pad, gain, w_bf, seg, qg, kg)


def _t5_bias(rel, tab_ref, h):
    half = N_BUCKETS // 2
    exact = half // 2
    n = jnp.abs(rel)
    nf = jnp.maximum(n, 1).astype(F32)
    large = exact + (jnp.log(nf / exact) / math.log(MAX_DISTANCE / exact) * (half - exact)).astype(jnp.int32)
    large = jnp.minimum(large, half - 1)
    bucket = jnp.where(rel > 0, half, 0) + jnp.where(n < exact, n, large)
    out = jnp.zeros(rel.shape, F32)
    for b in range(N_BUCKETS):
        out = jnp.where(bucket == b, tab_ref[b, h], out)
    return out


def _attn_body(tab_ref, q_ref, k_ref, v_ref, lq1_ref, lk1_ref, lq2_ref, lk2_ref, subln_ref,
               o_ref, bias_ref, m1_ref, l1_ref, a1_ref, m2_ref, l2_ref, a2_ref, *, n_main):
    T = ATT_BLOCK
    h = pl.program_id(0)
    qi = pl.program_id(1)

    @pl.when(qi == 0)
    def _():
        offsets = (-T, 0, T, -N_META, -N_META - T, -2 * T, 2 * T)
        for kind, off in enumerate(offsets):
            masked = kind in (3, 4)

            def rows(rc, carry, off=off, masked=masked, kind=kind):
                r0 = pl.multiple_of(rc * 8, 8)
                r = r0 + lax.broadcasted_iota(jnp.int32, (8, T), 0)
                c = lax.broadcasted_iota(jnp.int32, (8, T), 1)
                b = _t5_bias(off + c - r, tab_ref, h)
                if masked:
                    b = jnp.where(c < N_META, b, NEG_BIG)
                bias_ref[kind, pl.ds(r0, 8), :] = b
                return carry

            lax.fori_loop(0, T // 8, rows, 0)

    m1_ref[...] = jnp.full(m1_ref.shape, -jnp.inf, F32)
    m2_ref[...] = jnp.full(m2_ref.shape, -jnp.inf, F32)
    l1_ref[...] = jnp.zeros(l1_ref.shape, F32)
    l2_ref[...] = jnp.zeros(l2_ref.shape, F32)
    a1_ref[...] = jnp.zeros(a1_ref.shape, F32)
    a2_ref[...] = jnp.zeros(a2_ref.shape, F32)

    q = q_ref[...]
    q1 = q[:, :HEAD_DIM]
    q2 = q[:, HEAD_DIM:]
    nt = (((1,), (1,)), ((), ()))

    def one_map(s, vb, m_ref, l_ref, a_ref):
        m_old = m_ref[...]
        m_new = jnp.maximum(m_old, jnp.max(s, axis=-1, keepdims=True))
        alpha = jnp.exp(m_old - m_new)
        p = jnp.exp(s - m_new)
        l_ref[...] = alpha * l_ref[...] + jnp.sum(p, axis=-1, keepdims=True)
        a_ref[...] = alpha * a_ref[...] + jnp.dot(p.astype(BF16), vb, preferred_element_type=F32)
        m_ref[...] = m_new

    def step(ki, carry):
        koff = pl.multiple_of(ki * T, T)
        kb = k_ref[pl.ds(koff, T), :]
        vb = v_ref[pl.ds(koff, T), :]
        d = ki - qi
        kind = jnp.where(ki == n_main, jnp.where(qi == 0, 3, 4),
                         jnp.where(d <= -2, 5, jnp.where(d >= 2, 6, d + 1)))
        bias = bias_ref[kind]
        s1 = lax.dot_general(q1, kb[:, :HEAD_DIM], nt, preferred_element_type=F32) + bias
        one_map(s1, vb, m1_ref, l1_ref, a1_ref)
        s2 = lax.dot_general(q2, kb[:, HEAD_DIM:], nt, preferred_element_type=F32) + bias
        one_map(s2, vb, m2_ref, l2_ref, a2_ref)
        return carry

    lax.fori_loop(0, n_main + 1, step, 0)

    lam = (jnp.exp(jnp.sum(lq1_ref[...] * lk1_ref[...], axis=-1, keepdims=True))
           - jnp.exp(jnp.sum(lq2_ref[...] * lk2_ref[...], axis=-1, keepdims=True))
           + LAMBDA_INIT)
    o = a1_ref[...] / l1_ref[...] - lam * (a2_ref[...] / l2_ref[...])
    ms = jnp.mean(o * o, axis=-1, keepdims=True)
    o = o * lax.rsqrt(ms + EPS) * subln_ref[...] * (1.0 - LAMBDA_INIT)
    o_ref[...] = o.astype(BF16)


def _attention(proj, rel_bias, lq1, lk1, lq2, lk2, subln, seq):
    T = ATT_BLOCK
    n_main = seq // T
    rows = proj.shape[0]
    vec64 = lambda: _resident((1, HEAD_DIM), lambda h, qi: (0, 0))
    return pl.pallas_call(
        functools.partial(_attn_body, n_main=n_main),
        grid=(HEADS, n_main),
        in_specs=[
            pl.BlockSpec(memory_space=pltpu.SMEM),
            pl.BlockSpec((T, 2 * HEAD_DIM), lambda h, qi: (qi, h)),
            pl.BlockSpec((rows, 2 * HEAD_DIM), lambda h, qi: (0, HEADS + h)),
            pl.BlockSpec((rows, V_DIM), lambda h, qi: (0, 2 * HEADS + h)),
            vec64(), vec64(), vec64(), vec64(),
            _resident((1, V_DIM), lambda h, qi: (0, 0)),
        ],
        out_specs=pl.BlockSpec((T, V_DIM), lambda h, qi: (qi, h)),
        out_shape=jax.ShapeDtypeStruct((seq, ATT_WIDTH), BF16),
        scratch_shapes=[
            pltpu.VMEM((7, T, T), F32),
            pltpu.VMEM((T, 1), F32), pltpu.VMEM((T, 1), F32), pltpu.VMEM((T, V_DIM), F32),
            pltpu.VMEM((T, 1), F32), pltpu.VMEM((T, 1), F32), pltpu.VMEM((T, V_DIM), F32),
        ],
        compiler_params=pltpu.CompilerParams(
            dimension_semantics=("arbitrary", "arbitrary"), vmem_limit_bytes=VMEM_LIMIT),
        name="diff_attention",
    )(rel_bias, proj, proj, proj, lq1, lk1, lq2, lk2, subln)


def _s5_prep(a_re, a_im, log_step, b_re, b_im, c_re, c_im):
    hp = lax.Precision.HIGHEST
    C = S5_CHUNK
    dt = jnp.exp(log_step)[..., None]
    decay = jnp.exp(a_re * dt)
    ab_re = decay * jnp.cos(a_im * dt)
    ab_im = decay * jnp.sin(a_im * dt)
    den = a_re * a_re + a_im * a_im
    zr = ab_re - 1.0
    f_re = (zr * a_re + ab_im * a_im) / den
    f_im = (ab_im * a_re - zr * a_im) / den
    bb_re = f_re[..., None] * b_re - f_im[..., None] * b_im
    bb_im = f_re[..., None] * b_im + f_im[..., None] * b_re
    pr, pi = jnp.ones_like(ab_re), jnp.zeros_like(ab_re)
    pw_re, pw_im = [pr], [pi]
    for _ in range(C):
        pr, pi = pr * ab_re - pi * ab_im, pr * ab_im + pi * ab_re
        pw_re.append(pr)
        pw_im.append(pi)
    pw_re = jnp.stack(pw_re)
    pw_im = jnp.stack(pw_im)
    cp_re = c_re[None] * pw_re[:, :, :, None, :] - c_im[None] * pw_im[:, :, :, None, :]
    cp_im = c_re[None] * pw_im[:, :, :, None, :] + c_im[None] * pw_re[:, :, :, None, :]
    kern = (jnp.einsum('tdgpn,dgnq->tdgpq', cp_re[:C], bb_re, precision=hp)
            - jnp.einsum('tdgpn,dgnq->tdgpq', cp_im[:C], bb_im, precision=hp))
    jj = jnp.arange(C)[:, None]
    ii = jnp.arange(C)[None, :]
    kf = jnp.where((ii >= jj)[:, :, None, None, None], kern[:, 0][jnp.clip(ii - jj, 0, C - 1)], 0.0)
    kr = jnp.where((jj >= ii)[:, :, None, None, None], kern[:, 1][jnp.clip(jj - ii, 0, C - 1)], 0.0)
    mm = (kf + kr).transpose(2, 0, 4, 1, 3).reshape(SSM_GROUPS, C * SSM_GROUP, C * SSM_GROUP)

    def in_mat(pwr, pwi, d):
        re = pwr[:, :, :, None] * bb_re[d][None] - pwi[:, :, :, None] * bb_im[d][None]
        im = pwr[:, :, :, None] * bb_im[d][None] + pwi[:, :, :, None] * bb_re[d][None]
        return re.transpose(1, 0, 3, 2), im.transpose(1, 0, 3, 2)

    pf_re, pf_im = in_mat(pw_re[:C, 0][::-1], pw_im[:C, 0][::-1], 0)
    pr_re, pr_im = in_mat(pw_re[:C, 1], pw_im[:C, 1], 1)
    pp = jnp.concatenate([pf_re, pr_re, pf_im, pr_im], axis=-1).reshape(SSM_GROUPS, C * SSM_GROUP, 4 * SSM_STATE)

    def out_mat(cr, ci):
        return cr.transpose(1, 3, 0, 2), -ci.transpose(1, 3, 0, 2)

    qf_re, qf_im = out_mat(cp_re[1:, 0], cp_im[1:, 0])
    qr_re, qr_im = out_mat(cp_re[1:, 1][::-1], cp_im[1:, 1][::-1])
    qq = jnp.concatenate([qf_re, qr_re, qf_im, qr_im], axis=1).reshape(SSM_GROUPS, 4 * SSM_STATE, C * SSM_GROUP)
    lam16 = jnp.stack([jnp.concatenate([pw_re[C, 0], pw_re[C, 1]], axis=-1),
                       jnp.concatenate([pw_im[C, 0], pw_im[C, 1]], axis=-1)], axis=1)
    return pp.astype(BF16), mm.astype(BF16), qq.astype(BF16), lam16


def _s5_body(u_ref, pp_ref, mm_ref, qq_ref, lam_ref, y_ref,
             zre_ref, zim_ref, are_ref, aim_ref, bre_ref, bim_ref, *, n_chunks):
    GS = S5_GROUPS_PER_STEP
    NS = 2 * SSM_STATE
    for gi in range(GS):
        z = jnp.dot(u_ref[gi], pp_ref[gi], preferred_element_type=F32)
        zre_ref[:, gi, :] = z[:, :NS]
        zim_ref[:, gi, :] = z[:, NS:]

    ar = lam_ref[:, 0, :]
    ai = lam_ref[:, 1, :]
    fwd = lax.broadcasted_iota(jnp.int32, (GS, NS), 1) < SSM_STATE
    sre0 = jnp.where(fwd, zre_ref[n_chunks], 0.0)
    sim0 = jnp.where(fwd, zim_ref[n_chunks], 0.0)

    def scan_step(k, carry):
        sre, sim = carry
        kr = n_chunks - 1 - k
        are_ref[k] = sre
        aim_ref[k] = sim
        bre_ref[kr] = sre
        bim_ref[kr] = sim
        zr = jnp.where(fwd, zre_ref[k], zre_ref[kr])
        zi = jnp.where(fwd, zim_ref[k], zim_ref[kr])
        return ar * sre - ai * sim + zr, ar * sim + ai * sre + zi

    lax.fori_loop(0, n_chunks, scan_step, (sre0, sim0))

    fwd_rows = lax.broadcasted_iota(jnp.int32, (n_chunks, NS), 1) < SSM_STATE
    for gi in range(GS):
        s_re = jnp.where(fwd_rows, are_ref[:, gi, :], bre_ref[:, gi, :])
        s_im = jnp.where(fwd_rows, aim_ref[:, gi, :], bim_ref[:, gi, :])
        scat = jnp.concatenate([s_re, s_im], axis=1).astype(BF16)
        y = (jnp.dot(u_ref[gi, :n_chunks, :], mm_ref[gi], preferred_element_type=F32)
             + jnp.dot(scat, qq_ref[gi], preferred_element_type=F32))
        y_ref[gi] = y


def _s5(u_chunks, pp, mm, qq, lam16, n_chunks):
    GS = S5_GROUPS_PER_STEP
    n_rows = u_chunks.shape[1]
    W = S5_CHUNK * SSM_GROUP
    NS = 2 * SSM_STATE
    mat = lambda: pl.BlockSpec((GS, W, W), lambda g: (g, 0, 0))
    return pl.pallas_call(
        functools.partial(_s5_body, n_chunks=n_chunks),
        grid=(SSM_GROUPS // GS,),
        in_specs=[
            pl.BlockSpec((GS, n_rows, W), lambda g: (g, 0, 0)),
            mat(), mat(), mat(),
            pl.BlockSpec((GS, 2, NS), lambda g: (g, 0, 0)),
        ],
        out_specs=pl.BlockSpec((GS, n_chunks, W), lambda g: (g, 0, 0)),
        out_shape=jax.ShapeDtypeStruct((SSM_GROUPS, n_chunks, W), F32),
        scratch_shapes=[pltpu.VMEM((n_rows, GS, NS), F32), pltpu.VMEM((n_rows, GS, NS), F32)]
        + [pltpu.VMEM((n_chunks, GS, NS), F32) for _ in range(4)],
        compiler_params=pltpu.CompilerParams(
            dimension_semantics=("arbitrary",), vmem_limit_bytes=VMEM_LIMIT),
        name="s5_chunked",
    )(u_chunks, pp, mm, qq, lam16)


def _mix_out_body(x_ref, att_ref, y_ref, u_ref, d_ref, wglu_ref, bglu_ref, sn_ref, wout_ref,
                  nf_ref, rwt_ref, sg_ref, su_ref, sd_ref, h_ref, hn_ref, sc_ref):
    y = y_ref[...] + d_ref[...] * u_ref[...].astype(F32)
    y = jax.nn.gelu(y)
    gate = jax.nn.sigmoid(jnp.dot(y.astype(BF16), wglu_ref[...], preferred_element_type=F32) + bglu_ref[...])
    s = y * gate
    ms = jnp.mean(s * s, axis=-1, keepdims=True)
    ssm = (s * lax.rsqrt(ms + EPS) * sn_ref[...]).astype(BF16)
    h = (x_ref[...]
         + jnp.dot(att_ref[...], wout_ref[:ATT_WIDTH, :], preferred_element_type=F32)
         + jnp.dot(ssm, wout_ref[ATT_WIDTH:, :], preferred_element_type=F32))
    ms = jnp.mean(h * h, axis=-1, keepdims=True)
    hn = h * lax.rsqrt(ms + EPS) * nf_ref[...]
    hnb = hn.astype(BF16)
    hn_ref[...] = hnb
    logits_t = lax.dot_general(rwt_ref[...], hn, (((1,), (1,)), ((), ())),
                               precision=lax.Precision.HIGHEST, preferred_element_type=F32)
    sc_ref[...] = jax.nn.sigmoid(logits_t)
    g = jnp.dot(hnb, sg_ref[...], preferred_element_type=F32)
    up = jnp.dot(hnb, su_ref[...], preferred_element_type=F32)
    act = (jax.nn.silu(g) * up).astype(BF16)
    h_ref[...] = h + jnp.dot(act, sd_ref[...], preferred_element_type=F32)


def _mix_out(x2, att, y, proj, d_skip, w_glu, b_glu, ssm_norm, w_out, norm_ffn, router_wt,
             sh_gate, sh_up, sh_down):
    seq = x2.shape[0]
    R = MIX_ROW_BLOCK
    res = lambda shape: _resident(shape, lambda i: (0, 0))
    return pl.pallas_call(
        _mix_out_body,
        grid=(seq // R,),
        in_specs=[
            pl.BlockSpec((R, D_MODEL), lambda i: (i, 0)),
            pl.BlockSpec((R, ATT_WIDTH), lambda i: (i, 0)),
            pl.BlockSpec((R, SSM_WIDTH), lambda i: (i, 0)),
            pl.BlockSpec((R, SSM_WIDTH), lambda i: (i, 3)),
            res((1, SSM_WIDTH)), res((SSM_WIDTH, SSM_WIDTH)), res((1, SSM_WIDTH)), res((1, SSM_WIDTH)),
            res((D_MODEL, D_MODEL)), res((1, D_MODEL)), res((N_EXPERTS, D_MODEL)),
            res((D_MODEL, SHARED_HIDDEN)), res((D_MODEL, SHARED_HIDDEN)), res((SHARED_HIDDEN, D_MODEL)),
        ],
        out_specs=[
            pl.BlockSpec((R, D_MODEL), lambda i: (i, 0)),
            pl.BlockSpec((R, D_MODEL), lambda i: (i, 0)),
            pl.BlockSpec((N_EXPERTS, R), lambda i: (0, i)),
        ],
        out_shape=[
            jax.ShapeDtypeStruct((seq, D_MODEL), F32),
            jax.ShapeDtypeStruct((seq, D_MODEL), BF16),
            jax.ShapeDtypeStruct((N_EXPERTS, seq), F32),
        ],
        compiler_params=pltpu.CompilerParams(
            dimension_semantics=("arbitrary",), vmem_limit_bytes=VMEM_LIMIT),
        name="mix_out_shared",
    )(x2, att, y, proj, d_skip, w_glu, b_glu, ssm_norm, w_out, norm_ffn, router_wt,
      sh_gate, sh_up, sh_down)


def _route_body(sc_ref, rb_ref, w_ref):
    scores = sc_ref[...]
    R = scores.shape[1]
    per_group = N_EXPERTS // N_EXPERT_GROUPS
    choice = scores + rb_ref[...]
    c3 = choice.reshape(N_EXPERT_GROUPS, per_group, R)
    within = lax.broadcasted_iota(jnp.int32, c3.shape, 1)
    m1 = jnp.max(c3, axis=1, keepdims=True)
    first = jnp.min(jnp.where(c3 == m1, within, per_group), axis=1, keepdims=True)
    m2 = jnp.max(jnp.where(within == first, -jnp.inf, c3), axis=1, keepdims=True)
    grp = (m1 + m2).reshape(N_EXPERT_GROUPS, R)
    gidx = lax.broadcasted_iota(jnp.int32, grp.shape, 0)
    grank = jnp.zeros(grp.shape, jnp.int32)
    for b in range(N_EXPERT_GROUPS):
        gb = grp[b:b + 1, :]
        grank += ((gb > grp) | ((gb == grp) & (b < gidx))).astype(jnp.int32)
    gmask = grank < TOPK_GROUPS
    emask = jnp.broadcast_to(gmask[:, None, :], c3.shape).reshape(N_EXPERTS, R)
    val = jnp.where(emask, choice, -jnp.inf)
    eidx = lax.broadcasted_iota(jnp.int32, val.shape, 0)
    rank = jnp.zeros(val.shape, jnp.int32)
    for e in range(N_EXPERTS):
        ve = val[e:e + 1, :]
        rank += ((ve > val) | ((ve == val) & (e < eidx))).astype(jnp.int32)
    sel = rank < TOP_K
    w = jnp.where(sel, scores, 0.0)
    w_ref[...] = w / jnp.sum(w, axis=0, keepdims=True) * ROUTED_SCALE


def _route(scores_t, router_bias):
    seq = scores_t.shape[1]
    R = ROW_BLOCK
    return pl.pallas_call(
        _route_body,
        grid=(seq // R,),
        in_specs=[pl.BlockSpec((N_EXPERTS, R), lambda i: (0, i)),
                  _resident((N_EXPERTS, 1), lambda i: (0, 0))],
        out_specs=pl.BlockSpec((N_EXPERTS, R), lambda i: (0, i)),
        out_shape=jax.ShapeDtypeStruct((N_EXPERTS, seq), F32),
        compiler_params=pltpu.CompilerParams(dimension_semantics=("arbitrary",)),
        name="route",
    )(scores_t, router_bias)


def _moe_body(hn_ref, w_ref, h_ref, wg_ref, wu_ref, wd_ref, o_ref):
    e = pl.program_id(1)

    @pl.when(e == 0)
    def _():
        o_ref[...] = h_ref[...]

    x = hn_ref[...]
    g = jnp.dot(x, wg_ref[0], preferred_element_type=F32)
    up = jnp.dot(x, wu_ref[0], preferred_element_type=F32)
    act = (jax.nn.silu(g) * up).astype(BF16)
    y = jnp.dot(act, wd_ref[0], preferred_element_type=F32)
    wts = w_ref[...]
    lane = lax.broadcasted_iota(jnp.int32, wts.shape, 1)
    wcol = jnp.sum(jnp.where(lane == e, wts, 0.0), axis=1, keepdims=True)
    o_ref[...] += y * wcol


def _moe_dense(hn, wts, h, wg, wu, wd):
    seq = hn.shape[0]
    R = MOE_ROW_BLOCK
    return pl.pallas_call(
        _moe_body,
        grid=(seq // R, N_EXPERTS),
        in_specs=[
            pl.BlockSpec((R, D_MODEL), lambda b, e: (b, 0)),
            pl.BlockSpec((R, N_EXPERTS), lambda b, e: (b, 0)),
            pl.BlockSpec((R, D_MODEL), lambda b, e: (b, 0)),
            pl.BlockSpec((1, D_MODEL, EXPERT_HIDDEN), lambda b, e: (e, 0, 0)),
            pl.BlockSpec((1, D_MODEL, EXPERT_HIDDEN), lambda b, e: (e, 0, 0)),
            pl.BlockSpec((1, EXPERT_HIDDEN, D_MODEL), lambda b, e: (e, 0, 0)),
        ],
        out_specs=pl.BlockSpec((R, D_MODEL), lambda b, e: (b, 0)),
        out_shape=jax.ShapeDtypeStruct((seq, D_MODEL), F32),
        compiler_params=pltpu.CompilerParams(
            dimension_semantics=("arbitrary", "arbitrary"), vmem_limit_bytes=VMEM_LIMIT),
        name="moe_dense",
    )(hn, wts, h, wg, wu, wd)


def kernel(x, meta_tokens, rel_bias, norm_mix, w_in, q_norm, k_norm, lam_q1, lam_k1, lam_q2, lam_k2, subln, ssm_a_re, ssm_a_im, ssm_log_step, ssm_b_re, ssm_b_im, ssm_c_re, ssm_c_im, ssm_d, w_glu, b_glu, ssm_norm, w_out, norm_ffn, router_w, router_bias, w_gate, w_up, w_down, shared_gate, shared_up, shared_down):
    batch, seq, d = x.shape
    assert batch == 1 and d == D_MODEL and seq % ROW_BLOCK == 0 and seq % ATT_BLOCK == 0
    assert norm_mix.shape[0] == 1, "single layer"
    x2 = x.reshape(seq, d)
    meta_pad = jnp.zeros((ROW_BLOCK, d), F32).at[:N_META].set(meta_tokens.astype(F32))
    seg = jnp.kron(jnp.eye(QK_WIDTH // HEAD_DIM, dtype=F32),
                   jnp.full((HEAD_DIM, HEAD_DIM), 1.0 / HEAD_DIM, F32)).astype(BF16)
    qg = jnp.tile(q_norm[0].astype(F32), QK_WIDTH // HEAD_DIM)[None] * (HEAD_DIM ** -0.5)
    kg = jnp.tile(k_norm[0].astype(F32), QK_WIDTH // HEAD_DIM)[None]

    proj = _inproj(x2, meta_pad, norm_mix[0][None], w_in[0].astype(BF16), seg, qg, kg)

    att = _attention(proj, rel_bias.astype(F32), lam_q1[0][None], lam_k1[0][None],
                     lam_q2[0][None], lam_k2[0][None], subln[0][None], seq)

    n_rows = proj.shape[0] // S5_CHUNK
    n_chunks = seq // S5_CHUNK
    u_chunks = (proj[:, 2 * QK_WIDTH + ATT_WIDTH:]
                .reshape(n_rows, S5_CHUNK, SSM_GROUPS, SSM_GROUP)
                .transpose(2, 0, 1, 3).reshape(SSM_GROUPS, n_rows, S5_CHUNK * SSM_GROUP))
    pp, mm, qq, lam16 = _s5_prep(ssm_a_re[0].astype(F32), ssm_a_im[0].astype(F32),
                                 ssm_log_step[0].astype(F32), ssm_b_re[0].astype(F32),
                                 ssm_b_im[0].astype(F32), ssm_c_re[0].astype(F32),
                                 ssm_c_im[0].astype(F32))
    y_chunks = _s5(u_chunks, pp, mm, qq, lam16, n_chunks)
    y = (y_chunks.reshape(SSM_GROUPS, n_chunks, S5_CHUNK, SSM_GROUP)
         .transpose(1, 2, 0, 3).reshape(seq, SSM_WIDTH))

    h, hn, scores_t = _mix_out(
        x2, att, y, proj, ssm_d[0][None].astype(F32), w_glu[0].astype(BF16), b_glu[0][None].astype(F32),
        ssm_norm[0][None].astype(F32), w_out[0].astype(BF16), norm_ffn[0][None].astype(F32),
        router_w[0].astype(F32).T, shared_gate[0].astype(BF16), shared_up[0].astype(BF16),
        shared_down[0].astype(BF16))

    wts_t = _route(scores_t, router_bias[0].astype(F32)[:, None])
    out = _moe_dense(hn, wts_t.T, h, w_gate[0].astype(BF16), w_up[0].astype(BF16),
                     w_down[0].astype(BF16))
    return out.reshape(batch, seq, d)
```

```python
import functools
import math

import jax
import jax.numpy as jnp
from jax import lax
from jax.experimental import pallas as pl
from jax.experimental.pallas import tpu as pltpu

F32 = jnp.float32
BF16 = jnp.bfloat16

D_MODEL = 2048
N_META = 16
ATT_WIDTH = 1024
SSM_WIDTH = 1024
HEAD_DIM = 64
V_DIM = 128
HEADS = 8
QK_WIDTH = 1024
IN_WIDTH = 4096
V_OFFSET = 2 * QK_WIDTH
U_OFFSET = V_OFFSET + 2 * ATT_WIDTH
PROJ_WIDTH = U_OFFSET + SSM_WIDTH
SSM_GROUP = 16
SSM_GROUPS = 64
SSM_STATE = 64
N_BUCKETS = 32
MAX_DISTANCE = 128
N_EXPERTS = 64
TOP_K = 8
N_EXPERT_GROUPS = 8
TOPK_GROUPS = 4
EXPERT_HIDDEN = 512
SHARED_HIDDEN = 512
ROUTED_SCALE = 2.5
EPS = 1e-6
LAMBDA_INIT = 0.8 - 0.6 * math.exp(-0.3 * 0)

ROW_BLOCK = 512
ATT_BLOCK = 512
S5_CHUNK = 16
S5_GROUPS_PER_STEP = 8
MOE_ROW_BLOCK = 512
MIX_ROW_BLOCK = 256
NEG_BIG = -1e30
MAX_EXP_RANGE = 80.0
BOUND_MARGIN = 1.02
VMEM_LIMIT = 56 * 1024 * 1024


def _resident(shape, index_map):
    return pl.BlockSpec(shape, index_map, pipeline_mode=pl.Buffered(1))


def _inproj_body(x_ref, meta_ref, g_ref, w_ref, seg_ref, qg_ref, kg_ref, o_ref, *, n_xblk):
    i = pl.program_id(0)

    def run(src_ref):
        xv = src_ref[...]
        ms = jnp.mean(xv * xv, axis=-1, keepdims=True)
        hn = (xv * lax.rsqrt(ms + EPS) * g_ref[...]).astype(BF16)
        for s in range(4):
            ps = jnp.dot(hn, w_ref[:, s * 1024:(s + 1) * 1024], preferred_element_type=F32)
            if s < 2:
                gain = qg_ref if s == 0 else kg_ref
                msq = jnp.dot((ps * ps).astype(BF16), seg_ref[...], preferred_element_type=F32)
                ps = ps * lax.rsqrt(msq + EPS) * gain[...]
            pb = ps.astype(BF16)
            if s < 2:
                o_ref[:, s * 1024:(s + 1) * 1024] = pb
            elif s == 2:
                lane = lax.broadcasted_iota(jnp.int32, (pb.shape[0], V_DIM), 1)
                ones_col = jnp.where(lane == 0, 1.0, 0.0).astype(BF16)
                for hh in range(HEADS):
                    base = V_OFFSET + hh * 2 * V_DIM
                    o_ref[:, base:base + V_DIM] = pb[:, hh * V_DIM:(hh + 1) * V_DIM]
                    o_ref[:, base + V_DIM:base + 2 * V_DIM] = ones_col
            else:
                o_ref[:, U_OFFSET:U_OFFSET + SSM_WIDTH] = pb

    @pl.when(i < n_xblk)
    def _():
        run(x_ref)

    @pl.when(i == n_xblk)
    def _():
        run(meta_ref)


def _inproj(x2, meta_pad, gain, w_bf, seg, qg, kg):
    seq = x2.shape[0]
    n_xblk = seq // ROW_BLOCK
    rows = seq + ROW_BLOCK
    return pl.pallas_call(
        functools.partial(_inproj_body, n_xblk=n_xblk),
        grid=(n_xblk + 1,),
        in_specs=[
            pl.BlockSpec((ROW_BLOCK, D_MODEL), lambda i: (jnp.minimum(i, n_xblk - 1), 0)),
            _resident((ROW_BLOCK, D_MODEL), lambda i: (0, 0)),
            _resident((1, D_MODEL), lambda i: (0, 0)),
            _resident((D_MODEL, IN_WIDTH), lambda i: (0, 0)),
            _resident((QK_WIDTH, QK_WIDTH), lambda i: (0, 0)),
            _resident((1, QK_WIDTH), lambda i: (0, 0)),
            _resident((1, QK_WIDTH), lambda i: (0, 0)),
        ],
        out_specs=pl.BlockSpec((ROW_BLOCK, PROJ_WIDTH), lambda i: (i, 0)),
        out_shape=jax.ShapeDtypeStruct((rows, PROJ_WIDTH), BF16),
        compiler_params=pltpu.CompilerParams(
            dimension_semantics=("arbitrary",), vmem_limit_bytes=VMEM_LIMIT),
        name="inproj",
    )(x2, meta_pad, gain, w_bf, seg, qg, kg)


def _t5_bias(rel, tab_ref, h):
    half = N_BUCKETS // 2
    exact = half // 2
    n = jnp.abs(rel)
    nf = jnp.maximum(n, 1).astype(F32)
    large = exact + (jnp.log(nf / exact) / math.log(MAX_DISTANCE / exact) * (half - exact)).astype(jnp.int32)
    large = jnp.minimum(large, half - 1)
    bucket = jnp.where(rel > 0, half, 0) + jnp.where(n < exact, n, large)
    out = jnp.zeros(rel.shape, F32)
    for b in range(N_BUCKETS):
        out = jnp.where(bucket == b, tab_ref[b, h], out)
    return out


def _attn_body(tab_ref, bound_ref, q_ref, k_ref, v_ref, lq1_ref, lk1_ref, lq2_ref, lk2_ref,
               subln_ref, o_ref, bias_ref, acc1_ref, acc2_ref, m1_ref, m2_ref, *, n_main):
    T = ATT_BLOCK
    h = pl.program_id(0)
    qi = pl.program_id(1)
    bound = bound_ref[0]

    @pl.when(qi == 0)
    def _():
        offsets = (-T, 0, T, -N_META, -N_META - T, -2 * T, 2 * T)
        for kind, off in enumerate(offsets):
            masked = kind in (3, 4)

            def rows(rc, carry, off=off, masked=masked, kind=kind):
                r0 = pl.multiple_of(rc * 8, 8)
                r = r0 + lax.broadcasted_iota(jnp.int32, (8, T), 0)
                c = lax.broadcasted_iota(jnp.int32, (8, T), 1)
                b = _t5_bias(off + c - r, tab_ref, h) - bound
                if masked:
                    b = jnp.where(c < N_META, b, NEG_BIG)
                bias_ref[kind, pl.ds(r0, 8), :] = b
                return carry

            lax.fori_loop(0, T // 8, rows, 0)

    acc1_ref[...] = jnp.zeros(acc1_ref.shape, F32)
    acc2_ref[...] = jnp.zeros(acc2_ref.shape, F32)

    q = q_ref[...]
    q1 = q[:, :HEAD_DIM]
    q2 = q[:, HEAD_DIM:]
    nt = (((1,), (1,)), ((), ()))

    def tile(ki):
        koff = pl.multiple_of(ki * T, T)
        kb = k_ref[pl.ds(koff, T), :]
        va = v_ref[pl.ds(koff, T), :]
        d = ki - qi
        kind = jnp.where(ki == n_main, jnp.where(qi == 0, 3, 4),
                         jnp.where(d <= -2, 5, jnp.where(d >= 2, 6, d + 1)))
        return kb, va, bias_ref[kind]

    def bounded_step(ki, carry):
        kb, va, bias = tile(ki)
        s1 = lax.dot_general(q1, kb[:, :HEAD_DIM], nt, preferred_element_type=F32) + bias
        acc1_ref[...] += jnp.dot(jnp.exp(s1).astype(BF16), va, preferred_element_type=F32)
        s2 = lax.dot_general(q2, kb[:, HEAD_DIM:], nt, preferred_element_type=F32) + bias
        acc2_ref[...] += jnp.dot(jnp.exp(s2).astype(BF16), va, preferred_element_type=F32)
        return carry

    def online_map(s, va, m_ref, acc_ref):
        m_old = m_ref[...]
        m_new = jnp.maximum(m_old, jnp.max(s, axis=-1, keepdims=True))
        p = jnp.exp(s - m_new).astype(BF16)
        acc_ref[...] = (jnp.exp(m_old - m_new) * acc_ref[...]
                        + jnp.dot(p, va, preferred_element_type=F32))
        m_ref[...] = m_new

    def online_step(ki, carry):
        kb, va, bias = tile(ki)
        s1 = lax.dot_general(q1, kb[:, :HEAD_DIM], nt, preferred_element_type=F32) + bias
        online_map(s1, va, m1_ref, acc1_ref)
        s2 = lax.dot_general(q2, kb[:, HEAD_DIM:], nt, preferred_element_type=F32) + bias
        online_map(s2, va, m2_ref, acc2_ref)
        return carry

    no_running_max = 2.0 * bound <= MAX_EXP_RANGE

    @pl.when(no_running_max)
    def _():
        lax.fori_loop(0, n_main + 1, bounded_step, 0)

    @pl.when(jnp.logical_not(no_running_max))
    def _():
        m1_ref[...] = jnp.full(m1_ref.shape, -jnp.inf, F32)
        m2_ref[...] = jnp.full(m2_ref.shape, -jnp.inf, F32)
        lax.fori_loop(0, n_main + 1, online_step, 0)

    lam = (jnp.exp(jnp.sum(lq1_ref[...] * lk1_ref[...], axis=-1, keepdims=True))
           - jnp.exp(jnp.sum(lq2_ref[...] * lk2_ref[...], axis=-1, keepdims=True))
           + LAMBDA_INIT)
    a1 = acc1_ref[...]
    a2 = acc2_ref[...]
    o = (a1[:, :V_DIM] / a1[:, V_DIM:V_DIM + 1]
         - lam * (a2[:, :V_DIM] / a2[:, V_DIM:V_DIM + 1]))
    ms = jnp.mean(o * o, axis=-1, keepdims=True)
    o = o * lax.rsqrt(ms + EPS) * subln_ref[...] * (1.0 - LAMBDA_INIT)
    o_ref[...] = o.astype(BF16)


def _attention(proj, rel_bias, score_bound, lq1, lk1, lq2, lk2, subln, seq):
    T = ATT_BLOCK
    n_main = seq // T
    rows = proj.shape[0]
    vec64 = lambda: _resident((1, HEAD_DIM), lambda h, qi: (0, 0))
    return pl.pallas_call(
        functools.partial(_attn_body, n_main=n_main),
        grid=(HEADS, n_main),
        in_specs=[
            pl.BlockSpec(memory_space=pltpu.SMEM),
            pl.BlockSpec(memory_space=pltpu.SMEM),
            pl.BlockSpec((T, 2 * HEAD_DIM), lambda h, qi: (qi, h)),
            pl.BlockSpec((rows, 2 * HEAD_DIM), lambda h, qi: (0, HEADS + h)),
            pl.BlockSpec((rows, 2 * V_DIM), lambda h, qi: (0, V_OFFSET // (2 * V_DIM) + h)),
            vec64(), vec64(), vec64(), vec64(),
            _resident((1, V_DIM), lambda h, qi: (0, 0)),
        ],
        out_specs=pl.BlockSpec((T, V_DIM), lambda h, qi: (qi, h)),
        out_shape=jax.ShapeDtypeStruct((seq, ATT_WIDTH), BF16),
        scratch_shapes=[
            pltpu.VMEM((7, T, T), F32),
            pltpu.VMEM((T, 2 * V_DIM), F32), pltpu.VMEM((T, 2 * V_DIM), F32),
            pltpu.VMEM((T, 1), F32), pltpu.VMEM((T, 1), F32),
        ],
        compiler_params=pltpu.CompilerParams(
            dimension_semantics=("arbitrary", "arbitrary"), vmem_limit_bytes=VMEM_LIMIT),
        name="diff_attention",
    )(rel_bias, score_bound, proj, proj, proj, lq1, lk1, lq2, lk2, subln)


def _s5_prep(a_re, a_im, log_step, b_re, b_im, c_re, c_im):
    hp = lax.Precision.HIGHEST
    C = S5_CHUNK
    dt = jnp.exp(log_step)[..., None]
    decay = jnp.exp(a_re * dt)
    ab_re = decay * jnp.cos(a_im * dt)
    ab_im = decay * jnp.sin(a_im * dt)
    den = a_re * a_re + a_im * a_im
    zr = ab_re - 1.0
    f_re = (zr * a_re + ab_im * a_im) / den
    f_im = (ab_im * a_re - zr * a_im) / den
    bb_re = f_re[..., None] * b_re - f_im[..., None] * b_im
    bb_im = f_re[..., None] * b_im + f_im[..., None] * b_re
    pr, pi = jnp.ones_like(ab_re), jnp.zeros_like(ab_re)
    pw_re, pw_im = [pr], [pi]
    for _ in range(C):
        pr, pi = pr * ab_re - pi * ab_im, pr * ab_im + pi * ab_re
        pw_re.append(pr)
        pw_im.append(pi)
    pw_re = jnp.stack(pw_re)
    pw_im = jnp.stack(pw_im)
    cp_re = c_re[None] * pw_re[:, :, :, None, :] - c_im[None] * pw_im[:, :, :, None, :]
    cp_im = c_re[None] * pw_im[:, :, :, None, :] + c_im[None] * pw_re[:, :, :, None, :]
    kern = (jnp.einsum('tdgpn,dgnq->tdgpq', cp_re[:C], bb_re, precision=hp)
            - jnp.einsum('tdgpn,dgnq->tdgpq', cp_im[:C], bb_im, precision=hp))
    jj = jnp.arange(C)[:, None]
    ii = jnp.arange(C)[None, :]
    kf = jnp.where((ii >= jj)[:, :, None, None, None], kern[:, 0][jnp.clip(ii - jj, 0, C - 1)], 0.0)
    kr = jnp.where((jj >= ii)[:, :, None, None, None], kern[:, 1][jnp.clip(jj - ii, 0, C - 1)], 0.0)
    mm = (kf + kr).transpose(2, 0, 4, 1, 3).reshape(SSM_GROUPS, C * SSM_GROUP, C * SSM_GROUP)

    def in_mat(pwr, pwi, d):
        re = pwr[:, :, :, None] * bb_re[d][None] - pwi[:, :, :, None] * bb_im[d][None]
        im = pwr[:, :, :, None] * bb_im[d][None] + pwi[:, :, :, None] * bb_re[d][None]
        return re.transpose(1, 0, 3, 2), im.transpose(1, 0, 3, 2)

    pf_re, pf_im = in_mat(pw_re[:C, 0][::-1], pw_im[:C, 0][::-1], 0)
    pr_re, pr_im = in_mat(pw_re[:C, 1], pw_im[:C, 1], 1)
    pp = jnp.concatenate([pf_re, pr_re, pf_im, pr_im], axis=-1).reshape(SSM_GROUPS, C * SSM_GROUP, 4 * SSM_STATE)

    def out_mat(cr, ci):
        return cr.transpose(1, 3, 0, 2), -ci.transpose(1, 3, 0, 2)

    qf_re, qf_im = out_mat(cp_re[1:, 0], cp_im[1:, 0])
    qr_re, qr_im = out_mat(cp_re[1:, 1][::-1], cp_im[1:, 1][::-1])
    qq = jnp.concatenate([qf_re, qr_re, qf_im, qr_im], axis=1).reshape(SSM_GROUPS, 4 * SSM_STATE, C * SSM_GROUP)
    lam16 = jnp.stack([jnp.concatenate([pw_re[C, 0], pw_re[C, 1]], axis=-1),
                       jnp.concatenate([pw_im[C, 0], pw_im[C, 1]], axis=-1)], axis=1)
    return pp.astype(BF16), mm.astype(BF16), qq.astype(BF16), lam16


def _s5_body(u_ref, pp_ref, mm_ref, qq_ref, lam_ref, y_ref,
             zre_ref, zim_ref, are_ref, aim_ref, bre_ref, bim_ref, *, n_chunks):
    GS = S5_GROUPS_PER_STEP
    NS = 2 * SSM_STATE
    for gi in range(GS):
        z = jnp.dot(u_ref[gi], pp_ref[gi], preferred_element_type=F32)
        zre_ref[:, gi, :] = z[:, :NS]
        zim_ref[:, gi, :] = z[:, NS:]

    ar = lam_ref[:, 0, :]
    ai = lam_ref[:, 1, :]
    fwd = lax.broadcasted_iota(jnp.int32, (GS, NS), 1) < SSM_STATE
    sre0 = jnp.where(fwd, zre_ref[n_chunks], 0.0)
    sim0 = jnp.where(fwd, zim_ref[n_chunks], 0.0)

    def scan_step(k, carry):
        sre, sim = carry
        kr = n_chunks - 1 - k
        are_ref[k] = sre
        aim_ref[k] = sim
        bre_ref[kr] = sre
        bim_ref[kr] = sim
        zr = jnp.where(fwd, zre_ref[k], zre_ref[kr])
        zi = jnp.where(fwd, zim_ref[k], zim_ref[kr])
        return ar * sre - ai * sim + zr, ar * sim + ai * sre + zi

    lax.fori_loop(0, n_chunks, scan_step, (sre0, sim0))

    fwd_rows = lax.broadcasted_iota(jnp.int32, (n_chunks, NS), 1) < SSM_STATE
    for gi in range(GS):
        s_re = jnp.where(fwd_rows, are_ref[:, gi, :], bre_ref[:, gi, :])
        s_im = jnp.where(fwd_rows, aim_ref[:, gi, :], bim_ref[:, gi, :])
        scat = jnp.concatenate([s_re, s_im], axis=1).astype(BF16)
        y = (jnp.dot(u_ref[gi, :n_chunks, :], mm_ref[gi], preferred_element_type=F32)
             + jnp.dot(scat, qq_ref[gi], preferred_element_type=F32))
        y_ref[gi] = y


def _s5(u_chunks, pp, mm, qq, lam16, n_chunks):
    GS = S5_GROUPS_PER_STEP
    n_rows = u_chunks.shape[1]
    W = S5_CHUNK * SSM_GROUP
    NS = 2 * SSM_STATE
    mat = lambda: pl.BlockSpec((GS, W, W), lambda g: (g, 0, 0))
    return pl.pallas_call(
        functools.partial(_s5_body, n_chunks=n_chunks),
        grid=(SSM_GROUPS // GS,),
        in_specs=[
            pl.BlockSpec((GS, n_rows, W), lambda g: (g, 0, 0)),
            mat(), mat(), mat(),
            pl.BlockSpec((GS, 2, NS), lambda g: (g, 0, 0)),
        ],
        out_specs=pl.BlockSpec((GS, n_chunks, W), lambda g: (g, 0, 0)),
        out_shape=jax.ShapeDtypeStruct((SSM_GROUPS, n_chunks, W), F32),
        scratch_shapes=[pltpu.VMEM((n_rows, GS, NS), F32), pltpu.VMEM((n_rows, GS, NS), F32)]
        + [pltpu.VMEM((n_chunks, GS, NS), F32) for _ in range(4)],
        compiler_params=pltpu.CompilerParams(
            dimension_semantics=("arbitrary",), vmem_limit_bytes=VMEM_LIMIT),
        name="s5_chunked",
    )(u_chunks, pp, mm, qq, lam16)


def _mix_out_body(x_ref, att_ref, y_ref, u_ref, d_ref, wglu_ref, bglu_ref, sn_ref, wout_ref,
                  nf_ref, rwt_ref, sg_ref, su_ref, sd_ref, h_ref, hn_ref, sc_ref):
    y = y_ref[...] + d_ref[...] * u_ref[...].astype(F32)
    y = jax.nn.gelu(y)
    gate = jax.nn.sigmoid(jnp.dot(y.astype(BF16), wglu_ref[...], preferred_element_type=F32) + bglu_ref[...])
    s = y * gate
    ms = jnp.mean(s * s, axis=-1, keepdims=True)
    ssm = (s * lax.rsqrt(ms + EPS) * sn_ref[...]).astype(BF16)
    h = (x_ref[...]
         + jnp.dot(att_ref[...], wout_ref[:ATT_WIDTH, :], preferred_element_type=F32)
         + jnp.dot(ssm, wout_ref[ATT_WIDTH:, :], preferred_element_type=F32))
    ms = jnp.mean(h * h, axis=-1, keepdims=True)
    hn = h * lax.rsqrt(ms + EPS) * nf_ref[...]
    hnb = hn.astype(BF16)
    hn_ref[...] = hnb
    logits_t = lax.dot_general(rwt_ref[...], hn, (((1,), (1,)), ((), ())),
                               precision=lax.Precision.HIGHEST, preferred_element_type=F32)
    sc_ref[...] = jax.nn.sigmoid(logits_t)
    g = jnp.dot(hnb, sg_ref[...], preferred_element_type=F32)
    up = jnp.dot(hnb, su_ref[...], preferred_element_type=F32)
    act = (jax.nn.silu(g) * up).astype(BF16)
    h_ref[...] = h + jnp.dot(act, sd_ref[...], preferred_element_type=F32)


def _mix_out(x2, att, y, proj, d_skip, w_glu, b_glu, ssm_norm, w_out, norm_ffn, router_wt,
             sh_gate, sh_up, sh_down):
    seq = x2.shape[0]
    R = MIX_ROW_BLOCK
    res = lambda shape: _resident(shape, lambda i: (0, 0))
    return pl.pallas_call(
        _mix_out_body,
        grid=(seq // R,),
        in_specs=[
            pl.BlockSpec((R, D_MODEL), lambda i: (i, 0)),
            pl.BlockSpec((R, ATT_WIDTH), lambda i: (i, 0)),
            pl.BlockSpec((R, SSM_WIDTH), lambda i: (i, 0)),
            pl.BlockSpec((R, SSM_WIDTH), lambda i: (i, U_OFFSET // SSM_WIDTH)),
            res((1, SSM_WIDTH)), res((SSM_WIDTH, SSM_WIDTH)), res((1, SSM_WIDTH)), res((1, SSM_WIDTH)),
            res((D_MODEL, D_MODEL)), res((1, D_MODEL)), res((N_EXPERTS, D_MODEL)),
            res((D_MODEL, SHARED_HIDDEN)), res((D_MODEL, SHARED_HIDDEN)), res((SHARED_HIDDEN, D_MODEL)),
        ],
        out_specs=[
            pl.BlockSpec((R, D_MODEL), lambda i: (i, 0)),
            pl.BlockSpec((R, D_MODEL), lambda i: (i, 0)),
            pl.BlockSpec((N_EXPERTS, R), lambda i: (0, i)),
        ],
        out_shape=[
            jax.ShapeDtypeStruct((seq, D_MODEL), F32),
            jax.ShapeDtypeStruct((seq, D_MODEL), BF16),
            jax.ShapeDtypeStruct((N_EXPERTS, seq), F32),
        ],
        compiler_params=pltpu.CompilerParams(
            dimension_semantics=("arbitrary",), vmem_limit_bytes=VMEM_LIMIT),
        name="mix_out_shared",
    )(x2, att, y, proj, d_skip, w_glu, b_glu, ssm_norm, w_out, norm_ffn, router_wt,
      sh_gate, sh_up, sh_down)


def _route_body(sc_ref, rb_ref, w_ref):
    scores = sc_ref[...]
    R = scores.shape[1]
    per_group = N_EXPERTS // N_EXPERT_GROUPS
    choice = scores + rb_ref[...]
    c3 = choice.reshape(N_EXPERT_GROUPS, per_group, R)
    within = lax.broadcasted_iota(jnp.int32, c3.shape, 1)
    m1 = jnp.max(c3, axis=1, keepdims=True)
    first = jnp.min(jnp.where(c3 == m1, within, per_group), axis=1, keepdims=True)
    m2 = jnp.max(jnp.where(within == first, -jnp.inf, c3), axis=1, keepdims=True)
    grp = (m1 + m2).reshape(N_EXPERT_GROUPS, R)
    gidx = lax.broadcasted_iota(jnp.int32, grp.shape, 0)
    grank = jnp.zeros(grp.shape, jnp.int32)
    for b in range(N_EXPERT_GROUPS):
        gb = grp[b:b + 1, :]
        grank += ((gb > grp) | ((gb == grp) & (b < gidx))).astype(jnp.int32)
    gmask = grank < TOPK_GROUPS
    emask = jnp.broadcast_to(gmask[:, None, :], c3.shape).reshape(N_EXPERTS, R)
    val = jnp.where(emask, choice, -jnp.inf)
    eidx = lax.broadcasted_iota(jnp.int32, val.shape, 0)
    rank = jnp.zeros(val.shape, jnp.int32)
    for e in range(N_EXPERTS):
        ve = val[e:e + 1, :]
        rank += ((ve > val) | ((ve == val) & (e < eidx))).astype(jnp.int32)
    sel = rank < TOP_K
    w = jnp.where(sel, scores, 0.0)
    w_ref[...] = w / jnp.sum(w, axis=0, keepdims=True) * ROUTED_SCALE


def _route(scores_t, router_bias):
    seq = scores_t.shape[1]
    R = ROW_BLOCK
    return pl.pallas_call(
        _route_body,
        grid=(seq // R,),
        in_specs=[pl.BlockSpec((N_EXPERTS, R), lambda i: (0, i)),
                  _resident((N_EXPERTS, 1), lambda i: (0, 0))],
        out_specs=pl.BlockSpec((N_EXPERTS, R), lambda i: (0, i)),
        out_shape=jax.ShapeDtypeStruct((N_EXPERTS, seq), F32),
        compiler_params=pltpu.CompilerParams(dimension_semantics=("arbitrary",)),
        name="route",
    )(scores_t, router_bias)


def _moe_body(hn_ref, w_ref, h_ref, wg_ref, wu_ref, wd_ref, o_ref):
    e = pl.program_id(1)

    @pl.when(e == 0)
    def _():
        o_ref[...] = h_ref[...]

    x = hn_ref[...]
    g = jnp.dot(x, wg_ref[0], preferred_element_type=F32)
    up = jnp.dot(x, wu_ref[0], preferred_element_type=F32)
    act = (jax.nn.silu(g) * up).astype(BF16)
    y = jnp.dot(act, wd_ref[0], preferred_element_type=F32)
    wts = w_ref[...]
    lane = lax.broadcasted_iota(jnp.int32, wts.shape, 1)
    wcol = jnp.sum(jnp.where(lane == e, wts, 0.0), axis=1, keepdims=True)
    o_ref[...] += y * wcol


def _moe_dense(hn, wts, h, wg, wu, wd):
    seq = hn.shape[0]
    R = MOE_ROW_BLOCK
    return pl.pallas_call(
        _moe_body,
        grid=(seq // R, N_EXPERTS),
        in_specs=[
            pl.BlockSpec((R, D_MODEL), lambda b, e: (b, 0)),
            pl.BlockSpec((R, N_EXPERTS), lambda b, e: (b, 0)),
            pl.BlockSpec((R, D_MODEL), lambda b, e: (b, 0)),
            pl.BlockSpec((1, D_MODEL, EXPERT_HIDDEN), lambda b, e: (e, 0, 0)),
            pl.BlockSpec((1, D_MODEL, EXPERT_HIDDEN), lambda b, e: (e, 0, 0)),
            pl.BlockSpec((1, EXPERT_HIDDEN, D_MODEL), lambda b, e: (e, 0, 0)),
        ],
        out_specs=pl.BlockSpec((R, D_MODEL), lambda b, e: (b, 0)),
        out_shape=jax.ShapeDtypeStruct((seq, D_MODEL), F32),
        compiler_params=pltpu.CompilerParams(
            dimension_semantics=("arbitrary", "arbitrary"), vmem_limit_bytes=VMEM_LIMIT),
        name="moe_dense",
    )(hn, wts, h, wg, wu, wd)


def kernel(x, meta_tokens, rel_bias, norm_mix, w_in, q_norm, k_norm, lam_q1, lam_k1, lam_q2, lam_k2, subln, ssm_a_re, ssm_a_im, ssm_log_step, ssm_b_re, ssm_b_im, ssm_c_re, ssm_c_im, ssm_d, w_glu, b_glu, ssm_norm, w_out, norm_ffn, router_w, router_bias, w_gate, w_up, w_down, shared_gate, shared_up, shared_down):
    batch, seq, d = x.shape
    assert batch == 1 and d == D_MODEL and seq % ROW_BLOCK == 0 and seq % ATT_BLOCK == 0
    assert norm_mix.shape[0] == 1, "single layer"
    x2 = x.reshape(seq, d)
    meta_pad = jnp.zeros((ROW_BLOCK, d), F32).at[:N_META].set(meta_tokens.astype(F32))
    seg = jnp.kron(jnp.eye(QK_WIDTH // HEAD_DIM, dtype=F32),
                   jnp.full((HEAD_DIM, HEAD_DIM), 1.0 / HEAD_DIM, F32)).astype(BF16)
    qg = jnp.tile(q_norm[0].astype(F32), QK_WIDTH // HEAD_DIM)[None] * (HEAD_DIM ** -0.5)
    kg = jnp.tile(k_norm[0].astype(F32), QK_WIDTH // HEAD_DIM)[None]

    proj = _inproj(x2, meta_pad, norm_mix[0][None], w_in[0].astype(BF16), seg, qg, kg)

    score_bound = (BOUND_MARGIN * HEAD_DIM ** 0.5 * jnp.max(jnp.abs(q_norm[0].astype(F32)))
                   * jnp.max(jnp.abs(k_norm[0].astype(F32)))
                   + jnp.max(jnp.abs(rel_bias.astype(F32)))).reshape(1)
    att = _attention(proj, rel_bias.astype(F32), score_bound, lam_q1[0][None], lam_k1[0][None],
                     lam_q2[0][None], lam_k2[0][None], subln[0][None], seq)

    n_rows = proj.shape[0] // S5_CHUNK
    n_chunks = seq // S5_CHUNK
    u_chunks = (proj[:, U_OFFSET:]
                .reshape(n_rows, S5_CHUNK, SSM_GROUPS, SSM_GROUP)
                .transpose(2, 0, 1, 3).reshape(SSM_GROUPS, n_rows, S5_CHUNK * SSM_GROUP))
    pp, mm, qq, lam16 = _s5_prep(ssm_a_re[0].astype(F32), ssm_a_im[0].astype(F32),
                                 ssm_log_step[0].astype(F32), ssm_b_re[0].astype(F32),
                                 ssm_b_im[0].astype(F32), ssm_c_re[0].astype(F32),
                                 ssm_c_im[0].astype(F32))
    y_chunks = _s5(u_chunks, pp, mm, qq, lam16, n_chunks)
    y = (y_chunks.reshape(SSM_GROUPS, n_chunks, S5_CHUNK, SSM_GROUP)
         .transpose(1, 2, 0, 3).reshape(seq, SSM_WIDTH))

    h, hn, scores_t = _mix_out(
        x2, att, y, proj, ssm_d[0][None].astype(F32), w_glu[0].astype(BF16), b_glu[0][None].astype(F32),
        ssm_norm[0][None].astype(F32), w_out[0].astype(BF16), norm_ffn[0][None].astype(F32),
        router_w[0].astype(F32).T, shared_gate[0].astype(BF16), shared_up[0].astype(BF16),
        shared_down[0].astype(BF16))

    wts_t = _route(scores_t, router_bias[0].astype(F32)[:, None])
    out = _moe_dense(hn, wts_t.T, h, w_gate[0].astype(BF16), w_up[0].astype(BF16),
                     w_down[0].astype(BF16))
    return out.reshape(batch, seq, d)
```

```python
import functools
import math

import jax
import jax.numpy as jnp
from jax import lax
from jax.experimental import pallas as pl
from jax.experimental.pallas import tpu as pltpu

F32 = jnp.float32
BF16 = jnp.bfloat16

D_MODEL = 2048
N_META = 16
ATT_WIDTH = 1024
SSM_WIDTH = 1024
HEAD_DIM = 64
V_DIM = 128
HEADS = 8
QK_WIDTH = 1024
IN_WIDTH = 4096
V_OFFSET = 2 * QK_WIDTH
U_OFFSET = V_OFFSET + 2 * ATT_WIDTH
PROJ_WIDTH = U_OFFSET + SSM_WIDTH
SSM_GROUP = 16
SSM_GROUPS = 64
SSM_STATE = 64
N_BUCKETS = 32
MAX_DISTANCE = 128
N_EXPERTS = 64
TOP_K = 8
N_EXPERT_GROUPS = 8
TOPK_GROUPS = 4
EXPERT_HIDDEN = 512
SHARED_HIDDEN = 512
ROUTED_SCALE = 2.5
EPS = 1e-6
LAMBDA_INIT = 0.8 - 0.6 * math.exp(-0.3 * 0)

ROW_BLOCK = 512
ATT_BLOCK = 512
S5_CHUNK = 16
S5_GROUPS_PER_STEP = 8
MOE_TOKEN_BLOCK = 256
MOE_UNIT = 16
MOE_SLOT_BLOCK = 256
MOE_BLOCK_ROWS = -(-(MOE_TOKEN_BLOCK * TOP_K + N_EXPERTS * (MOE_UNIT - 1)) // MOE_SLOT_BLOCK) * MOE_SLOT_BLOCK
NOT_ROUTED = -1e6
MIX_ROW_BLOCK = 256
NEG_BIG = -1e30
MAX_EXP_RANGE = 80.0
BOUND_MARGIN = 1.02
VMEM_LIMIT = 56 * 1024 * 1024


def _resident(shape, index_map):
    return pl.BlockSpec(shape, index_map, pipeline_mode=pl.Buffered(1))


def _inproj_body(x_ref, meta_ref, g_ref, w_ref, seg_ref, qg_ref, kg_ref, o_ref, *, n_xblk):
    i = pl.program_id(0)

    def run(src_ref):
        xv = src_ref[...]
        ms = jnp.mean(xv * xv, axis=-1, keepdims=True)
        hn = (xv * lax.rsqrt(ms + EPS) * g_ref[...]).astype(BF16)
        for s in range(4):
            ps = jnp.dot(hn, w_ref[:, s * 1024:(s + 1) * 1024], preferred_element_type=F32)
            if s < 2:
                gain = qg_ref if s == 0 else kg_ref
                msq = jnp.dot((ps * ps).astype(BF16), seg_ref[...], preferred_element_type=F32)
                ps = ps * lax.rsqrt(msq + EPS) * gain[...]
            pb = ps.astype(BF16)
            if s < 2:
                o_ref[:, s * 1024:(s + 1) * 1024] = pb
            elif s == 2:
                lane = lax.broadcasted_iota(jnp.int32, (pb.shape[0], V_DIM), 1)
                ones_col = jnp.where(lane == 0, 1.0, 0.0).astype(BF16)
                for hh in range(HEADS):
                    base = V_OFFSET + hh * 2 * V_DIM
                    o_ref[:, base:base + V_DIM] = pb[:, hh * V_DIM:(hh + 1) * V_DIM]
                    o_ref[:, base + V_DIM:base + 2 * V_DIM] = ones_col
            else:
                o_ref[:, U_OFFSET:U_OFFSET + SSM_WIDTH] = pb

    @pl.when(i < n_xblk)
    def _():
        run(x_ref)

    @pl.when(i == n_xblk)
    def _():
        run(meta_ref)


def _inproj(x2, meta_pad, gain, w_bf, seg, qg, kg):
    seq = x2.shape[0]
    n_xblk = seq // ROW_BLOCK
    rows = seq + ROW_BLOCK
    return pl.pallas_call(
        functools.partial(_inproj_body, n_xblk=n_xblk),
        grid=(n_xblk + 1,),
        in_specs=[
            pl.BlockSpec((ROW_BLOCK, D_MODEL), lambda i: (jnp.minimum(i, n_xblk - 1), 0)),
            _resident((ROW_BLOCK, D_MODEL), lambda i: (0, 0)),
            _resident((1, D_MODEL), lambda i: (0, 0)),
            _resident((D_MODEL, IN_WIDTH), lambda i: (0, 0)),
            _resident((QK_WIDTH, QK_WIDTH), lambda i: (0, 0)),
            _resident((1, QK_WIDTH), lambda i: (0, 0)),
            _resident((1, QK_WIDTH), lambda i: (0, 0)),
        ],
        out_specs=pl.BlockSpec((ROW_BLOCK, PROJ_WIDTH), lambda i: (i, 0)),
        out_shape=jax.ShapeDtypeStruct((rows, PROJ_WIDTH), BF16),
        compiler_params=pltpu.CompilerParams(
            dimension_semantics=("arbitrary",), vmem_limit_bytes=VMEM_LIMIT),
        name="inproj",
    )(x2, meta_pad, gain, w_bf, seg, qg, kg)


def _t5_bias(rel, tab_ref, h):
    half = N_BUCKETS // 2
    exact = half // 2
    n = jnp.abs(rel)
    nf = jnp.maximum(n, 1).astype(F32)
    large = exact + (jnp.log(nf / exact) / math.log(MAX_DISTANCE / exact) * (half - exact)).astype(jnp.int32)
    large = jnp.minimum(large, half - 1)
    bucket = jnp.where(rel > 0, half, 0) + jnp.where(n < exact, n, large)
    out = jnp.zeros(rel.shape, F32)
    for b in range(N_BUCKETS):
        out = jnp.where(bucket == b, tab_ref[b, h], out)
    return out


def _attn_body(tab_ref, bound_ref, q_ref, k_ref, v_ref, lq1_ref, lk1_ref, lq2_ref, lk2_ref,
               subln_ref, o_ref, bias_ref, acc1_ref, acc2_ref, m1_ref, m2_ref, *, n_main):
    T = ATT_BLOCK
    h = pl.program_id(0)
    qi = pl.program_id(1)
    bound = bound_ref[0]

    @pl.when(qi == 0)
    def _():
        offsets = (-T, 0, T, -N_META, -N_META - T, -2 * T, 2 * T)
        for kind, off in enumerate(offsets):
            masked = kind in (3, 4)

            def rows(rc, carry, off=off, masked=masked, kind=kind):
                r0 = pl.multiple_of(rc * 8, 8)
                r = r0 + lax.broadcasted_iota(jnp.int32, (8, T), 0)
                c = lax.broadcasted_iota(jnp.int32, (8, T), 1)
                b = _t5_bias(off + c - r, tab_ref, h) - bound
                if masked:
                    b = jnp.where(c < N_META, b, NEG_BIG)
                bias_ref[kind, pl.ds(r0, 8), :] = b
                return carry

            lax.fori_loop(0, T // 8, rows, 0)

    acc1_ref[...] = jnp.zeros(acc1_ref.shape, F32)
    acc2_ref[...] = jnp.zeros(acc2_ref.shape, F32)

    q = q_ref[...]
    q1 = q[:, :HEAD_DIM]
    q2 = q[:, HEAD_DIM:]
    nt = (((1,), (1,)), ((), ()))

    def tile(ki):
        koff = pl.multiple_of(ki * T, T)
        kb = k_ref[pl.ds(koff, T), :]
        va = v_ref[pl.ds(koff, T), :]
        d = ki - qi
        kind = jnp.where(ki == n_main, jnp.where(qi == 0, 3, 4),
                         jnp.where(d <= -2, 5, jnp.where(d >= 2, 6, d + 1)))
        return kb, va, bias_ref[kind]

    def bounded_step(ki, carry):
        kb, va, bias = tile(ki)
        s1 = lax.dot_general(q1, kb[:, :HEAD_DIM], nt, preferred_element_type=F32) + bias
        acc1_ref[...] += jnp.dot(jnp.exp(s1).astype(BF16), va, preferred_element_type=F32)
        s2 = lax.dot_general(q2, kb[:, HEAD_DIM:], nt, preferred_element_type=F32) + bias
        acc2_ref[...] += jnp.dot(jnp.exp(s2).astype(BF16), va, preferred_element_type=F32)
        return carry

    def online_map(s, va, m_ref, acc_ref):
        m_old = m_ref[...]
        m_new = jnp.maximum(m_old, jnp.max(s, axis=-1, keepdims=True))
        p = jnp.exp(s - m_new).astype(BF16)
        acc_ref[...] = (jnp.exp(m_old - m_new) * acc_ref[...]
                        + jnp.dot(p, va, preferred_element_type=F32))
        m_ref[...] = m_new

    def online_step(ki, carry):
        kb, va, bias = tile(ki)
        s1 = lax.dot_general(q1, kb[:, :HEAD_DIM], nt, preferred_element_type=F32) + bias
        online_map(s1, va, m1_ref, acc1_ref)
        s2 = lax.dot_general(q2, kb[:, HEAD_DIM:], nt, preferred_element_type=F32) + bias
        online_map(s2, va, m2_ref, acc2_ref)
        return carry

    no_running_max = 2.0 * bound <= MAX_EXP_RANGE

    @pl.when(no_running_max)
    def _():
        lax.fori_loop(0, n_main + 1, bounded_step, 0)

    @pl.when(jnp.logical_not(no_running_max))
    def _():
        m1_ref[...] = jnp.full(m1_ref.shape, -jnp.inf, F32)
        m2_ref[...] = jnp.full(m2_ref.shape, -jnp.inf, F32)
        lax.fori_loop(0, n_main + 1, online_step, 0)

    lam = (jnp.exp(jnp.sum(lq1_ref[...] * lk1_ref[...], axis=-1, keepdims=True))
           - jnp.exp(jnp.sum(lq2_ref[...] * lk2_ref[...], axis=-1, keepdims=True))
           + LAMBDA_INIT)
    a1 = acc1_ref[...]
    a2 = acc2_ref[...]
    o = (a1[:, :V_DIM] / a1[:, V_DIM:V_DIM + 1]
         - lam * (a2[:, :V_DIM] / a2[:, V_DIM:V_DIM + 1]))
    ms = jnp.mean(o * o, axis=-1, keepdims=True)
    o = o * lax.rsqrt(ms + EPS) * subln_ref[...] * (1.0 - LAMBDA_INIT)
    o_ref[...] = o.astype(BF16)


def _attention(proj, rel_bias, score_bound, lq1, lk1, lq2, lk2, subln, seq):
    T = ATT_BLOCK
    n_main = seq // T
    rows = proj.shape[0]
    vec64 = lambda: _resident((1, HEAD_DIM), lambda h, qi: (0, 0))
    return pl.pallas_call(
        functools.partial(_attn_body, n_main=n_main),
        grid=(HEADS, n_main),
        in_specs=[
            pl.BlockSpec(memory_space=pltpu.SMEM),
            pl.BlockSpec(memory_space=pltpu.SMEM),
            pl.BlockSpec((T, 2 * HEAD_DIM), lambda h, qi: (qi, h)),
            pl.BlockSpec((rows, 2 * HEAD_DIM), lambda h, qi: (0, HEADS + h)),
            pl.BlockSpec((rows, 2 * V_DIM), lambda h, qi: (0, V_OFFSET // (2 * V_DIM) + h)),
            vec64(), vec64(), vec64(), vec64(),
            _resident((1, V_DIM), lambda h, qi: (0, 0)),
        ],
        out_specs=pl.BlockSpec((T, V_DIM), lambda h, qi: (qi, h)),
        out_shape=jax.ShapeDtypeStruct((seq, ATT_WIDTH), BF16),
        scratch_shapes=[
            pltpu.VMEM((7, T, T), F32),
            pltpu.VMEM((T, 2 * V_DIM), F32), pltpu.VMEM((T, 2 * V_DIM), F32),
            pltpu.VMEM((T, 1), F32), pltpu.VMEM((T, 1), F32),
        ],
        compiler_params=pltpu.CompilerParams(
            dimension_semantics=("arbitrary", "arbitrary"), vmem_limit_bytes=VMEM_LIMIT),
        name="diff_attention",
    )(rel_bias, score_bound, proj, proj, proj, lq1, lk1, lq2, lk2, subln)


def _s5_prep(a_re, a_im, log_step, b_re, b_im, c_re, c_im):
    hp = lax.Precision.HIGHEST
    C = S5_CHUNK
    dt = jnp.exp(log_step)[..., None]
    decay = jnp.exp(a_re * dt)
    ab_re = decay * jnp.cos(a_im * dt)
    ab_im = decay * jnp.sin(a_im * dt)
    den = a_re * a_re + a_im * a_im
    zr = ab_re - 1.0
    f_re = (zr * a_re + ab_im * a_im) / den
    f_im = (ab_im * a_re - zr * a_im) / den
    bb_re = f_re[..., None] * b_re - f_im[..., None] * b_im
    bb_im = f_re[..., None] * b_im + f_im[..., None] * b_re
    pr, pi = jnp.ones_like(ab_re), jnp.zeros_like(ab_re)
    pw_re, pw_im = [pr], [pi]
    for _ in range(C):
        pr, pi = pr * ab_re - pi * ab_im, pr * ab_im + pi * ab_re
        pw_re.append(pr)
        pw_im.append(pi)
    pw_re = jnp.stack(pw_re)
    pw_im = jnp.stack(pw_im)
    cp_re = c_re[None] * pw_re[:, :, :, None, :] - c_im[None] * pw_im[:, :, :, None, :]
    cp_im = c_re[None] * pw_im[:, :, :, None, :] + c_im[None] * pw_re[:, :, :, None, :]
    kern = (jnp.einsum('tdgpn,dgnq->tdgpq', cp_re[:C], bb_re, precision=hp)
            - jnp.einsum('tdgpn,dgnq->tdgpq', cp_im[:C], bb_im, precision=hp))
    jj = jnp.arange(C)[:, None]
    ii = jnp.arange(C)[None, :]
    kf = jnp.where((ii >= jj)[:, :, None, None, None], kern[:, 0][jnp.clip(ii - jj, 0, C - 1)], 0.0)
    kr = jnp.where((jj >= ii)[:, :, None, None, None], kern[:, 1][jnp.clip(jj - ii, 0, C - 1)], 0.0)
    mm = (kf + kr).transpose(2, 0, 4, 1, 3).reshape(SSM_GROUPS, C * SSM_GROUP, C * SSM_GROUP)

    def in_mat(pwr, pwi, d):
        re = pwr[:, :, :, None] * bb_re[d][None] - pwi[:, :, :, None] * bb_im[d][None]
        im = pwr[:, :, :, None] * bb_im[d][None] + pwi[:, :, :, None] * bb_re[d][None]
        return re.transpose(1, 0, 3, 2), im.transpose(1, 0, 3, 2)

    pf_re, pf_im = in_mat(pw_re[:C, 0][::-1], pw_im[:C, 0][::-1], 0)
    pr_re, pr_im = in_mat(pw_re[:C, 1], pw_im[:C, 1], 1)
    pp = jnp.concatenate([pf_re, pr_re, pf_im, pr_im], axis=-1).reshape(SSM_GROUPS, C * SSM_GROUP, 4 * SSM_STATE)

    def out_mat(cr, ci):
        return cr.transpose(1, 3, 0, 2), -ci.transpose(1, 3, 0, 2)

    qf_re, qf_im = out_mat(cp_re[1:, 0], cp_im[1:, 0])
    qr_re, qr_im = out_mat(cp_re[1:, 1][::-1], cp_im[1:, 1][::-1])
    qq = jnp.concatenate([qf_re, qr_re, qf_im, qr_im], axis=1).reshape(SSM_GROUPS, 4 * SSM_STATE, C * SSM_GROUP)
    lam16 = jnp.stack([jnp.concatenate([pw_re[C, 0], pw_re[C, 1]], axis=-1),
                       jnp.concatenate([pw_im[C, 0], pw_im[C, 1]], axis=-1)], axis=1)
    return pp.astype(BF16), mm.astype(BF16), qq.astype(BF16), lam16


def _s5_body(u_ref, pp_ref, mm_ref, qq_ref, lam_ref, y_ref,
             zre_ref, zim_ref, are_ref, aim_ref, bre_ref, bim_ref, *, n_chunks):
    GS = S5_GROUPS_PER_STEP
    NS = 2 * SSM_STATE
    for gi in range(GS):
        z = jnp.dot(u_ref[gi], pp_ref[gi], preferred_element_type=F32)
        zre_ref[:, gi, :] = z[:, :NS]
        zim_ref[:, gi, :] = z[:, NS:]

    ar = lam_ref[:, 0, :]
    ai = lam_ref[:, 1, :]
    fwd = lax.broadcasted_iota(jnp.int32, (GS, NS), 1) < SSM_STATE
    sre0 = jnp.where(fwd, zre_ref[n_chunks], 0.0)
    sim0 = jnp.where(fwd, zim_ref[n_chunks], 0.0)

    def scan_step(k, carry):
        sre, sim = carry
        kr = n_chunks - 1 - k
        are_ref[k] = sre
        aim_ref[k] = sim
        bre_ref[kr] = sre
        bim_ref[kr] = sim
        zr = jnp.where(fwd, zre_ref[k], zre_ref[kr])
        zi = jnp.where(fwd, zim_ref[k], zim_ref[kr])
        return ar * sre - ai * sim + zr, ar * sim + ai * sre + zi

    lax.fori_loop(0, n_chunks, scan_step, (sre0, sim0))

    fwd_rows = lax.broadcasted_iota(jnp.int32, (n_chunks, NS), 1) < SSM_STATE
    for gi in range(GS):
        s_re = jnp.where(fwd_rows, are_ref[:, gi, :], bre_ref[:, gi, :])
        s_im = jnp.where(fwd_rows, aim_ref[:, gi, :], bim_ref[:, gi, :])
        scat = jnp.concatenate([s_re, s_im], axis=1).astype(BF16)
        y = (jnp.dot(u_ref[gi, :n_chunks, :], mm_ref[gi], preferred_element_type=F32)
             + jnp.dot(scat, qq_ref[gi], preferred_element_type=F32))
        y_ref[gi] = y


def _s5(u_chunks, pp, mm, qq, lam16, n_chunks):
    GS = S5_GROUPS_PER_STEP
    n_rows = u_chunks.shape[1]
    W = S5_CHUNK * SSM_GROUP
    NS = 2 * SSM_STATE
    mat = lambda: pl.BlockSpec((GS, W, W), lambda g: (g, 0, 0))
    return pl.pallas_call(
        functools.partial(_s5_body, n_chunks=n_chunks),
        grid=(SSM_GROUPS // GS,),
        in_specs=[
            pl.BlockSpec((GS, n_rows, W), lambda g: (g, 0, 0)),
            mat(), mat(), mat(),
            pl.BlockSpec((GS, 2, NS), lambda g: (g, 0, 0)),
        ],
        out_specs=pl.BlockSpec((GS, n_chunks, W), lambda g: (g, 0, 0)),
        out_shape=jax.ShapeDtypeStruct((SSM_GROUPS, n_chunks, W), F32),
        scratch_shapes=[pltpu.VMEM((n_rows, GS, NS), F32), pltpu.VMEM((n_rows, GS, NS), F32)]
        + [pltpu.VMEM((n_chunks, GS, NS), F32) for _ in range(4)],
        compiler_params=pltpu.CompilerParams(
            dimension_semantics=("arbitrary",), vmem_limit_bytes=VMEM_LIMIT),
        name="s5_chunked",
    )(u_chunks, pp, mm, qq, lam16)


def _mix_out_body(x_ref, att_ref, y_ref, u_ref, d_ref, wglu_ref, bglu_ref, sn_ref, wout_ref,
                  nf_ref, rwt_ref, sg_ref, su_ref, sd_ref, h_ref, hn_ref, sc_ref):
    y = y_ref[...] + d_ref[...] * u_ref[...].astype(F32)
    y = jax.nn.gelu(y)
    gate = jax.nn.sigmoid(jnp.dot(y.astype(BF16), wglu_ref[...], preferred_element_type=F32) + bglu_ref[...])
    s = y * gate
    ms = jnp.mean(s * s, axis=-1, keepdims=True)
    ssm = (s * lax.rsqrt(ms + EPS) * sn_ref[...]).astype(BF16)
    h = (x_ref[...]
         + jnp.dot(att_ref[...], wout_ref[:ATT_WIDTH, :], preferred_element_type=F32)
         + jnp.dot(ssm, wout_ref[ATT_WIDTH:, :], preferred_element_type=F32))
    ms = jnp.mean(h * h, axis=-1, keepdims=True)
    hn = h * lax.rsqrt(ms + EPS) * nf_ref[...]
    hnb = hn.astype(BF16)
    hn_ref[...] = hnb
    logits_t = lax.dot_general(rwt_ref[...], hn, (((1,), (1,)), ((), ())),
                               precision=lax.Precision.HIGHEST, preferred_element_type=F32)
    sc_ref[...] = jax.nn.sigmoid(logits_t)
    g = jnp.dot(hnb, sg_ref[...], preferred_element_type=F32)
    up = jnp.dot(hnb, su_ref[...], preferred_element_type=F32)
    act = (jax.nn.silu(g) * up).astype(BF16)
    h_ref[...] = h + jnp.dot(act, sd_ref[...], preferred_element_type=F32)


def _mix_out(x2, att, y, proj, d_skip, w_glu, b_glu, ssm_norm, w_out, norm_ffn, router_wt,
             sh_gate, sh_up, sh_down):
    seq = x2.shape[0]
    R = MIX_ROW_BLOCK
    res = lambda shape: _resident(shape, lambda i: (0, 0))
    return pl.pallas_call(
        _mix_out_body,
        grid=(seq // R,),
        in_specs=[
            pl.BlockSpec((R, D_MODEL), lambda i: (i, 0)),
            pl.BlockSpec((R, ATT_WIDTH), lambda i: (i, 0)),
            pl.BlockSpec((R, SSM_WIDTH), lambda i: (i, 0)),
            pl.BlockSpec((R, SSM_WIDTH), lambda i: (i, U_OFFSET // SSM_WIDTH)),
            res((1, SSM_WIDTH)), res((SSM_WIDTH, SSM_WIDTH)), res((1, SSM_WIDTH)), res((1, SSM_WIDTH)),
            res((D_MODEL, D_MODEL)), res((1, D_MODEL)), res((N_EXPERTS, D_MODEL)),
            res((D_MODEL, SHARED_HIDDEN)), res((D_MODEL, SHARED_HIDDEN)), res((SHARED_HIDDEN, D_MODEL)),
        ],
        out_specs=[
            pl.BlockSpec((R, D_MODEL), lambda i: (i, 0)),
            pl.BlockSpec((R, D_MODEL), lambda i: (i, 0)),
            pl.BlockSpec((N_EXPERTS, R), lambda i: (0, i)),
        ],
        out_shape=[
            jax.ShapeDtypeStruct((seq, D_MODEL), F32),
            jax.ShapeDtypeStruct((seq, D_MODEL), BF16),
            jax.ShapeDtypeStruct((N_EXPERTS, seq), F32),
        ],
        compiler_params=pltpu.CompilerParams(
            dimension_semantics=("arbitrary",), vmem_limit_bytes=VMEM_LIMIT),
        name="mix_out_shared",
    )(x2, att, y, proj, d_skip, w_glu, b_glu, ssm_norm, w_out, norm_ffn, router_wt,
      sh_gate, sh_up, sh_down)


def _route_body(sc_ref, rb_ref, tri_ref, w_ref, slot_ref, cnt_ref):
    scores = sc_ref[...]
    R = scores.shape[1]
    per_group = N_EXPERTS // N_EXPERT_GROUPS
    choice = scores + rb_ref[...]
    c3 = choice.reshape(N_EXPERT_GROUPS, per_group, R)
    within = lax.broadcasted_iota(jnp.int32, c3.shape, 1)
    m1 = jnp.max(c3, axis=1, keepdims=True)
    first = jnp.min(jnp.where(c3 == m1, within, per_group), axis=1, keepdims=True)
    m2 = jnp.max(jnp.where(within == first, -jnp.inf, c3), axis=1, keepdims=True)
    grp = (m1 + m2).reshape(N_EXPERT_GROUPS, R)
    gidx = lax.broadcasted_iota(jnp.int32, grp.shape, 0)
    grank = jnp.zeros(grp.shape, jnp.int32)
    for b in range(N_EXPERT_GROUPS):
        gb = grp[b:b + 1, :]
        grank += ((gb > grp) | ((gb == grp) & (b < gidx))).astype(jnp.int32)
    gmask = grank < TOPK_GROUPS
    emask = jnp.broadcast_to(gmask[:, None, :], c3.shape).reshape(N_EXPERTS, R)
    val = jnp.where(emask, choice, -jnp.inf)
    eidx = lax.broadcasted_iota(jnp.int32, val.shape, 0)
    rank = jnp.zeros(val.shape, jnp.int32)
    for e in range(N_EXPERTS):
        ve = val[e:e + 1, :]
        rank += ((ve > val) | ((ve == val) & (e < eidx))).astype(jnp.int32)
    sel = rank < TOP_K
    w = jnp.where(sel, scores, 0.0)
    w_ref[...] = w / jnp.sum(w, axis=0, keepdims=True) * ROUTED_SCALE
    cum = jnp.dot(jnp.where(sel, 1.0, 0.0).astype(BF16), tri_ref[...], preferred_element_type=F32)
    slot_ref[...] = jnp.where(sel, cum - 1.0, NOT_ROUTED)
    cnt_ref[0] = jnp.broadcast_to(cum[:, R - 1:R], (N_EXPERTS, 128))


def _route(scores_t, router_bias, tri):
    seq = scores_t.shape[1]
    R = MOE_TOKEN_BLOCK
    n_blk = seq // R
    blk = pl.BlockSpec((N_EXPERTS, R), lambda i: (0, i))
    return pl.pallas_call(
        _route_body,
        grid=(n_blk,),
        in_specs=[blk, _resident((N_EXPERTS, 1), lambda i: (0, 0)), _resident((R, R), lambda i: (0, 0))],
        out_specs=[blk, blk, pl.BlockSpec((1, N_EXPERTS, 128), lambda i: (i, 0, 0))],
        out_shape=[jax.ShapeDtypeStruct((N_EXPERTS, seq), F32),
                   jax.ShapeDtypeStruct((N_EXPERTS, seq), F32),
                   jax.ShapeDtypeStruct((n_blk, N_EXPERTS, 128), F32)],
        compiler_params=pltpu.CompilerParams(dimension_semantics=("arbitrary",)),
        name="route",
    )(scores_t, router_bias, tri)


def _moe_tables(cnt):
    n_blk = cnt.shape[0]
    U, RB, SB = MOE_UNIT, MOE_BLOCK_ROWS, MOE_SLOT_BLOCK
    NU = SB // U
    pc = (cnt + U - 1) // U * U
    off = jnp.cumsum(pc, axis=1) - pc
    upc = (pc // U).T.reshape(-1)
    seg_end = jnp.cumsum(upc)
    seg_start = seg_end - upc
    units_e = jnp.sum(pc // U, axis=0)
    ustart_e = jnp.cumsum(units_e) - units_e
    nblk_e = (units_e + NU - 1) // NU
    blk_end = jnp.cumsum(nblk_e)
    blk_start = blk_end - nblk_e
    n_act = blk_end[-1]
    max_blocks = n_blk * (RB // SB) + N_EXPERTS
    i = jnp.arange(max_blocks, dtype=jnp.int32)
    active = i < n_act
    last_e = jnp.minimum(jnp.searchsorted(blk_end, n_act - 1, side='right'), N_EXPERTS - 1)
    be = jnp.where(active, jnp.minimum(jnp.searchsorted(blk_end, i, side='right'), N_EXPERTS - 1), last_e)
    first = active & (i == blk_start[be])
    local = (i - blk_start[be])[:, None] * NU + jnp.arange(NU, dtype=jnp.int32)[None, :]
    valid = active[:, None] & (local < units_e[be][:, None])
    g = ustart_e[be][:, None] + local
    seg = jnp.minimum(jnp.searchsorted(seg_end, g, side='right'), N_EXPERTS * n_blk - 1)
    e_of = seg // n_blk
    b_of = seg % n_blk
    unit = (b_of * RB + off[b_of, e_of]) // U + (g - seg_start[seg])
    spare = n_blk * RB // U
    src = jnp.where(valid, unit, spare)
    dst = jnp.where(valid, unit, spare + (i % 2)[:, None] * NU + jnp.arange(NU, dtype=jnp.int32)[None, :])
    i32 = lambda a: a.astype(jnp.int32)
    return (off, pc, i32(src.reshape(-1)), i32(dst.reshape(-1)), i32(be), i32(first),
            i32(n_act.reshape(1)))


def _split_hi_lo(pos):
    hi = jnp.floor(pos * (1.0 / 64.0))
    return hi.astype(BF16), (pos - 64.0 * hi).astype(BF16)


def _dispatch_body(hn_ref, slot_ref, offc_ref, offl_ref, pcl_ref, xs_ref, *, n_blk):
    b = pl.program_id(0)
    RB, TB = MOE_BLOCK_ROWS, MOE_TOKEN_BLOCK

    @pl.when(b == n_blk)
    def _():
        xs_ref[...] = jnp.zeros(xs_ref.shape, BF16)

    @pl.when(b < n_blk)
    def _():
        pos = slot_ref[...] + offc_ref[0]
        hi, lo = _split_hi_lo(pos)
        r = lax.broadcasted_iota(jnp.int32, (RB, N_EXPERTS), 0).astype(F32)
        off = offl_ref[0]
        owner = jnp.where((r >= off) & (r < off + pcl_ref[0]), 1.0, 0.0).astype(BF16)
        p = (64.0 * jnp.dot(owner, hi, preferred_element_type=F32)
             + jnp.dot(owner, lo, preferred_element_type=F32))
        rr = lax.broadcasted_iota(jnp.int32, (RB, TB), 0).astype(F32)
        onehot = jnp.where(jnp.abs(p - rr) < 0.5, 1.0, 0.0).astype(BF16)
        x = hn_ref[...]
        C = MOE_SLOT_BLOCK
        for c in range(RB // C):
            xs_ref[0, c * C:(c + 1) * C, :] = jnp.dot(
                onehot[c * C:(c + 1) * C], x, preferred_element_type=F32).astype(BF16)


def _dispatch(hn, slots, off, pc):
    n_blk = off.shape[0]
    RB, TB = MOE_BLOCK_ROWS, MOE_TOKEN_BLOCK
    clamp = lambda b: jnp.minimum(b, n_blk - 1)
    return pl.pallas_call(
        functools.partial(_dispatch_body, n_blk=n_blk),
        grid=(n_blk + 1,),
        in_specs=[
            pl.BlockSpec((TB, D_MODEL), lambda b: (clamp(b), 0)),
            pl.BlockSpec((N_EXPERTS, TB), lambda b: (0, clamp(b))),
            pl.BlockSpec((1, N_EXPERTS, 1), lambda b: (clamp(b), 0, 0)),
            pl.BlockSpec((1, 1, N_EXPERTS), lambda b: (clamp(b), 0, 0)),
            pl.BlockSpec((1, 1, N_EXPERTS), lambda b: (clamp(b), 0, 0)),
        ],
        out_specs=pl.BlockSpec((1, RB, D_MODEL), lambda b: (b, 0, 0)),
        out_shape=jax.ShapeDtypeStruct((n_blk + 1, RB, D_MODEL), BF16),
        compiler_params=pltpu.CompilerParams(
            dimension_semantics=("arbitrary",), vmem_limit_bytes=VMEM_LIMIT),
        name="moe_dispatch",
    )(hn, slots, off[:, :, None], off[:, None, :], pc[:, None, :])


def _expert_body(src_ref, dst_ref, bexp_ref, first_ref, nact_ref,
                 xs_hbm, ys_init_hbm, wg_ref, wu_ref, wd_ref, ys_hbm,
                 xbuf, ybuf, wgb, wub, wdb, sem_in, sem_out):
    del bexp_ref, ys_init_hbm
    i = pl.program_id(0)
    n_act = nact_ref[0]
    cur = lax.rem(i, 2)
    U = MOE_UNIT
    NU = MOE_SLOT_BLOCK // U

    def in_copy(blk, buf, u):
        return pltpu.make_async_copy(xs_hbm.at[src_ref[blk * NU + u]],
                                     xbuf.at[buf, pl.ds(u * U, U)], sem_in.at[buf])

    def out_copy(blk, buf, u):
        return pltpu.make_async_copy(ybuf.at[buf, pl.ds(u * U, U)],
                                     ys_hbm.at[dst_ref[blk * NU + u]], sem_out.at[buf])

    @pl.when(i == 0)
    def _():
        for u in range(NU):
            in_copy(0, 0, u).start()

    @pl.when(i < n_act)
    def _():
        @pl.when(i + 1 < n_act)
        def _():
            for u in range(NU):
                in_copy(i + 1, 1 - cur, u).start()

        @pl.when(first_ref[i] == 1)
        def _():
            wgb[...] = wg_ref[0].astype(BF16)
            wub[...] = wu_ref[0].astype(BF16)
            wdb[...] = wd_ref[0].astype(BF16)

        for u in range(NU):
            in_copy(i, cur, u).wait()

        @pl.when(i >= 2)
        def _():
            for u in range(NU):
                out_copy(i - 2, cur, u).wait()

        x = xbuf[cur]
        g = jnp.dot(x, wgb[...], preferred_element_type=F32)
        up = jnp.dot(x, wub[...], preferred_element_type=F32)
        act = (jax.nn.silu(g) * up).astype(BF16)
        ybuf[cur] = jnp.dot(act, wdb[...], preferred_element_type=F32).astype(BF16)
        for u in range(NU):
            out_copy(i, cur, u).start()

        @pl.when(i == n_act - 1)
        def _():
            for u in range(NU):
                out_copy(i, cur, u).wait()

            @pl.when(i >= 1)
            def _():
                for u in range(NU):
                    out_copy(i - 1, 1 - cur, u).wait()


def _experts(xs, ys_init, src, dst, bexp, first, n_act, wg, wu, wd):
    U, SB = MOE_UNIT, MOE_SLOT_BLOCK
    n_units = xs.shape[0] * xs.shape[1] // U
    max_blocks = bexp.shape[0]
    unit_view = lambda a: a.reshape(n_units, U, D_MODEL)
    wspec = lambda shape: pl.BlockSpec((1,) + shape, lambda i, src, dst, bexp, first, nact: (bexp[i], 0, 0))
    grid_spec = pltpu.PrefetchScalarGridSpec(
        num_scalar_prefetch=5,
        grid=(max_blocks,),
        in_specs=[
            pl.BlockSpec(memory_space=pl.ANY),
            pl.BlockSpec(memory_space=pl.ANY),
            wspec((D_MODEL, EXPERT_HIDDEN)), wspec((D_MODEL, EXPERT_HIDDEN)), wspec((EXPERT_HIDDEN, D_MODEL)),
        ],
        out_specs=pl.BlockSpec(memory_space=pl.ANY),
        scratch_shapes=[
            pltpu.VMEM((2, SB, D_MODEL), BF16), pltpu.VMEM((2, SB, D_MODEL), BF16),
            pltpu.VMEM((D_MODEL, EXPERT_HIDDEN), BF16), pltpu.VMEM((D_MODEL, EXPERT_HIDDEN), BF16),
            pltpu.VMEM((EXPERT_HIDDEN, D_MODEL), BF16),
            pltpu.SemaphoreType.DMA((2,)), pltpu.SemaphoreType.DMA((2,)),
        ],
    )
    ys = pl.pallas_call(
        _expert_body,
        grid_spec=grid_spec,
        out_shape=jax.ShapeDtypeStruct((n_units, U, D_MODEL), BF16),
        input_output_aliases={6: 0},
        compiler_params=pltpu.CompilerParams(
            dimension_semantics=("arbitrary",), vmem_limit_bytes=VMEM_LIMIT),
        name="moe_experts",
    )(src, dst, bexp, first, n_act, unit_view(xs), unit_view(ys_init), wg, wu, wd)
    return ys.reshape(xs.shape)


def _combine_body(h_ref, ys_ref, slot_ref, w_ref, offl_ref, offc_ref, pcc_ref, o_ref):
    RB, TB = MOE_BLOCK_ROWS, MOE_TOKEN_BLOCK
    pos = slot_ref[...] + offl_ref[0]
    hi, lo = _split_hi_lo(pos)
    r = lax.broadcasted_iota(jnp.int32, (N_EXPERTS, RB), 1).astype(F32)
    off = offc_ref[0]
    owner = jnp.where((r >= off) & (r < off + pcc_ref[0]), 1.0, 0.0).astype(BF16)
    p = (64.0 * jnp.dot(hi, owner, preferred_element_type=F32)
         + jnp.dot(lo, owner, preferred_element_type=F32))
    wr = jnp.dot(w_ref[...].astype(BF16), owner, preferred_element_type=F32)
    rr = lax.broadcasted_iota(jnp.int32, (TB, RB), 1).astype(F32)
    gather_w = jnp.where(jnp.abs(p - rr) < 0.5, wr, 0.0).astype(BF16)
    o_ref[...] = h_ref[...] + jnp.dot(gather_w, ys_ref[0], preferred_element_type=F32)


def _combine(h, ys, slots_tok, w_tok, off, pc):
    n_blk = off.shape[0]
    RB, TB = MOE_BLOCK_ROWS, MOE_TOKEN_BLOCK
    seq = h.shape[0]
    return pl.pallas_call(
        _combine_body,
        grid=(n_blk,),
        in_specs=[
            pl.BlockSpec((TB, D_MODEL), lambda b: (b, 0)),
            pl.BlockSpec((1, RB, D_MODEL), lambda b: (b, 0, 0)),
            pl.BlockSpec((TB, N_EXPERTS), lambda b: (b, 0)),
            pl.BlockSpec((TB, N_EXPERTS), lambda b: (b, 0)),
            pl.BlockSpec((1, 1, N_EXPERTS), lambda b: (b, 0, 0)),
            pl.BlockSpec((1, N_EXPERTS, 1), lambda b: (b, 0, 0)),
            pl.BlockSpec((1, N_EXPERTS, 1), lambda b: (b, 0, 0)),
        ],
        out_specs=pl.BlockSpec((TB, D_MODEL), lambda b: (b, 0)),
        out_shape=jax.ShapeDtypeStruct((seq, D_MODEL), F32),
        compiler_params=pltpu.CompilerParams(
            dimension_semantics=("arbitrary",), vmem_limit_bytes=VMEM_LIMIT),
        name="moe_combine",
    )(h, ys, slots_tok, w_tok, off[:, None, :], off[:, :, None], pc[:, :, None])


def kernel(x, meta_tokens, rel_bias, norm_mix, w_in, q_norm, k_norm, lam_q1, lam_k1, lam_q2, lam_k2, subln, ssm_a_re, ssm_a_im, ssm_log_step, ssm_b_re, ssm_b_im, ssm_c_re, ssm_c_im, ssm_d, w_glu, b_glu, ssm_norm, w_out, norm_ffn, router_w, router_bias, w_gate, w_up, w_down, shared_gate, shared_up, shared_down):
    batch, seq, d = x.shape
    assert batch == 1 and d == D_MODEL and seq % ROW_BLOCK == 0 and seq % ATT_BLOCK == 0
    assert norm_mix.shape[0] == 1, "single layer"
    x2 = x.reshape(seq, d)
    meta_pad = jnp.zeros((ROW_BLOCK, d), F32).at[:N_META].set(meta_tokens.astype(F32))
    seg = jnp.kron(jnp.eye(QK_WIDTH // HEAD_DIM, dtype=F32),
                   jnp.full((HEAD_DIM, HEAD_DIM), 1.0 / HEAD_DIM, F32)).astype(BF16)
    qg = jnp.tile(q_norm[0].astype(F32), QK_WIDTH // HEAD_DIM)[None] * (HEAD_DIM ** -0.5)
    kg = jnp.tile(k_norm[0].astype(F32), QK_WIDTH // HEAD_DIM)[None]

    proj = _inproj(x2, meta_pad, norm_mix[0][None], w_in[0].astype(BF16), seg, qg, kg)

    score_bound = (BOUND_MARGIN * HEAD_DIM ** 0.5 * jnp.max(jnp.abs(q_norm[0].astype(F32)))
                   * jnp.max(jnp.abs(k_norm[0].astype(F32)))
                   + jnp.max(jnp.abs(rel_bias.astype(F32)))).reshape(1)
    att = _attention(proj, rel_bias.astype(F32), score_bound, lam_q1[0][None], lam_k1[0][None],
                     lam_q2[0][None], lam_k2[0][None], subln[0][None], seq)

    n_rows = proj.shape[0] // S5_CHUNK
    n_chunks = seq // S5_CHUNK
    u_chunks = (proj[:, U_OFFSET:]
                .reshape(n_rows, S5_CHUNK, SSM_GROUPS, SSM_GROUP)
                .transpose(2, 0, 1, 3).reshape(SSM_GROUPS, n_rows, S5_CHUNK * SSM_GROUP))
    pp, mm, qq, lam16 = _s5_prep(ssm_a_re[0].astype(F32), ssm_a_im[0].astype(F32),
                                 ssm_log_step[0].astype(F32), ssm_b_re[0].astype(F32),
                                 ssm_b_im[0].astype(F32), ssm_c_re[0].astype(F32),
                                 ssm_c_im[0].astype(F32))
    y_chunks = _s5(u_chunks, pp, mm, qq, lam16, n_chunks)
    y = (y_chunks.reshape(SSM_GROUPS, n_chunks, S5_CHUNK, SSM_GROUP)
         .transpose(1, 2, 0, 3).reshape(seq, SSM_WIDTH))

    h, hn, scores_t = _mix_out(
        x2, att, y, proj, ssm_d[0][None].astype(F32), w_glu[0].astype(BF16), b_glu[0][None].astype(F32),
        ssm_norm[0][None].astype(F32), w_out[0].astype(BF16), norm_ffn[0][None].astype(F32),
        router_w[0].astype(F32).T, shared_gate[0].astype(BF16), shared_up[0].astype(BF16),
        shared_down[0].astype(BF16))

    tb = MOE_TOKEN_BLOCK
    tri = (jnp.arange(tb)[:, None] <= jnp.arange(tb)[None, :]).astype(BF16)
    wts_t, slots_t, cnt = _route(scores_t, router_bias[0].astype(F32)[:, None], tri)
    off, pc, src, dst, bexp, first, n_act = _moe_tables(cnt[:, :, 0].astype(jnp.int32))
    off_f, pc_f = off.astype(F32), pc.astype(F32)
    xs = _dispatch(hn, slots_t, off_f, pc_f)
    ys = _experts(xs, jnp.zeros(xs.shape, BF16), src, dst, bexp, first, n_act,
                  w_gate[0], w_up[0], w_down[0])
    out = _combine(h, ys, slots_t.T, wts_t.T, off_f, pc_f)
    return out.reshape(batch, seq, d)
```

```python
import functools
import math

import jax
import jax.numpy as jnp
from jax import lax
from jax.experimental import pallas as pl
from jax.experimental.pallas import tpu as pltpu

F32 = jnp.float32
BF16 = jnp.bfloat16

D_MODEL = 2048
N_META = 16
ATT_WIDTH = 1024
SSM_WIDTH = 1024
HEAD_DIM = 64
V_DIM = 128
HEADS = 8
QK_WIDTH = 1024
IN_WIDTH = 4096
V_OFFSET = 2 * QK_WIDTH
U_OFFSET = V_OFFSET + 2 * ATT_WIDTH
PROJ_WIDTH = U_OFFSET + SSM_WIDTH
SSM_GROUP = 16
SSM_GROUPS = 64
SSM_STATE = 64
N_BUCKETS = 32
MAX_DISTANCE = 128
N_EXPERTS = 64
TOP_K = 8
N_EXPERT_GROUPS = 8
TOPK_GROUPS = 4
EXPERT_HIDDEN = 512
SHARED_HIDDEN = 512
ROUTED_SCALE = 2.5
EPS = 1e-6
LAMBDA_INIT = 0.8 - 0.6 * math.exp(-0.3 * 0)

ROW_BLOCK = 512
ATT_BLOCK = 512
S5_CHUNK = 16
S5_GROUPS_PER_STEP = 8
MOE_TOKEN_BLOCK = 256
MOE_UNIT = 16
MOE_SLOT_BLOCK = 256
MOE_BLOCK_ROWS = -(-(MOE_TOKEN_BLOCK * TOP_K + N_EXPERTS * (MOE_UNIT - 1)) // MOE_SLOT_BLOCK) * MOE_SLOT_BLOCK
NOT_ROUTED = -1e6
MIX_ROW_BLOCK = 256
NEG_BIG = -1e30
MAX_EXP_RANGE = 80.0
BOUND_MARGIN = 1.02
VMEM_LIMIT = 56 * 1024 * 1024


def _resident(shape, index_map):
    return pl.BlockSpec(shape, index_map, pipeline_mode=pl.Buffered(1))


def _inproj_body(x_ref, meta_ref, g_ref, w_ref, seg_ref, qg_ref, kg_ref, o_ref, *, n_xblk):
    i = pl.program_id(0)

    def run(src_ref):
        xv = src_ref[...]
        ms = jnp.mean(xv * xv, axis=-1, keepdims=True)
        hn = (xv * lax.rsqrt(ms + EPS) * g_ref[...]).astype(BF16)
        for s in range(4):
            ps = jnp.dot(hn, w_ref[:, s * 1024:(s + 1) * 1024], preferred_element_type=F32)
            if s < 2:
                gain = qg_ref if s == 0 else kg_ref
                msq = jnp.dot((ps * ps).astype(BF16), seg_ref[...], preferred_element_type=F32)
                ps = ps * lax.rsqrt(msq + EPS) * gain[...]
            pb = ps.astype(BF16)
            if s < 2:
                o_ref[:, s * 1024:(s + 1) * 1024] = pb
            elif s == 2:
                lane = lax.broadcasted_iota(jnp.int32, (pb.shape[0], V_DIM), 1)
                ones_col = jnp.where(lane == 0, 1.0, 0.0).astype(BF16)
                for hh in range(HEADS):
                    base = V_OFFSET + hh * 2 * V_DIM
                    o_ref[:, base:base + V_DIM] = pb[:, hh * V_DIM:(hh + 1) * V_DIM]
                    o_ref[:, base + V_DIM:base + 2 * V_DIM] = ones_col
            else:
                o_ref[:, U_OFFSET:U_OFFSET + SSM_WIDTH] = pb

    @pl.when(i < n_xblk)
    def _():
        run(x_ref)

    @pl.when(i == n_xblk)
    def _():
        run(meta_ref)


def _inproj(x2, meta_pad, gain, w_bf, seg, qg, kg):
    seq = x2.shape[0]
    n_xblk = seq // ROW_BLOCK
    rows = seq + ROW_BLOCK
    return pl.pallas_call(
        functools.partial(_inproj_body, n_xblk=n_xblk),
        grid=(n_xblk + 1,),
        in_specs=[
            pl.BlockSpec((ROW_BLOCK, D_MODEL), lambda i: (jnp.minimum(i, n_xblk - 1), 0)),
            _resident((ROW_BLOCK, D_MODEL), lambda i: (0, 0)),
            _resident((1, D_MODEL), lambda i: (0, 0)),
            _resident((D_MODEL, IN_WIDTH), lambda i: (0, 0)),
            _resident((QK_WIDTH, QK_WIDTH), lambda i: (0, 0)),
            _resident((1, QK_WIDTH), lambda i: (0, 0)),
            _resident((1, QK_WIDTH), lambda i: (0, 0)),
        ],
        out_specs=pl.BlockSpec((ROW_BLOCK, PROJ_WIDTH), lambda i: (i, 0)),
        out_shape=jax.ShapeDtypeStruct((rows, PROJ_WIDTH), BF16),
        compiler_params=pltpu.CompilerParams(
            dimension_semantics=("arbitrary",), vmem_limit_bytes=VMEM_LIMIT),
        name="inproj",
    )(x2, meta_pad, gain, w_bf, seg, qg, kg)


def _t5_bias(rel, tab_ref, h):
    half = N_BUCKETS // 2
    exact = half // 2
    n = jnp.abs(rel)
    nf = jnp.maximum(n, 1).astype(F32)
    large = exact + (jnp.log(nf / exact) / math.log(MAX_DISTANCE / exact) * (half - exact)).astype(jnp.int32)
    large = jnp.minimum(large, half - 1)
    bucket = jnp.where(rel > 0, half, 0) + jnp.where(n < exact, n, large)
    out = jnp.zeros(rel.shape, F32)
    for b in range(N_BUCKETS):
        out = jnp.where(bucket == b, tab_ref[b, h], out)
    return out


def _attn_body(tab_ref, bound_ref, q_ref, k_ref, v_ref, lq1_ref, lk1_ref, lq2_ref, lk2_ref,
               subln_ref, o_ref, bias_ref, acc1_ref, acc2_ref, m1_ref, m2_ref, *, n_main):
    T = ATT_BLOCK
    h = pl.program_id(0)
    qi = pl.program_id(1)
    bound = bound_ref[0]

    @pl.when(qi == 0)
    def _():
        offsets = (-T, 0, T, -N_META, -N_META - T, -2 * T, 2 * T)
        for kind, off in enumerate(offsets):
            masked = kind in (3, 4)

            def rows(rc, carry, off=off, masked=masked, kind=kind):
                r0 = pl.multiple_of(rc * 8, 8)
                r = r0 + lax.broadcasted_iota(jnp.int32, (8, T), 0)
                c = lax.broadcasted_iota(jnp.int32, (8, T), 1)
                b = _t5_bias(off + c - r, tab_ref, h) - bound
                if masked:
                    b = jnp.where(c < N_META, b, NEG_BIG)
                bias_ref[kind, pl.ds(r0, 8), :] = b
                return carry

            lax.fori_loop(0, T // 8, rows, 0)

    acc1_ref[...] = jnp.zeros(acc1_ref.shape, F32)
    acc2_ref[...] = jnp.zeros(acc2_ref.shape, F32)

    q = q_ref[...]
    q1 = q[:, :HEAD_DIM]
    q2 = q[:, HEAD_DIM:]
    nt = (((1,), (1,)), ((), ()))

    def tile(ki):
        koff = pl.multiple_of(ki * T, T)
        kb = k_ref[pl.ds(koff, T), :]
        va = v_ref[pl.ds(koff, T), :]
        d = ki - qi
        kind = jnp.where(ki == n_main, jnp.where(qi == 0, 3, 4),
                         jnp.where(d <= -2, 5, jnp.where(d >= 2, 6, d + 1)))
        return kb, va, bias_ref[kind]

    def bounded_step(ki, carry):
        kb, va, bias = tile(ki)
        s1 = lax.dot_general(q1, kb[:, :HEAD_DIM], nt, preferred_element_type=F32) + bias
        acc1_ref[...] += jnp.dot(jnp.exp(s1).astype(BF16), va, preferred_element_type=F32)
        s2 = lax.dot_general(q2, kb[:, HEAD_DIM:], nt, preferred_element_type=F32) + bias
        acc2_ref[...] += jnp.dot(jnp.exp(s2).astype(BF16), va, preferred_element_type=F32)
        return carry

    def online_map(s, va, m_ref, acc_ref):
        m_old = m_ref[...]
        m_new = jnp.maximum(m_old, jnp.max(s, axis=-1, keepdims=True))
        p = jnp.exp(s - m_new).astype(BF16)
        acc_ref[...] = (jnp.exp(m_old - m_new) * acc_ref[...]
                        + jnp.dot(p, va, preferred_element_type=F32))
        m_ref[...] = m_new

    def online_step(ki, carry):
        kb, va, bias = tile(ki)
        s1 = lax.dot_general(q1, kb[:, :HEAD_DIM], nt, preferred_element_type=F32) + bias
        online_map(s1, va, m1_ref, acc1_ref)
        s2 = lax.dot_general(q2, kb[:, HEAD_DIM:], nt, preferred_element_type=F32) + bias
        online_map(s2, va, m2_ref, acc2_ref)
        return carry

    no_running_max = 2.0 * bound <= MAX_EXP_RANGE

    @pl.when(no_running_max)
    def _():
        unroll = next(u for u in (3, 2, 1) if (n_main + 1) % u == 0)
        lax.fori_loop(0, n_main + 1, bounded_step, 0, unroll=unroll)

    @pl.when(jnp.logical_not(no_running_max))
    def _():
        m1_ref[...] = jnp.full(m1_ref.shape, -jnp.inf, F32)
        m2_ref[...] = jnp.full(m2_ref.shape, -jnp.inf, F32)
        lax.fori_loop(0, n_main + 1, online_step, 0)

    lam = (jnp.exp(jnp.sum(lq1_ref[...] * lk1_ref[...], axis=-1, keepdims=True))
           - jnp.exp(jnp.sum(lq2_ref[...] * lk2_ref[...], axis=-1, keepdims=True))
           + LAMBDA_INIT)
    a1 = acc1_ref[...]
    a2 = acc2_ref[...]
    o = (a1[:, :V_DIM] / a1[:, V_DIM:V_DIM + 1]
         - lam * (a2[:, :V_DIM] / a2[:, V_DIM:V_DIM + 1]))
    ms = jnp.mean(o * o, axis=-1, keepdims=True)
    o = o * lax.rsqrt(ms + EPS) * subln_ref[...] * (1.0 - LAMBDA_INIT)
    o_ref[...] = o.astype(BF16)


def _attention(proj, rel_bias, score_bound, lq1, lk1, lq2, lk2, subln, seq):
    T = ATT_BLOCK
    n_main = seq // T
    rows = proj.shape[0]
    vec64 = lambda: _resident((1, HEAD_DIM), lambda h, qi: (0, 0))
    return pl.pallas_call(
        functools.partial(_attn_body, n_main=n_main),
        grid=(HEADS, n_main),
        in_specs=[
            pl.BlockSpec(memory_space=pltpu.SMEM),
            pl.BlockSpec(memory_space=pltpu.SMEM),
            pl.BlockSpec((T, 2 * HEAD_DIM), lambda h, qi: (qi, h)),
            pl.BlockSpec((rows, 2 * HEAD_DIM), lambda h, qi: (0, HEADS + h)),
            pl.BlockSpec((rows, 2 * V_DIM), lambda h, qi: (0, V_OFFSET // (2 * V_DIM) + h)),
            vec64(), vec64(), vec64(), vec64(),
            _resident((1, V_DIM), lambda h, qi: (0, 0)),
        ],
        out_specs=pl.BlockSpec((T, V_DIM), lambda h, qi: (qi, h)),
        out_shape=jax.ShapeDtypeStruct((seq, ATT_WIDTH), BF16),
        scratch_shapes=[
            pltpu.VMEM((7, T, T), F32),
            pltpu.VMEM((T, 2 * V_DIM), F32), pltpu.VMEM((T, 2 * V_DIM), F32),
            pltpu.VMEM((T, 1), F32), pltpu.VMEM((T, 1), F32),
        ],
        compiler_params=pltpu.CompilerParams(
            dimension_semantics=("arbitrary", "arbitrary"), vmem_limit_bytes=VMEM_LIMIT),
        name="diff_attention",
    )(rel_bias, score_bound, proj, proj, proj, lq1, lk1, lq2, lk2, subln)


def _s5_prep(a_re, a_im, log_step, b_re, b_im, c_re, c_im):
    hp = lax.Precision.HIGHEST
    C = S5_CHUNK
    dt = jnp.exp(log_step)[..., None]
    decay = jnp.exp(a_re * dt)
    ab_re = decay * jnp.cos(a_im * dt)
    ab_im = decay * jnp.sin(a_im * dt)
    den = a_re * a_re + a_im * a_im
    zr = ab_re - 1.0
    f_re = (zr * a_re + ab_im * a_im) / den
    f_im = (ab_im * a_re - zr * a_im) / den
    bb_re = f_re[..., None] * b_re - f_im[..., None] * b_im
    bb_im = f_re[..., None] * b_im + f_im[..., None] * b_re
    pr, pi = jnp.ones_like(ab_re), jnp.zeros_like(ab_re)
    pw_re, pw_im = [pr], [pi]
    for _ in range(C):
        pr, pi = pr * ab_re - pi * ab_im, pr * ab_im + pi * ab_re
        pw_re.append(pr)
        pw_im.append(pi)
    pw_re = jnp.stack(pw_re)
    pw_im = jnp.stack(pw_im)
    cp_re = c_re[None] * pw_re[:, :, :, None, :] - c_im[None] * pw_im[:, :, :, None, :]
    cp_im = c_re[None] * pw_im[:, :, :, None, :] + c_im[None] * pw_re[:, :, :, None, :]
    kern = (jnp.einsum('tdgpn,dgnq->tdgpq', cp_re[:C], bb_re, precision=hp)
            - jnp.einsum('tdgpn,dgnq->tdgpq', cp_im[:C], bb_im, precision=hp))
    jj = jnp.arange(C)[:, None]
    ii = jnp.arange(C)[None, :]
    kf = jnp.where((ii >= jj)[:, :, None, None, None], kern[:, 0][jnp.clip(ii - jj, 0, C - 1)], 0.0)
    kr = jnp.where((jj >= ii)[:, :, None, None, None], kern[:, 1][jnp.clip(jj - ii, 0, C - 1)], 0.0)
    mm = (kf + kr).transpose(2, 0, 4, 1, 3).reshape(SSM_GROUPS, C * SSM_GROUP, C * SSM_GROUP)

    def in_mat(pwr, pwi, d):
        re = pwr[:, :, :, None] * bb_re[d][None] - pwi[:, :, :, None] * bb_im[d][None]
        im = pwr[:, :, :, None] * bb_im[d][None] + pwi[:, :, :, None] * bb_re[d][None]
        return re.transpose(1, 0, 3, 2), im.transpose(1, 0, 3, 2)

    pf_re, pf_im = in_mat(pw_re[:C, 0][::-1], pw_im[:C, 0][::-1], 0)
    pr_re, pr_im = in_mat(pw_re[:C, 1], pw_im[:C, 1], 1)
    pp = jnp.concatenate([pf_re, pr_re, pf_im, pr_im], axis=-1).reshape(SSM_GROUPS, C * SSM_GROUP, 4 * SSM_STATE)

    def out_mat(cr, ci):
        return cr.transpose(1, 3, 0, 2), -ci.transpose(1, 3, 0, 2)

    qf_re, qf_im = out_mat(cp_re[1:, 0], cp_im[1:, 0])
    qr_re, qr_im = out_mat(cp_re[1:, 1][::-1], cp_im[1:, 1][::-1])
    qq = jnp.concatenate([qf_re, qr_re, qf_im, qr_im], axis=1).reshape(SSM_GROUPS, 4 * SSM_STATE, C * SSM_GROUP)
    lam16 = jnp.stack([jnp.concatenate([pw_re[C, 0], pw_re[C, 1]], axis=-1),
                       jnp.concatenate([pw_im[C, 0], pw_im[C, 1]], axis=-1)], axis=1)
    return pp.astype(BF16), mm.astype(BF16), qq.astype(BF16), lam16


def _s5_body(u_ref, pp_ref, mm_ref, qq_ref, lam_ref, y_ref,
             zre_ref, zim_ref, are_ref, aim_ref, bre_ref, bim_ref, *, n_chunks):
    GS = S5_GROUPS_PER_STEP
    NS = 2 * SSM_STATE
    for gi in range(GS):
        z = jnp.dot(u_ref[gi], pp_ref[gi], preferred_element_type=F32)
        zre_ref[:, gi, :] = z[:, :NS]
        zim_ref[:, gi, :] = z[:, NS:]

    ar = lam_ref[:, 0, :]
    ai = lam_ref[:, 1, :]
    fwd = lax.broadcasted_iota(jnp.int32, (GS, NS), 1) < SSM_STATE
    sre0 = jnp.where(fwd, zre_ref[n_chunks], 0.0)
    sim0 = jnp.where(fwd, zim_ref[n_chunks], 0.0)

    def scan_step(k, carry):
        sre, sim = carry
        kr = n_chunks - 1 - k
        are_ref[k] = sre
        aim_ref[k] = sim
        bre_ref[kr] = sre
        bim_ref[kr] = sim
        zr = jnp.where(fwd, zre_ref[k], zre_ref[kr])
        zi = jnp.where(fwd, zim_ref[k], zim_ref[kr])
        return ar * sre - ai * sim + zr, ar * sim + ai * sre + zi

    lax.fori_loop(0, n_chunks, scan_step, (sre0, sim0))

    fwd_rows = lax.broadcasted_iota(jnp.int32, (n_chunks, NS), 1) < SSM_STATE
    for gi in range(GS):
        s_re = jnp.where(fwd_rows, are_ref[:, gi, :], bre_ref[:, gi, :])
        s_im = jnp.where(fwd_rows, aim_ref[:, gi, :], bim_ref[:, gi, :])
        scat = jnp.concatenate([s_re, s_im], axis=1).astype(BF16)
        y = (jnp.dot(u_ref[gi, :n_chunks, :], mm_ref[gi], preferred_element_type=F32)
             + jnp.dot(scat, qq_ref[gi], preferred_element_type=F32))
        y_ref[gi] = y


def _s5(u_chunks, pp, mm, qq, lam16, n_chunks):
    GS = S5_GROUPS_PER_STEP
    n_rows = u_chunks.shape[1]
    W = S5_CHUNK * SSM_GROUP
    NS = 2 * SSM_STATE
    mat = lambda: pl.BlockSpec((GS, W, W), lambda g: (g, 0, 0))
    return pl.pallas_call(
        functools.partial(_s5_body, n_chunks=n_chunks),
        grid=(SSM_GROUPS // GS,),
        in_specs=[
            pl.BlockSpec((GS, n_rows, W), lambda g: (g, 0, 0)),
            mat(), mat(), mat(),
            pl.BlockSpec((GS, 2, NS), lambda g: (g, 0, 0)),
        ],
        out_specs=pl.BlockSpec((GS, n_chunks, W), lambda g: (g, 0, 0)),
        out_shape=jax.ShapeDtypeStruct((SSM_GROUPS, n_chunks, W), F32),
        scratch_shapes=[pltpu.VMEM((n_rows, GS, NS), F32), pltpu.VMEM((n_rows, GS, NS), F32)]
        + [pltpu.VMEM((n_chunks, GS, NS), F32) for _ in range(4)],
        compiler_params=pltpu.CompilerParams(
            dimension_semantics=("arbitrary",), vmem_limit_bytes=VMEM_LIMIT),
        name="s5_chunked",
    )(u_chunks, pp, mm, qq, lam16)


def _mix_out_body(x_ref, att_ref, y_ref, u_ref, d_ref, wglu_ref, bglu_ref, sn_ref, wout_ref,
                  nf_ref, rwt_ref, sg_ref, su_ref, sd_ref, h_ref, hn_ref, sc_ref):
    y = y_ref[...] + d_ref[...] * u_ref[...].astype(F32)
    y = jax.nn.gelu(y)
    gate = jax.nn.sigmoid(jnp.dot(y.astype(BF16), wglu_ref[...], preferred_element_type=F32) + bglu_ref[...])
    s = y * gate
    ms = jnp.mean(s * s, axis=-1, keepdims=True)
    ssm = (s * lax.rsqrt(ms + EPS) * sn_ref[...]).astype(BF16)
    h = (x_ref[...]
         + jnp.dot(att_ref[...], wout_ref[:ATT_WIDTH, :], preferred_element_type=F32)
         + jnp.dot(ssm, wout_ref[ATT_WIDTH:, :], preferred_element_type=F32))
    ms = jnp.mean(h * h, axis=-1, keepdims=True)
    hn = h * lax.rsqrt(ms + EPS) * nf_ref[...]
    hnb = hn.astype(BF16)
    hn_ref[...] = hnb
    logits_t = lax.dot_general(rwt_ref[...], hn, (((1,), (1,)), ((), ())),
                               precision=lax.Precision.HIGHEST, preferred_element_type=F32)
    sc_ref[...] = jax.nn.sigmoid(logits_t)
    g = jnp.dot(hnb, sg_ref[...], preferred_element_type=F32)
    up = jnp.dot(hnb, su_ref[...], preferred_element_type=F32)
    act = (jax.nn.silu(g) * up).astype(BF16)
    h_ref[...] = h + jnp.dot(act, sd_ref[...], preferred_element_type=F32)


def _mix_out(x2, att, y, proj, d_skip, w_glu, b_glu, ssm_norm, w_out, norm_ffn, router_wt,
             sh_gate, sh_up, sh_down):
    seq = x2.shape[0]
    R = MIX_ROW_BLOCK
    res = lambda shape: _resident(shape, lambda i: (0, 0))
    return pl.pallas_call(
        _mix_out_body,
        grid=(seq // R,),
        in_specs=[
            pl.BlockSpec((R, D_MODEL), lambda i: (i, 0)),
            pl.BlockSpec((R, ATT_WIDTH), lambda i: (i, 0)),
            pl.BlockSpec((R, SSM_WIDTH), lambda i: (i, 0)),
            pl.BlockSpec((R, SSM_WIDTH), lambda i: (i, U_OFFSET // SSM_WIDTH)),
            res((1, SSM_WIDTH)), res((SSM_WIDTH, SSM_WIDTH)), res((1, SSM_WIDTH)), res((1, SSM_WIDTH)),
            res((D_MODEL, D_MODEL)), res((1, D_MODEL)), res((N_EXPERTS, D_MODEL)),
            res((D_MODEL, SHARED_HIDDEN)), res((D_MODEL, SHARED_HIDDEN)), res((SHARED_HIDDEN, D_MODEL)),
        ],
        out_specs=[
            pl.BlockSpec((R, D_MODEL), lambda i: (i, 0)),
            pl.BlockSpec((R, D_MODEL), lambda i: (i, 0)),
            pl.BlockSpec((N_EXPERTS, R), lambda i: (0, i)),
        ],
        out_shape=[
            jax.ShapeDtypeStruct((seq, D_MODEL), F32),
            jax.ShapeDtypeStruct((seq, D_MODEL), BF16),
            jax.ShapeDtypeStruct((N_EXPERTS, seq), F32),
        ],
        compiler_params=pltpu.CompilerParams(
            dimension_semantics=("arbitrary",), vmem_limit_bytes=VMEM_LIMIT),
        name="mix_out_shared",
    )(x2, att, y, proj, d_skip, w_glu, b_glu, ssm_norm, w_out, norm_ffn, router_wt,
      sh_gate, sh_up, sh_down)


def _route_body(sc_ref, rb_ref, tri_ref, w_ref, slot_ref, cnt_ref):
    scores = sc_ref[...]
    R = scores.shape[1]
    per_group = N_EXPERTS // N_EXPERT_GROUPS
    choice = scores + rb_ref[...]
    c3 = choice.reshape(N_EXPERT_GROUPS, per_group, R)
    within = lax.broadcasted_iota(jnp.int32, c3.shape, 1)
    m1 = jnp.max(c3, axis=1, keepdims=True)
    first = jnp.min(jnp.where(c3 == m1, within, per_group), axis=1, keepdims=True)
    m2 = jnp.max(jnp.where(within == first, -jnp.inf, c3), axis=1, keepdims=True)
    grp = (m1 + m2).reshape(N_EXPERT_GROUPS, R)
    gidx = lax.broadcasted_iota(jnp.int32, grp.shape, 0)
    grank = jnp.zeros(grp.shape, jnp.int32)
    for b in range(N_EXPERT_GROUPS):
        gb = grp[b:b + 1, :]
        grank += ((gb > grp) | ((gb == grp) & (b < gidx))).astype(jnp.int32)
    gmask = grank < TOPK_GROUPS
    emask = jnp.broadcast_to(gmask[:, None, :], c3.shape).reshape(N_EXPERTS, R)
    val = jnp.where(emask, choice, -jnp.inf)
    eidx = lax.broadcasted_iota(jnp.int32, val.shape, 0)
    rank = jnp.zeros(val.shape, jnp.int32)
    for e in range(N_EXPERTS):
        ve = val[e:e + 1, :]
        rank += ((ve > val) | ((ve == val) & (e < eidx))).astype(jnp.int32)
    sel = rank < TOP_K
    w = jnp.where(sel, scores, 0.0)
    w_ref[...] = w / jnp.sum(w, axis=0, keepdims=True) * ROUTED_SCALE
    cum = jnp.dot(jnp.where(sel, 1.0, 0.0).astype(BF16), tri_ref[...], preferred_element_type=F32)
    slot_ref[...] = jnp.where(sel, cum - 1.0, NOT_ROUTED)
    cnt_ref[0] = jnp.broadcast_to(cum[:, R - 1:R], (N_EXPERTS, 128))


def _route(scores_t, router_bias, tri):
    seq = scores_t.shape[1]
    R = MOE_TOKEN_BLOCK
    n_blk = seq // R
    blk = pl.BlockSpec((N_EXPERTS, R), lambda i: (0, i))
    return pl.pallas_call(
        _route_body,
        grid=(n_blk,),
        in_specs=[blk, _resident((N_EXPERTS, 1), lambda i: (0, 0)), _resident((R, R), lambda i: (0, 0))],
        out_specs=[blk, blk, pl.BlockSpec((1, N_EXPERTS, 128), lambda i: (i, 0, 0))],
        out_shape=[jax.ShapeDtypeStruct((N_EXPERTS, seq), F32),
                   jax.ShapeDtypeStruct((N_EXPERTS, seq), F32),
                   jax.ShapeDtypeStruct((n_blk, N_EXPERTS, 128), F32)],
        compiler_params=pltpu.CompilerParams(dimension_semantics=("arbitrary",)),
        name="route",
    )(scores_t, router_bias, tri)


def _moe_tables(cnt):
    n_blk = cnt.shape[0]
    U, RB, SB = MOE_UNIT, MOE_BLOCK_ROWS, MOE_SLOT_BLOCK
    NU = SB // U
    E = N_EXPERTS
    pc = (cnt + U - 1) // U * U
    off = jnp.cumsum(pc, axis=1) - pc
    upc_t = (pc // U).T
    cum_t = jnp.cumsum(upc_t, axis=1)
    units_e = cum_t[:, -1]
    nblk_e = (units_e + NU - 1) // NU
    blk_end = jnp.cumsum(nblk_e)
    blk_start = blk_end - nblk_e
    n_act = blk_end[-1]
    max_blocks = n_blk * (RB // SB) + E
    i = jnp.arange(max_blocks, dtype=jnp.int32)
    active = i < n_act
    count_le = lambda edges, v: jnp.sum((edges <= v[..., None]).astype(jnp.int32), axis=-1)
    last_e = jnp.minimum(count_le(blk_end, n_act - 1), E - 1)
    be = jnp.where(active, jnp.minimum(count_le(blk_end[None, :], i), E - 1), last_e)
    oh_e = be[:, None] == jnp.arange(E, dtype=jnp.int32)[None, :]
    pick_e = lambda v: jnp.sum(jnp.where(oh_e, v[None, :], 0), axis=1)
    pick_e2 = lambda m: jnp.sum(jnp.where(oh_e[:, :, None], m[None, :, :], 0), axis=1)
    bstart_i = pick_e(blk_start)
    first = active & (i == bstart_i)
    local = (i - bstart_i)[:, None] * NU + jnp.arange(NU, dtype=jnp.int32)[None, :]
    valid = active[:, None] & (local < pick_e(units_e)[:, None])
    cum_i = pick_e2(cum_t)
    b_of = jnp.minimum(count_le(cum_i[:, None, :], local), n_blk - 1)
    oh_b = b_of[:, :, None] == jnp.arange(n_blk, dtype=jnp.int32)[None, None, :]
    pick_b = lambda m: jnp.sum(jnp.where(oh_b, m[:, None, :], 0), axis=2)
    seg_start = pick_b(cum_i) - pick_b(pick_e2(upc_t))
    unit = (b_of * RB + pick_b(pick_e2(off.T))) // U + (local - seg_start)
    spare = n_blk * RB // U
    src = jnp.where(valid, unit, spare)
    dst = jnp.where(valid, unit, spare + (i % 2)[:, None] * NU + jnp.arange(NU, dtype=jnp.int32)[None, :])
    i32 = lambda a: a.astype(jnp.int32)
    return (off, pc, i32(src.reshape(-1)), i32(dst.reshape(-1)), i32(be), i32(first),
            i32(n_act.reshape(1)))


def _split_hi_lo(pos):
    hi = jnp.floor(pos * (1.0 / 64.0))
    return hi.astype(BF16), (pos - 64.0 * hi).astype(BF16)


def _dispatch_body(hn_ref, slot_ref, offc_ref, offl_ref, pcl_ref, xs_ref, *, n_blk):
    b = pl.program_id(0)
    RB, TB = MOE_BLOCK_ROWS, MOE_TOKEN_BLOCK

    @pl.when(b == n_blk)
    def _():
        xs_ref[...] = jnp.zeros(xs_ref.shape, BF16)

    @pl.when(b < n_blk)
    def _():
        pos = slot_ref[...] + offc_ref[0]
        hi, lo = _split_hi_lo(pos)
        r = lax.broadcasted_iota(jnp.int32, (RB, N_EXPERTS), 0).astype(F32)
        off = offl_ref[0]
        owner = jnp.where((r >= off) & (r < off + pcl_ref[0]), 1.0, 0.0).astype(BF16)
        p = (64.0 * jnp.dot(owner, hi, preferred_element_type=F32)
             + jnp.dot(owner, lo, preferred_element_type=F32))
        rr = lax.broadcasted_iota(jnp.int32, (RB, TB), 0).astype(F32)
        onehot = jnp.where(jnp.abs(p - rr) < 0.5, 1.0, 0.0).astype(BF16)
        x = hn_ref[...]
        C = MOE_SLOT_BLOCK
        for c in range(RB // C):
            xs_ref[0, c * C:(c + 1) * C, :] = jnp.dot(
                onehot[c * C:(c + 1) * C], x, preferred_element_type=F32).astype(BF16)


def _dispatch(hn, slots, off, pc):
    n_blk = off.shape[0]
    RB, TB = MOE_BLOCK_ROWS, MOE_TOKEN_BLOCK
    clamp = lambda b: jnp.minimum(b, n_blk - 1)
    return pl.pallas_call(
        functools.partial(_dispatch_body, n_blk=n_blk),
        grid=(n_blk + 1,),
        in_specs=[
            pl.BlockSpec((TB, D_MODEL), lambda b: (clamp(b), 0)),
            pl.BlockSpec((N_EXPERTS, TB), lambda b: (0, clamp(b))),
            pl.BlockSpec((1, N_EXPERTS, 1), lambda b: (clamp(b), 0, 0)),
            pl.BlockSpec((1, 1, N_EXPERTS), lambda b: (clamp(b), 0, 0)),
            pl.BlockSpec((1, 1, N_EXPERTS), lambda b: (clamp(b), 0, 0)),
        ],
        out_specs=pl.BlockSpec((1, RB, D_MODEL), lambda b: (b, 0, 0)),
        out_shape=jax.ShapeDtypeStruct((n_blk + 1, RB, D_MODEL), BF16),
        compiler_params=pltpu.CompilerParams(
            dimension_semantics=("arbitrary",), vmem_limit_bytes=VMEM_LIMIT),
        name="moe_dispatch",
    )(hn, slots, off[:, :, None], off[:, None, :], pc[:, None, :])


def _expert_body(src_ref, dst_ref, bexp_ref, first_ref, nact_ref,
                 xs_hbm, ys_init_hbm, wg_ref, wu_ref, wd_ref, ys_hbm,
                 xbuf, ybuf, wgb, wub, wdb, sem_in, sem_out):
    del bexp_ref, ys_init_hbm
    i = pl.program_id(0)
    n_act = nact_ref[0]
    cur = lax.rem(i, 2)
    U = MOE_UNIT
    NU = MOE_SLOT_BLOCK // U

    def in_copy(blk, buf, u):
        return pltpu.make_async_copy(xs_hbm.at[src_ref[blk * NU + u]],
                                     xbuf.at[buf, pl.ds(u * U, U)], sem_in.at[buf])

    def out_copy(blk, buf, u):
        return pltpu.make_async_copy(ybuf.at[buf, pl.ds(u * U, U)],
                                     ys_hbm.at[dst_ref[blk * NU + u]], sem_out.at[buf])

    @pl.when(i == 0)
    def _():
        for u in range(NU):
            in_copy(0, 0, u).start()

    @pl.when(i < n_act)
    def _():
        @pl.when(i + 1 < n_act)
        def _():
            for u in range(NU):
                in_copy(i + 1, 1 - cur, u).start()

        @pl.when(first_ref[i] == 1)
        def _():
            wgb[...] = wg_ref[0].astype(BF16)
            wub[...] = wu_ref[0].astype(BF16)
            wdb[...] = wd_ref[0].astype(BF16)

        for u in range(NU):
            in_copy(i, cur, u).wait()

        @pl.when(i >= 2)
        def _():
            for u in range(NU):
                out_copy(i - 2, cur, u).wait()

        x = xbuf[cur]
        g = jnp.dot(x, wgb[...], preferred_element_type=F32)
        up = jnp.dot(x, wub[...], preferred_element_type=F32)
        act = (jax.nn.silu(g) * up).astype(BF16)
        ybuf[cur] = jnp.dot(act, wdb[...], preferred_element_type=F32).astype(BF16)
        for u in range(NU):
            out_copy(i, cur, u).start()

        @pl.when(i == n_act - 1)
        def _():
            for u in range(NU):
                out_copy(i, cur, u).wait()

            @pl.when(i >= 1)
            def _():
                for u in range(NU):
                    out_copy(i - 1, 1 - cur, u).wait()


def _experts(xs, ys_init, src, dst, bexp, first, n_act, wg, wu, wd):
    U, SB = MOE_UNIT, MOE_SLOT_BLOCK
    n_units = xs.shape[0] * xs.shape[1] // U
    max_blocks = bexp.shape[0]
    unit_view = lambda a: a.reshape(n_units, U, D_MODEL)
    wspec = lambda shape: pl.BlockSpec((1,) + shape, lambda i, src, dst, bexp, first, nact: (bexp[i], 0, 0))
    grid_spec = pltpu.PrefetchScalarGridSpec(
        num_scalar_prefetch=5,
        grid=(max_blocks,),
        in_specs=[
            pl.BlockSpec(memory_space=pl.ANY),
            pl.BlockSpec(memory_space=pl.ANY),
            wspec((D_MODEL, EXPERT_HIDDEN)), wspec((D_MODEL, EXPERT_HIDDEN)), wspec((EXPERT_HIDDEN, D_MODEL)),
        ],
        out_specs=pl.BlockSpec(memory_space=pl.ANY),
        scratch_shapes=[
            pltpu.VMEM((2, SB, D_MODEL), BF16), pltpu.VMEM((2, SB, D_MODEL), BF16),
            pltpu.VMEM((D_MODEL, EXPERT_HIDDEN), BF16), pltpu.VMEM((D_MODEL, EXPERT_HIDDEN), BF16),
            pltpu.VMEM((EXPERT_HIDDEN, D_MODEL), BF16),
            pltpu.SemaphoreType.DMA((2,)), pltpu.SemaphoreType.DMA((2,)),
        ],
    )
    ys = pl.pallas_call(
        _expert_body,
        grid_spec=grid_spec,
        out_shape=jax.ShapeDtypeStruct((n_units, U, D_MODEL), BF16),
        input_output_aliases={6: 0},
        compiler_params=pltpu.CompilerParams(
            dimension_semantics=("arbitrary",), vmem_limit_bytes=VMEM_LIMIT),
        name="moe_experts",
    )(src, dst, bexp, first, n_act, unit_view(xs), unit_view(ys_init), wg, wu, wd)
    return ys.reshape(xs.shape)


def _combine_body(h_ref, ys_ref, slot_ref, w_ref, offl_ref, offc_ref, pcc_ref, o_ref):
    RB, TB = MOE_BLOCK_ROWS, MOE_TOKEN_BLOCK
    pos = slot_ref[...] + offl_ref[0]
    hi, lo = _split_hi_lo(pos)
    r = lax.broadcasted_iota(jnp.int32, (N_EXPERTS, RB), 1).astype(F32)
    off = offc_ref[0]
    owner = jnp.where((r >= off) & (r < off + pcc_ref[0]), 1.0, 0.0).astype(BF16)
    p = (64.0 * jnp.dot(hi, owner, preferred_element_type=F32)
         + jnp.dot(lo, owner, preferred_element_type=F32))
    wr = jnp.dot(w_ref[...].astype(BF16), owner, preferred_element_type=F32)
    rr = lax.broadcasted_iota(jnp.int32, (TB, RB), 1).astype(F32)
    gather_w = jnp.where(jnp.abs(p - rr) < 0.5, wr, 0.0).astype(BF16)
    o_ref[...] = h_ref[...] + jnp.dot(gather_w, ys_ref[0], preferred_element_type=F32)


def _combine(h, ys, slots_tok, w_tok, off, pc):
    n_blk = off.shape[0]
    RB, TB = MOE_BLOCK_ROWS, MOE_TOKEN_BLOCK
    seq = h.shape[0]
    return pl.pallas_call(
        _combine_body,
        grid=(n_blk,),
        in_specs=[
            pl.BlockSpec((TB, D_MODEL), lambda b: (b, 0)),
            pl.BlockSpec((1, RB, D_MODEL), lambda b: (b, 0, 0)),
            pl.BlockSpec((TB, N_EXPERTS), lambda b: (b, 0)),
            pl.BlockSpec((TB, N_EXPERTS), lambda b: (b, 0)),
            pl.BlockSpec((1, 1, N_EXPERTS), lambda b: (b, 0, 0)),
            pl.BlockSpec((1, N_EXPERTS, 1), lambda b: (b, 0, 0)),
            pl.BlockSpec((1, N_EXPERTS, 1), lambda b: (b, 0, 0)),
        ],
        out_specs=pl.BlockSpec((TB, D_MODEL), lambda b: (b, 0)),
        out_shape=jax.ShapeDtypeStruct((seq, D_MODEL), F32),
        compiler_params=pltpu.CompilerParams(
            dimension_semantics=("arbitrary",), vmem_limit_bytes=VMEM_LIMIT),
        name="moe_combine",
    )(h, ys, slots_tok, w_tok, off[:, None, :], off[:, :, None], pc[:, :, None])


def kernel(x, meta_tokens, rel_bias, norm_mix, w_in, q_norm, k_norm, lam_q1, lam_k1, lam_q2, lam_k2, subln, ssm_a_re, ssm_a_im, ssm_log_step, ssm_b_re, ssm_b_im, ssm_c_re, ssm_c_im, ssm_d, w_glu, b_glu, ssm_norm, w_out, norm_ffn, router_w, router_bias, w_gate, w_up, w_down, shared_gate, shared_up, shared_down):
    batch, seq, d = x.shape
    assert batch == 1 and d == D_MODEL and seq % ROW_BLOCK == 0 and seq % ATT_BLOCK == 0
    assert norm_mix.shape[0] == 1, "single layer"
    x2 = x.reshape(seq, d)
    meta_pad = jnp.zeros((ROW_BLOCK, d), F32).at[:N_META].set(meta_tokens.astype(F32))
    seg = jnp.kron(jnp.eye(QK_WIDTH // HEAD_DIM, dtype=F32),
                   jnp.full((HEAD_DIM, HEAD_DIM), 1.0 / HEAD_DIM, F32)).astype(BF16)
    qg = jnp.tile(q_norm[0].astype(F32), QK_WIDTH // HEAD_DIM)[None] * (HEAD_DIM ** -0.5)
    kg = jnp.tile(k_norm[0].astype(F32), QK_WIDTH // HEAD_DIM)[None]

    proj = _inproj(x2, meta_pad, norm_mix[0][None], w_in[0].astype(BF16), seg, qg, kg)

    score_bound = (BOUND_MARGIN * HEAD_DIM ** 0.5 * jnp.max(jnp.abs(q_norm[0].astype(F32)))
                   * jnp.max(jnp.abs(k_norm[0].astype(F32)))
                   + jnp.max(jnp.abs(rel_bias.astype(F32)))).reshape(1)
    att = _attention(proj, rel_bias.astype(F32), score_bound, lam_q1[0][None], lam_k1[0][None],
                     lam_q2[0][None], lam_k2[0][None], subln[0][None], seq)

    n_rows = proj.shape[0] // S5_CHUNK
    n_chunks = seq // S5_CHUNK
    u_chunks = (proj[:, U_OFFSET:]
                .reshape(n_rows, S5_CHUNK, SSM_GROUPS, SSM_GROUP)
                .transpose(2, 0, 1, 3).reshape(SSM_GROUPS, n_rows, S5_CHUNK * SSM_GROUP))
    pp, mm, qq, lam16 = _s5_prep(ssm_a_re[0].astype(F32), ssm_a_im[0].astype(F32),
                                 ssm_log_step[0].astype(F32), ssm_b_re[0].astype(F32),
                                 ssm_b_im[0].astype(F32), ssm_c_re[0].astype(F32),
                                 ssm_c_im[0].astype(F32))
    y_chunks = _s5(u_chunks, pp, mm, qq, lam16, n_chunks)
    y = (y_chunks.reshape(SSM_GROUPS, n_chunks, S5_CHUNK, SSM_GROUP)
         .transpose(1, 2, 0, 3).reshape(seq, SSM_WIDTH))

    h, hn, scores_t = _mix_out(
        x2, att, y, proj, ssm_d[0][None].astype(F32), w_glu[0].astype(BF16), b_glu[0][None].astype(F32),
        ssm_norm[0][None].astype(F32), w_out[0].astype(BF16), norm_ffn[0][None].astype(F32),
        router_w[0].astype(F32).T, shared_gate[0].astype(BF16), shared_up[0].astype(BF16),
        shared_down[0].astype(BF16))

    tb = MOE_TOKEN_BLOCK
    tri = (jnp.arange(tb)[:, None] <= jnp.arange(tb)[None, :]).astype(BF16)
    wts_t, slots_t, cnt = _route(scores_t, router_bias[0].astype(F32)[:, None], tri)
    off, pc, src, dst, bexp, first, n_act = _moe_tables(cnt[:, :, 0].astype(jnp.int32))
    off_f, pc_f = off.astype(F32), pc.astype(F32)
    xs = _dispatch(hn, slots_t, off_f, pc_f)
    ys = _experts(xs, jnp.zeros(xs.shape, BF16), src, dst, bexp, first, n_act,
                  w_gate[0], w_up[0], w_down[0])
    out = _combine(h, ys, slots_t.T, wts_t.T, off_f, pc_f)
    return out.reshape(batch, seq, d)
```

```python
import functools
import math

import jax
import jax.numpy as jnp
from jax import lax
from jax.experimental import pallas as pl
from jax.experimental.pallas import tpu as pltpu

F32 = jnp.float32
BF16 = jnp.bfloat16

D_MODEL = 2048
N_META = 16
ATT_WIDTH = 1024
SSM_WIDTH = 1024
HEAD_DIM = 64
V_DIM = 128
HEADS = 8
QK_WIDTH = 1024
IN_WIDTH = 4096
V_OFFSET = 2 * QK_WIDTH
U_OFFSET = V_OFFSET + 2 * ATT_WIDTH
PROJ_WIDTH = U_OFFSET + SSM_WIDTH
SSM_GROUP = 16
SSM_GROUPS = 64
SSM_STATE = 64
N_BUCKETS = 32
MAX_DISTANCE = 128
N_EXPERTS = 64
TOP_K = 8
N_EXPERT_GROUPS = 8
TOPK_GROUPS = 4
EXPERT_HIDDEN = 512
SHARED_HIDDEN = 512
ROUTED_SCALE = 2.5
EPS = 1e-6
LAMBDA_INIT = 0.8 - 0.6 * math.exp(-0.3 * 0)

ROW_BLOCK = 512
ATT_BLOCK = 512
S5_CHUNK = 16
S5_GROUPS_PER_STEP = 8
MOE_TOKEN_BLOCK = 256
MOE_UNIT = 16
MOE_SLOT_BLOCK = 512
MOE_BLOCK_ROWS = -(-(MOE_TOKEN_BLOCK * TOP_K + N_EXPERTS * (MOE_UNIT - 1)) // MOE_SLOT_BLOCK) * MOE_SLOT_BLOCK
NOT_ROUTED = -1e6
MIX_ROW_BLOCK = 256
NEG_BIG = -1e30
MAX_EXP_RANGE = 80.0
BOUND_MARGIN = 1.02
VMEM_LIMIT = 56 * 1024 * 1024


def _resident(shape, index_map):
    return pl.BlockSpec(shape, index_map, pipeline_mode=pl.Buffered(1))


def _inproj_body(x_ref, meta_ref, g_ref, w_ref, seg_ref, qg_ref, kg_ref, o_ref, *, n_xblk):
    i = pl.program_id(0)

    def run(src_ref):
        xv = src_ref[...]
        ms = jnp.mean(xv * xv, axis=-1, keepdims=True)
        hn = (xv * lax.rsqrt(ms + EPS) * g_ref[...]).astype(BF16)
        for s in range(4):
            ps = jnp.dot(hn, w_ref[:, s * 1024:(s + 1) * 1024], preferred_element_type=F32)
            if s < 2:
                gain = qg_ref if s == 0 else kg_ref
                msq = jnp.dot((ps * ps).astype(BF16), seg_ref[...], preferred_element_type=F32)
                ps = ps * lax.rsqrt(msq + EPS) * gain[...]
            pb = ps.astype(BF16)
            if s < 2:
                o_ref[:, s * 1024:(s + 1) * 1024] = pb
            elif s == 2:
                lane = lax.broadcasted_iota(jnp.int32, (pb.shape[0], V_DIM), 1)
                ones_col = jnp.where(lane == 0, 1.0, 0.0).astype(BF16)
                for hh in range(HEADS):
                    base = V_OFFSET + hh * 2 * V_DIM
                    o_ref[:, base:base + V_DIM] = pb[:, hh * V_DIM:(hh + 1) * V_DIM]
                    o_ref[:, base + V_DIM:base + 2 * V_DIM] = ones_col
            else:
                o_ref[:, U_OFFSET:U_OFFSET + SSM_WIDTH] = pb

    @pl.when(i < n_xblk)
    def _():
        run(x_ref)

    @pl.when(i == n_xblk)
    def _():
        run(meta_ref)


def _inproj(x2, meta_pad, gain, w_bf, seg, qg, kg):
    seq = x2.shape[0]
    n_xblk = seq // ROW_BLOCK
    rows = seq + ROW_BLOCK
    return pl.pallas_call(
        functools.partial(_inproj_body, n_xblk=n_xblk),
        grid=(n_xblk + 1,),
        in_specs=[
            pl.BlockSpec((ROW_BLOCK, D_MODEL), lambda i: (jnp.minimum(i, n_xblk - 1), 0)),
            _resident((ROW_BLOCK, D_MODEL), lambda i: (0, 0)),
            _resident((1, D_MODEL), lambda i: (0, 0)),
            _resident((D_MODEL, IN_WIDTH), lambda i: (0, 0)),
            _resident((QK_WIDTH, QK_WIDTH), lambda i: (0, 0)),
            _resident((1, QK_WIDTH), lambda i: (0, 0)),
            _resident((1, QK_WIDTH), lambda i: (0, 0)),
        ],
        out_specs=pl.BlockSpec((ROW_BLOCK, PROJ_WIDTH), lambda i: (i, 0)),
        out_shape=jax.ShapeDtypeStruct((rows, PROJ_WIDTH), BF16),
        compiler_params=pltpu.CompilerParams(
            dimension_semantics=("arbitrary",), vmem_limit_bytes=VMEM_LIMIT),
        name="inproj",
    )(x2, meta_pad, gain, w_bf, seg, qg, kg)


def _t5_bias(rel, tab_ref, h):
    half = N_BUCKETS // 2
    exact = half // 2
    n = jnp.abs(rel)
    nf = jnp.maximum(n, 1).astype(F32)
    large = exact + (jnp.log(nf / exact) / math.log(MAX_DISTANCE / exact) * (half - exact)).astype(jnp.int32)
    large = jnp.minimum(large, half - 1)
    bucket = jnp.where(rel > 0, half, 0) + jnp.where(n < exact, n, large)
    out = jnp.zeros(rel.shape, F32)
    for b in range(N_BUCKETS):
        out = jnp.where(bucket == b, tab_ref[b, h], out)
    return out


def _attn_body(tab_ref, bound_ref, q_ref, k_ref, v_ref, lq1_ref, lk1_ref, lq2_ref, lk2_ref,
               subln_ref, o_ref, bias_ref, acc1_ref, acc2_ref, m1_ref, m2_ref, *, n_main):
    T = ATT_BLOCK
    h = pl.program_id(0)
    qi = pl.program_id(1)
    bound = bound_ref[0]

    @pl.when(qi == 0)
    def _():
        offsets = (-T, 0, T, -N_META, -N_META - T, -2 * T, 2 * T)
        for kind, off in enumerate(offsets):
            masked = kind in (3, 4)

            def rows(rc, carry, off=off, masked=masked, kind=kind):
                r0 = pl.multiple_of(rc * 8, 8)
                r = r0 + lax.broadcasted_iota(jnp.int32, (8, T), 0)
                c = lax.broadcasted_iota(jnp.int32, (8, T), 1)
                b = _t5_bias(off + c - r, tab_ref, h) - bound
                if masked:
                    b = jnp.where(c < N_META, b, NEG_BIG)
                bias_ref[kind, pl.ds(r0, 8), :] = b
                return carry

            lax.fori_loop(0, T // 8, rows, 0)

    acc1_ref[...] = jnp.zeros(acc1_ref.shape, F32)
    acc2_ref[...] = jnp.zeros(acc2_ref.shape, F32)

    q = q_ref[...]
    q1 = q[:, :HEAD_DIM]
    q2 = q[:, HEAD_DIM:]
    nt = (((1,), (1,)), ((), ()))

    def tile(ki):
        koff = pl.multiple_of(ki * T, T)
        kb = k_ref[pl.ds(koff, T), :]
        va = v_ref[pl.ds(koff, T), :]
        d = ki - qi
        kind = jnp.where(ki == n_main, jnp.where(qi == 0, 3, 4),
                         jnp.where(d <= -2, 5, jnp.where(d >= 2, 6, d + 1)))
        return kb, va, bias_ref[kind]

    def bounded_step(ki, carry):
        kb, va, bias = tile(ki)
        s1 = lax.dot_general(q1, kb[:, :HEAD_DIM], nt, preferred_element_type=F32) + bias
        acc1_ref[...] += jnp.dot(jnp.exp(s1).astype(BF16), va, preferred_element_type=F32)
        s2 = lax.dot_general(q2, kb[:, HEAD_DIM:], nt, preferred_element_type=F32) + bias
        acc2_ref[...] += jnp.dot(jnp.exp(s2).astype(BF16), va, preferred_element_type=F32)
        return carry

    def online_map(s, va, m_ref, acc_ref):
        m_old = m_ref[...]
        m_new = jnp.maximum(m_old, jnp.max(s, axis=-1, keepdims=True))
        p = jnp.exp(s - m_new).astype(BF16)
        acc_ref[...] = (jnp.exp(m_old - m_new) * acc_ref[...]
                        + jnp.dot(p, va, preferred_element_type=F32))
        m_ref[...] = m_new

    def online_step(ki, carry):
        kb, va, bias = tile(ki)
        s1 = lax.dot_general(q1, kb[:, :HEAD_DIM], nt, preferred_element_type=F32) + bias
        online_map(s1, va, m1_ref, acc1_ref)
        s2 = lax.dot_general(q2, kb[:, HEAD_DIM:], nt, preferred_element_type=F32) + bias
        online_map(s2, va, m2_ref, acc2_ref)
        return carry

    no_running_max = 2.0 * bound <= MAX_EXP_RANGE

    @pl.when(no_running_max)
    def _():
        unroll = next(u for u in (11, 3, 2, 1) if (n_main + 1) % u == 0)
        lax.fori_loop(0, n_main + 1, bounded_step, 0, unroll=unroll)

    @pl.when(jnp.logical_not(no_running_max))
    def _():
        m1_ref[...] = jnp.full(m1_ref.shape, -jnp.inf, F32)
        m2_ref[...] = jnp.full(m2_ref.shape, -jnp.inf, F32)
        lax.fori_loop(0, n_main + 1, online_step, 0)

    lam = (jnp.exp(jnp.sum(lq1_ref[...] * lk1_ref[...], axis=-1, keepdims=True))
           - jnp.exp(jnp.sum(lq2_ref[...] * lk2_ref[...], axis=-1, keepdims=True))
           + LAMBDA_INIT)
    a1 = acc1_ref[...]
    a2 = acc2_ref[...]
    o = (a1[:, :V_DIM] / a1[:, V_DIM:V_DIM + 1]
         - lam * (a2[:, :V_DIM] / a2[:, V_DIM:V_DIM + 1]))
    ms = jnp.mean(o * o, axis=-1, keepdims=True)
    o = o * lax.rsqrt(ms + EPS) * subln_ref[...] * (1.0 - LAMBDA_INIT)
    o_ref[...] = o.astype(BF16)


def _attention(proj, rel_bias, score_bound, lq1, lk1, lq2, lk2, subln, seq):
    T = ATT_BLOCK
    n_main = seq // T
    rows = proj.shape[0]
    vec64 = lambda: _resident((1, HEAD_DIM), lambda h, qi: (0, 0))
    return pl.pallas_call(
        functools.partial(_attn_body, n_main=n_main),
        grid=(HEADS, n_main),
        in_specs=[
            pl.BlockSpec(memory_space=pltpu.SMEM),
            pl.BlockSpec(memory_space=pltpu.SMEM),
            pl.BlockSpec((T, 2 * HEAD_DIM), lambda h, qi: (qi, h)),
            pl.BlockSpec((rows, 2 * HEAD_DIM), lambda h, qi: (0, HEADS + h)),
            pl.BlockSpec((rows, 2 * V_DIM), lambda h, qi: (0, V_OFFSET // (2 * V_DIM) + h)),
            vec64(), vec64(), vec64(), vec64(),
            _resident((1, V_DIM), lambda h, qi: (0, 0)),
        ],
        out_specs=pl.BlockSpec((T, V_DIM), lambda h, qi: (qi, h)),
        out_shape=jax.ShapeDtypeStruct((seq, ATT_WIDTH), BF16),
        scratch_shapes=[
            pltpu.VMEM((7, T, T), F32),
            pltpu.VMEM((T, 2 * V_DIM), F32), pltpu.VMEM((T, 2 * V_DIM), F32),
            pltpu.VMEM((T, 1), F32), pltpu.VMEM((T, 1), F32),
        ],
        compiler_params=pltpu.CompilerParams(
            dimension_semantics=("arbitrary", "arbitrary"), vmem_limit_bytes=VMEM_LIMIT),
        name="diff_attention",
    )(rel_bias, score_bound, proj, proj, proj, lq1, lk1, lq2, lk2, subln)


def _s5_prep(a_re, a_im, log_step, b_re, b_im, c_re, c_im):
    hp = lax.Precision.HIGHEST
    C = S5_CHUNK
    dt = jnp.exp(log_step)[..., None]
    decay = jnp.exp(a_re * dt)
    ab_re = decay * jnp.cos(a_im * dt)
    ab_im = decay * jnp.sin(a_im * dt)
    den = a_re * a_re + a_im * a_im
    zr = ab_re - 1.0
    f_re = (zr * a_re + ab_im * a_im) / den
    f_im = (ab_im * a_re - zr * a_im) / den
    bb_re = f_re[..., None] * b_re - f_im[..., None] * b_im
    bb_im = f_re[..., None] * b_im + f_im[..., None] * b_re
    pr, pi = jnp.ones_like(ab_re), jnp.zeros_like(ab_re)
    pw_re, pw_im = [pr], [pi]
    for _ in range(C):
        pr, pi = pr * ab_re - pi * ab_im, pr * ab_im + pi * ab_re
        pw_re.append(pr)
        pw_im.append(pi)
    pw_re = jnp.stack(pw_re)
    pw_im = jnp.stack(pw_im)
    cp_re = c_re[None] * pw_re[:, :, :, None, :] - c_im[None] * pw_im[:, :, :, None, :]
    cp_im = c_re[None] * pw_im[:, :, :, None, :] + c_im[None] * pw_re[:, :, :, None, :]
    kern = (jnp.einsum('tdgpn,dgnq->tdgpq', cp_re[:C], bb_re, precision=hp)
            - jnp.einsum('tdgpn,dgnq->tdgpq', cp_im[:C], bb_im, precision=hp))
    jj = jnp.arange(C)[:, None]
    ii = jnp.arange(C)[None, :]
    kf = jnp.where((ii >= jj)[:, :, None, None, None], kern[:, 0][jnp.clip(ii - jj, 0, C - 1)], 0.0)
    kr = jnp.where((jj >= ii)[:, :, None, None, None], kern[:, 1][jnp.clip(jj - ii, 0, C - 1)], 0.0)
    mm = (kf + kr).transpose(2, 0, 4, 1, 3).reshape(SSM_GROUPS, C * SSM_GROUP, C * SSM_GROUP)

    def in_mat(pwr, pwi, d):
        re = pwr[:, :, :, None] * bb_re[d][None] - pwi[:, :, :, None] * bb_im[d][None]
        im = pwr[:, :, :, None] * bb_im[d][None] + pwi[:, :, :, None] * bb_re[d][None]
        return re.transpose(1, 0, 3, 2), im.transpose(1, 0, 3, 2)

    pf_re, pf_im = in_mat(pw_re[:C, 0][::-1], pw_im[:C, 0][::-1], 0)
    pr_re, pr_im = in_mat(pw_re[:C, 1], pw_im[:C, 1], 1)
    pp = jnp.concatenate([pf_re, pr_re, pf_im, pr_im], axis=-1).reshape(SSM_GROUPS, C * SSM_GROUP, 4 * SSM_STATE)

    def out_mat(cr, ci):
        return cr.transpose(1, 3, 0, 2), -ci.transpose(1, 3, 0, 2)

    qf_re, qf_im = out_mat(cp_re[1:, 0], cp_im[1:, 0])
    qr_re, qr_im = out_mat(cp_re[1:, 1][::-1], cp_im[1:, 1][::-1])
    qq = jnp.concatenate([qf_re, qr_re, qf_im, qr_im], axis=1).reshape(SSM_GROUPS, 4 * SSM_STATE, C * SSM_GROUP)
    lam16 = jnp.stack([jnp.concatenate([pw_re[C, 0], pw_re[C, 1]], axis=-1),
                       jnp.concatenate([pw_im[C, 0], pw_im[C, 1]], axis=-1)], axis=1)
    return pp.astype(BF16), mm.astype(BF16), qq.astype(BF16), lam16


def _s5_body(u_ref, pp_ref, mm_ref, qq_ref, lam_ref, y_ref,
             zre_ref, zim_ref, are_ref, aim_ref, bre_ref, bim_ref, *, n_chunks):
    GS = S5_GROUPS_PER_STEP
    NS = 2 * SSM_STATE
    for gi in range(GS):
        z = jnp.dot(u_ref[gi], pp_ref[gi], preferred_element_type=F32)
        zre_ref[:, gi, :] = z[:, :NS]
        zim_ref[:, gi, :] = z[:, NS:]

    ar = lam_ref[:, 0, :]
    ai = lam_ref[:, 1, :]
    fwd = lax.broadcasted_iota(jnp.int32, (GS, NS), 1) < SSM_STATE
    sre0 = jnp.where(fwd, zre_ref[n_chunks], 0.0)
    sim0 = jnp.where(fwd, zim_ref[n_chunks], 0.0)

    def scan_step(k, carry):
        sre, sim = carry
        kr = n_chunks - 1 - k
        are_ref[k] = sre
        aim_ref[k] = sim
        bre_ref[kr] = sre
        bim_ref[kr] = sim
        zr = jnp.where(fwd, zre_ref[k], zre_ref[kr])
        zi = jnp.where(fwd, zim_ref[k], zim_ref[kr])
        return ar * sre - ai * sim + zr, ar * sim + ai * sre + zi

    lax.fori_loop(0, n_chunks, scan_step, (sre0, sim0))

    fwd_rows = lax.broadcasted_iota(jnp.int32, (n_chunks, NS), 1) < SSM_STATE
    for gi in range(GS):
        s_re = jnp.where(fwd_rows, are_ref[:, gi, :], bre_ref[:, gi, :])
        s_im = jnp.where(fwd_rows, aim_ref[:, gi, :], bim_ref[:, gi, :])
        scat = jnp.concatenate([s_re, s_im], axis=1).astype(BF16)
        y = (jnp.dot(u_ref[gi, :n_chunks, :], mm_ref[gi], preferred_element_type=F32)
             + jnp.dot(scat, qq_ref[gi], preferred_element_type=F32))
        y_ref[gi] = y


def _s5(u_chunks, pp, mm, qq, lam16, n_chunks):
    GS = S5_GROUPS_PER_STEP
    n_rows = u_chunks.shape[1]
    W = S5_CHUNK * SSM_GROUP
    NS = 2 * SSM_STATE
    mat = lambda: pl.BlockSpec((GS, W, W), lambda g: (g, 0, 0))
    return pl.pallas_call(
        functools.partial(_s5_body, n_chunks=n_chunks),
        grid=(SSM_GROUPS // GS,),
        in_specs=[
            pl.BlockSpec((GS, n_rows, W), lambda g: (g, 0, 0)),
            mat(), mat(), mat(),
            pl.BlockSpec((GS, 2, NS), lambda g: (g, 0, 0)),
        ],
        out_specs=pl.BlockSpec((GS, n_chunks, W), lambda g: (g, 0, 0)),
        out_shape=jax.ShapeDtypeStruct((SSM_GROUPS, n_chunks, W), F32),
        scratch_shapes=[pltpu.VMEM((n_rows, GS, NS), F32), pltpu.VMEM((n_rows, GS, NS), F32)]
        + [pltpu.VMEM((n_chunks, GS, NS), F32) for _ in range(4)],
        compiler_params=pltpu.CompilerParams(
            dimension_semantics=("arbitrary",), vmem_limit_bytes=VMEM_LIMIT),
        name="s5_chunked",
    )(u_chunks, pp, mm, qq, lam16)


def _mix_out_body(x_ref, att_ref, y_ref, u_ref, d_ref, wglu_ref, bglu_ref, sn_ref, wout_ref,
                  nf_ref, rwt_ref, sg_ref, su_ref, sd_ref, h_ref, hn_ref, sc_ref):
    y = y_ref[...] + d_ref[...] * u_ref[...].astype(F32)
    y = jax.nn.gelu(y)
    gate = jax.nn.sigmoid(jnp.dot(y.astype(BF16), wglu_ref[...], preferred_element_type=F32) + bglu_ref[...])
    s = y * gate
    ms = jnp.mean(s * s, axis=-1, keepdims=True)
    ssm = (s * lax.rsqrt(ms + EPS) * sn_ref[...]).astype(BF16)
    h = (x_ref[...]
         + jnp.dot(att_ref[...], wout_ref[:ATT_WIDTH, :], preferred_element_type=F32)
         + jnp.dot(ssm, wout_ref[ATT_WIDTH:, :], preferred_element_type=F32))
    ms = jnp.mean(h * h, axis=-1, keepdims=True)
    hn = h * lax.rsqrt(ms + EPS) * nf_ref[...]
    hnb = hn.astype(BF16)
    hn_ref[...] = hnb
    logits_t = lax.dot_general(rwt_ref[...], hn, (((1,), (1,)), ((), ())),
                               precision=lax.Precision.HIGHEST, preferred_element_type=F32)
    sc_ref[...] = jax.nn.sigmoid(logits_t)
    g = jnp.dot(hnb, sg_ref[...], preferred_element_type=F32)
    up = jnp.dot(hnb, su_ref[...], preferred_element_type=F32)
    act = (jax.nn.silu(g) * up).astype(BF16)
    h_ref[...] = h + jnp.dot(act, sd_ref[...], preferred_element_type=F32)


def _mix_out(x2, att, y, proj, d_skip, w_glu, b_glu, ssm_norm, w_out, norm_ffn, router_wt,
             sh_gate, sh_up, sh_down):
    seq = x2.shape[0]
    R = MIX_ROW_BLOCK
    res = lambda shape: _resident(shape, lambda i: (0, 0))
    return pl.pallas_call(
        _mix_out_body,
        grid=(seq // R,),
        in_specs=[
            pl.BlockSpec((R, D_MODEL), lambda i: (i, 0)),
            pl.BlockSpec((R, ATT_WIDTH), lambda i: (i, 0)),
            pl.BlockSpec((R, SSM_WIDTH), lambda i: (i, 0)),
            pl.BlockSpec((R, SSM_WIDTH), lambda i: (i, U_OFFSET // SSM_WIDTH)),
            res((1, SSM_WIDTH)), res((SSM_WIDTH, SSM_WIDTH)), res((1, SSM_WIDTH)), res((1, SSM_WIDTH)),
            res((D_MODEL, D_MODEL)), res((1, D_MODEL)), res((N_EXPERTS, D_MODEL)),
            res((D_MODEL, SHARED_HIDDEN)), res((D_MODEL, SHARED_HIDDEN)), res((SHARED_HIDDEN, D_MODEL)),
        ],
        out_specs=[
            pl.BlockSpec((R, D_MODEL), lambda i: (i, 0)),
            pl.BlockSpec((R, D_MODEL), lambda i: (i, 0)),
            pl.BlockSpec((N_EXPERTS, R), lambda i: (0, i)),
        ],
        out_shape=[
            jax.ShapeDtypeStruct((seq, D_MODEL), F32),
            jax.ShapeDtypeStruct((seq, D_MODEL), BF16),
            jax.ShapeDtypeStruct((N_EXPERTS, seq), F32),
        ],
        compiler_params=pltpu.CompilerParams(
            dimension_semantics=("arbitrary",), vmem_limit_bytes=VMEM_LIMIT),
        name="mix_out_shared",
    )(x2, att, y, proj, d_skip, w_glu, b_glu, ssm_norm, w_out, norm_ffn, router_wt,
      sh_gate, sh_up, sh_down)


def _route_body(sc_ref, rb_ref, tri_ref, w_ref, slot_ref, cnt_ref):
    scores = sc_ref[...]
    R = scores.shape[1]
    per_group = N_EXPERTS // N_EXPERT_GROUPS
    choice = scores + rb_ref[...]
    c3 = choice.reshape(N_EXPERT_GROUPS, per_group, R)
    within = lax.broadcasted_iota(jnp.int32, c3.shape, 1)
    m1 = jnp.max(c3, axis=1, keepdims=True)
    first = jnp.min(jnp.where(c3 == m1, within, per_group), axis=1, keepdims=True)
    m2 = jnp.max(jnp.where(within == first, -jnp.inf, c3), axis=1, keepdims=True)
    grp = (m1 + m2).reshape(N_EXPERT_GROUPS, R)
    gidx = lax.broadcasted_iota(jnp.int32, grp.shape, 0)
    grank = jnp.zeros(grp.shape, jnp.int32)
    for b in range(N_EXPERT_GROUPS):
        gb = grp[b:b + 1, :]
        grank += ((gb > grp) | ((gb == grp) & (b < gidx))).astype(jnp.int32)
    gmask = grank < TOPK_GROUPS
    emask = jnp.broadcast_to(gmask[:, None, :], c3.shape).reshape(N_EXPERTS, R)
    val = jnp.where(emask, choice, -jnp.inf)
    eidx = lax.broadcasted_iota(jnp.int32, val.shape, 0)
    rank = jnp.zeros(val.shape, jnp.int32)
    for e in range(N_EXPERTS):
        ve = val[e:e + 1, :]
        rank += ((ve > val) | ((ve == val) & (e < eidx))).astype(jnp.int32)
    sel = rank < TOP_K
    w = jnp.where(sel, scores, 0.0)
    w_ref[...] = w / jnp.sum(w, axis=0, keepdims=True) * ROUTED_SCALE
    cum = jnp.dot(jnp.where(sel, 1.0, 0.0).astype(BF16), tri_ref[...], preferred_element_type=F32)
    slot_ref[...] = jnp.where(sel, cum - 1.0, NOT_ROUTED)
    cnt_ref[0] = jnp.broadcast_to(cum[:, R - 1:R], (N_EXPERTS, 128))


def _route(scores_t, router_bias, tri):
    seq = scores_t.shape[1]
    R = MOE_TOKEN_BLOCK
    n_blk = seq // R
    blk = pl.BlockSpec((N_EXPERTS, R), lambda i: (0, i))
    return pl.pallas_call(
        _route_body,
        grid=(n_blk,),
        in_specs=[blk, _resident((N_EXPERTS, 1), lambda i: (0, 0)), _resident((R, R), lambda i: (0, 0))],
        out_specs=[blk, blk, pl.BlockSpec((1, N_EXPERTS, 128), lambda i: (i, 0, 0))],
        out_shape=[jax.ShapeDtypeStruct((N_EXPERTS, seq), F32),
                   jax.ShapeDtypeStruct((N_EXPERTS, seq), F32),
                   jax.ShapeDtypeStruct((n_blk, N_EXPERTS, 128), F32)],
        compiler_params=pltpu.CompilerParams(dimension_semantics=("arbitrary",)),
        name="route",
    )(scores_t, router_bias, tri)


def _moe_tables(cnt):
    n_blk = cnt.shape[0]
    U, RB, SB = MOE_UNIT, MOE_BLOCK_ROWS, MOE_SLOT_BLOCK
    NU = SB // U
    E = N_EXPERTS
    pc = (cnt + U - 1) // U * U
    off = jnp.cumsum(pc, axis=1) - pc
    upc_t = (pc // U).T
    cum_t = jnp.cumsum(upc_t, axis=1)
    units_e = cum_t[:, -1]
    nblk_e = (units_e + NU - 1) // NU
    blk_end = jnp.cumsum(nblk_e)
    blk_start = blk_end - nblk_e
    n_act = blk_end[-1]
    max_blocks = n_blk * (RB // SB) + E
    i = jnp.arange(max_blocks, dtype=jnp.int32)
    active = i < n_act
    count_le = lambda edges, v: jnp.sum((edges <= v[..., None]).astype(jnp.int32), axis=-1)
    last_e = jnp.minimum(count_le(blk_end, n_act - 1), E - 1)
    be = jnp.where(active, jnp.minimum(count_le(blk_end[None, :], i), E - 1), last_e)
    oh_e = be[:, None] == jnp.arange(E, dtype=jnp.int32)[None, :]
    pick_e = lambda v: jnp.sum(jnp.where(oh_e, v[None, :], 0), axis=1)
    pick_e2 = lambda m: jnp.sum(jnp.where(oh_e[:, :, None], m[None, :, :], 0), axis=1)
    bstart_i = pick_e(blk_start)
    first = active & (i == bstart_i)
    local = (i - bstart_i)[:, None] * NU + jnp.arange(NU, dtype=jnp.int32)[None, :]
    valid = active[:, None] & (local < pick_e(units_e)[:, None])
    cum_i = pick_e2(cum_t)
    b_of = jnp.minimum(count_le(cum_i[:, None, :], local), n_blk - 1)
    oh_b = b_of[:, :, None] == jnp.arange(n_blk, dtype=jnp.int32)[None, None, :]
    pick_b = lambda m: jnp.sum(jnp.where(oh_b, m[:, None, :], 0), axis=2)
    seg_start = pick_b(cum_i) - pick_b(pick_e2(upc_t))
    unit = (b_of * RB + pick_b(pick_e2(off.T))) // U + (local - seg_start)
    spare = n_blk * RB // U
    src = jnp.where(valid, unit, spare)
    dst = jnp.where(valid, unit,
                    spare + (1 + i % 2)[:, None] * NU + jnp.arange(NU, dtype=jnp.int32)[None, :])
    short = active & (pick_e(units_e) - (i - bstart_i) * NU <= NU // 2)
    i32 = lambda a: a.astype(jnp.int32)
    flags = i32(first) + 2 * i32(short)
    return (off, pc, i32(src.reshape(-1)), i32(dst.reshape(-1)), i32(be), flags,
            i32(n_act.reshape(1)))


def _split_hi_lo(pos):
    hi = jnp.floor(pos * (1.0 / 64.0))
    return hi.astype(BF16), (pos - 64.0 * hi).astype(BF16)


def _dispatch_body(hn_ref, slot_ref, offc_ref, offl_ref, pcl_ref, xs_ref, *, n_blk):
    b = pl.program_id(0)
    RB, TB = MOE_BLOCK_ROWS, MOE_TOKEN_BLOCK

    @pl.when(b == n_blk)
    def _():
        xs_ref[...] = jnp.zeros(xs_ref.shape, BF16)

    @pl.when(b < n_blk)
    def _():
        pos = slot_ref[...] + offc_ref[0]
        hi, lo = _split_hi_lo(pos)
        r = lax.broadcasted_iota(jnp.int32, (RB, N_EXPERTS), 0).astype(F32)
        off = offl_ref[0]
        owner = jnp.where((r >= off) & (r < off + pcl_ref[0]), 1.0, 0.0).astype(BF16)
        p = (64.0 * jnp.dot(owner, hi, preferred_element_type=F32)
             + jnp.dot(owner, lo, preferred_element_type=F32))
        rr = lax.broadcasted_iota(jnp.int32, (RB, TB), 0).astype(F32)
        onehot = jnp.where(jnp.abs(p - rr) < 0.5, 1.0, 0.0).astype(BF16)
        x = hn_ref[...]
        C = MOE_SLOT_BLOCK
        for c in range(RB // C):
            xs_ref[0, c * C:(c + 1) * C, :] = jnp.dot(
                onehot[c * C:(c + 1) * C], x, preferred_element_type=F32).astype(BF16)


def _dispatch(hn, slots, off, pc):
    n_blk = off.shape[0]
    RB, TB = MOE_BLOCK_ROWS, MOE_TOKEN_BLOCK
    clamp = lambda b: jnp.minimum(b, n_blk - 1)
    return pl.pallas_call(
        functools.partial(_dispatch_body, n_blk=n_blk),
        grid=(n_blk + 1,),
        in_specs=[
            pl.BlockSpec((TB, D_MODEL), lambda b: (clamp(b), 0)),
            pl.BlockSpec((N_EXPERTS, TB), lambda b: (0, clamp(b))),
            pl.BlockSpec((1, N_EXPERTS, 1), lambda b: (clamp(b), 0, 0)),
            pl.BlockSpec((1, 1, N_EXPERTS), lambda b: (clamp(b), 0, 0)),
            pl.BlockSpec((1, 1, N_EXPERTS), lambda b: (clamp(b), 0, 0)),
        ],
        out_specs=pl.BlockSpec((1, RB, D_MODEL), lambda b: (b, 0, 0)),
        out_shape=jax.ShapeDtypeStruct((n_blk + 1, RB, D_MODEL), BF16),
        compiler_params=pltpu.CompilerParams(
            dimension_semantics=("arbitrary",), vmem_limit_bytes=VMEM_LIMIT),
        name="moe_dispatch",
    )(hn, slots, off[:, :, None], off[:, None, :], pc[:, None, :])


def _expert_body(src_ref, dst_ref, bexp_ref, first_ref, nact_ref,
                 xs_hbm, wg_ref, wu_ref, wd_ref, ys_hbm,
                 xbuf, ybuf, wgb, wub, wdb, sem_in, sem_out):
    del bexp_ref
    i = pl.program_id(0)
    n_act = nact_ref[0]
    cur = lax.rem(i, 2)
    U = MOE_UNIT
    NU = MOE_SLOT_BLOCK // U

    def in_copy(blk, buf, u):
        return pltpu.make_async_copy(xs_hbm.at[src_ref[blk * NU + u]],
                                     xbuf.at[buf, pl.ds(u * U, U)], sem_in.at[buf])

    def out_copy(blk, buf, u):
        return pltpu.make_async_copy(ybuf.at[buf, pl.ds(u * U, U)],
                                     ys_hbm.at[dst_ref[blk * NU + u]], sem_out.at[buf])

    @pl.when(i == 0)
    def _():
        for u in range(NU):
            in_copy(0, 0, u).start()

    @pl.when(i < n_act)
    def _():
        @pl.when(i + 1 < n_act)
        def _():
            for u in range(NU):
                in_copy(i + 1, 1 - cur, u).start()

        @pl.when(first_ref[i] % 2 == 1)
        def _():
            wgb[...] = wg_ref[0].astype(BF16)
            wub[...] = wu_ref[0].astype(BF16)
            wdb[...] = wd_ref[0].astype(BF16)

        for u in range(NU):
            in_copy(i, cur, u).wait()

        @pl.when(i >= 2)
        def _():
            for u in range(NU):
                out_copy(i - 2, cur, u).wait()

        def swiglu_rows(rows):
            x = xbuf[cur, :rows]
            g = jnp.dot(x, wgb[...], preferred_element_type=F32)
            up = jnp.dot(x, wub[...], preferred_element_type=F32)
            act = (jax.nn.silu(g) * up).astype(BF16)
            ybuf[cur, :rows] = jnp.dot(act, wdb[...], preferred_element_type=F32).astype(BF16)

        half = MOE_SLOT_BLOCK // 2

        @pl.when(first_ref[i] < 2)
        def _():
            swiglu_rows(MOE_SLOT_BLOCK)

        @pl.when(first_ref[i] >= 2)
        def _():
            swiglu_rows(half)
            ybuf[cur, half:] = jnp.zeros((half, D_MODEL), BF16)

        for u in range(NU):
            out_copy(i, cur, u).start()

        @pl.when(i == n_act - 1)
        def _():
            for u in range(NU):
                out_copy(i, cur, u).wait()

            @pl.when(i >= 1)
            def _():
                for u in range(NU):
                    out_copy(i - 1, 1 - cur, u).wait()


def _experts(xs, src, dst, bexp, first, n_act, wg, wu, wd):
    U, SB = MOE_UNIT, MOE_SLOT_BLOCK
    n_units = xs.shape[0] * xs.shape[1] // U
    max_blocks = bexp.shape[0]
    unit_view = lambda a: a.reshape(n_units, U, D_MODEL)
    wspec = lambda shape: pl.BlockSpec((1,) + shape, lambda i, src, dst, bexp, first, nact: (bexp[i], 0, 0))
    grid_spec = pltpu.PrefetchScalarGridSpec(
        num_scalar_prefetch=5,
        grid=(max_blocks,),
        in_specs=[
            pl.BlockSpec(memory_space=pl.ANY),
            wspec((D_MODEL, EXPERT_HIDDEN)), wspec((D_MODEL, EXPERT_HIDDEN)), wspec((EXPERT_HIDDEN, D_MODEL)),
        ],
        out_specs=pl.BlockSpec(memory_space=pl.ANY),
        scratch_shapes=[
            pltpu.VMEM((2, SB, D_MODEL), BF16), pltpu.VMEM((2, SB, D_MODEL), BF16),
            pltpu.VMEM((D_MODEL, EXPERT_HIDDEN), BF16), pltpu.VMEM((D_MODEL, EXPERT_HIDDEN), BF16),
            pltpu.VMEM((EXPERT_HIDDEN, D_MODEL), BF16),
            pltpu.SemaphoreType.DMA((2,)), pltpu.SemaphoreType.DMA((2,)),
        ],
    )
    ys = pl.pallas_call(
        _expert_body,
        grid_spec=grid_spec,
        out_shape=jax.ShapeDtypeStruct((n_units, U, D_MODEL), BF16),
        input_output_aliases={5: 0},
        compiler_params=pltpu.CompilerParams(
            dimension_semantics=("arbitrary",), vmem_limit_bytes=VMEM_LIMIT),
        name="moe_experts",
    )(src, dst, bexp, first, n_act, unit_view(xs), wg, wu, wd)
    return ys.reshape(xs.shape)


def _combine_body(h_ref, ys_ref, slot_ref, w_ref, offl_ref, offc_ref, pcc_ref, o_ref):
    RB, TB = MOE_BLOCK_ROWS, MOE_TOKEN_BLOCK
    pos = slot_ref[...] + offl_ref[0]
    hi, lo = _split_hi_lo(pos)
    r = lax.broadcasted_iota(jnp.int32, (N_EXPERTS, RB), 1).astype(F32)
    off = offc_ref[0]
    owner = jnp.where((r >= off) & (r < off + pcc_ref[0]), 1.0, 0.0).astype(BF16)
    p = (64.0 * jnp.dot(hi, owner, preferred_element_type=F32)
         + jnp.dot(lo, owner, preferred_element_type=F32))
    wr = jnp.dot(w_ref[...].astype(BF16), owner, preferred_element_type=F32)
    rr = lax.broadcasted_iota(jnp.int32, (TB, RB), 1).astype(F32)
    gather_w = jnp.where(jnp.abs(p - rr) < 0.5, wr, 0.0).astype(BF16)
    o_ref[...] = h_ref[...] + jnp.dot(gather_w, ys_ref[0], preferred_element_type=F32)


def _combine(h, ys, slots_tok, w_tok, off, pc):
    n_blk = off.shape[0]
    RB, TB = MOE_BLOCK_ROWS, MOE_TOKEN_BLOCK
    seq = h.shape[0]
    return pl.pallas_call(
        _combine_body,
        grid=(n_blk,),
        in_specs=[
            pl.BlockSpec((TB, D_MODEL), lambda b: (b, 0)),
            pl.BlockSpec((1, RB, D_MODEL), lambda b: (b, 0, 0)),
            pl.BlockSpec((TB, N_EXPERTS), lambda b: (b, 0)),
            pl.BlockSpec((TB, N_EXPERTS), lambda b: (b, 0)),
            pl.BlockSpec((1, 1, N_EXPERTS), lambda b: (b, 0, 0)),
            pl.BlockSpec((1, N_EXPERTS, 1), lambda b: (b, 0, 0)),
            pl.BlockSpec((1, N_EXPERTS, 1), lambda b: (b, 0, 0)),
        ],
        out_specs=pl.BlockSpec((TB, D_MODEL), lambda b: (b, 0)),
        out_shape=jax.ShapeDtypeStruct((seq, D_MODEL), F32),
        compiler_params=pltpu.CompilerParams(
            dimension_semantics=("arbitrary",), vmem_limit_bytes=VMEM_LIMIT),
        name="moe_combine",
    )(h, ys, slots_tok, w_tok, off[:, None, :], off[:, :, None], pc[:, :, None])


def kernel(x, meta_tokens, rel_bias, norm_mix, w_in, q_norm, k_norm, lam_q1, lam_k1, lam_q2, lam_k2, subln, ssm_a_re, ssm_a_im, ssm_log_step, ssm_b_re, ssm_b_im, ssm_c_re, ssm_c_im, ssm_d, w_glu, b_glu, ssm_norm, w_out, norm_ffn, router_w, router_bias, w_gate, w_up, w_down, shared_gate, shared_up, shared_down):
    batch, seq, d = x.shape
    assert batch == 1 and d == D_MODEL and seq % ROW_BLOCK == 0 and seq % ATT_BLOCK == 0
    assert norm_mix.shape[0] == 1, "single layer"
    x2 = x.reshape(seq, d)
    meta_pad = jnp.zeros((ROW_BLOCK, d), F32).at[:N_META].set(meta_tokens.astype(F32))
    seg = jnp.kron(jnp.eye(QK_WIDTH // HEAD_DIM, dtype=F32),
                   jnp.full((HEAD_DIM, HEAD_DIM), 1.0 / HEAD_DIM, F32)).astype(BF16)
    qg = jnp.tile(q_norm[0].astype(F32), QK_WIDTH // HEAD_DIM)[None] * (HEAD_DIM ** -0.5)
    kg = jnp.tile(k_norm[0].astype(F32), QK_WIDTH // HEAD_DIM)[None]

    proj = _inproj(x2, meta_pad, norm_mix[0][None], w_in[0].astype(BF16), seg, qg, kg)

    score_bound = (BOUND_MARGIN * HEAD_DIM ** 0.5 * jnp.max(jnp.abs(q_norm[0].astype(F32)))
                   * jnp.max(jnp.abs(k_norm[0].astype(F32)))
                   + jnp.max(jnp.abs(rel_bias.astype(F32)))).reshape(1)
    att = _attention(proj, rel_bias.astype(F32), score_bound, lam_q1[0][None], lam_k1[0][None],
                     lam_q2[0][None], lam_k2[0][None], subln[0][None], seq)

    n_rows = proj.shape[0] // S5_CHUNK
    n_chunks = seq // S5_CHUNK
    u_chunks = (proj[:, U_OFFSET:]
                .reshape(n_rows, S5_CHUNK, SSM_GROUPS, SSM_GROUP)
                .transpose(2, 0, 1, 3).reshape(SSM_GROUPS, n_rows, S5_CHUNK * SSM_GROUP))
    pp, mm, qq, lam16 = _s5_prep(ssm_a_re[0].astype(F32), ssm_a_im[0].astype(F32),
                                 ssm_log_step[0].astype(F32), ssm_b_re[0].astype(F32),
                                 ssm_b_im[0].astype(F32), ssm_c_re[0].astype(F32),
                                 ssm_c_im[0].astype(F32))
    y_chunks = _s5(u_chunks, pp, mm, qq, lam16, n_chunks)
    y = (y_chunks.reshape(SSM_GROUPS, n_chunks, S5_CHUNK, SSM_GROUP)
         .transpose(1, 2, 0, 3).reshape(seq, SSM_WIDTH))

    h, hn, scores_t = _mix_out(
        x2, att, y, proj, ssm_d[0][None].astype(F32), w_glu[0].astype(BF16), b_glu[0][None].astype(F32),
        ssm_norm[0][None].astype(F32), w_out[0].astype(BF16), norm_ffn[0][None].astype(F32),
        router_w[0].astype(F32).T, shared_gate[0].astype(BF16), shared_up[0].astype(BF16),
        shared_down[0].astype(BF16))

    tb = MOE_TOKEN_BLOCK
    tri = (jnp.arange(tb)[:, None] <= jnp.arange(tb)[None, :]).astype(BF16)
    wts_t, slots_t, cnt = _route(scores_t, router_bias[0].astype(F32)[:, None], tri)
    off, pc, src, dst, bexp, first, n_act = _moe_tables(cnt[:, :, 0].astype(jnp.int32))
    off_f, pc_f = off.astype(F32), pc.astype(F32)
    xs = _dispatch(hn, slots_t, off_f, pc_f)
    ys = _experts(xs, src, dst, bexp, first, n_act, w_gate[0], w_up[0], w_down[0])
    out = _combine(h, ys, slots_t.T, wts_t.T, off_f, pc_f)
    return out.reshape(batch, seq, d)
```

```python
import functools
import math

import jax
import jax.numpy as jnp
from jax import lax
from jax.experimental import pallas as pl
from jax.experimental.pallas import tpu as pltpu

F32 = jnp.float32
BF16 = jnp.bfloat16

D_MODEL = 2048
N_META = 16
ATT_WIDTH = 1024
SSM_WIDTH = 1024
HEAD_DIM = 64
V_DIM = 128
HEADS = 8
QK_WIDTH = 1024
IN_WIDTH = 4096
V_OFFSET = 2 * QK_WIDTH
U_OFFSET = V_OFFSET + 2 * ATT_WIDTH
PROJ_WIDTH = U_OFFSET + SSM_WIDTH
SSM_GROUP = 16
SSM_GROUPS = 64
SSM_STATE = 64
N_BUCKETS = 32
MAX_DISTANCE = 128
N_EXPERTS = 64
TOP_K = 8
N_EXPERT_GROUPS = 8
TOPK_GROUPS = 4
EXPERT_HIDDEN = 512
SHARED_HIDDEN = 512
ROUTED_SCALE = 2.5
EPS = 1e-6
LAMBDA_INIT = 0.8 - 0.6 * math.exp(-0.3 * 0)

ROW_BLOCK = 512
ATT_BLOCK = 512
S5_CHUNK = 16
S5_GROUPS_PER_STEP = 8
MOE_TOKEN_BLOCK = 256
MOE_UNIT = 16
MOE_SLOT_BLOCK = 512
MOE_BLOCK_ROWS = -(-(MOE_TOKEN_BLOCK * TOP_K + N_EXPERTS * (MOE_UNIT - 1)) // MOE_SLOT_BLOCK) * MOE_SLOT_BLOCK
NOT_ROUTED = -1e6
MIX_ROW_BLOCK = 256
NEG_BIG = -1e30
MAX_EXP_RANGE = 80.0
BOUND_MARGIN = 1.02
VMEM_LIMIT = 56 * 1024 * 1024


def _resident(shape, index_map):
    return pl.BlockSpec(shape, index_map, pipeline_mode=pl.Buffered(1))


def _inproj_body(x_ref, meta_ref, g_ref, w_ref, seg_ref, qg_ref, kg_ref, o_ref, *, n_xblk):
    i = pl.program_id(0)

    def run(src_ref):
        xv = src_ref[...]
        ms = jnp.mean(xv * xv, axis=-1, keepdims=True)
        hn = (xv * lax.rsqrt(ms + EPS) * g_ref[...]).astype(BF16)
        for s in range(4):
            ps = jnp.dot(hn, w_ref[:, s * 1024:(s + 1) * 1024], preferred_element_type=F32)
            if s < 2:
                gain = qg_ref if s == 0 else kg_ref
                msq = jnp.dot((ps * ps).astype(BF16), seg_ref[...], preferred_element_type=F32)
                ps = ps * lax.rsqrt(msq + EPS) * gain[...]
            pb = ps.astype(BF16)
            if s < 2:
                o_ref[:, s * 1024:(s + 1) * 1024] = pb
            elif s == 2:
                lane = lax.broadcasted_iota(jnp.int32, (pb.shape[0], V_DIM), 1)
                ones_col = jnp.where(lane == 0, 1.0, 0.0).astype(BF16)
                for hh in range(HEADS):
                    base = V_OFFSET + hh * 2 * V_DIM
                    o_ref[:, base:base + V_DIM] = pb[:, hh * V_DIM:(hh + 1) * V_DIM]
                    o_ref[:, base + V_DIM:base + 2 * V_DIM] = ones_col
            else:
                o_ref[:, U_OFFSET:U_OFFSET + SSM_WIDTH] = pb

    @pl.when(i < n_xblk)
    def _():
        run(x_ref)

    @pl.when(i == n_xblk)
    def _():
        run(meta_ref)


def _inproj(x2, meta_pad, gain, w_bf, seg, qg, kg):
    seq = x2.shape[0]
    n_xblk = seq // ROW_BLOCK
    rows = seq + ROW_BLOCK
    return pl.pallas_call(
        functools.partial(_inproj_body, n_xblk=n_xblk),
        grid=(n_xblk + 1,),
        in_specs=[
            pl.BlockSpec((ROW_BLOCK, D_MODEL), lambda i: (jnp.minimum(i, n_xblk - 1), 0)),
            _resident((ROW_BLOCK, D_MODEL), lambda i: (0, 0)),
            _resident((1, D_MODEL), lambda i: (0, 0)),
            _resident((D_MODEL, IN_WIDTH), lambda i: (0, 0)),
            _resident((QK_WIDTH, QK_WIDTH), lambda i: (0, 0)),
            _resident((1, QK_WIDTH), lambda i: (0, 0)),
            _resident((1, QK_WIDTH), lambda i: (0, 0)),
        ],
        out_specs=pl.BlockSpec((ROW_BLOCK, PROJ_WIDTH), lambda i: (i, 0)),
        out_shape=jax.ShapeDtypeStruct((rows, PROJ_WIDTH), BF16),
        compiler_params=pltpu.CompilerParams(
            dimension_semantics=("arbitrary",), vmem_limit_bytes=VMEM_LIMIT),
        name="inproj",
    )(x2, meta_pad, gain, w_bf, seg, qg, kg)


def _t5_bias(rel, tab_ref, h):
    half = N_BUCKETS // 2
    exact = half // 2
    n = jnp.abs(rel)
    nf = jnp.maximum(n, 1).astype(F32)
    large = exact + (jnp.log(nf / exact) / math.log(MAX_DISTANCE / exact) * (half - exact)).astype(jnp.int32)
    large = jnp.minimum(large, half - 1)
    bucket = jnp.where(rel > 0, half, 0) + jnp.where(n < exact, n, large)
    out = jnp.zeros(rel.shape, F32)
    for b in range(N_BUCKETS):
        out = jnp.where(bucket == b, tab_ref[b, h], out)
    return out


def _attn_body(tab_ref, bound_ref, q_ref, k_ref, v_ref, lq1_ref, lk1_ref, lq2_ref, lk2_ref,
               subln_ref, o_ref, bias_ref, acc1_ref, acc2_ref, m1_ref, m2_ref, *, n_main):
    T = ATT_BLOCK
    h = pl.program_id(0)
    qi = pl.program_id(1)
    bound = bound_ref[0]

    @pl.when(qi == 0)
    def _():
        offsets = (-T, 0, T, -N_META, -N_META - T, -2 * T, 2 * T)
        for kind, off in enumerate(offsets):
            masked = kind in (3, 4)

            def rows(rc, carry, off=off, masked=masked, kind=kind):
                r0 = pl.multiple_of(rc * 8, 8)
                r = r0 + lax.broadcasted_iota(jnp.int32, (8, T), 0)
                c = lax.broadcasted_iota(jnp.int32, (8, T), 1)
                b = _t5_bias(off + c - r, tab_ref, h) - bound
                if masked:
                    b = jnp.where(c < N_META, b, NEG_BIG)
                bias_ref[kind, pl.ds(r0, 8), :] = b
                return carry

            lax.fori_loop(0, T // 8, rows, 0)

    acc1_ref[...] = jnp.zeros(acc1_ref.shape, F32)
    acc2_ref[...] = jnp.zeros(acc2_ref.shape, F32)

    q = q_ref[...]
    q1 = q[:, :HEAD_DIM]
    q2 = q[:, HEAD_DIM:]
    nt = (((1,), (1,)), ((), ()))

    def tile(ki):
        koff = pl.multiple_of(ki * T, T)
        kb = k_ref[pl.ds(koff, T), :]
        va = v_ref[pl.ds(koff, T), :]
        d = ki - qi
        kind = jnp.where(ki == n_main, jnp.where(qi == 0, 3, 4),
                         jnp.where(d <= -2, 5, jnp.where(d >= 2, 6, d + 1)))
        return kb, va, bias_ref[kind]

    def bounded_step(ki, carry):
        kb, va, bias = tile(ki)
        s1 = lax.dot_general(q1, kb[:, :HEAD_DIM], nt, preferred_element_type=F32) + bias
        acc1_ref[...] += jnp.dot(jnp.exp(s1).astype(BF16), va, preferred_element_type=F32)
        s2 = lax.dot_general(q2, kb[:, HEAD_DIM:], nt, preferred_element_type=F32) + bias
        acc2_ref[...] += jnp.dot(jnp.exp(s2).astype(BF16), va, preferred_element_type=F32)
        return carry

    def online_map(s, va, m_ref, acc_ref):
        m_old = m_ref[...]
        m_new = jnp.maximum(m_old, jnp.max(s, axis=-1, keepdims=True))
        p = jnp.exp(s - m_new).astype(BF16)
        acc_ref[...] = (jnp.exp(m_old - m_new) * acc_ref[...]
                        + jnp.dot(p, va, preferred_element_type=F32))
        m_ref[...] = m_new

    def online_step(ki, carry):
        kb, va, bias = tile(ki)
        s1 = lax.dot_general(q1, kb[:, :HEAD_DIM], nt, preferred_element_type=F32) + bias
        online_map(s1, va, m1_ref, acc1_ref)
        s2 = lax.dot_general(q2, kb[:, HEAD_DIM:], nt, preferred_element_type=F32) + bias
        online_map(s2, va, m2_ref, acc2_ref)
        return carry

    no_running_max = 2.0 * bound <= MAX_EXP_RANGE

    @pl.when(no_running_max)
    def _():
        unroll = next(u for u in (11, 3, 2, 1) if (n_main + 1) % u == 0)
        lax.fori_loop(0, n_main + 1, bounded_step, 0, unroll=unroll)

    @pl.when(jnp.logical_not(no_running_max))
    def _():
        m1_ref[...] = jnp.full(m1_ref.shape, -jnp.inf, F32)
        m2_ref[...] = jnp.full(m2_ref.shape, -jnp.inf, F32)
        lax.fori_loop(0, n_main + 1, online_step, 0)

    lam = (jnp.exp(jnp.sum(lq1_ref[...] * lk1_ref[...], axis=-1, keepdims=True))
           - jnp.exp(jnp.sum(lq2_ref[...] * lk2_ref[...], axis=-1, keepdims=True))
           + LAMBDA_INIT)
    a1 = acc1_ref[...]
    a2 = acc2_ref[...]
    o = (a1[:, :V_DIM] / a1[:, V_DIM:V_DIM + 1]
         - lam * (a2[:, :V_DIM] / a2[:, V_DIM:V_DIM + 1]))
    ms = jnp.mean(o * o, axis=-1, keepdims=True)
    o = o * lax.rsqrt(ms + EPS) * subln_ref[...] * (1.0 - LAMBDA_INIT)
    o_ref[...] = o.astype(BF16)


def _attention(proj, rel_bias, score_bound, lq1, lk1, lq2, lk2, subln, seq):
    T = ATT_BLOCK
    n_main = seq // T
    rows = proj.shape[0]
    vec64 = lambda: _resident((1, HEAD_DIM), lambda h, qi: (0, 0))
    return pl.pallas_call(
        functools.partial(_attn_body, n_main=n_main),
        grid=(HEADS, n_main),
        in_specs=[
            pl.BlockSpec(memory_space=pltpu.SMEM),
            pl.BlockSpec(memory_space=pltpu.SMEM),
            pl.BlockSpec((T, 2 * HEAD_DIM), lambda h, qi: (qi, h)),
            pl.BlockSpec((rows, 2 * HEAD_DIM), lambda h, qi: (0, HEADS + h)),
            pl.BlockSpec((rows, 2 * V_DIM), lambda h, qi: (0, V_OFFSET // (2 * V_DIM) + h)),
            vec64(), vec64(), vec64(), vec64(),
            _resident((1, V_DIM), lambda h, qi: (0, 0)),
        ],
        out_specs=pl.BlockSpec((T, V_DIM), lambda h, qi: (qi, h)),
        out_shape=jax.ShapeDtypeStruct((seq, ATT_WIDTH), BF16),
        scratch_shapes=[
            pltpu.VMEM((7, T, T), F32),
            pltpu.VMEM((T, 2 * V_DIM), F32), pltpu.VMEM((T, 2 * V_DIM), F32),
            pltpu.VMEM((T, 1), F32), pltpu.VMEM((T, 1), F32),
        ],
        compiler_params=pltpu.CompilerParams(
            dimension_semantics=("arbitrary", "arbitrary"), vmem_limit_bytes=VMEM_LIMIT),
        name="diff_attention",
    )(rel_bias, score_bound, proj, proj, proj, lq1, lk1, lq2, lk2, subln)


def _s5_prep(a_re, a_im, log_step, b_re, b_im, c_re, c_im):
    C = S5_CHUNK
    dt = jnp.exp(log_step)[..., None]
    decay = jnp.exp(a_re * dt)
    ab_re = decay * jnp.cos(a_im * dt)
    ab_im = decay * jnp.sin(a_im * dt)
    den = a_re * a_re + a_im * a_im
    zr = ab_re - 1.0
    f_re = (zr * a_re + ab_im * a_im) / den
    f_im = (ab_im * a_re - zr * a_im) / den
    bb_re = f_re[..., None] * b_re - f_im[..., None] * b_im
    bb_im = f_re[..., None] * b_im + f_im[..., None] * b_re
    pr, pi = jnp.ones_like(ab_re), jnp.zeros_like(ab_re)
    pw_re, pw_im = [pr], [pi]
    for _ in range(C):
        pr, pi = pr * ab_re - pi * ab_im, pr * ab_im + pi * ab_re
        pw_re.append(pr)
        pw_im.append(pi)
    G, P, W = SSM_GROUPS, SSM_GROUP, C * SSM_GROUP
    pw_re = jnp.stack(pw_re, axis=-1)
    pw_im = jnp.stack(pw_im, axis=-1)
    ct_re = c_re.transpose(0, 1, 3, 2)
    ct_im = c_im.transpose(0, 1, 3, 2)
    cp_re = ct_re[:, :, :, None, :] * pw_re[..., None] - ct_im[:, :, :, None, :] * pw_im[..., None]
    cp_im = ct_re[:, :, :, None, :] * pw_im[..., None] + ct_im[:, :, :, None, :] * pw_re[..., None]
    bt_re = bb_re.transpose(0, 1, 3, 2)
    bt_im = bb_im.transpose(0, 1, 3, 2)

    def response(d):
        prod = (cp_re[d][:, None, :, :C, :] * bt_re[d][:, :, :, None, None]
                - cp_im[d][:, None, :, :C, :] * bt_im[d][:, :, :, None, None])
        return jnp.sum(prod, axis=2)

    ext_f = jnp.pad(response(0).reshape(G, P, W), ((0, 0), (0, 0), (W, 0)))
    ext_r = jnp.pad(response(1)[:, :, ::-1, :].reshape(G, P, W), ((0, 0), (0, 0), (0, W)))
    mm = jnp.stack([ext_f[:, :, W - P * j:2 * W - P * j] + ext_r[:, :, (C - 1 - j) * P:(C - 1 - j) * P + W]
                    for j in range(C)], axis=1).reshape(G, W, W)

    def in_mat(d, reverse_powers):
        pr_ = pw_re[d][:, :, :C].transpose(0, 2, 1)
        pi_ = pw_im[d][:, :, :C].transpose(0, 2, 1)
        if reverse_powers:
            pr_, pi_ = pr_[:, ::-1], pi_[:, ::-1]
        re = pr_[:, :, None, :] * bt_re[d][:, None] - pi_[:, :, None, :] * bt_im[d][:, None]
        im = pr_[:, :, None, :] * bt_im[d][:, None] + pi_[:, :, None, :] * bt_re[d][:, None]
        return re, im

    pf_re, pf_im = in_mat(0, True)
    pr_re, pr_im = in_mat(1, False)
    pp = jnp.concatenate([pf_re, pr_re, pf_im, pr_im], axis=-1).reshape(G, W, 4 * SSM_STATE)

    qq = jnp.concatenate([cp_re[0][:, :, 1:], cp_re[1][:, :, :0:-1],
                          -cp_im[0][:, :, 1:], -cp_im[1][:, :, :0:-1]], axis=1).reshape(G, 4 * SSM_STATE, W)
    lam16 = jnp.stack([jnp.concatenate([pw_re[0][:, :, C], pw_re[1][:, :, C]], axis=-1),
                       jnp.concatenate([pw_im[0][:, :, C], pw_im[1][:, :, C]], axis=-1)], axis=1)
    return pp.astype(BF16), mm.astype(BF16), qq.astype(BF16), lam16


def _s5_body(u_ref, pp_ref, mm_ref, qq_ref, lam_ref, y_ref,
             zre_ref, zim_ref, are_ref, aim_ref, bre_ref, bim_ref, *, n_chunks):
    GS = S5_GROUPS_PER_STEP
    NS = 2 * SSM_STATE
    for gi in range(GS):
        z = jnp.dot(u_ref[gi], pp_ref[gi], preferred_element_type=F32)
        zre_ref[:, gi, :] = z[:, :NS]
        zim_ref[:, gi, :] = z[:, NS:]

    ar = lam_ref[:, 0, :]
    ai = lam_ref[:, 1, :]
    fwd = lax.broadcasted_iota(jnp.int32, (GS, NS), 1) < SSM_STATE
    sre0 = jnp.where(fwd, zre_ref[n_chunks], 0.0)
    sim0 = jnp.where(fwd, zim_ref[n_chunks], 0.0)

    def scan_step(k, carry):
        sre, sim = carry
        kr = n_chunks - 1 - k
        are_ref[k] = sre
        aim_ref[k] = sim
        bre_ref[kr] = sre
        bim_ref[kr] = sim
        zr = jnp.where(fwd, zre_ref[k], zre_ref[kr])
        zi = jnp.where(fwd, zim_ref[k], zim_ref[kr])
        return ar * sre - ai * sim + zr, ar * sim + ai * sre + zi

    lax.fori_loop(0, n_chunks, scan_step, (sre0, sim0))

    fwd_rows = lax.broadcasted_iota(jnp.int32, (n_chunks, NS), 1) < SSM_STATE
    for gi in range(GS):
        s_re = jnp.where(fwd_rows, are_ref[:, gi, :], bre_ref[:, gi, :])
        s_im = jnp.where(fwd_rows, aim_ref[:, gi, :], bim_ref[:, gi, :])
        scat = jnp.concatenate([s_re, s_im], axis=1).astype(BF16)
        y = (jnp.dot(u_ref[gi, :n_chunks, :], mm_ref[gi], preferred_element_type=F32)
             + jnp.dot(scat, qq_ref[gi], preferred_element_type=F32))
        y_ref[gi] = y.astype(BF16)


def _s5(u_chunks, pp, mm, qq, lam16, n_chunks):
    GS = S5_GROUPS_PER_STEP
    n_rows = u_chunks.shape[1]
    W = S5_CHUNK * SSM_GROUP
    NS = 2 * SSM_STATE
    mat = lambda: pl.BlockSpec((GS, W, W), lambda g: (g, 0, 0))
    return pl.pallas_call(
        functools.partial(_s5_body, n_chunks=n_chunks),
        grid=(SSM_GROUPS // GS,),
        in_specs=[
            pl.BlockSpec((GS, n_rows, W), lambda g: (g, 0, 0)),
            mat(), mat(), mat(),
            pl.BlockSpec((GS, 2, NS), lambda g: (g, 0, 0)),
        ],
        out_specs=pl.BlockSpec((GS, n_chunks, W), lambda g: (g, 0, 0)),
        out_shape=jax.ShapeDtypeStruct((SSM_GROUPS, n_chunks, W), BF16),
        scratch_shapes=[pltpu.VMEM((n_rows, GS, NS), F32), pltpu.VMEM((n_rows, GS, NS), F32)]
        + [pltpu.VMEM((n_chunks, GS, NS), F32) for _ in range(4)],
        compiler_params=pltpu.CompilerParams(
            dimension_semantics=("arbitrary",), vmem_limit_bytes=VMEM_LIMIT),
        name="s5_chunked",
    )(u_chunks, pp, mm, qq, lam16)


def _mix_out_body(x_ref, att_ref, y_ref, u_ref, d_ref, wglu_ref, bglu_ref, sn_ref, wout_ref,
                  nf_ref, rwt_ref, sg_ref, su_ref, sd_ref, h_ref, hn_ref, sc_ref):
    y = y_ref[...].astype(F32) + d_ref[...] * u_ref[...].astype(F32)
    y = jax.nn.gelu(y)
    gate = jax.nn.sigmoid(jnp.dot(y.astype(BF16), wglu_ref[...], preferred_element_type=F32) + bglu_ref[...])
    s = y * gate
    ms = jnp.mean(s * s, axis=-1, keepdims=True)
    ssm = (s * lax.rsqrt(ms + EPS) * sn_ref[...]).astype(BF16)
    h = (x_ref[...]
         + jnp.dot(att_ref[...], wout_ref[:ATT_WIDTH, :], preferred_element_type=F32)
         + jnp.dot(ssm, wout_ref[ATT_WIDTH:, :], preferred_element_type=F32))
    ms = jnp.mean(h * h, axis=-1, keepdims=True)
    hn = h * lax.rsqrt(ms + EPS) * nf_ref[...]
    hnb = hn.astype(BF16)
    hn_ref[...] = hnb
    logits_t = lax.dot_general(rwt_ref[...], hn, (((1,), (1,)), ((), ())),
                               precision=lax.Precision.HIGHEST, preferred_element_type=F32)
    sc_ref[...] = jax.nn.sigmoid(logits_t)
    g = jnp.dot(hnb, sg_ref[...], preferred_element_type=F32)
    up = jnp.dot(hnb, su_ref[...], preferred_element_type=F32)
    act = (jax.nn.silu(g) * up).astype(BF16)
    h_ref[...] = h + jnp.dot(act, sd_ref[...], preferred_element_type=F32)


def _mix_out(x2, att, y, proj, d_skip, w_glu, b_glu, ssm_norm, w_out, norm_ffn, router_wt,
             sh_gate, sh_up, sh_down):
    seq = x2.shape[0]
    R = MIX_ROW_BLOCK
    res = lambda shape: _resident(shape, lambda i: (0, 0))
    return pl.pallas_call(
        _mix_out_body,
        grid=(seq // R,),
        in_specs=[
            pl.BlockSpec((R, D_MODEL), lambda i: (i, 0)),
            pl.BlockSpec((R, ATT_WIDTH), lambda i: (i, 0)),
            pl.BlockSpec((R, SSM_WIDTH), lambda i: (i, 0)),
            pl.BlockSpec((R, SSM_WIDTH), lambda i: (i, U_OFFSET // SSM_WIDTH)),
            res((1, SSM_WIDTH)), res((SSM_WIDTH, SSM_WIDTH)), res((1, SSM_WIDTH)), res((1, SSM_WIDTH)),
            res((D_MODEL, D_MODEL)), res((1, D_MODEL)), res((N_EXPERTS, D_MODEL)),
            res((D_MODEL, SHARED_HIDDEN)), res((D_MODEL, SHARED_HIDDEN)), res((SHARED_HIDDEN, D_MODEL)),
        ],
        out_specs=[
            pl.BlockSpec((R, D_MODEL), lambda i: (i, 0)),
            pl.BlockSpec((R, D_MODEL), lambda i: (i, 0)),
            pl.BlockSpec((N_EXPERTS, R), lambda i: (0, i)),
        ],
        out_shape=[
            jax.ShapeDtypeStruct((seq, D_MODEL), F32),
            jax.ShapeDtypeStruct((seq, D_MODEL), BF16),
            jax.ShapeDtypeStruct((N_EXPERTS, seq), F32),
        ],
        compiler_params=pltpu.CompilerParams(
            dimension_semantics=("arbitrary",), vmem_limit_bytes=VMEM_LIMIT),
        name="mix_out_shared",
    )(x2, att, y, proj, d_skip, w_glu, b_glu, ssm_norm, w_out, norm_ffn, router_wt,
      sh_gate, sh_up, sh_down)


def _route_body(sc_ref, rb_ref, tri_ref, w_ref, slot_ref, cnt_ref):
    scores = sc_ref[...]
    R = scores.shape[1]
    per_group = N_EXPERTS // N_EXPERT_GROUPS
    choice = scores + rb_ref[...]
    c3 = choice.reshape(N_EXPERT_GROUPS, per_group, R)
    within = lax.broadcasted_iota(jnp.int32, c3.shape, 1)
    m1 = jnp.max(c3, axis=1, keepdims=True)
    first = jnp.min(jnp.where(c3 == m1, within, per_group), axis=1, keepdims=True)
    m2 = jnp.max(jnp.where(within == first, -jnp.inf, c3), axis=1, keepdims=True)
    grp = (m1 + m2).reshape(N_EXPERT_GROUPS, R)
    gidx = lax.broadcasted_iota(jnp.int32, grp.shape, 0)
    grank = jnp.zeros(grp.shape, jnp.int32)
    for b in range(N_EXPERT_GROUPS):
        gb = grp[b:b + 1, :]
        grank += ((gb > grp) | ((gb == grp) & (b < gidx))).astype(jnp.int32)
    gmask = grank < TOPK_GROUPS
    emask = jnp.broadcast_to(gmask[:, None, :], c3.shape).reshape(N_EXPERTS, R)
    val = jnp.where(emask, choice, -jnp.inf)
    eidx = lax.broadcasted_iota(jnp.int32, val.shape, 0)
    rank = jnp.zeros(val.shape, jnp.int32)
    for e in range(N_EXPERTS):
        ve = val[e:e + 1, :]
        rank += ((ve > val) | ((ve == val) & (e < eidx))).astype(jnp.int32)
    sel = rank < TOP_K
    w = jnp.where(sel, scores, 0.0)
    w_ref[...] = w / jnp.sum(w, axis=0, keepdims=True) * ROUTED_SCALE
    cum = jnp.dot(jnp.where(sel, 1.0, 0.0).astype(BF16), tri_ref[...], preferred_element_type=F32)
    slot_ref[...] = jnp.where(sel, cum - 1.0, NOT_ROUTED)
    cnt_ref[0] = jnp.broadcast_to(cum[:, R - 1:R], (N_EXPERTS, 128))


def _route(scores_t, router_bias, tri):
    seq = scores_t.shape[1]
    R = MOE_TOKEN_BLOCK
    n_blk = seq // R
    blk = pl.BlockSpec((N_EXPERTS, R), lambda i: (0, i))
    return pl.pallas_call(
        _route_body,
        grid=(n_blk,),
        in_specs=[blk, _resident((N_EXPERTS, 1), lambda i: (0, 0)), _resident((R, R), lambda i: (0, 0))],
        out_specs=[blk, blk, pl.BlockSpec((1, N_EXPERTS, 128), lambda i: (i, 0, 0))],
        out_shape=[jax.ShapeDtypeStruct((N_EXPERTS, seq), F32),
                   jax.ShapeDtypeStruct((N_EXPERTS, seq), F32),
                   jax.ShapeDtypeStruct((n_blk, N_EXPERTS, 128), F32)],
        compiler_params=pltpu.CompilerParams(dimension_semantics=("arbitrary",)),
        name="route",
    )(scores_t, router_bias, tri)


def _moe_tables(cnt):
    n_blk = cnt.shape[0]
    U, RB, SB = MOE_UNIT, MOE_BLOCK_ROWS, MOE_SLOT_BLOCK
    NU = SB // U
    E = N_EXPERTS
    pc = (cnt + U - 1) // U * U
    off = jnp.cumsum(pc, axis=1) - pc
    upc_t = (pc // U).T
    cum_t = jnp.cumsum(upc_t, axis=1)
    units_e = cum_t[:, -1]
    nblk_e = (units_e + NU - 1) // NU
    blk_end = jnp.cumsum(nblk_e)
    blk_start = blk_end - nblk_e
    n_act = blk_end[-1]
    max_blocks = n_blk * (RB // SB) + E
    i = jnp.arange(max_blocks, dtype=jnp.int32)
    active = i < n_act
    count_le = lambda edges, v: jnp.sum((edges <= v[..., None]).astype(jnp.int32), axis=-1)
    last_e = jnp.minimum(count_le(blk_end, n_act - 1), E - 1)
    be = jnp.where(active, jnp.minimum(count_le(blk_end[None, :], i), E - 1), last_e)
    oh_e = be[:, None] == jnp.arange(E, dtype=jnp.int32)[None, :]
    pick_e = lambda v: jnp.sum(jnp.where(oh_e, v[None, :], 0), axis=1)
    pick_e2 = lambda m: jnp.sum(jnp.where(oh_e[:, :, None], m[None, :, :], 0), axis=1)
    bstart_i = pick_e(blk_start)
    first = active & (i == bstart_i)
    local = (i - bstart_i)[:, None] * NU + jnp.arange(NU, dtype=jnp.int32)[None, :]
    valid = active[:, None] & (local < pick_e(units_e)[:, None])
    cum_i = pick_e2(cum_t)
    b_of = jnp.minimum(count_le(cum_i[:, None, :], local), n_blk - 1)
    oh_b = b_of[:, :, None] == jnp.arange(n_blk, dtype=jnp.int32)[None, None, :]
    pick_b = lambda m: jnp.sum(jnp.where(oh_b, m[:, None, :], 0), axis=2)
    seg_start = pick_b(cum_i) - pick_b(pick_e2(upc_t))
    unit = (b_of * RB + pick_b(pick_e2(off.T))) // U + (local - seg_start)
    spare = n_blk * RB // U
    src = jnp.where(valid, unit, spare)
    dst = jnp.where(valid, unit,
                    spare + (1 + i % 2)[:, None] * NU + jnp.arange(NU, dtype=jnp.int32)[None, :])
    short = active & (pick_e(units_e) - (i - bstart_i) * NU <= NU // 2)
    i32 = lambda a: a.astype(jnp.int32)
    flags = i32(first) + 2 * i32(short)
    return (off, pc, i32(src.reshape(-1)), i32(dst.reshape(-1)), i32(be), flags,
            i32(n_act.reshape(1)))


def _split_hi_lo(pos):
    hi = jnp.floor(pos * (1.0 / 64.0))
    return hi.astype(BF16), (pos - 64.0 * hi).astype(BF16)


def _dispatch_body(hn_ref, slot_ref, offc_ref, offl_ref, pcl_ref, xs_ref, *, n_blk):
    b = pl.program_id(0)
    RB, TB = MOE_BLOCK_ROWS, MOE_TOKEN_BLOCK

    @pl.when(b == n_blk)
    def _():
        xs_ref[...] = jnp.zeros(xs_ref.shape, BF16)

    @pl.when(b < n_blk)
    def _():
        pos = slot_ref[...] + offc_ref[0]
        hi, lo = _split_hi_lo(pos)
        r = lax.broadcasted_iota(jnp.int32, (RB, N_EXPERTS), 0).astype(F32)
        off = offl_ref[0]
        owner = jnp.where((r >= off) & (r < off + pcl_ref[0]), 1.0, 0.0).astype(BF16)
        p = (64.0 * jnp.dot(owner, hi, preferred_element_type=F32)
             + jnp.dot(owner, lo, preferred_element_type=F32))
        rr = lax.broadcasted_iota(jnp.int32, (RB, TB), 0).astype(F32)
        onehot = jnp.where(jnp.abs(p - rr) < 0.5, 1.0, 0.0).astype(BF16)
        x = hn_ref[...]
        C = MOE_SLOT_BLOCK
        for c in range(RB // C):
            xs_ref[0, c * C:(c + 1) * C, :] = jnp.dot(
                onehot[c * C:(c + 1) * C], x, preferred_element_type=F32).astype(BF16)


def _dispatch(hn, slots, off, pc):
    n_blk = off.shape[0]
    RB, TB = MOE_BLOCK_ROWS, MOE_TOKEN_BLOCK
    clamp = lambda b: jnp.minimum(b, n_blk - 1)
    return pl.pallas_call(
        functools.partial(_dispatch_body, n_blk=n_blk),
        grid=(n_blk + 1,),
        in_specs=[
            pl.BlockSpec((TB, D_MODEL), lambda b: (clamp(b), 0)),
            pl.BlockSpec((N_EXPERTS, TB), lambda b: (0, clamp(b))),
            pl.BlockSpec((1, N_EXPERTS, 1), lambda b: (clamp(b), 0, 0)),
            pl.BlockSpec((1, 1, N_EXPERTS), lambda b: (clamp(b), 0, 0)),
            pl.BlockSpec((1, 1, N_EXPERTS), lambda b: (clamp(b), 0, 0)),
        ],
        out_specs=pl.BlockSpec((1, RB, D_MODEL), lambda b: (b, 0, 0)),
        out_shape=jax.ShapeDtypeStruct((n_blk + 1, RB, D_MODEL), BF16),
        compiler_params=pltpu.CompilerParams(
            dimension_semantics=("arbitrary",), vmem_limit_bytes=VMEM_LIMIT),
        name="moe_dispatch",
    )(hn, slots, off[:, :, None], off[:, None, :], pc[:, None, :])


def _expert_body(src_ref, dst_ref, bexp_ref, first_ref, nact_ref,
                 xs_hbm, wg_ref, wu_ref, wd_ref, ys_hbm,
                 xbuf, ybuf, wgb, wub, wdb, sem_in, sem_out):
    del bexp_ref
    i = pl.program_id(0)
    n_act = nact_ref[0]
    cur = lax.rem(i, 2)
    U = MOE_UNIT
    NU = MOE_SLOT_BLOCK // U

    def in_copy(blk, buf, u):
        return pltpu.make_async_copy(xs_hbm.at[src_ref[blk * NU + u]],
                                     xbuf.at[buf, pl.ds(u * U, U)], sem_in.at[buf])

    def out_copy(blk, buf, u):
        return pltpu.make_async_copy(ybuf.at[buf, pl.ds(u * U, U)],
                                     ys_hbm.at[dst_ref[blk * NU + u]], sem_out.at[buf])

    @pl.when(i == 0)
    def _():
        for u in range(NU):
            in_copy(0, 0, u).start()

    @pl.when(i < n_act)
    def _():
        @pl.when(i + 1 < n_act)
        def _():
            for u in range(NU):
                in_copy(i + 1, 1 - cur, u).start()

        @pl.when(first_ref[i] % 2 == 1)
        def _():
            wgb[...] = wg_ref[0].astype(BF16)
            wub[...] = wu_ref[0].astype(BF16)
            wdb[...] = wd_ref[0].astype(BF16)

        for u in range(NU):
            in_copy(i, cur, u).wait()

        @pl.when(i >= 2)
        def _():
            for u in range(NU):
                out_copy(i - 2, cur, u).wait()

        def swiglu_rows(rows):
            x = xbuf[cur, :rows]
            g = jnp.dot(x, wgb[...], preferred_element_type=F32)
            up = jnp.dot(x, wub[...], preferred_element_type=F32)
            act = (jax.nn.silu(g) * up).astype(BF16)
            ybuf[cur, :rows] = jnp.dot(act, wdb[...], preferred_element_type=F32).astype(BF16)

        half = MOE_SLOT_BLOCK // 2

        @pl.when(first_ref[i] < 2)
        def _():
            swiglu_rows(MOE_SLOT_BLOCK)

        @pl.when(first_ref[i] >= 2)
        def _():
            swiglu_rows(half)
            ybuf[cur, half:] = jnp.zeros((half, D_MODEL), BF16)

        for u in range(NU):
            out_copy(i, cur, u).start()

        @pl.when(i == n_act - 1)
        def _():
            for u in range(NU):
                out_copy(i, cur, u).wait()

            @pl.when(i >= 1)
            def _():
                for u in range(NU):
                    out_copy(i - 1, 1 - cur, u).wait()


def _experts(xs, src, dst, bexp, first, n_act, wg, wu, wd):
    U, SB = MOE_UNIT, MOE_SLOT_BLOCK
    n_units = xs.shape[0] * xs.shape[1] // U
    max_blocks = bexp.shape[0]
    unit_view = lambda a: a.reshape(n_units, U, D_MODEL)
    wspec = lambda shape: pl.BlockSpec((1,) + shape, lambda i, src, dst, bexp, first, nact: (bexp[i], 0, 0))
    grid_spec = pltpu.PrefetchScalarGridSpec(
        num_scalar_prefetch=5,
        grid=(max_blocks,),
        in_specs=[
            pl.BlockSpec(memory_space=pl.ANY),
            wspec((D_MODEL, EXPERT_HIDDEN)), wspec((D_MODEL, EXPERT_HIDDEN)), wspec((EXPERT_HIDDEN, D_MODEL)),
        ],
        out_specs=pl.BlockSpec(memory_space=pl.ANY),
        scratch_shapes=[
            pltpu.VMEM((2, SB, D_MODEL), BF16), pltpu.VMEM((2, SB, D_MODEL), BF16),
            pltpu.VMEM((D_MODEL, EXPERT_HIDDEN), BF16), pltpu.VMEM((D_MODEL, EXPERT_HIDDEN), BF16),
            pltpu.VMEM((EXPERT_HIDDEN, D_MODEL), BF16),
            pltpu.SemaphoreType.DMA((2,)), pltpu.SemaphoreType.DMA((2,)),
        ],
    )
    ys = pl.pallas_call(
        _expert_body,
        grid_spec=grid_spec,
        out_shape=jax.ShapeDtypeStruct((n_units, U, D_MODEL), BF16),
        input_output_aliases={5: 0},
        compiler_params=pltpu.CompilerParams(
            dimension_semantics=("arbitrary",), vmem_limit_bytes=VMEM_LIMIT),
        name="moe_experts",
    )(src, dst, bexp, first, n_act, unit_view(xs), wg, wu, wd)
    return ys.reshape(xs.shape)


def _combine_body(h_ref, ys_ref, slot_ref, w_ref, offl_ref, offc_ref, pcc_ref, o_ref):
    RB, TB = MOE_BLOCK_ROWS, MOE_TOKEN_BLOCK
    pos = slot_ref[...] + offl_ref[0]
    hi, lo = _split_hi_lo(pos)
    r = lax.broadcasted_iota(jnp.int32, (N_EXPERTS, RB), 1).astype(F32)
    off = offc_ref[0]
    owner = jnp.where((r >= off) & (r < off + pcc_ref[0]), 1.0, 0.0).astype(BF16)
    p = (64.0 * jnp.dot(hi, owner, preferred_element_type=F32)
         + jnp.dot(lo, owner, preferred_element_type=F32))
    wr = jnp.dot(w_ref[...].astype(BF16), owner, preferred_element_type=F32)
    rr = lax.broadcasted_iota(jnp.int32, (TB, RB), 1).astype(F32)
    gather_w = jnp.where(jnp.abs(p - rr) < 0.5, wr, 0.0).astype(BF16)
    o_ref[...] = h_ref[...] + jnp.dot(gather_w, ys_ref[0], preferred_element_type=F32)


def _combine(h, ys, slots_tok, w_tok, off, pc):
    n_blk = off.shape[0]
    RB, TB = MOE_BLOCK_ROWS, MOE_TOKEN_BLOCK
    seq = h.shape[0]
    return pl.pallas_call(
        _combine_body,
        grid=(n_blk,),
        in_specs=[
            pl.BlockSpec((TB, D_MODEL), lambda b: (b, 0)),
            pl.BlockSpec((1, RB, D_MODEL), lambda b: (b, 0, 0)),
            pl.BlockSpec((TB, N_EXPERTS), lambda b: (b, 0)),
            pl.BlockSpec((TB, N_EXPERTS), lambda b: (b, 0)),
            pl.BlockSpec((1, 1, N_EXPERTS), lambda b: (b, 0, 0)),
            pl.BlockSpec((1, N_EXPERTS, 1), lambda b: (b, 0, 0)),
            pl.BlockSpec((1, N_EXPERTS, 1), lambda b: (b, 0, 0)),
        ],
        out_specs=pl.BlockSpec((TB, D_MODEL), lambda b: (b, 0)),
        out_shape=jax.ShapeDtypeStruct((seq, D_MODEL), F32),
        compiler_params=pltpu.CompilerParams(
            dimension_semantics=("arbitrary",), vmem_limit_bytes=VMEM_LIMIT),
        name="moe_combine",
    )(h, ys, slots_tok, w_tok, off[:, None, :], off[:, :, None], pc[:, :, None])


def kernel(x, meta_tokens, rel_bias, norm_mix, w_in, q_norm, k_norm, lam_q1, lam_k1, lam_q2, lam_k2, subln, ssm_a_re, ssm_a_im, ssm_log_step, ssm_b_re, ssm_b_im, ssm_c_re, ssm_c_im, ssm_d, w_glu, b_glu, ssm_norm, w_out, norm_ffn, router_w, router_bias, w_gate, w_up, w_down, shared_gate, shared_up, shared_down):
    batch, seq, d = x.shape
    assert batch == 1 and d == D_MODEL and seq % ROW_BLOCK == 0 and seq % ATT_BLOCK == 0
    assert norm_mix.shape[0] == 1, "single layer"
    x2 = x.reshape(seq, d)
    meta_pad = jnp.zeros((ROW_BLOCK, d), F32).at[:N_META].set(meta_tokens.astype(F32))
    seg = jnp.kron(jnp.eye(QK_WIDTH // HEAD_DIM, dtype=F32),
                   jnp.full((HEAD_DIM, HEAD_DIM), 1.0 / HEAD_DIM, F32)).astype(BF16)
    qg = jnp.tile(q_norm[0].astype(F32), QK_WIDTH // HEAD_DIM)[None] * (HEAD_DIM ** -0.5)
    kg = jnp.tile(k_norm[0].astype(F32), QK_WIDTH // HEAD_DIM)[None]

    proj = _inproj(x2, meta_pad, norm_mix[0][None], w_in[0].astype(BF16), seg, qg, kg)

    score_bound = (BOUND_MARGIN * HEAD_DIM ** 0.5 * jnp.max(jnp.abs(q_norm[0].astype(F32)))
                   * jnp.max(jnp.abs(k_norm[0].astype(F32)))
                   + jnp.max(jnp.abs(rel_bias.astype(F32)))).reshape(1)
    att = _attention(proj, rel_bias.astype(F32), score_bound, lam_q1[0][None], lam_k1[0][None],
                     lam_q2[0][None], lam_k2[0][None], subln[0][None], seq)

    n_rows = proj.shape[0] // S5_CHUNK
    n_chunks = seq // S5_CHUNK
    u_chunks = (proj[:, U_OFFSET:]
                .reshape(n_rows, S5_CHUNK, SSM_GROUPS, SSM_GROUP)
                .transpose(2, 0, 1, 3).reshape(SSM_GROUPS, n_rows, S5_CHUNK * SSM_GROUP))
    pp, mm, qq, lam16 = _s5_prep(ssm_a_re[0].astype(F32), ssm_a_im[0].astype(F32),
                                 ssm_log_step[0].astype(F32), ssm_b_re[0].astype(F32),
                                 ssm_b_im[0].astype(F32), ssm_c_re[0].astype(F32),
                                 ssm_c_im[0].astype(F32))
    y_chunks = _s5(u_chunks, pp, mm, qq, lam16, n_chunks)
    y = (y_chunks.reshape(SSM_GROUPS, n_chunks, S5_CHUNK, SSM_GROUP)
         .transpose(1, 2, 0, 3).reshape(seq, SSM_WIDTH))

    h, hn, scores_t = _mix_out(
        x2, att, y, proj, ssm_d[0][None].astype(F32), w_glu[0].astype(BF16), b_glu[0][None].astype(F32),
        ssm_norm[0][None].astype(F32), w_out[0].astype(BF16), norm_ffn[0][None].astype(F32),
        router_w[0].astype(F32).T, shared_gate[0].astype(BF16), shared_up[0].astype(BF16),
        shared_down[0].astype(BF16))

    tb = MOE_TOKEN_BLOCK
    tri = (jnp.arange(tb)[:, None] <= jnp.arange(tb)[None, :]).astype(BF16)
    wts_t, slots_t, cnt = _route(scores_t, router_bias[0].astype(F32)[:, None], tri)
    off, pc, src, dst, bexp, first, n_act = _moe_tables(cnt[:, :, 0].astype(jnp.int32))
    off_f, pc_f = off.astype(F32), pc.astype(F32)
    xs = _dispatch(hn, slots_t, off_f, pc_f)
    ys = _experts(xs, src, dst, bexp, first, n_act, w_gate[0], w_up[0], w_down[0])
    out = _combine(h, ys, slots_t.T, wts_t.T, off_f, pc_f)
    return out.reshape(batch, seq, d)
```

```python
import functools
import math

import jax
import jax.numpy as jnp
from jax import lax
from jax.experimental import pallas as pl
from jax.experimental.pallas import tpu as pltpu

F32 = jnp.float32
BF16 = jnp.bfloat16

D_MODEL = 2048
N_META = 16
ATT_WIDTH = 1024
SSM_WIDTH = 1024
HEAD_DIM = 64
V_DIM = 128
HEADS = 8
QK_WIDTH = 1024
IN_WIDTH = 4096
V_OFFSET = 2 * QK_WIDTH
PROJ_WIDTH = V_OFFSET + 2 * ATT_WIDTH
SSM_GROUP = 16
SSM_GROUPS = 64
SSM_STATE = 64
N_BUCKETS = 32
MAX_DISTANCE = 128
N_EXPERTS = 64
TOP_K = 8
N_EXPERT_GROUPS = 8
TOPK_GROUPS = 4
EXPERT_HIDDEN = 512
SHARED_HIDDEN = 512
ROUTED_SCALE = 2.5
EPS = 1e-6
LAMBDA_INIT = 0.8 - 0.6 * math.exp(-0.3 * 0)

ROW_BLOCK = 512
ATT_BLOCK = 512
S5_CHUNK = 16
S5_GROUPS_PER_STEP = 8
S5_OCTET_WIDTH = S5_GROUPS_PER_STEP * S5_CHUNK * SSM_GROUP
MOE_TOKEN_BLOCK = 256
MOE_UNIT = 16
MOE_SLOT_BLOCK = 512
MOE_BLOCK_ROWS = -(-(MOE_TOKEN_BLOCK * TOP_K + N_EXPERTS * (MOE_UNIT - 1)) // MOE_SLOT_BLOCK) * MOE_SLOT_BLOCK
NOT_ROUTED = -1e6
MIX_ROW_BLOCK = 256
NEG_BIG = -1e30
MAX_EXP_RANGE = 80.0
BOUND_MARGIN = 1.02
VMEM_LIMIT = 56 * 1024 * 1024


def _resident(shape, index_map):
    return pl.BlockSpec(shape, index_map, pipeline_mode=pl.Buffered(1))


def _inproj_body(x_ref, meta_ref, g_ref, w_ref, seg_ref, qg_ref, kg_ref, o_ref, u_ref, *, n_xblk):
    i = pl.program_id(0)

    def run(src_ref):
        xv = src_ref[...]
        ms = jnp.mean(xv * xv, axis=-1, keepdims=True)
        hn = (xv * lax.rsqrt(ms + EPS) * g_ref[...]).astype(BF16)
        for s in range(4):
            ps = jnp.dot(hn, w_ref[:, s * 1024:(s + 1) * 1024], preferred_element_type=F32)
            if s < 2:
                gain = qg_ref if s == 0 else kg_ref
                msq = jnp.dot((ps * ps).astype(BF16), seg_ref[...], preferred_element_type=F32)
                ps = ps * lax.rsqrt(msq + EPS) * gain[...]
            if s == 3:
                u_ref[...] = ps
                continue
            pb = ps.astype(BF16)
            if s < 2:
                o_ref[:, s * 1024:(s + 1) * 1024] = pb
            else:
                lane = lax.broadcasted_iota(jnp.int32, (pb.shape[0], V_DIM), 1)
                ones_col = jnp.where(lane == 0, 1.0, 0.0).astype(BF16)
                for hh in range(HEADS):
                    base = V_OFFSET + hh * 2 * V_DIM
                    o_ref[:, base:base + V_DIM] = pb[:, hh * V_DIM:(hh + 1) * V_DIM]
                    o_ref[:, base + V_DIM:base + 2 * V_DIM] = ones_col

    @pl.when(i < n_xblk)
    def _():
        run(x_ref)

    @pl.when(i == n_xblk)
    def _():
        run(meta_ref)


def _inproj(x2, meta_pad, gain, w_bf, seg, qg, kg):
    seq = x2.shape[0]
    n_xblk = seq // ROW_BLOCK
    rows = seq + ROW_BLOCK
    return pl.pallas_call(
        functools.partial(_inproj_body, n_xblk=n_xblk),
        grid=(n_xblk + 1,),
        in_specs=[
            pl.BlockSpec((ROW_BLOCK, D_MODEL), lambda i: (jnp.minimum(i, n_xblk - 1), 0)),
            _resident((ROW_BLOCK, D_MODEL), lambda i: (0, 0)),
            _resident((1, D_MODEL), lambda i: (0, 0)),
            _resident((D_MODEL, IN_WIDTH), lambda i: (0, 0)),
            _resident((QK_WIDTH, QK_WIDTH), lambda i: (0, 0)),
            _resident((1, QK_WIDTH), lambda i: (0, 0)),
            _resident((1, QK_WIDTH), lambda i: (0, 0)),
        ],
        out_specs=[pl.BlockSpec((ROW_BLOCK, PROJ_WIDTH), lambda i: (i, 0)),
                   pl.BlockSpec((ROW_BLOCK, SSM_WIDTH), lambda i: (i, 0))],
        out_shape=[jax.ShapeDtypeStruct((rows, PROJ_WIDTH), BF16),
                   jax.ShapeDtypeStruct((rows, SSM_WIDTH), F32)],
        compiler_params=pltpu.CompilerParams(
            dimension_semantics=("arbitrary",), vmem_limit_bytes=VMEM_LIMIT),
        name="inproj",
    )(x2, meta_pad, gain, w_bf, seg, qg, kg)


def _t5_bias(rel, tab_ref, h):
    half = N_BUCKETS // 2
    exact = half // 2
    n = jnp.abs(rel)
    nf = jnp.maximum(n, 1).astype(F32)
    large = exact + (jnp.log(nf / exact) / math.log(MAX_DISTANCE / exact) * (half - exact)).astype(jnp.int32)
    large = jnp.minimum(large, half - 1)
    bucket = jnp.where(rel > 0, half, 0) + jnp.where(n < exact, n, large)
    out = jnp.zeros(rel.shape, F32)
    for b in range(N_BUCKETS):
        out = jnp.where(bucket == b, tab_ref[b, h], out)
    return out


def _attn_body(tab_ref, bound_ref, q_ref, k_ref, v_ref, lq1_ref, lk1_ref, lq2_ref, lk2_ref,
               subln_ref, o_ref, bias_ref, acc1_ref, acc2_ref, m1_ref, m2_ref, *, n_main):
    T = ATT_BLOCK
    h = pl.program_id(0)
    qi = pl.program_id(1)
    bound = bound_ref[0]

    @pl.when(qi == 0)
    def _():
        offsets = (-T, 0, T, -N_META, -N_META - T, -2 * T, 2 * T)
        for kind, off in enumerate(offsets):
            masked = kind in (3, 4)

            def rows(rc, carry, off=off, masked=masked, kind=kind):
                r0 = pl.multiple_of(rc * 8, 8)
                r = r0 + lax.broadcasted_iota(jnp.int32, (8, T), 0)
                c = lax.broadcasted_iota(jnp.int32, (8, T), 1)
                b = _t5_bias(off + c - r, tab_ref, h) - bound
                if masked:
                    b = jnp.where(c < N_META, b, NEG_BIG)
                bias_ref[kind, pl.ds(r0, 8), :] = b
                return carry

            lax.fori_loop(0, T // 8, rows, 0)

    acc1_ref[...] = jnp.zeros(acc1_ref.shape, F32)
    acc2_ref[...] = jnp.zeros(acc2_ref.shape, F32)

    q = q_ref[...]
    q1 = q[:, :HEAD_DIM]
    q2 = q[:, HEAD_DIM:]
    nt = (((1,), (1,)), ((), ()))

    def tile(ki):
        koff = pl.multiple_of(ki * T, T)
        kb = k_ref[pl.ds(koff, T), :]
        va = v_ref[pl.ds(koff, T), :]
        d = ki - qi
        kind = jnp.where(ki == n_main, jnp.where(qi == 0, 3, 4),
                         jnp.where(d <= -2, 5, jnp.where(d >= 2, 6, d + 1)))
        return kb, va, bias_ref[kind]

    def bounded_step(ki, carry):
        kb, va, bias = tile(ki)
        s1 = lax.dot_general(q1, kb[:, :HEAD_DIM], nt, preferred_element_type=F32) + bias
        acc1_ref[...] += jnp.dot(jnp.exp(s1).astype(BF16), va, preferred_element_type=F32)
        s2 = lax.dot_general(q2, kb[:, HEAD_DIM:], nt, preferred_element_type=F32) + bias
        acc2_ref[...] += jnp.dot(jnp.exp(s2).astype(BF16), va, preferred_element_type=F32)
        return carry

    def online_map(s, va, m_ref, acc_ref):
        m_old = m_ref[...]
        m_new = jnp.maximum(m_old, jnp.max(s, axis=-1, keepdims=True))
        p = jnp.exp(s - m_new).astype(BF16)
        acc_ref[...] = (jnp.exp(m_old - m_new) * acc_ref[...]
                        + jnp.dot(p, va, preferred_element_type=F32))
        m_ref[...] = m_new

    def online_step(ki, carry):
        kb, va, bias = tile(ki)
        s1 = lax.dot_general(q1, kb[:, :HEAD_DIM], nt, preferred_element_type=F32) + bias
        online_map(s1, va, m1_ref, acc1_ref)
        s2 = lax.dot_general(q2, kb[:, HEAD_DIM:], nt, preferred_element_type=F32) + bias
        online_map(s2, va, m2_ref, acc2_ref)
        return carry

    no_running_max = 2.0 * bound <= MAX_EXP_RANGE

    @pl.when(no_running_max)
    def _():
        unroll = next(u for u in (11, 3, 2, 1) if (n_main + 1) % u == 0)
        lax.fori_loop(0, n_main + 1, bounded_step, 0, unroll=unroll)

    @pl.when(jnp.logical_not(no_running_max))
    def _():
        m1_ref[...] = jnp.full(m1_ref.shape, -jnp.inf, F32)
        m2_ref[...] = jnp.full(m2_ref.shape, -jnp.inf, F32)
        lax.fori_loop(0, n_main + 1, online_step, 0)

    lam = (jnp.exp(jnp.sum(lq1_ref[...] * lk1_ref[...], axis=-1, keepdims=True))
           - jnp.exp(jnp.sum(lq2_ref[...] * lk2_ref[...], axis=-1, keepdims=True))
           + LAMBDA_INIT)
    a1 = acc1_ref[...]
    a2 = acc2_ref[...]
    o = (a1[:, :V_DIM] / a1[:, V_DIM:V_DIM + 1]
         - lam * (a2[:, :V_DIM] / a2[:, V_DIM:V_DIM + 1]))
    ms = jnp.mean(o * o, axis=-1, keepdims=True)
    o = o * lax.rsqrt(ms + EPS) * subln_ref[...] * (1.0 - LAMBDA_INIT)
    o_ref[...] = o.astype(BF16)


def _attention(proj, rel_bias, score_bound, lq1, lk1, lq2, lk2, subln, seq):
    T = ATT_BLOCK
    n_main = seq // T
    rows = proj.shape[0]
    vec64 = lambda: _resident((1, HEAD_DIM), lambda h, qi: (0, 0))
    return pl.pallas_call(
        functools.partial(_attn_body, n_main=n_main),
        grid=(HEADS, n_main),
        in_specs=[
            pl.BlockSpec(memory_space=pltpu.SMEM),
            pl.BlockSpec(memory_space=pltpu.SMEM),
            pl.BlockSpec((T, 2 * HEAD_DIM), lambda h, qi: (qi, h)),
            pl.BlockSpec((rows, 2 * HEAD_DIM), lambda h, qi: (0, HEADS + h)),
            pl.BlockSpec((rows, 2 * V_DIM), lambda h, qi: (0, V_OFFSET // (2 * V_DIM) + h)),
            vec64(), vec64(), vec64(), vec64(),
            _resident((1, V_DIM), lambda h, qi: (0, 0)),
        ],
        out_specs=pl.BlockSpec((T, V_DIM), lambda h, qi: (qi, h)),
        out_shape=jax.ShapeDtypeStruct((seq, ATT_WIDTH), BF16),
        scratch_shapes=[
            pltpu.VMEM((7, T, T), F32),
            pltpu.VMEM((T, 2 * V_DIM), F32), pltpu.VMEM((T, 2 * V_DIM), F32),
            pltpu.VMEM((T, 1), F32), pltpu.VMEM((T, 1), F32),
        ],
        compiler_params=pltpu.CompilerParams(
            dimension_semantics=("arbitrary", "arbitrary"), vmem_limit_bytes=VMEM_LIMIT),
        name="diff_attention",
    )(rel_bias, score_bound, proj, proj, proj, lq1, lk1, lq2, lk2, subln)


def _lane_regroup_matrix():
    out_lane = jnp.arange(S5_OCTET_WIDTH, dtype=jnp.int32)
    g8 = out_lane // (S5_CHUNK * SSM_GROUP)
    j = (out_lane % (S5_CHUNK * SSM_GROUP)) // SSM_GROUP
    p = out_lane % SSM_GROUP
    in_lane = j * 128 + g8 * SSM_GROUP + p
    return (jnp.arange(S5_OCTET_WIDTH, dtype=jnp.int32)[:, None] == in_lane[None, :]).astype(BF16)


def _to_chunks_body(u_ref, perm_ref, o_ref):
    x = jnp.concatenate([u_ref[:, j, :].astype(BF16) for j in range(S5_CHUNK)], axis=1)
    r = jnp.dot(x, perm_ref[...], preferred_element_type=F32)
    w = S5_CHUNK * SSM_GROUP
    for g8 in range(S5_GROUPS_PER_STEP):
        o_ref[g8] = r[:, g8 * w:(g8 + 1) * w].astype(BF16)


def _to_chunks(u3, perm):
    n_rows = u3.shape[0]
    w = S5_CHUNK * SSM_GROUP
    return pl.pallas_call(
        _to_chunks_body,
        grid=(SSM_GROUPS // S5_GROUPS_PER_STEP,),
        in_specs=[pl.BlockSpec((n_rows, S5_CHUNK, 128), lambda o: (0, 0, o)),
                  _resident((S5_OCTET_WIDTH, S5_OCTET_WIDTH), lambda o: (0, 0))],
        out_specs=pl.BlockSpec((S5_GROUPS_PER_STEP, n_rows, w), lambda o: (o, 0, 0)),
        out_shape=jax.ShapeDtypeStruct((SSM_GROUPS, n_rows, w), BF16),
        compiler_params=pltpu.CompilerParams(
            dimension_semantics=("arbitrary",), vmem_limit_bytes=VMEM_LIMIT),
        name="s5_to_chunks",
    )(u3, perm)


def _from_chunks_body(y_ref, perm_t_ref, o_ref):
    x = jnp.concatenate([y_ref[g8] for g8 in range(S5_GROUPS_PER_STEP)], axis=1)
    r = jnp.dot(x, perm_t_ref[...], preferred_element_type=F32)
    for j in range(S5_CHUNK):
        o_ref[:, j, :] = r[:, j * 128:(j + 1) * 128]


def _from_chunks(y_chunks, perm_t):
    n_chunks = y_chunks.shape[1]
    w = S5_CHUNK * SSM_GROUP
    return pl.pallas_call(
        _from_chunks_body,
        grid=(SSM_GROUPS // S5_GROUPS_PER_STEP,),
        in_specs=[pl.BlockSpec((S5_GROUPS_PER_STEP, n_chunks, w), lambda o: (o, 0, 0)),
                  _resident((S5_OCTET_WIDTH, S5_OCTET_WIDTH), lambda o: (0, 0))],
        out_specs=pl.BlockSpec((n_chunks, S5_CHUNK, 128), lambda o: (0, 0, o)),
        out_shape=jax.ShapeDtypeStruct((n_chunks, S5_CHUNK, SSM_WIDTH), F32),
        compiler_params=pltpu.CompilerParams(
            dimension_semantics=("arbitrary",), vmem_limit_bytes=VMEM_LIMIT),
        name="s5_from_chunks",
    )(y_chunks, perm_t)


def _s5_prep(a_re, a_im, log_step, b_re, b_im, c_re, c_im):
    C = S5_CHUNK
    dt = jnp.exp(log_step)[..., None]
    decay = jnp.exp(a_re * dt)
    ab_re = decay * jnp.cos(a_im * dt)
    ab_im = decay * jnp.sin(a_im * dt)
    den = a_re * a_re + a_im * a_im
    zr = ab_re - 1.0
    f_re = (zr * a_re + ab_im * a_im) / den
    f_im = (ab_im * a_re - zr * a_im) / den
    bb_re = f_re[..., None] * b_re - f_im[..., None] * b_im
    bb_im = f_re[..., None] * b_im + f_im[..., None] * b_re
    pr, pi = jnp.ones_like(ab_re), jnp.zeros_like(ab_re)
    pw_re, pw_im = [pr], [pi]
    for _ in range(C):
        pr, pi = pr * ab_re - pi * ab_im, pr * ab_im + pi * ab_re
        pw_re.append(pr)
        pw_im.append(pi)
    G, P, W = SSM_GROUPS, SSM_GROUP, C * SSM_GROUP
    pw_re = jnp.stack(pw_re, axis=-1)
    pw_im = jnp.stack(pw_im, axis=-1)
    ct_re = c_re.transpose(0, 1, 3, 2)
    ct_im = c_im.transpose(0, 1, 3, 2)
    cp_re = ct_re[:, :, :, None, :] * pw_re[..., None] - ct_im[:, :, :, None, :] * pw_im[..., None]
    cp_im = ct_re[:, :, :, None, :] * pw_im[..., None] + ct_im[:, :, :, None, :] * pw_re[..., None]
    bt_re = bb_re.transpose(0, 1, 3, 2)
    bt_im = bb_im.transpose(0, 1, 3, 2)

    def response(d):
        prod = (cp_re[d][:, None, :, :C, :] * bt_re[d][:, :, :, None, None]
                - cp_im[d][:, None, :, :C, :] * bt_im[d][:, :, :, None, None])
        return jnp.sum(prod, axis=2)

    ext_f = jnp.pad(response(0).reshape(G, P, W), ((0, 0), (0, 0), (W, 0)))
    ext_r = jnp.pad(response(1)[:, :, ::-1, :].reshape(G, P, W), ((0, 0), (0, 0), (0, W)))
    mm = jnp.stack([ext_f[:, :, W - P * j:2 * W - P * j] + ext_r[:, :, (C - 1 - j) * P:(C - 1 - j) * P + W]
                    for j in range(C)], axis=1).reshape(G, W, W)

    def in_mat(d, reverse_powers):
        pr_ = pw_re[d][:, :, :C].transpose(0, 2, 1)
        pi_ = pw_im[d][:, :, :C].transpose(0, 2, 1)
        if reverse_powers:
            pr_, pi_ = pr_[:, ::-1], pi_[:, ::-1]
        re = pr_[:, :, None, :] * bt_re[d][:, None] - pi_[:, :, None, :] * bt_im[d][:, None]
        im = pr_[:, :, None, :] * bt_im[d][:, None] + pi_[:, :, None, :] * bt_re[d][:, None]
        return re, im

    pf_re, pf_im = in_mat(0, True)
    pr_re, pr_im = in_mat(1, False)
    pp = jnp.concatenate([pf_re, pr_re, pf_im, pr_im], axis=-1).reshape(G, W, 4 * SSM_STATE)

    qq = jnp.concatenate([cp_re[0][:, :, 1:], cp_re[1][:, :, :0:-1],
                          -cp_im[0][:, :, 1:], -cp_im[1][:, :, :0:-1]], axis=1).reshape(G, 4 * SSM_STATE, W)
    lam16 = jnp.stack([jnp.concatenate([pw_re[0][:, :, C], pw_re[1][:, :, C]], axis=-1),
                       jnp.concatenate([pw_im[0][:, :, C], pw_im[1][:, :, C]], axis=-1)], axis=1)
    return pp.astype(BF16), mm.astype(BF16), qq.astype(BF16), lam16


def _s5_body(u_ref, pp_ref, mm_ref, qq_ref, lam_ref, y_ref,
             zre_ref, zim_ref, are_ref, aim_ref, bre_ref, bim_ref, *, n_chunks):
    GS = S5_GROUPS_PER_STEP
    NS = 2 * SSM_STATE
    for gi in range(GS):
        z = jnp.dot(u_ref[gi], pp_ref[gi], preferred_element_type=F32)
        zre_ref[:, gi, :] = z[:, :NS]
        zim_ref[:, gi, :] = z[:, NS:]

    ar = lam_ref[:, 0, :]
    ai = lam_ref[:, 1, :]
    fwd = lax.broadcasted_iota(jnp.int32, (GS, NS), 1) < SSM_STATE
    sre0 = jnp.where(fwd, zre_ref[n_chunks], 0.0)
    sim0 = jnp.where(fwd, zim_ref[n_chunks], 0.0)

    def scan_step(k, carry):
        sre, sim = carry
        kr = n_chunks - 1 - k
        are_ref[k] = sre
        aim_ref[k] = sim
        bre_ref[kr] = sre
        bim_ref[kr] = sim
        zr = jnp.where(fwd, zre_ref[k], zre_ref[kr])
        zi = jnp.where(fwd, zim_ref[k], zim_ref[kr])
        return ar * sre - ai * sim + zr, ar * sim + ai * sre + zi

    lax.fori_loop(0, n_chunks, scan_step, (sre0, sim0))

    fwd_rows = lax.broadcasted_iota(jnp.int32, (n_chunks, NS), 1) < SSM_STATE
    for gi in range(GS):
        s_re = jnp.where(fwd_rows, are_ref[:, gi, :], bre_ref[:, gi, :])
        s_im = jnp.where(fwd_rows, aim_ref[:, gi, :], bim_ref[:, gi, :])
        scat = jnp.concatenate([s_re, s_im], axis=1).astype(BF16)
        y = (jnp.dot(u_ref[gi, :n_chunks, :], mm_ref[gi], preferred_element_type=F32)
             + jnp.dot(scat, qq_ref[gi], preferred_element_type=F32))
        y_ref[gi] = y.astype(BF16)


def _s5(u_chunks, pp, mm, qq, lam16, n_chunks):
    GS = S5_GROUPS_PER_STEP
    n_rows = u_chunks.shape[1]
    W = S5_CHUNK * SSM_GROUP
    NS = 2 * SSM_STATE
    mat = lambda: pl.BlockSpec((GS, W, W), lambda g: (g, 0, 0))
    return pl.pallas_call(
        functools.partial(_s5_body, n_chunks=n_chunks),
        grid=(SSM_GROUPS // GS,),
        in_specs=[
            pl.BlockSpec((GS, n_rows, W), lambda g: (g, 0, 0)),
            mat(), mat(), mat(),
            pl.BlockSpec((GS, 2, NS), lambda g: (g, 0, 0)),
        ],
        out_specs=pl.BlockSpec((GS, n_chunks, W), lambda g: (g, 0, 0)),
        out_shape=jax.ShapeDtypeStruct((SSM_GROUPS, n_chunks, W), BF16),
        scratch_shapes=[pltpu.VMEM((n_rows, GS, NS), F32), pltpu.VMEM((n_rows, GS, NS), F32)]
        + [pltpu.VMEM((n_chunks, GS, NS), F32) for _ in range(4)],
        compiler_params=pltpu.CompilerParams(
            dimension_semantics=("arbitrary",), vmem_limit_bytes=VMEM_LIMIT),
        name="s5_chunked",
    )(u_chunks, pp, mm, qq, lam16)


def _mix_out_body(x_ref, att_ref, y_ref, u_ref, d_ref, wglu_ref, bglu_ref, sn_ref, wout_ref,
                  nf_ref, rwt_ref, sg_ref, su_ref, sd_ref, h_ref, hn_ref, sc_ref):
    y = y_ref[...].astype(F32) + d_ref[...] * u_ref[...].astype(F32)
    y = jax.nn.gelu(y)
    gate = jax.nn.sigmoid(jnp.dot(y.astype(BF16), wglu_ref[...], preferred_element_type=F32) + bglu_ref[...])
    s = y * gate
    ms = jnp.mean(s * s, axis=-1, keepdims=True)
    ssm = (s * lax.rsqrt(ms + EPS) * sn_ref[...]).astype(BF16)
    h = (x_ref[...]
         + jnp.dot(att_ref[...], wout_ref[:ATT_WIDTH, :], preferred_element_type=F32)
         + jnp.dot(ssm, wout_ref[ATT_WIDTH:, :], preferred_element_type=F32))
    ms = jnp.mean(h * h, axis=-1, keepdims=True)
    hn = h * lax.rsqrt(ms + EPS) * nf_ref[...]
    hnb = hn.astype(BF16)
    hn_ref[...] = hnb
    logits_t = lax.dot_general(rwt_ref[...], hn, (((1,), (1,)), ((), ())),
                               precision=lax.Precision.HIGHEST, preferred_element_type=F32)
    sc_ref[...] = jax.nn.sigmoid(logits_t)
    g = jnp.dot(hnb, sg_ref[...], preferred_element_type=F32)
    up = jnp.dot(hnb, su_ref[...], preferred_element_type=F32)
    act = (jax.nn.silu(g) * up).astype(BF16)
    h_ref[...] = h + jnp.dot(act, sd_ref[...], preferred_element_type=F32)


def _mix_out(x2, att, y, u, d_skip, w_glu, b_glu, ssm_norm, w_out, norm_ffn, router_wt,
             sh_gate, sh_up, sh_down):
    seq = x2.shape[0]
    R = MIX_ROW_BLOCK
    res = lambda shape: _resident(shape, lambda i: (0, 0))
    return pl.pallas_call(
        _mix_out_body,
        grid=(seq // R,),
        in_specs=[
            pl.BlockSpec((R, D_MODEL), lambda i: (i, 0)),
            pl.BlockSpec((R, ATT_WIDTH), lambda i: (i, 0)),
            pl.BlockSpec((R, SSM_WIDTH), lambda i: (i, 0)),
            pl.BlockSpec((R, SSM_WIDTH), lambda i: (i, 0)),
            res((1, SSM_WIDTH)), res((SSM_WIDTH, SSM_WIDTH)), res((1, SSM_WIDTH)), res((1, SSM_WIDTH)),
            res((D_MODEL, D_MODEL)), res((1, D_MODEL)), res((N_EXPERTS, D_MODEL)),
            res((D_MODEL, SHARED_HIDDEN)), res((D_MODEL, SHARED_HIDDEN)), res((SHARED_HIDDEN, D_MODEL)),
        ],
        out_specs=[
            pl.BlockSpec((R, D_MODEL), lambda i: (i, 0)),
            pl.BlockSpec((R, D_MODEL), lambda i: (i, 0)),
            pl.BlockSpec((N_EXPERTS, R), lambda i: (0, i)),
        ],
        out_shape=[
            jax.ShapeDtypeStruct((seq, D_MODEL), F32),
            jax.ShapeDtypeStruct((seq, D_MODEL), BF16),
            jax.ShapeDtypeStruct((N_EXPERTS, seq), F32),
        ],
        compiler_params=pltpu.CompilerParams(
            dimension_semantics=("arbitrary",), vmem_limit_bytes=VMEM_LIMIT),
        name="mix_out_shared",
    )(x2, att, y, u, d_skip, w_glu, b_glu, ssm_norm, w_out, norm_ffn, router_wt,
      sh_gate, sh_up, sh_down)


def _route_body(sc_ref, rb_ref, tri_ref, w_ref, slot_ref, cnt_ref):
    scores = sc_ref[...]
    R = scores.shape[1]
    per_group = N_EXPERTS // N_EXPERT_GROUPS
    choice = scores + rb_ref[...]
    c3 = choice.reshape(N_EXPERT_GROUPS, per_group, R)
    within = lax.broadcasted_iota(jnp.int32, c3.shape, 1)
    m1 = jnp.max(c3, axis=1, keepdims=True)
    first = jnp.min(jnp.where(c3 == m1, within, per_group), axis=1, keepdims=True)
    m2 = jnp.max(jnp.where(within == first, -jnp.inf, c3), axis=1, keepdims=True)
    grp = (m1 + m2).reshape(N_EXPERT_GROUPS, R)
    gidx = lax.broadcasted_iota(jnp.int32, grp.shape, 0)
    grank = jnp.zeros(grp.shape, jnp.int32)
    for b in range(N_EXPERT_GROUPS):
        gb = grp[b:b + 1, :]
        grank += ((gb > grp) | ((gb == grp) & (b < gidx))).astype(jnp.int32)
    gmask = grank < TOPK_GROUPS
    emask = jnp.broadcast_to(gmask[:, None, :], c3.shape).reshape(N_EXPERTS, R)
    val = jnp.where(emask, choice, -jnp.inf)
    eidx = lax.broadcasted_iota(jnp.int32, val.shape, 0)
    rank = jnp.zeros(val.shape, jnp.int32)
    for e in range(N_EXPERTS):
        ve = val[e:e + 1, :]
        rank += ((ve > val) | ((ve == val) & (e < eidx))).astype(jnp.int32)
    sel = rank < TOP_K
    w = jnp.where(sel, scores, 0.0)
    w_ref[...] = w / jnp.sum(w, axis=0, keepdims=True) * ROUTED_SCALE
    cum = jnp.dot(jnp.where(sel, 1.0, 0.0).astype(BF16), tri_ref[...], preferred_element_type=F32)
    slot_ref[...] = jnp.where(sel, cum - 1.0, NOT_ROUTED)
    cnt_ref[0] = jnp.broadcast_to(cum[:, R - 1:R], (N_EXPERTS, 128))


def _route(scores_t, router_bias, tri):
    seq = scores_t.shape[1]
    R = MOE_TOKEN_BLOCK
    n_blk = seq // R
    blk = pl.BlockSpec((N_EXPERTS, R), lambda i: (0, i))
    return pl.pallas_call(
        _route_body,
        grid=(n_blk,),
        in_specs=[blk, _resident((N_EXPERTS, 1), lambda i: (0, 0)), _resident((R, R), lambda i: (0, 0))],
        out_specs=[blk, blk, pl.BlockSpec((1, N_EXPERTS, 128), lambda i: (i, 0, 0))],
        out_shape=[jax.ShapeDtypeStruct((N_EXPERTS, seq), F32),
                   jax.ShapeDtypeStruct((N_EXPERTS, seq), F32),
                   jax.ShapeDtypeStruct((n_blk, N_EXPERTS, 128), F32)],
        compiler_params=pltpu.CompilerParams(dimension_semantics=("arbitrary",)),
        name="route",
    )(scores_t, router_bias, tri)


def _moe_tables(cnt):
    n_blk = cnt.shape[0]
    U, RB, SB = MOE_UNIT, MOE_BLOCK_ROWS, MOE_SLOT_BLOCK
    NU = SB // U
    E = N_EXPERTS
    pc = (cnt + U - 1) // U * U
    off = jnp.cumsum(pc, axis=1) - pc
    upc_t = (pc // U).T
    cum_t = jnp.cumsum(upc_t, axis=1)
    units_e = cum_t[:, -1]
    nblk_e = (units_e + NU - 1) // NU
    blk_end = jnp.cumsum(nblk_e)
    blk_start = blk_end - nblk_e
    n_act = blk_end[-1]
    max_blocks = n_blk * (RB // SB) + E
    i = jnp.arange(max_blocks, dtype=jnp.int32)
    active = i < n_act
    count_le = lambda edges, v: jnp.sum((edges <= v[..., None]).astype(jnp.int32), axis=-1)
    last_e = jnp.minimum(count_le(blk_end, n_act - 1), E - 1)
    be = jnp.where(active, jnp.minimum(count_le(blk_end[None, :], i), E - 1), last_e)
    oh_e = be[:, None] == jnp.arange(E, dtype=jnp.int32)[None, :]
    pick_e = lambda v: jnp.sum(jnp.where(oh_e, v[None, :], 0), axis=1)
    pick_e2 = lambda m: jnp.sum(jnp.where(oh_e[:, :, None], m[None, :, :], 0), axis=1)
    bstart_i = pick_e(blk_start)
    first = active & (i == bstart_i)
    local = (i - bstart_i)[:, None] * NU + jnp.arange(NU, dtype=jnp.int32)[None, :]
    valid = active[:, None] & (local < pick_e(units_e)[:, None])
    cum_i = pick_e2(cum_t)
    b_of = jnp.minimum(count_le(cum_i[:, None, :], local), n_blk - 1)
    oh_b = b_of[:, :, None] == jnp.arange(n_blk, dtype=jnp.int32)[None, None, :]
    pick_b = lambda m: jnp.sum(jnp.where(oh_b, m[:, None, :], 0), axis=2)
    seg_start = pick_b(cum_i) - pick_b(pick_e2(upc_t))
    unit = (b_of * RB + pick_b(pick_e2(off.T))) // U + (local - seg_start)
    spare = n_blk * RB // U
    src = jnp.where(valid, unit, spare)
    dst = jnp.where(valid, unit,
                    spare + (1 + i % 2)[:, None] * NU + jnp.arange(NU, dtype=jnp.int32)[None, :])
    short = active & (pick_e(units_e) - (i - bstart_i) * NU <= NU // 2)
    i32 = lambda a: a.astype(jnp.int32)
    flags = i32(first) + 2 * i32(short)
    return (off, pc, i32(src.reshape(-1)), i32(dst.reshape(-1)), i32(be), flags,
            i32(n_act.reshape(1)))


def _split_hi_lo(pos):
    hi = jnp.floor(pos * (1.0 / 64.0))
    return hi.astype(BF16), (pos - 64.0 * hi).astype(BF16)


def _dispatch_body(hn_ref, slot_ref, offc_ref, offl_ref, pcl_ref, xs_ref, *, n_blk):
    b = pl.program_id(0)
    RB, TB = MOE_BLOCK_ROWS, MOE_TOKEN_BLOCK

    @pl.when(b == n_blk)
    def _():
        xs_ref[...] = jnp.zeros(xs_ref.shape, BF16)

    @pl.when(b < n_blk)
    def _():
        pos = slot_ref[...] + offc_ref[0]
        hi, lo = _split_hi_lo(pos)
        r = lax.broadcasted_iota(jnp.int32, (RB, N_EXPERTS), 0).astype(F32)
        off = offl_ref[0]
        owner = jnp.where((r >= off) & (r < off + pcl_ref[0]), 1.0, 0.0).astype(BF16)
        p = (64.0 * jnp.dot(owner, hi, preferred_element_type=F32)
             + jnp.dot(owner, lo, preferred_element_type=F32))
        rr = lax.broadcasted_iota(jnp.int32, (RB, TB), 0).astype(F32)
        onehot = jnp.where(jnp.abs(p - rr) < 0.5, 1.0, 0.0).astype(BF16)
        x = hn_ref[...]
        C = MOE_SLOT_BLOCK
        for c in range(RB // C):
            xs_ref[0, c * C:(c + 1) * C, :] = jnp.dot(
                onehot[c * C:(c + 1) * C], x, preferred_element_type=F32).astype(BF16)


def _dispatch(hn, slots, off, pc):
    n_blk = off.shape[0]
    RB, TB = MOE_BLOCK_ROWS, MOE_TOKEN_BLOCK
    clamp = lambda b: jnp.minimum(b, n_blk - 1)
    return pl.pallas_call(
        functools.partial(_dispatch_body, n_blk=n_blk),
        grid=(n_blk + 1,),
        in_specs=[
            pl.BlockSpec((TB, D_MODEL), lambda b: (clamp(b), 0)),
            pl.BlockSpec((N_EXPERTS, TB), lambda b: (0, clamp(b))),
            pl.BlockSpec((1, N_EXPERTS, 1), lambda b: (clamp(b), 0, 0)),
            pl.BlockSpec((1, 1, N_EXPERTS), lambda b: (clamp(b), 0, 0)),
            pl.BlockSpec((1, 1, N_EXPERTS), lambda b: (clamp(b), 0, 0)),
        ],
        out_specs=pl.BlockSpec((1, RB, D_MODEL), lambda b: (b, 0, 0)),
        out_shape=jax.ShapeDtypeStruct((n_blk + 1, RB, D_MODEL), BF16),
        compiler_params=pltpu.CompilerParams(
            dimension_semantics=("arbitrary",), vmem_limit_bytes=VMEM_LIMIT),
        name="moe_dispatch",
    )(hn, slots, off[:, :, None], off[:, None, :], pc[:, None, :])


def _expert_body(src_ref, dst_ref, bexp_ref, first_ref, nact_ref,
                 xs_hbm, wg_ref, wu_ref, wd_ref, ys_hbm,
                 xbuf, ybuf, wgb, wub, wdb, sem_in, sem_out):
    del bexp_ref
    i = pl.program_id(0)
    n_act = nact_ref[0]
    cur = lax.rem(i, 2)
    U = MOE_UNIT
    NU = MOE_SLOT_BLOCK // U

    def in_copy(blk, buf, u):
        return pltpu.make_async_copy(xs_hbm.at[src_ref[blk * NU + u]],
                                     xbuf.at[buf, pl.ds(u * U, U)], sem_in.at[buf])

    def out_copy(blk, buf, u):
        return pltpu.make_async_copy(ybuf.at[buf, pl.ds(u * U, U)],
                                     ys_hbm.at[dst_ref[blk * NU + u]], sem_out.at[buf])

    @pl.when(i == 0)
    def _():
        for u in range(NU):
            in_copy(0, 0, u).start()

    @pl.when(i < n_act)
    def _():
        @pl.when(i + 1 < n_act)
        def _():
            for u in range(NU):
                in_copy(i + 1, 1 - cur, u).start()

        @pl.when(first_ref[i] % 2 == 1)
        def _():
            wgb[...] = wg_ref[0].astype(BF16)
            wub[...] = wu_ref[0].astype(BF16)
            wdb[...] = wd_ref[0].astype(BF16)

        for u in range(NU):
            in_copy(i, cur, u).wait()

        @pl.when(i >= 2)
        def _():
            for u in range(NU):
                out_copy(i - 2, cur, u).wait()

        def swiglu_rows(rows):
            x = xbuf[cur, :rows]
            g = jnp.dot(x, wgb[...], preferred_element_type=F32)
            up = jnp.dot(x, wub[...], preferred_element_type=F32)
            act = (jax.nn.silu(g) * up).astype(BF16)
            ybuf[cur, :rows] = jnp.dot(act, wdb[...], preferred_element_type=F32).astype(BF16)

        half = MOE_SLOT_BLOCK // 2

        @pl.when(first_ref[i] < 2)
        def _():
            swiglu_rows(MOE_SLOT_BLOCK)

        @pl.when(first_ref[i] >= 2)
        def _():
            swiglu_rows(half)
            ybuf[cur, half:] = jnp.zeros((half, D_MODEL), BF16)

        for u in range(NU):
            out_copy(i, cur, u).start()

        @pl.when(i == n_act - 1)
        def _():
            for u in range(NU):
                out_copy(i, cur, u).wait()

            @pl.when(i >= 1)
            def _():
                for u in range(NU):
                    out_copy(i - 1, 1 - cur, u).wait()


def _experts(xs, src, dst, bexp, first, n_act, wg, wu, wd):
    U, SB = MOE_UNIT, MOE_SLOT_BLOCK
    n_units = xs.shape[0] * xs.shape[1] // U
    max_blocks = bexp.shape[0]
    unit_view = lambda a: a.reshape(n_units, U, D_MODEL)
    wspec = lambda shape: pl.BlockSpec((1,) + shape, lambda i, src, dst, bexp, first, nact: (bexp[i], 0, 0))
    grid_spec = pltpu.PrefetchScalarGridSpec(
        num_scalar_prefetch=5,
        grid=(max_blocks,),
        in_specs=[
            pl.BlockSpec(memory_space=pl.ANY),
            wspec((D_MODEL, EXPERT_HIDDEN)), wspec((D_MODEL, EXPERT_HIDDEN)), wspec((EXPERT_HIDDEN, D_MODEL)),
        ],
        out_specs=pl.BlockSpec(memory_space=pl.ANY),
        scratch_shapes=[
            pltpu.VMEM((2, SB, D_MODEL), BF16), pltpu.VMEM((2, SB, D_MODEL), BF16),
            pltpu.VMEM((D_MODEL, EXPERT_HIDDEN), BF16), pltpu.VMEM((D_MODEL, EXPERT_HIDDEN), BF16),
            pltpu.VMEM((EXPERT_HIDDEN, D_MODEL), BF16),
            pltpu.SemaphoreType.DMA((2,)), pltpu.SemaphoreType.DMA((2,)),
        ],
    )
    ys = pl.pallas_call(
        _expert_body,
        grid_spec=grid_spec,
        out_shape=jax.ShapeDtypeStruct((n_units, U, D_MODEL), BF16),
        input_output_aliases={5: 0},
        compiler_params=pltpu.CompilerParams(
            dimension_semantics=("arbitrary",), vmem_limit_bytes=VMEM_LIMIT),
        name="moe_experts",
    )(src, dst, bexp, first, n_act, unit_view(xs), wg, wu, wd)
    return ys.reshape(xs.shape)


def _combine_body(h_ref, ys_ref, slot_ref, w_ref, offl_ref, offc_ref, pcc_ref, o_ref):
    RB, TB = MOE_BLOCK_ROWS, MOE_TOKEN_BLOCK
    pos = slot_ref[...] + offl_ref[0]
    hi, lo = _split_hi_lo(pos)
    r = lax.broadcasted_iota(jnp.int32, (N_EXPERTS, RB), 1).astype(F32)
    off = offc_ref[0]
    owner = jnp.where((r >= off) & (r < off + pcc_ref[0]), 1.0, 0.0).astype(BF16)
    p = (64.0 * jnp.dot(hi, owner, preferred_element_type=F32)
         + jnp.dot(lo, owner, preferred_element_type=F32))
    wr = jnp.dot(w_ref[...].astype(BF16), owner, preferred_element_type=F32)
    rr = lax.broadcasted_iota(jnp.int32, (TB, RB), 1).astype(F32)
    gather_w = jnp.where(jnp.abs(p - rr) < 0.5, wr, 0.0).astype(BF16)
    o_ref[...] = h_ref[...] + jnp.dot(gather_w, ys_ref[0], preferred_element_type=F32)


def _combine(h, ys, slots_tok, w_tok, off, pc):
    n_blk = off.shape[0]
    RB, TB = MOE_BLOCK_ROWS, MOE_TOKEN_BLOCK
    seq = h.shape[0]
    return pl.pallas_call(
        _combine_body,
        grid=(n_blk,),
        in_specs=[
            pl.BlockSpec((TB, D_MODEL), lambda b: (b, 0)),
            pl.BlockSpec((1, RB, D_MODEL), lambda b: (b, 0, 0)),
            pl.BlockSpec((TB, N_EXPERTS), lambda b: (b, 0)),
            pl.BlockSpec((TB, N_EXPERTS), lambda b: (b, 0)),
            pl.BlockSpec((1, 1, N_EXPERTS), lambda b: (b, 0, 0)),
            pl.BlockSpec((1, N_EXPERTS, 1), lambda b: (b, 0, 0)),
            pl.BlockSpec((1, N_EXPERTS, 1), lambda b: (b, 0, 0)),
        ],
        out_specs=pl.BlockSpec((TB, D_MODEL), lambda b: (b, 0)),
        out_shape=jax.ShapeDtypeStruct((seq, D_MODEL), F32),
        compiler_params=pltpu.CompilerParams(
            dimension_semantics=("arbitrary",), vmem_limit_bytes=VMEM_LIMIT),
        name="moe_combine",
    )(h, ys, slots_tok, w_tok, off[:, None, :], off[:, :, None], pc[:, :, None])


def kernel(x, meta_tokens, rel_bias, norm_mix, w_in, q_norm, k_norm, lam_q1, lam_k1, lam_q2, lam_k2, subln, ssm_a_re, ssm_a_im, ssm_log_step, ssm_b_re, ssm_b_im, ssm_c_re, ssm_c_im, ssm_d, w_glu, b_glu, ssm_norm, w_out, norm_ffn, router_w, router_bias, w_gate, w_up, w_down, shared_gate, shared_up, shared_down):
    batch, seq, d = x.shape
    assert batch == 1 and d == D_MODEL and seq % ROW_BLOCK == 0 and seq % ATT_BLOCK == 0
    assert norm_mix.shape[0] == 1, "single layer"
    x2 = x.reshape(seq, d)
    meta_pad = jnp.zeros((ROW_BLOCK, d), F32).at[:N_META].set(meta_tokens.astype(F32))
    seg = jnp.kron(jnp.eye(QK_WIDTH // HEAD_DIM, dtype=F32),
                   jnp.full((HEAD_DIM, HEAD_DIM), 1.0 / HEAD_DIM, F32)).astype(BF16)
    qg = jnp.tile(q_norm[0].astype(F32), QK_WIDTH // HEAD_DIM)[None] * (HEAD_DIM ** -0.5)
    kg = jnp.tile(k_norm[0].astype(F32), QK_WIDTH // HEAD_DIM)[None]

    proj, u = _inproj(x2, meta_pad, norm_mix[0][None], w_in[0].astype(BF16), seg, qg, kg)

    score_bound = (BOUND_MARGIN * HEAD_DIM ** 0.5 * jnp.max(jnp.abs(q_norm[0].astype(F32)))
                   * jnp.max(jnp.abs(k_norm[0].astype(F32)))
                   + jnp.max(jnp.abs(rel_bias.astype(F32)))).reshape(1)
    att = _attention(proj, rel_bias.astype(F32), score_bound, lam_q1[0][None], lam_k1[0][None],
                     lam_q2[0][None], lam_k2[0][None], subln[0][None], seq)

    n_rows = proj.shape[0] // S5_CHUNK
    n_chunks = seq // S5_CHUNK
    perm = _lane_regroup_matrix()
    u_chunks = _to_chunks(u.reshape(n_rows, S5_CHUNK, SSM_WIDTH), perm)
    pp, mm, qq, lam16 = _s5_prep(ssm_a_re[0].astype(F32), ssm_a_im[0].astype(F32),
                                 ssm_log_step[0].astype(F32), ssm_b_re[0].astype(F32),
                                 ssm_b_im[0].astype(F32), ssm_c_re[0].astype(F32),
                                 ssm_c_im[0].astype(F32))
    y_chunks = _s5(u_chunks, pp, mm, qq, lam16, n_chunks)
    y = _from_chunks(y_chunks, perm.T).reshape(seq, SSM_WIDTH)

    h, hn, scores_t = _mix_out(
        x2, att, y, u, ssm_d[0][None].astype(F32), w_glu[0].astype(BF16), b_glu[0][None].astype(F32),
        ssm_norm[0][None].astype(F32), w_out[0].astype(BF16), norm_ffn[0][None].astype(F32),
        router_w[0].astype(F32).T, shared_gate[0].astype(BF16), shared_up[0].astype(BF16),
        shared_down[0].astype(BF16))

    tb = MOE_TOKEN_BLOCK
    tri = (jnp.arange(tb)[:, None] <= jnp.arange(tb)[None, :]).astype(BF16)
    wts_t, slots_t, cnt = _route(scores_t, router_bias[0].astype(F32)[:, None], tri)
    off, pc, src, dst, bexp, first, n_act = _moe_tables(cnt[:, :, 0].astype(jnp.int32))
    off_f, pc_f = off.astype(F32), pc.astype(F32)
    xs = _dispatch(hn, slots_t, off_f, pc_f)
    ys = _experts(xs, src, dst, bexp, first, n_act, w_gate[0], w_up[0], w_down[0])
    out = _combine(h, ys, slots_t.T, wts_t.T, off_f, pc_f)
    return out.reshape(batch, seq, d)
```

```python
import functools
import math

import jax
import jax.numpy as jnp
from jax import lax
from jax.experimental import pallas as pl
from jax.experimental.pallas import tpu as pltpu

F32 = jnp.float32
BF16 = jnp.bfloat16

D_MODEL = 2048
N_META = 16
ATT_WIDTH = 1024
SSM_WIDTH = 1024
HEAD_DIM = 64
V_DIM = 128
HEADS = 8
QK_WIDTH = 1024
IN_WIDTH = 4096
V_OFFSET = 2 * QK_WIDTH
PROJ_WIDTH = V_OFFSET + 2 * ATT_WIDTH
SSM_GROUP = 16
SSM_GROUPS = 64
SSM_STATE = 64
N_BUCKETS = 32
MAX_DISTANCE = 128
N_EXPERTS = 64
TOP_K = 8
N_EXPERT_GROUPS = 8
TOPK_GROUPS = 4
EXPERT_HIDDEN = 512
SHARED_HIDDEN = 512
ROUTED_SCALE = 2.5
EPS = 1e-6
LAMBDA_INIT = 0.8 - 0.6 * math.exp(-0.3 * 0)

ROW_BLOCK = 512
ATT_BLOCK = 512
S5_CHUNK = 16
S5_GROUPS_PER_STEP = 8
S5_OCTET_WIDTH = S5_GROUPS_PER_STEP * S5_CHUNK * SSM_GROUP
MOE_TOKEN_BLOCK = 256
MOE_UNIT = 16
MOE_SLOT_BLOCK = 512
MOE_BLOCK_ROWS = -(-(MOE_TOKEN_BLOCK * TOP_K + N_EXPERTS * (MOE_UNIT - 1)) // MOE_SLOT_BLOCK) * MOE_SLOT_BLOCK
NOT_ROUTED = -1e6
MIX_ROW_BLOCK = 256
NEG_BIG = -1e30
MAX_EXP_RANGE = 80.0
BOUND_MARGIN = 1.02
VMEM_LIMIT = 56 * 1024 * 1024


def _resident(shape, index_map):
    return pl.BlockSpec(shape, index_map, pipeline_mode=pl.Buffered(1))


def _inproj_body(x_ref, meta_ref, g_ref, w_ref, seg_ref, qg_ref, kg_ref, o_ref, u_ref, *, n_xblk):
    i = pl.program_id(0)

    def run(src_ref):
        xv = src_ref[...]
        ms = jnp.mean(xv * xv, axis=-1, keepdims=True)
        hn = (xv * lax.rsqrt(ms + EPS) * g_ref[...]).astype(BF16)
        for s in range(4):
            ps = jnp.dot(hn, w_ref[:, s * 1024:(s + 1) * 1024], preferred_element_type=F32)
            if s < 2:
                gain = qg_ref if s == 0 else kg_ref
                msq = jnp.dot((ps * ps).astype(BF16), seg_ref[...], preferred_element_type=F32)
                ps = ps * lax.rsqrt(msq + EPS) * gain[...]
            if s == 3:
                u_ref[...] = ps
                continue
            pb = ps.astype(BF16)
            if s < 2:
                o_ref[:, s * 1024:(s + 1) * 1024] = pb
            else:
                lane = lax.broadcasted_iota(jnp.int32, (pb.shape[0], V_DIM), 1)
                ones_col = jnp.where(lane == 0, 1.0, 0.0).astype(BF16)
                for hh in range(HEADS):
                    base = V_OFFSET + hh * 2 * V_DIM
                    o_ref[:, base:base + V_DIM] = pb[:, hh * V_DIM:(hh + 1) * V_DIM]
                    o_ref[:, base + V_DIM:base + 2 * V_DIM] = ones_col

    @pl.when(i < n_xblk)
    def _():
        run(x_ref)

    @pl.when(i == n_xblk)
    def _():
        run(meta_ref)


def _inproj(x2, meta_pad, gain, w_bf, seg, qg, kg):
    seq = x2.shape[0]
    n_xblk = seq // ROW_BLOCK
    rows = seq + ROW_BLOCK
    return pl.pallas_call(
        functools.partial(_inproj_body, n_xblk=n_xblk),
        grid=(n_xblk + 1,),
        in_specs=[
            pl.BlockSpec((ROW_BLOCK, D_MODEL), lambda i: (jnp.minimum(i, n_xblk - 1), 0)),
            _resident((ROW_BLOCK, D_MODEL), lambda i: (0, 0)),
            _resident((1, D_MODEL), lambda i: (0, 0)),
            _resident((D_MODEL, IN_WIDTH), lambda i: (0, 0)),
            _resident((QK_WIDTH, QK_WIDTH), lambda i: (0, 0)),
            _resident((1, QK_WIDTH), lambda i: (0, 0)),
            _resident((1, QK_WIDTH), lambda i: (0, 0)),
        ],
        out_specs=[pl.BlockSpec((ROW_BLOCK, PROJ_WIDTH), lambda i: (i, 0)),
                   pl.BlockSpec((ROW_BLOCK, SSM_WIDTH), lambda i: (i, 0))],
        out_shape=[jax.ShapeDtypeStruct((rows, PROJ_WIDTH), BF16),
                   jax.ShapeDtypeStruct((rows, SSM_WIDTH), F32)],
        compiler_params=pltpu.CompilerParams(
            dimension_semantics=("arbitrary",), vmem_limit_bytes=VMEM_LIMIT),
        name="inproj",
    )(x2, meta_pad, gain, w_bf, seg, qg, kg)


def _t5_bias(rel, tab_ref, h):
    half = N_BUCKETS // 2
    exact = half // 2
    n = jnp.abs(rel)
    nf = jnp.maximum(n, 1).astype(F32)
    large = exact + (jnp.log(nf / exact) / math.log(MAX_DISTANCE / exact) * (half - exact)).astype(jnp.int32)
    large = jnp.minimum(large, half - 1)
    bucket = jnp.where(rel > 0, half, 0) + jnp.where(n < exact, n, large)
    out = jnp.zeros(rel.shape, F32)
    for b in range(N_BUCKETS):
        out = jnp.where(bucket == b, tab_ref[b, h], out)
    return out


def _attn_body(tab_ref, bound_ref, q_ref, k_ref, v_ref, lq1_ref, lk1_ref, lq2_ref, lk2_ref,
               subln_ref, o_ref, bias_ref, acc1_ref, acc2_ref, m1_ref, m2_ref, *, n_main):
    T = ATT_BLOCK
    h = pl.program_id(0)
    qi = pl.program_id(1)
    bound = bound_ref[0]

    @pl.when(qi == 0)
    def _():
        offsets = (-T, 0, T, -N_META, -N_META - T, -2 * T, 2 * T)
        for kind, off in enumerate(offsets):
            masked = kind in (3, 4)

            def rows(rc, carry, off=off, masked=masked, kind=kind):
                r0 = pl.multiple_of(rc * 8, 8)
                r = r0 + lax.broadcasted_iota(jnp.int32, (8, T), 0)
                c = lax.broadcasted_iota(jnp.int32, (8, T), 1)
                b = _t5_bias(off + c - r, tab_ref, h) - bound
                if masked:
                    b = jnp.where(c < N_META, b, NEG_BIG)
                bias_ref[kind, pl.ds(r0, 8), :] = b
                return carry

            lax.fori_loop(0, T // 8, rows, 0)

    acc1_ref[...] = jnp.zeros(acc1_ref.shape, F32)
    acc2_ref[...] = jnp.zeros(acc2_ref.shape, F32)

    q = q_ref[...]
    q1 = q[:, :HEAD_DIM]
    q2 = q[:, HEAD_DIM:]
    nt = (((1,), (1,)), ((), ()))

    def tile(ki):
        koff = pl.multiple_of(ki * T, T)
        kb = k_ref[pl.ds(koff, T), :]
        va = v_ref[pl.ds(koff, T), :]
        d = ki - qi
        kind = jnp.where(ki == n_main, jnp.where(qi == 0, 3, 4),
                         jnp.where(d <= -2, 5, jnp.where(d >= 2, 6, d + 1)))
        return kb, va, bias_ref[kind]

    def bounded_step(ki, carry):
        kb, va, bias = tile(ki)
        s1 = lax.dot_general(q1, kb[:, :HEAD_DIM], nt, preferred_element_type=F32) + bias
        acc1_ref[...] += jnp.dot(jnp.exp(s1).astype(BF16), va, preferred_element_type=F32)
        s2 = lax.dot_general(q2, kb[:, HEAD_DIM:], nt, preferred_element_type=F32) + bias
        acc2_ref[...] += jnp.dot(jnp.exp(s2).astype(BF16), va, preferred_element_type=F32)
        return carry

    def online_map(s, va, m_ref, acc_ref):
        m_old = m_ref[...]
        m_new = jnp.maximum(m_old, jnp.max(s, axis=-1, keepdims=True))
        p = jnp.exp(s - m_new).astype(BF16)
        acc_ref[...] = (jnp.exp(m_old - m_new) * acc_ref[...]
                        + jnp.dot(p, va, preferred_element_type=F32))
        m_ref[...] = m_new

    def online_step(ki, carry):
        kb, va, bias = tile(ki)
        s1 = lax.dot_general(q1, kb[:, :HEAD_DIM], nt, preferred_element_type=F32) + bias
        online_map(s1, va, m1_ref, acc1_ref)
        s2 = lax.dot_general(q2, kb[:, HEAD_DIM:], nt, preferred_element_type=F32) + bias
        online_map(s2, va, m2_ref, acc2_ref)
        return carry

    no_running_max = 2.0 * bound <= MAX_EXP_RANGE

    @pl.when(no_running_max)
    def _():
        unroll = next(u for u in (33, 11, 3, 2, 1) if (n_main + 1) % u == 0)
        lax.fori_loop(0, n_main + 1, bounded_step, 0, unroll=unroll)

    @pl.when(jnp.logical_not(no_running_max))
    def _():
        m1_ref[...] = jnp.full(m1_ref.shape, -jnp.inf, F32)
        m2_ref[...] = jnp.full(m2_ref.shape, -jnp.inf, F32)
        lax.fori_loop(0, n_main + 1, online_step, 0)

    lam = (jnp.exp(jnp.sum(lq1_ref[...] * lk1_ref[...], axis=-1, keepdims=True))
           - jnp.exp(jnp.sum(lq2_ref[...] * lk2_ref[...], axis=-1, keepdims=True))
           + LAMBDA_INIT)
    a1 = acc1_ref[...]
    a2 = acc2_ref[...]
    o = (a1[:, :V_DIM] / a1[:, V_DIM:V_DIM + 1]
         - lam * (a2[:, :V_DIM] / a2[:, V_DIM:V_DIM + 1]))
    ms = jnp.mean(o * o, axis=-1, keepdims=True)
    o = o * lax.rsqrt(ms + EPS) * subln_ref[...] * (1.0 - LAMBDA_INIT)
    o_ref[...] = o.astype(BF16)


def _attention(proj, rel_bias, score_bound, lq1, lk1, lq2, lk2, subln, seq):
    T = ATT_BLOCK
    n_main = seq // T
    rows = proj.shape[0]
    vec64 = lambda: _resident((1, HEAD_DIM), lambda h, qi: (0, 0))
    return pl.pallas_call(
        functools.partial(_attn_body, n_main=n_main),
        grid=(HEADS, n_main),
        in_specs=[
            pl.BlockSpec(memory_space=pltpu.SMEM),
            pl.BlockSpec(memory_space=pltpu.SMEM),
            pl.BlockSpec((T, 2 * HEAD_DIM), lambda h, qi: (qi, h)),
            pl.BlockSpec((rows, 2 * HEAD_DIM), lambda h, qi: (0, HEADS + h)),
            pl.BlockSpec((rows, 2 * V_DIM), lambda h, qi: (0, V_OFFSET // (2 * V_DIM) + h)),
            vec64(), vec64(), vec64(), vec64(),
            _resident((1, V_DIM), lambda h, qi: (0, 0)),
        ],
        out_specs=pl.BlockSpec((T, V_DIM), lambda h, qi: (qi, h)),
        out_shape=jax.ShapeDtypeStruct((seq, ATT_WIDTH), BF16),
        scratch_shapes=[
            pltpu.VMEM((7, T, T), F32),
            pltpu.VMEM((T, 2 * V_DIM), F32), pltpu.VMEM((T, 2 * V_DIM), F32),
            pltpu.VMEM((T, 1), F32), pltpu.VMEM((T, 1), F32),
        ],
        compiler_params=pltpu.CompilerParams(
            dimension_semantics=("arbitrary", "arbitrary"), vmem_limit_bytes=VMEM_LIMIT),
        name="diff_attention",
    )(rel_bias, score_bound, proj, proj, proj, lq1, lk1, lq2, lk2, subln)


def _lane_regroup_matrix():
    out_lane = jnp.arange(S5_OCTET_WIDTH, dtype=jnp.int32)
    g8 = out_lane // (S5_CHUNK * SSM_GROUP)
    j = (out_lane % (S5_CHUNK * SSM_GROUP)) // SSM_GROUP
    p = out_lane % SSM_GROUP
    in_lane = j * 128 + g8 * SSM_GROUP + p
    return (jnp.arange(S5_OCTET_WIDTH, dtype=jnp.int32)[:, None] == in_lane[None, :]).astype(BF16)


def _to_chunks_body(u_ref, perm_ref, o_ref):
    x = jnp.concatenate([u_ref[:, j, :].astype(BF16) for j in range(S5_CHUNK)], axis=1)
    r = jnp.dot(x, perm_ref[...], preferred_element_type=F32)
    w = S5_CHUNK * SSM_GROUP
    for g8 in range(S5_GROUPS_PER_STEP):
        o_ref[g8] = r[:, g8 * w:(g8 + 1) * w].astype(BF16)


def _to_chunks(u3, perm):
    n_rows = u3.shape[0]
    w = S5_CHUNK * SSM_GROUP
    return pl.pallas_call(
        _to_chunks_body,
        grid=(SSM_GROUPS // S5_GROUPS_PER_STEP,),
        in_specs=[pl.BlockSpec((n_rows, S5_CHUNK, 128), lambda o: (0, 0, o)),
                  _resident((S5_OCTET_WIDTH, S5_OCTET_WIDTH), lambda o: (0, 0))],
        out_specs=pl.BlockSpec((S5_GROUPS_PER_STEP, n_rows, w), lambda o: (o, 0, 0)),
        out_shape=jax.ShapeDtypeStruct((SSM_GROUPS, n_rows, w), BF16),
        compiler_params=pltpu.CompilerParams(
            dimension_semantics=("arbitrary",), vmem_limit_bytes=VMEM_LIMIT),
        name="s5_to_chunks",
    )(u3, perm)


def _from_chunks_body(y_ref, perm_t_ref, o_ref):
    x = jnp.concatenate([y_ref[g8] for g8 in range(S5_GROUPS_PER_STEP)], axis=1)
    r = jnp.dot(x, perm_t_ref[...], preferred_element_type=F32)
    for j in range(S5_CHUNK):
        o_ref[:, j, :] = r[:, j * 128:(j + 1) * 128]


def _from_chunks(y_chunks, perm_t):
    n_chunks = y_chunks.shape[1]
    w = S5_CHUNK * SSM_GROUP
    return pl.pallas_call(
        _from_chunks_body,
        grid=(SSM_GROUPS // S5_GROUPS_PER_STEP,),
        in_specs=[pl.BlockSpec((S5_GROUPS_PER_STEP, n_chunks, w), lambda o: (o, 0, 0)),
                  _resident((S5_OCTET_WIDTH, S5_OCTET_WIDTH), lambda o: (0, 0))],
        out_specs=pl.BlockSpec((n_chunks, S5_CHUNK, 128), lambda o: (0, 0, o)),
        out_shape=jax.ShapeDtypeStruct((n_chunks, S5_CHUNK, SSM_WIDTH), F32),
        compiler_params=pltpu.CompilerParams(
            dimension_semantics=("arbitrary",), vmem_limit_bytes=VMEM_LIMIT),
        name="s5_from_chunks",
    )(y_chunks, perm_t)


def _s5_prep(a_re, a_im, log_step, b_re, b_im, c_re, c_im):
    C = S5_CHUNK
    dt = jnp.exp(log_step)[..., None]
    decay = jnp.exp(a_re * dt)
    ab_re = decay * jnp.cos(a_im * dt)
    ab_im = decay * jnp.sin(a_im * dt)
    den = a_re * a_re + a_im * a_im
    zr = ab_re - 1.0
    f_re = (zr * a_re + ab_im * a_im) / den
    f_im = (ab_im * a_re - zr * a_im) / den
    bb_re = f_re[..., None] * b_re - f_im[..., None] * b_im
    bb_im = f_re[..., None] * b_im + f_im[..., None] * b_re
    pr, pi = jnp.ones_like(ab_re), jnp.zeros_like(ab_re)
    pw_re, pw_im = [pr], [pi]
    for _ in range(C):
        pr, pi = pr * ab_re - pi * ab_im, pr * ab_im + pi * ab_re
        pw_re.append(pr)
        pw_im.append(pi)
    G, P, W = SSM_GROUPS, SSM_GROUP, C * SSM_GROUP
    pw_re = jnp.stack(pw_re, axis=-1)
    pw_im = jnp.stack(pw_im, axis=-1)
    ct_re = c_re.transpose(0, 1, 3, 2)
    ct_im = c_im.transpose(0, 1, 3, 2)
    cp_re = ct_re[:, :, :, None, :] * pw_re[..., None] - ct_im[:, :, :, None, :] * pw_im[..., None]
    cp_im = ct_re[:, :, :, None, :] * pw_im[..., None] + ct_im[:, :, :, None, :] * pw_re[..., None]
    bt_re = bb_re.transpose(0, 1, 3, 2)
    bt_im = bb_im.transpose(0, 1, 3, 2)

    def response(d):
        prod = (cp_re[d][:, None, :, :C, :] * bt_re[d][:, :, :, None, None]
                - cp_im[d][:, None, :, :C, :] * bt_im[d][:, :, :, None, None])
        return jnp.sum(prod, axis=2)

    ext_f = jnp.pad(response(0).reshape(G, P, W), ((0, 0), (0, 0), (W, 0)))
    ext_r = jnp.pad(response(1)[:, :, ::-1, :].reshape(G, P, W), ((0, 0), (0, 0), (0, W)))
    mm = jnp.stack([ext_f[:, :, W - P * j:2 * W - P * j] + ext_r[:, :, (C - 1 - j) * P:(C - 1 - j) * P + W]
                    for j in range(C)], axis=1).reshape(G, W, W)

    def in_mat(d, reverse_powers):
        pr_ = pw_re[d][:, :, :C].transpose(0, 2, 1)
        pi_ = pw_im[d][:, :, :C].transpose(0, 2, 1)
        if reverse_powers:
            pr_, pi_ = pr_[:, ::-1], pi_[:, ::-1]
        re = pr_[:, :, None, :] * bt_re[d][:, None] - pi_[:, :, None, :] * bt_im[d][:, None]
        im = pr_[:, :, None, :] * bt_im[d][:, None] + pi_[:, :, None, :] * bt_re[d][:, None]
        return re, im

    pf_re, pf_im = in_mat(0, True)
    pr_re, pr_im = in_mat(1, False)
    pp = jnp.concatenate([pf_re, pr_re, pf_im, pr_im], axis=-1).reshape(G, W, 4 * SSM_STATE)

    qq = jnp.concatenate([cp_re[0][:, :, 1:], cp_re[1][:, :, :0:-1],
                          -cp_im[0][:, :, 1:], -cp_im[1][:, :, :0:-1]], axis=1).reshape(G, 4 * SSM_STATE, W)
    lam16 = jnp.stack([jnp.concatenate([pw_re[0][:, :, C], pw_re[1][:, :, C]], axis=-1),
                       jnp.concatenate([pw_im[0][:, :, C], pw_im[1][:, :, C]], axis=-1)], axis=0)
    return pp.astype(BF16), mm.astype(BF16), qq.astype(BF16), lam16


def _s5_body(u_ref, pp_ref, mm_ref, qq_ref, lam_ref, y_ref,
             zre_ref, zim_ref, are_ref, aim_ref, bre_ref, bim_ref, *, n_chunks):
    GS = S5_GROUPS_PER_STEP
    NS = 2 * SSM_STATE
    for gi in range(GS):
        z = jnp.dot(u_ref[gi], pp_ref[gi], preferred_element_type=F32)
        zre_ref[:, gi, :] = z[:, :NS]
        zim_ref[:, gi, :] = z[:, NS:]

    ar = lam_ref[0]
    ai = lam_ref[1]
    fwd = lax.broadcasted_iota(jnp.int32, (GS, NS), 1) < SSM_STATE
    sre0 = jnp.where(fwd, zre_ref[n_chunks], 0.0)
    sim0 = jnp.where(fwd, zim_ref[n_chunks], 0.0)

    def scan_step(k, carry):
        sre, sim = carry
        kr = n_chunks - 1 - k
        are_ref[k] = sre
        aim_ref[k] = sim
        bre_ref[kr] = sre
        bim_ref[kr] = sim
        zr = jnp.where(fwd, zre_ref[k], zre_ref[kr])
        zi = jnp.where(fwd, zim_ref[k], zim_ref[kr])
        return ar * sre - ai * sim + zr, ar * sim + ai * sre + zi

    lax.fori_loop(0, n_chunks, scan_step, (sre0, sim0))

    fwd_rows = lax.broadcasted_iota(jnp.int32, (n_chunks, NS), 1) < SSM_STATE
    for gi in range(GS):
        s_re = jnp.where(fwd_rows, are_ref[:, gi, :], bre_ref[:, gi, :])
        s_im = jnp.where(fwd_rows, aim_ref[:, gi, :], bim_ref[:, gi, :])
        scat = jnp.concatenate([s_re, s_im], axis=1).astype(BF16)
        y = (jnp.dot(u_ref[gi, :n_chunks, :], mm_ref[gi], preferred_element_type=F32)
             + jnp.dot(scat, qq_ref[gi], preferred_element_type=F32))
        y_ref[gi] = y.astype(BF16)


def _s5(u_chunks, pp, mm, qq, lam16, n_chunks):
    GS = S5_GROUPS_PER_STEP
    n_rows = u_chunks.shape[1]
    W = S5_CHUNK * SSM_GROUP
    NS = 2 * SSM_STATE
    mat = lambda: pl.BlockSpec((GS, W, W), lambda g: (g, 0, 0))
    return pl.pallas_call(
        functools.partial(_s5_body, n_chunks=n_chunks),
        grid=(SSM_GROUPS // GS,),
        in_specs=[
            pl.BlockSpec((GS, n_rows, W), lambda g: (g, 0, 0)),
            mat(), mat(), mat(),
            pl.BlockSpec((2, GS, NS), lambda g: (0, g, 0)),
        ],
        out_specs=pl.BlockSpec((GS, n_chunks, W), lambda g: (g, 0, 0)),
        out_shape=jax.ShapeDtypeStruct((SSM_GROUPS, n_chunks, W), BF16),
        scratch_shapes=[pltpu.VMEM((n_rows, GS, NS), F32), pltpu.VMEM((n_rows, GS, NS), F32)]
        + [pltpu.VMEM((n_chunks, GS, NS), F32) for _ in range(4)],
        compiler_params=pltpu.CompilerParams(
            dimension_semantics=("arbitrary",), vmem_limit_bytes=VMEM_LIMIT),
        name="s5_chunked",
    )(u_chunks, pp, mm, qq, lam16)


def _mix_out_body(x_ref, att_ref, y_ref, u_ref, d_ref, wglu_ref, bglu_ref, sn_ref, wout_ref,
                  nf_ref, rwt_ref, sg_ref, su_ref, sd_ref, h_ref, hn_ref, sc_ref):
    y = y_ref[...].astype(F32) + d_ref[...] * u_ref[...].astype(F32)
    y = jax.nn.gelu(y)
    gate = jax.nn.sigmoid(jnp.dot(y.astype(BF16), wglu_ref[...], preferred_element_type=F32) + bglu_ref[...])
    s = y * gate
    ms = jnp.mean(s * s, axis=-1, keepdims=True)
    ssm = (s * lax.rsqrt(ms + EPS) * sn_ref[...]).astype(BF16)
    h = (x_ref[...]
         + jnp.dot(att_ref[...], wout_ref[:ATT_WIDTH, :], preferred_element_type=F32)
         + jnp.dot(ssm, wout_ref[ATT_WIDTH:, :], preferred_element_type=F32))
    ms = jnp.mean(h * h, axis=-1, keepdims=True)
    hn = h * lax.rsqrt(ms + EPS) * nf_ref[...]
    hnb = hn.astype(BF16)
    hn_ref[...] = hnb
    logits_t = lax.dot_general(rwt_ref[...], hn, (((1,), (1,)), ((), ())),
                               precision=lax.Precision.HIGHEST, preferred_element_type=F32)
    sc_ref[...] = jax.nn.sigmoid(logits_t)
    g = jnp.dot(hnb, sg_ref[...], preferred_element_type=F32)
    up = jnp.dot(hnb, su_ref[...], preferred_element_type=F32)
    act = (jax.nn.silu(g) * up).astype(BF16)
    h_ref[...] = h + jnp.dot(act, sd_ref[...], preferred_element_type=F32)


def _mix_out(x2, att, y, u, d_skip, w_glu, b_glu, ssm_norm, w_out, norm_ffn, router_wt,
             sh_gate, sh_up, sh_down):
    seq = x2.shape[0]
    R = MIX_ROW_BLOCK
    res = lambda shape: _resident(shape, lambda i: (0, 0))
    return pl.pallas_call(
        _mix_out_body,
        grid=(seq // R,),
        in_specs=[
            pl.BlockSpec((R, D_MODEL), lambda i: (i, 0)),
            pl.BlockSpec((R, ATT_WIDTH), lambda i: (i, 0)),
            pl.BlockSpec((R, SSM_WIDTH), lambda i: (i, 0)),
            pl.BlockSpec((R, SSM_WIDTH), lambda i: (i, 0)),
            res((1, SSM_WIDTH)), res((SSM_WIDTH, SSM_WIDTH)), res((1, SSM_WIDTH)), res((1, SSM_WIDTH)),
            res((D_MODEL, D_MODEL)), res((1, D_MODEL)), res((N_EXPERTS, D_MODEL)),
            res((D_MODEL, SHARED_HIDDEN)), res((D_MODEL, SHARED_HIDDEN)), res((SHARED_HIDDEN, D_MODEL)),
        ],
        out_specs=[
            pl.BlockSpec((R, D_MODEL), lambda i: (i, 0)),
            pl.BlockSpec((R, D_MODEL), lambda i: (i, 0)),
            pl.BlockSpec((N_EXPERTS, R), lambda i: (0, i)),
        ],
        out_shape=[
            jax.ShapeDtypeStruct((seq, D_MODEL), F32),
            jax.ShapeDtypeStruct((seq, D_MODEL), BF16),
            jax.ShapeDtypeStruct((N_EXPERTS, seq), F32),
        ],
        compiler_params=pltpu.CompilerParams(
            dimension_semantics=("arbitrary",), vmem_limit_bytes=VMEM_LIMIT),
        name="mix_out_shared",
    )(x2, att, y, u, d_skip, w_glu, b_glu, ssm_norm, w_out, norm_ffn, router_wt,
      sh_gate, sh_up, sh_down)


def _route_body(sc_ref, rb_ref, tri_ref, w_ref, slot_ref, cnt_ref):
    scores = sc_ref[...]
    R = scores.shape[1]
    per_group = N_EXPERTS // N_EXPERT_GROUPS
    choice = scores + rb_ref[...]
    c3 = choice.reshape(N_EXPERT_GROUPS, per_group, R)
    within = lax.broadcasted_iota(jnp.int32, c3.shape, 1)
    m1 = jnp.max(c3, axis=1, keepdims=True)
    first = jnp.min(jnp.where(c3 == m1, within, per_group), axis=1, keepdims=True)
    m2 = jnp.max(jnp.where(within == first, -jnp.inf, c3), axis=1, keepdims=True)
    grp = (m1 + m2).reshape(N_EXPERT_GROUPS, R)
    gidx = lax.broadcasted_iota(jnp.int32, grp.shape, 0)
    grank = jnp.zeros(grp.shape, jnp.int32)
    for b in range(N_EXPERT_GROUPS):
        gb = grp[b:b + 1, :]
        grank += ((gb > grp) | ((gb == grp) & (b < gidx))).astype(jnp.int32)
    gmask = grank < TOPK_GROUPS
    emask = jnp.broadcast_to(gmask[:, None, :], c3.shape).reshape(N_EXPERTS, R)
    val = jnp.where(emask, choice, -jnp.inf)
    eidx = lax.broadcasted_iota(jnp.int32, val.shape, 0)
    rank = jnp.zeros(val.shape, jnp.int32)
    for e in range(N_EXPERTS):
        ve = val[e:e + 1, :]
        rank += ((ve > val) | ((ve == val) & (e < eidx))).astype(jnp.int32)
    sel = rank < TOP_K
    w = jnp.where(sel, scores, 0.0)
    w_ref[...] = w / jnp.sum(w, axis=0, keepdims=True) * ROUTED_SCALE
    cum = jnp.dot(jnp.where(sel, 1.0, 0.0).astype(BF16), tri_ref[...], preferred_element_type=F32)
    slot_ref[...] = jnp.where(sel, cum - 1.0, NOT_ROUTED)
    cnt_ref[0] = jnp.broadcast_to(cum[:, R - 1:R], (N_EXPERTS, 128))


def _route(scores_t, router_bias, tri):
    seq = scores_t.shape[1]
    R = MOE_TOKEN_BLOCK
    n_blk = seq // R
    blk = pl.BlockSpec((N_EXPERTS, R), lambda i: (0, i))
    return pl.pallas_call(
        _route_body,
        grid=(n_blk,),
        in_specs=[blk, _resident((N_EXPERTS, 1), lambda i: (0, 0)), _resident((R, R), lambda i: (0, 0))],
        out_specs=[blk, blk, pl.BlockSpec((1, N_EXPERTS, 128), lambda i: (i, 0, 0))],
        out_shape=[jax.ShapeDtypeStruct((N_EXPERTS, seq), F32),
                   jax.ShapeDtypeStruct((N_EXPERTS, seq), F32),
                   jax.ShapeDtypeStruct((n_blk, N_EXPERTS, 128), F32)],
        compiler_params=pltpu.CompilerParams(dimension_semantics=("arbitrary",)),
        name="route",
    )(scores_t, router_bias, tri)


def _moe_tables(cnt):
    n_blk = cnt.shape[0]
    U, RB, SB = MOE_UNIT, MOE_BLOCK_ROWS, MOE_SLOT_BLOCK
    NU = SB // U
    E = N_EXPERTS
    pc = (cnt + U - 1) // U * U
    off = jnp.cumsum(pc, axis=1) - pc
    upc_t = (pc // U).T
    cum_t = jnp.cumsum(upc_t, axis=1)
    units_e = cum_t[:, -1]
    nblk_e = (units_e + NU - 1) // NU
    blk_end = jnp.cumsum(nblk_e)
    blk_start = blk_end - nblk_e
    n_act = blk_end[-1]
    max_blocks = n_blk * (RB // SB) + E
    i = jnp.arange(max_blocks, dtype=jnp.int32)
    active = i < n_act
    count_le = lambda edges, v: jnp.sum((edges <= v[..., None]).astype(jnp.int32), axis=-1)
    last_e = jnp.minimum(count_le(blk_end, n_act - 1), E - 1)
    be = jnp.where(active, jnp.minimum(count_le(blk_end[None, :], i), E - 1), last_e)
    oh_e = be[:, None] == jnp.arange(E, dtype=jnp.int32)[None, :]
    pick_e = lambda v: jnp.sum(jnp.where(oh_e, v[None, :], 0), axis=1)
    pick_e2 = lambda m: jnp.sum(jnp.where(oh_e[:, :, None], m[None, :, :], 0), axis=1)
    bstart_i = pick_e(blk_start)
    first = active & (i == bstart_i)
    local = (i - bstart_i)[:, None] * NU + jnp.arange(NU, dtype=jnp.int32)[None, :]
    valid = active[:, None] & (local < pick_e(units_e)[:, None])
    cum_i = pick_e2(cum_t)
    b_of = jnp.minimum(count_le(cum_i[:, None, :], local), n_blk - 1)
    oh_b = b_of[:, :, None] == jnp.arange(n_blk, dtype=jnp.int32)[None, None, :]
    pick_b = lambda m: jnp.sum(jnp.where(oh_b, m[:, None, :], 0), axis=2)
    seg_start = pick_b(cum_i) - pick_b(pick_e2(upc_t))
    unit = (b_of * RB + pick_b(pick_e2(off.T))) // U + (local - seg_start)
    spare = n_blk * RB // U
    src = jnp.where(valid, unit, spare)
    dst = jnp.where(valid, unit,
                    spare + (1 + i % 2)[:, None] * NU + jnp.arange(NU, dtype=jnp.int32)[None, :])
    short = active & (pick_e(units_e) - (i - bstart_i) * NU <= NU // 2)
    i32 = lambda a: a.astype(jnp.int32)
    flags = i32(first) + 2 * i32(short)
    return (off, pc, i32(src.reshape(-1)), i32(dst.reshape(-1)), i32(be), flags,
            i32(n_act.reshape(1)))


def _split_hi_lo(pos):
    hi = jnp.floor(pos * (1.0 / 64.0))
    return hi.astype(BF16), (pos - 64.0 * hi).astype(BF16)


def _dispatch_body(hn_ref, slot_ref, offc_ref, offl_ref, pcl_ref, xs_ref, *, n_blk):
    b = pl.program_id(0)
    RB, TB = MOE_BLOCK_ROWS, MOE_TOKEN_BLOCK

    @pl.when(b == n_blk)
    def _():
        xs_ref[...] = jnp.zeros(xs_ref.shape, BF16)

    @pl.when(b < n_blk)
    def _():
        pos = slot_ref[...] + offc_ref[0]
        hi, lo = _split_hi_lo(pos)
        r = lax.broadcasted_iota(jnp.int32, (RB, N_EXPERTS), 0).astype(F32)
        off = offl_ref[0]
        owner = jnp.where((r >= off) & (r < off + pcl_ref[0]), 1.0, 0.0).astype(BF16)
        p = (64.0 * jnp.dot(owner, hi, preferred_element_type=F32)
             + jnp.dot(owner, lo, preferred_element_type=F32))
        rr = lax.broadcasted_iota(jnp.int32, (RB, TB), 0).astype(F32)
        onehot = jnp.where(jnp.abs(p - rr) < 0.5, 1.0, 0.0).astype(BF16)
        x = hn_ref[...]
        C = MOE_SLOT_BLOCK
        for c in range(RB // C):
            xs_ref[0, c * C:(c + 1) * C, :] = jnp.dot(
                onehot[c * C:(c + 1) * C], x, preferred_element_type=F32).astype(BF16)


def _dispatch(hn, slots, off, pc):
    n_blk = off.shape[0]
    RB, TB = MOE_BLOCK_ROWS, MOE_TOKEN_BLOCK
    clamp = lambda b: jnp.minimum(b, n_blk - 1)
    return pl.pallas_call(
        functools.partial(_dispatch_body, n_blk=n_blk),
        grid=(n_blk + 1,),
        in_specs=[
            pl.BlockSpec((TB, D_MODEL), lambda b: (clamp(b), 0)),
            pl.BlockSpec((N_EXPERTS, TB), lambda b: (0, clamp(b))),
            pl.BlockSpec((1, N_EXPERTS, 1), lambda b: (clamp(b), 0, 0)),
            pl.BlockSpec((1, 1, N_EXPERTS), lambda b: (clamp(b), 0, 0)),
            pl.BlockSpec((1, 1, N_EXPERTS), lambda b: (clamp(b), 0, 0)),
        ],
        out_specs=pl.BlockSpec((1, RB, D_MODEL), lambda b: (b, 0, 0)),
        out_shape=jax.ShapeDtypeStruct((n_blk + 1, RB, D_MODEL), BF16),
        compiler_params=pltpu.CompilerParams(
            dimension_semantics=("arbitrary",), vmem_limit_bytes=VMEM_LIMIT),
        name="moe_dispatch",
    )(hn, slots, off[:, :, None], off[:, None, :], pc[:, None, :])


def _expert_body(src_ref, dst_ref, bexp_ref, first_ref, nact_ref,
                 xs_hbm, wg_ref, wu_ref, wd_ref, ys_hbm,
                 xbuf, ybuf, wgb, wub, wdb, sem_in, sem_out):
    del bexp_ref
    i = pl.program_id(0)
    n_act = nact_ref[0]
    cur = lax.rem(i, 2)
    U = MOE_UNIT
    NU = MOE_SLOT_BLOCK // U

    def in_copy(blk, buf, u):
        return pltpu.make_async_copy(xs_hbm.at[src_ref[blk * NU + u]],
                                     xbuf.at[buf, pl.ds(u * U, U)], sem_in.at[buf])

    def out_copy(blk, buf, u):
        return pltpu.make_async_copy(ybuf.at[buf, pl.ds(u * U, U)],
                                     ys_hbm.at[dst_ref[blk * NU + u]], sem_out.at[buf])

    @pl.when(i == 0)
    def _():
        for u in range(NU):
            in_copy(0, 0, u).start()

    @pl.when(i < n_act)
    def _():
        @pl.when(i + 1 < n_act)
        def _():
            for u in range(NU):
                in_copy(i + 1, 1 - cur, u).start()

        @pl.when(first_ref[i] % 2 == 1)
        def _():
            wgb[...] = wg_ref[0].astype(BF16)
            wub[...] = wu_ref[0].astype(BF16)
            wdb[...] = wd_ref[0].astype(BF16)

        for u in range(NU):
            in_copy(i, cur, u).wait()

        @pl.when(i >= 2)
        def _():
            for u in range(NU):
                out_copy(i - 2, cur, u).wait()

        def swiglu_rows(rows):
            x = xbuf[cur, :rows]
            g = jnp.dot(x, wgb[...], preferred_element_type=F32)
            up = jnp.dot(x, wub[...], preferred_element_type=F32)
            act = (jax.nn.silu(g) * up).astype(BF16)
            ybuf[cur, :rows] = jnp.dot(act, wdb[...], preferred_element_type=F32).astype(BF16)

        half = MOE_SLOT_BLOCK // 2

        @pl.when(first_ref[i] < 2)
        def _():
            swiglu_rows(MOE_SLOT_BLOCK)

        @pl.when(first_ref[i] >= 2)
        def _():
            swiglu_rows(half)
            ybuf[cur, half:] = jnp.zeros((half, D_MODEL), BF16)

        for u in range(NU):
            out_copy(i, cur, u).start()

        @pl.when(i == n_act - 1)
        def _():
            for u in range(NU):
                out_copy(i, cur, u).wait()

            @pl.when(i >= 1)
            def _():
                for u in range(NU):
                    out_copy(i - 1, 1 - cur, u).wait()


def _experts(xs, src, dst, bexp, first, n_act, wg, wu, wd):
    U, SB = MOE_UNIT, MOE_SLOT_BLOCK
    n_units = xs.shape[0] * xs.shape[1] // U
    max_blocks = bexp.shape[0]
    unit_view = lambda a: a.reshape(n_units, U, D_MODEL)
    wspec = lambda shape: pl.BlockSpec((1,) + shape, lambda i, src, dst, bexp, first, nact: (bexp[i], 0, 0))
    grid_spec = pltpu.PrefetchScalarGridSpec(
        num_scalar_prefetch=5,
        grid=(max_blocks,),
        in_specs=[
            pl.BlockSpec(memory_space=pl.ANY),
            wspec((D_MODEL, EXPERT_HIDDEN)), wspec((D_MODEL, EXPERT_HIDDEN)), wspec((EXPERT_HIDDEN, D_MODEL)),
        ],
        out_specs=pl.BlockSpec(memory_space=pl.ANY),
        scratch_shapes=[
            pltpu.VMEM((2, SB, D_MODEL), BF16), pltpu.VMEM((2, SB, D_MODEL), BF16),
            pltpu.VMEM((D_MODEL, EXPERT_HIDDEN), BF16), pltpu.VMEM((D_MODEL, EXPERT_HIDDEN), BF16),
            pltpu.VMEM((EXPERT_HIDDEN, D_MODEL), BF16),
            pltpu.SemaphoreType.DMA((2,)), pltpu.SemaphoreType.DMA((2,)),
        ],
    )
    ys = pl.pallas_call(
        _expert_body,
        grid_spec=grid_spec,
        out_shape=jax.ShapeDtypeStruct((n_units, U, D_MODEL), BF16),
        input_output_aliases={5: 0},
        compiler_params=pltpu.CompilerParams(
            dimension_semantics=("arbitrary",), vmem_limit_bytes=VMEM_LIMIT),
        name="moe_experts",
    )(src, dst, bexp, first, n_act, unit_view(xs), wg, wu, wd)
    return ys.reshape(xs.shape)


def _combine_body(h_ref, ys_ref, slot_ref, w_ref, offl_ref, offc_ref, pcc_ref, o_ref):
    RB, TB = MOE_BLOCK_ROWS, MOE_TOKEN_BLOCK
    pos = slot_ref[...] + offl_ref[0]
    hi, lo = _split_hi_lo(pos)
    r = lax.broadcasted_iota(jnp.int32, (N_EXPERTS, RB), 1).astype(F32)
    off = offc_ref[0]
    owner = jnp.where((r >= off) & (r < off + pcc_ref[0]), 1.0, 0.0).astype(BF16)
    p = (64.0 * jnp.dot(hi, owner, preferred_element_type=F32)
         + jnp.dot(lo, owner, preferred_element_type=F32))
    wr = jnp.dot(w_ref[...].astype(BF16), owner, preferred_element_type=F32)
    rr = lax.broadcasted_iota(jnp.int32, (TB, RB), 1).astype(F32)
    gather_w = jnp.where(jnp.abs(p - rr) < 0.5, wr, 0.0).astype(BF16)
    o_ref[...] = h_ref[...] + jnp.dot(gather_w, ys_ref[0], preferred_element_type=F32)


def _combine(h, ys, slots_tok, w_tok, off, pc):
    n_blk = off.shape[0]
    RB, TB = MOE_BLOCK_ROWS, MOE_TOKEN_BLOCK
    seq = h.shape[0]
    return pl.pallas_call(
        _combine_body,
        grid=(n_blk,),
        in_specs=[
            pl.BlockSpec((TB, D_MODEL), lambda b: (b, 0)),
            pl.BlockSpec((1, RB, D_MODEL), lambda b: (b, 0, 0)),
            pl.BlockSpec((TB, N_EXPERTS), lambda b: (b, 0)),
            pl.BlockSpec((TB, N_EXPERTS), lambda b: (b, 0)),
            pl.BlockSpec((1, 1, N_EXPERTS), lambda b: (b, 0, 0)),
            pl.BlockSpec((1, N_EXPERTS, 1), lambda b: (b, 0, 0)),
            pl.BlockSpec((1, N_EXPERTS, 1), lambda b: (b, 0, 0)),
        ],
        out_specs=pl.BlockSpec((TB, D_MODEL), lambda b: (b, 0)),
        out_shape=jax.ShapeDtypeStruct((seq, D_MODEL), F32),
        compiler_params=pltpu.CompilerParams(
            dimension_semantics=("arbitrary",), vmem_limit_bytes=VMEM_LIMIT),
        name="moe_combine",
    )(h, ys, slots_tok, w_tok, off[:, None, :], off[:, :, None], pc[:, :, None])


def kernel(x, meta_tokens, rel_bias, norm_mix, w_in, q_norm, k_norm, lam_q1, lam_k1, lam_q2, lam_k2, subln, ssm_a_re, ssm_a_im, ssm_log_step, ssm_b_re, ssm_b_im, ssm_c_re, ssm_c_im, ssm_d, w_glu, b_glu, ssm_norm, w_out, norm_ffn, router_w, router_bias, w_gate, w_up, w_down, shared_gate, shared_up, shared_down):
    batch, seq, d = x.shape
    assert batch == 1 and d == D_MODEL and seq % ROW_BLOCK == 0 and seq % ATT_BLOCK == 0
    assert norm_mix.shape[0] == 1, "single layer"
    x2 = x.reshape(seq, d)
    meta_pad = jnp.zeros((ROW_BLOCK, d), F32).at[:N_META].set(meta_tokens.astype(F32))
    seg = jnp.kron(jnp.eye(QK_WIDTH // HEAD_DIM, dtype=F32),
                   jnp.full((HEAD_DIM, HEAD_DIM), 1.0 / HEAD_DIM, F32)).astype(BF16)
    qg = jnp.tile(q_norm[0].astype(F32), QK_WIDTH // HEAD_DIM)[None] * (HEAD_DIM ** -0.5)
    kg = jnp.tile(k_norm[0].astype(F32), QK_WIDTH // HEAD_DIM)[None]

    proj, u = _inproj(x2, meta_pad, norm_mix[0][None], w_in[0].astype(BF16), seg, qg, kg)

    score_bound = (BOUND_MARGIN * HEAD_DIM ** 0.5 * jnp.max(jnp.abs(q_norm[0].astype(F32)))
                   * jnp.max(jnp.abs(k_norm[0].astype(F32)))
                   + jnp.max(jnp.abs(rel_bias.astype(F32)))).reshape(1)
    att = _attention(proj, rel_bias.astype(F32), score_bound, lam_q1[0][None], lam_k1[0][None],
                     lam_q2[0][None], lam_k2[0][None], subln[0][None], seq)

    n_rows = proj.shape[0] // S5_CHUNK
    n_chunks = seq // S5_CHUNK
    perm = _lane_regroup_matrix()
    u_chunks = _to_chunks(u.reshape(n_rows, S5_CHUNK, SSM_WIDTH), perm)
    pp, mm, qq, lam16 = _s5_prep(ssm_a_re[0].astype(F32), ssm_a_im[0].astype(F32),
                                 ssm_log_step[0].astype(F32), ssm_b_re[0].astype(F32),
                                 ssm_b_im[0].astype(F32), ssm_c_re[0].astype(F32),
                                 ssm_c_im[0].astype(F32))
    y_chunks = _s5(u_chunks, pp, mm, qq, lam16, n_chunks)
    y = _from_chunks(y_chunks, perm.T).reshape(seq, SSM_WIDTH)

    h, hn, scores_t = _mix_out(
        x2, att, y, u, ssm_d[0][None].astype(F32), w_glu[0].astype(BF16), b_glu[0][None].astype(F32),
        ssm_norm[0][None].astype(F32), w_out[0].astype(BF16), norm_ffn[0][None].astype(F32),
        router_w[0].astype(F32).T, shared_gate[0].astype(BF16), shared_up[0].astype(BF16),
        shared_down[0].astype(BF16))

    tb = MOE_TOKEN_BLOCK
    tri = (jnp.arange(tb)[:, None] <= jnp.arange(tb)[None, :]).astype(BF16)
    wts_t, slots_t, cnt = _route(scores_t, router_bias[0].astype(F32)[:, None], tri)
    off, pc, src, dst, bexp, first, n_act = _moe_tables(cnt[:, :, 0].astype(jnp.int32))
    off_f, pc_f = off.astype(F32), pc.astype(F32)
    xs = _dispatch(hn, slots_t, off_f, pc_f)
    ys = _experts(xs, src, dst, bexp, first, n_act, w_gate[0], w_up[0], w_down[0])
    out = _combine(h, ys, slots_t.T, wts_t.T, off_f, pc_f)
    return out.reshape(batch, seq, d)
```

```python
import functools
import math

import jax
import jax.numpy as jnp
from jax import lax
from jax.experimental import pallas as pl
from jax.experimental.pallas import tpu as pltpu

F32 = jnp.float32
BF16 = jnp.bfloat16

D_MODEL = 2048
N_META = 16
ATT_WIDTH = 1024
SSM_WIDTH = 1024
HEAD_DIM = 64
V_DIM = 128
HEADS = 8
QK_WIDTH = 1024
IN_WIDTH = 4096
V_OFFSET = 2 * QK_WIDTH
PROJ_WIDTH = V_OFFSET + 2 * ATT_WIDTH
SSM_GROUP = 16
SSM_GROUPS = 64
SSM_STATE = 64
N_BUCKETS = 32
MAX_DISTANCE = 128
N_EXPERTS = 64
TOP_K = 8
N_EXPERT_GROUPS = 8
TOPK_GROUPS = 4
EXPERT_HIDDEN = 512
SHARED_HIDDEN = 512
ROUTED_SCALE = 2.5
EPS = 1e-6
LAMBDA_INIT = 0.8 - 0.6 * math.exp(-0.3 * 0)

ROW_BLOCK = 512
ATT_BLOCK = 512
S5_CHUNK = 16
S5_GROUPS_PER_STEP = 8
S5_OCTET_WIDTH = S5_GROUPS_PER_STEP * S5_CHUNK * SSM_GROUP
assert S5_GROUPS_PER_STEP * SSM_GROUP == 128
MOE_TOKEN_BLOCK = 256
MOE_UNIT = 16
MOE_SLOT_BLOCK = 512
MOE_BLOCK_ROWS = -(-(MOE_TOKEN_BLOCK * TOP_K + N_EXPERTS * (MOE_UNIT - 1)) // MOE_SLOT_BLOCK) * MOE_SLOT_BLOCK
NOT_ROUTED = -1e6
MIX_ROW_BLOCK = 256
NEG_BIG = -1e30
MAX_EXP_RANGE = 80.0
BOUND_MARGIN = 1.02
LANES = 128
VMEM_LIMIT = 56 * 2 ** 20


def _resident(shape, index_map):
    return pl.BlockSpec(shape, index_map, pipeline_mode=pl.Buffered(1))


def _inproj_body(x_ref, meta_ref, g_ref, w_ref, seg_ref, qg_ref, kg_ref, o_ref, u_ref, *, n_xblk):
    i = pl.program_id(0)

    def run(src_ref):
        xv = src_ref[...]
        ms = jnp.mean(xv * xv, axis=-1, keepdims=True)
        hn = (xv * lax.rsqrt(ms + EPS) * g_ref[...]).astype(BF16)
        sec = IN_WIDTH // 4
        for s in range(4):
            ps = jnp.dot(hn, w_ref[:, s * sec:(s + 1) * sec], preferred_element_type=F32)
            if s < 2:
                gain = qg_ref if s == 0 else kg_ref
                msq = jnp.dot((ps * ps).astype(BF16), seg_ref[...], preferred_element_type=F32)
                ps = ps * lax.rsqrt(msq + EPS) * gain[...]
            if s == 3:
                u_ref[...] = ps
                continue
            pb = ps.astype(BF16)
            if s < 2:
                o_ref[:, s * sec:(s + 1) * sec] = pb
            else:
                lane = lax.broadcasted_iota(jnp.int32, (pb.shape[0], V_DIM), 1)
                ones_col = jnp.where(lane == 0, 1.0, 0.0).astype(BF16)
                for hh in range(HEADS):
                    base = V_OFFSET + hh * 2 * V_DIM
                    o_ref[:, base:base + V_DIM] = pb[:, hh * V_DIM:(hh + 1) * V_DIM]
                    o_ref[:, base + V_DIM:base + 2 * V_DIM] = ones_col

    @pl.when(i < n_xblk)
    def _():
        run(x_ref)

    @pl.when(i == n_xblk)
    def _():
        run(meta_ref)


def _inproj(x2, meta_pad, gain, w_bf, seg, qg, kg):
    seq = x2.shape[0]
    n_xblk = seq // ROW_BLOCK
    rows = seq + ROW_BLOCK
    return pl.pallas_call(
        functools.partial(_inproj_body, n_xblk=n_xblk),
        grid=(n_xblk + 1,),
        in_specs=[
            pl.BlockSpec((ROW_BLOCK, D_MODEL), lambda i: (jnp.minimum(i, n_xblk - 1), 0)),
            _resident((ROW_BLOCK, D_MODEL), lambda i: (0, 0)),
            _resident((1, D_MODEL), lambda i: (0, 0)),
            _resident((D_MODEL, IN_WIDTH), lambda i: (0, 0)),
            _resident((QK_WIDTH, QK_WIDTH), lambda i: (0, 0)),
            _resident((1, QK_WIDTH), lambda i: (0, 0)),
            _resident((1, QK_WIDTH), lambda i: (0, 0)),
        ],
        out_specs=[pl.BlockSpec((ROW_BLOCK, PROJ_WIDTH), lambda i: (i, 0)),
                   pl.BlockSpec((ROW_BLOCK, SSM_WIDTH), lambda i: (i, 0))],
        out_shape=[jax.ShapeDtypeStruct((rows, PROJ_WIDTH), BF16),
                   jax.ShapeDtypeStruct((rows, SSM_WIDTH), F32)],
        compiler_params=pltpu.CompilerParams(
            dimension_semantics=("arbitrary",), vmem_limit_bytes=VMEM_LIMIT),
        name="inproj",
    )(x2, meta_pad, gain, w_bf, seg, qg, kg)


def _t5_bias(rel, tab_ref, h):
    half = N_BUCKETS // 2
    exact = half // 2
    n = jnp.abs(rel)
    nf = jnp.maximum(n, 1).astype(F32)
    large = exact + (jnp.log(nf / exact) / math.log(MAX_DISTANCE / exact) * (half - exact)).astype(jnp.int32)
    large = jnp.minimum(large, half - 1)
    bucket = jnp.where(rel > 0, half, 0) + jnp.where(n < exact, n, large)
    out = jnp.zeros(rel.shape, F32)
    for b in range(N_BUCKETS):
        out = jnp.where(bucket == b, tab_ref[b, h], out)
    return out


def _attn_body(tab_ref, bound_ref, q_ref, k_ref, v_ref, lq1_ref, lk1_ref, lq2_ref, lk2_ref,
               subln_ref, o_ref, bias_ref, acc1_ref, acc2_ref, m1_ref, m2_ref, *, n_main):
    T = ATT_BLOCK
    h = pl.program_id(0)
    qi = pl.program_id(1)
    bound = bound_ref[0]

    @pl.when(qi == 0)
    def _():
        offsets = (-T, 0, T, -N_META, -N_META - T, -2 * T, 2 * T)
        for kind, off in enumerate(offsets):
            masked = kind in (3, 4)

            def rows(rc, carry, off=off, masked=masked, kind=kind):
                r0 = pl.multiple_of(rc * 8, 8)
                r = r0 + lax.broadcasted_iota(jnp.int32, (8, T), 0)
                c = lax.broadcasted_iota(jnp.int32, (8, T), 1)
                b = _t5_bias(off + c - r, tab_ref, h) - bound
                if masked:
                    b = jnp.where(c < N_META, b, NEG_BIG)
                bias_ref[kind, pl.ds(r0, 8), :] = b
                return carry

            lax.fori_loop(0, T // 8, rows, 0)

    acc1_ref[...] = jnp.zeros(acc1_ref.shape, F32)
    acc2_ref[...] = jnp.zeros(acc2_ref.shape, F32)

    q = q_ref[...]
    q1 = q[:, :HEAD_DIM]
    q2 = q[:, HEAD_DIM:]
    nt = (((1,), (1,)), ((), ()))

    def tile(ki):
        koff = pl.multiple_of(ki * T, T)
        kb = k_ref[pl.ds(koff, T), :]
        va = v_ref[pl.ds(koff, T), :]
        d = ki - qi
        kind = jnp.where(ki == n_main, jnp.where(qi == 0, 3, 4),
                         jnp.where(d <= -2, 5, jnp.where(d >= 2, 6, d + 1)))
        return kb, va, bias_ref[kind]

    def bounded_step(ki, carry):
        kb, va, bias = tile(ki)
        s1 = lax.dot_general(q1, kb[:, :HEAD_DIM], nt, preferred_element_type=F32) + bias
        acc1_ref[...] += jnp.dot(jnp.exp(s1).astype(BF16), va, preferred_element_type=F32)
        s2 = lax.dot_general(q2, kb[:, HEAD_DIM:], nt, preferred_element_type=F32) + bias
        acc2_ref[...] += jnp.dot(jnp.exp(s2).astype(BF16), va, preferred_element_type=F32)
        return carry

    def online_map(s, va, m_ref, acc_ref):
        m_old = m_ref[...]
        m_new = jnp.maximum(m_old, jnp.max(s, axis=-1, keepdims=True))
        p = jnp.exp(s - m_new).astype(BF16)
        acc_ref[...] = (jnp.exp(m_old - m_new) * acc_ref[...]
                        + jnp.dot(p, va, preferred_element_type=F32))
        m_ref[...] = m_new

    def online_step(ki, carry):
        kb, va, bias = tile(ki)
        s1 = lax.dot_general(q1, kb[:, :HEAD_DIM], nt, preferred_element_type=F32) + bias
        online_map(s1, va, m1_ref, acc1_ref)
        s2 = lax.dot_general(q2, kb[:, HEAD_DIM:], nt, preferred_element_type=F32) + bias
        online_map(s2, va, m2_ref, acc2_ref)
        return carry

    no_running_max = 2.0 * bound <= MAX_EXP_RANGE

    @pl.when(no_running_max)
    def _():
        unroll = next(u for u in (33, 11, 3, 2, 1) if (n_main + 1) % u == 0)
        lax.fori_loop(0, n_main + 1, bounded_step, 0, unroll=unroll)

    @pl.when(jnp.logical_not(no_running_max))
    def _():
        m1_ref[...] = jnp.full(m1_ref.shape, -jnp.inf, F32)
        m2_ref[...] = jnp.full(m2_ref.shape, -jnp.inf, F32)
        lax.fori_loop(0, n_main + 1, online_step, 0)

    lam = (jnp.exp(jnp.sum(lq1_ref[...] * lk1_ref[...], axis=-1, keepdims=True))
           - jnp.exp(jnp.sum(lq2_ref[...] * lk2_ref[...], axis=-1, keepdims=True))
           + LAMBDA_INIT)
    a1 = acc1_ref[...]
    a2 = acc2_ref[...]
    o = (a1[:, :V_DIM] / a1[:, V_DIM:V_DIM + 1]
         - lam * (a2[:, :V_DIM] / a2[:, V_DIM:V_DIM + 1]))
    ms = jnp.mean(o * o, axis=-1, keepdims=True)
    o = o * lax.rsqrt(ms + EPS) * subln_ref[...] * (1.0 - LAMBDA_INIT)
    o_ref[...] = o.astype(BF16)


def _attention(proj, rel_bias, score_bound, lq1, lk1, lq2, lk2, subln, seq):
    T = ATT_BLOCK
    n_main = seq // T
    rows = proj.shape[0]
    vec64 = lambda: _resident((1, HEAD_DIM), lambda h, qi: (0, 0))
    return pl.pallas_call(
        functools.partial(_attn_body, n_main=n_main),
        grid=(HEADS, n_main),
        in_specs=[
            pl.BlockSpec(memory_space=pltpu.SMEM),
            pl.BlockSpec(memory_space=pltpu.SMEM),
            pl.BlockSpec((T, 2 * HEAD_DIM), lambda h, qi: (qi, h)),
            pl.BlockSpec((rows, 2 * HEAD_DIM), lambda h, qi: (0, HEADS + h)),
            pl.BlockSpec((rows, 2 * V_DIM), lambda h, qi: (0, V_OFFSET // (2 * V_DIM) + h)),
            vec64(), vec64(), vec64(), vec64(),
            _resident((1, V_DIM), lambda h, qi: (0, 0)),
        ],
        out_specs=pl.BlockSpec((T, V_DIM), lambda h, qi: (qi, h)),
        out_shape=jax.ShapeDtypeStruct((seq, ATT_WIDTH), BF16),
        scratch_shapes=[
            pltpu.VMEM((7, T, T), F32),
            pltpu.VMEM((T, 2 * V_DIM), F32), pltpu.VMEM((T, 2 * V_DIM), F32),
            pltpu.VMEM((T, 1), F32), pltpu.VMEM((T, 1), F32),
        ],
        compiler_params=pltpu.CompilerParams(
            dimension_semantics=("arbitrary", "arbitrary"), vmem_limit_bytes=VMEM_LIMIT),
        name="diff_attention",
    )(rel_bias, score_bound, proj, proj, proj, lq1, lk1, lq2, lk2, subln)


def _lane_regroup_matrix():
    out_lane = jnp.arange(S5_OCTET_WIDTH, dtype=jnp.int32)
    g8 = out_lane // (S5_CHUNK * SSM_GROUP)
    j = (out_lane % (S5_CHUNK * SSM_GROUP)) // SSM_GROUP
    p = out_lane % SSM_GROUP
    in_lane = j * LANES + g8 * SSM_GROUP + p
    return (jnp.arange(S5_OCTET_WIDTH, dtype=jnp.int32)[:, None] == in_lane[None, :]).astype(BF16)


def _to_chunks_body(u_ref, perm_ref, o_ref):
    x = jnp.concatenate([u_ref[:, j, :].astype(BF16) for j in range(S5_CHUNK)], axis=1)
    r = jnp.dot(x, perm_ref[...], preferred_element_type=F32)
    w = S5_CHUNK * SSM_GROUP
    for g8 in range(S5_GROUPS_PER_STEP):
        o_ref[g8] = r[:, g8 * w:(g8 + 1) * w].astype(BF16)


def _to_chunks(u3, perm):
    n_rows = u3.shape[0]
    w = S5_CHUNK * SSM_GROUP
    return pl.pallas_call(
        _to_chunks_body,
        grid=(SSM_GROUPS // S5_GROUPS_PER_STEP,),
        in_specs=[pl.BlockSpec((n_rows, S5_CHUNK, LANES), lambda o: (0, 0, o)),
                  _resident((S5_OCTET_WIDTH, S5_OCTET_WIDTH), lambda o: (0, 0))],
        out_specs=pl.BlockSpec((S5_GROUPS_PER_STEP, n_rows, w), lambda o: (o, 0, 0)),
        out_shape=jax.ShapeDtypeStruct((SSM_GROUPS, n_rows, w), BF16),
        compiler_params=pltpu.CompilerParams(
            dimension_semantics=("arbitrary",), vmem_limit_bytes=VMEM_LIMIT),
        name="s5_to_chunks",
    )(u3, perm)


def _from_chunks_body(y_ref, perm_t_ref, o_ref):
    x = jnp.concatenate([y_ref[g8] for g8 in range(S5_GROUPS_PER_STEP)], axis=1)
    r = jnp.dot(x, perm_t_ref[...], preferred_element_type=F32)
    for j in range(S5_CHUNK):
        o_ref[:, j, :] = r[:, j * LANES:(j + 1) * LANES]


def _from_chunks(y_chunks, perm_t):
    n_chunks = y_chunks.shape[1]
    w = S5_CHUNK * SSM_GROUP
    return pl.pallas_call(
        _from_chunks_body,
        grid=(SSM_GROUPS // S5_GROUPS_PER_STEP,),
        in_specs=[pl.BlockSpec((S5_GROUPS_PER_STEP, n_chunks, w), lambda o: (o, 0, 0)),
                  _resident((S5_OCTET_WIDTH, S5_OCTET_WIDTH), lambda o: (0, 0))],
        out_specs=pl.BlockSpec((n_chunks, S5_CHUNK, LANES), lambda o: (0, 0, o)),
        out_shape=jax.ShapeDtypeStruct((n_chunks, S5_CHUNK, SSM_WIDTH), F32),
        compiler_params=pltpu.CompilerParams(
            dimension_semantics=("arbitrary",), vmem_limit_bytes=VMEM_LIMIT),
        name="s5_from_chunks",
    )(y_chunks, perm_t)


def _s5_prep(a_re, a_im, log_step, b_re, b_im, c_re, c_im):
    C = S5_CHUNK
    dt = jnp.exp(log_step)[..., None]
    decay = jnp.exp(a_re * dt)
    ab_re = decay * jnp.cos(a_im * dt)
    ab_im = decay * jnp.sin(a_im * dt)
    den = a_re * a_re + a_im * a_im
    zr = ab_re - 1.0
    f_re = (zr * a_re + ab_im * a_im) / den
    f_im = (ab_im * a_re - zr * a_im) / den
    bb_re = f_re[..., None] * b_re - f_im[..., None] * b_im
    bb_im = f_re[..., None] * b_im + f_im[..., None] * b_re
    pr, pi = jnp.ones_like(ab_re), jnp.zeros_like(ab_re)
    pw_re, pw_im = [pr], [pi]
    for _ in range(C):
        pr, pi = pr * ab_re - pi * ab_im, pr * ab_im + pi * ab_re
        pw_re.append(pr)
        pw_im.append(pi)
    G, P, W = SSM_GROUPS, SSM_GROUP, C * SSM_GROUP
    pw_re = jnp.stack(pw_re, axis=-1)
    pw_im = jnp.stack(pw_im, axis=-1)
    ct_re = c_re.transpose(0, 1, 3, 2)
    ct_im = c_im.transpose(0, 1, 3, 2)
    cp_re = ct_re[:, :, :, None, :] * pw_re[..., None] - ct_im[:, :, :, None, :] * pw_im[..., None]
    cp_im = ct_re[:, :, :, None, :] * pw_im[..., None] + ct_im[:, :, :, None, :] * pw_re[..., None]
    bt_re = bb_re.transpose(0, 1, 3, 2)
    bt_im = bb_im.transpose(0, 1, 3, 2)

    def response(d):
        prod = (cp_re[d][:, None, :, :C, :] * bt_re[d][:, :, :, None, None]
                - cp_im[d][:, None, :, :C, :] * bt_im[d][:, :, :, None, None])
        return jnp.sum(prod, axis=2)

    ext_f = jnp.pad(response(0).reshape(G, P, W), ((0, 0), (0, 0), (W, 0)))
    ext_r = jnp.pad(response(1)[:, :, ::-1, :].reshape(G, P, W), ((0, 0), (0, 0), (0, W)))
    mm = jnp.stack([ext_f[:, :, W - P * j:2 * W - P * j] + ext_r[:, :, (C - 1 - j) * P:(C - 1 - j) * P + W]
                    for j in range(C)], axis=1).reshape(G, W, W)

    def in_mat(d, reverse_powers):
        pr_ = pw_re[d][:, :, :C].transpose(0, 2, 1)
        pi_ = pw_im[d][:, :, :C].transpose(0, 2, 1)
        if reverse_powers:
            pr_, pi_ = pr_[:, ::-1], pi_[:, ::-1]
        re = pr_[:, :, None, :] * bt_re[d][:, None] - pi_[:, :, None, :] * bt_im[d][:, None]
        im = pr_[:, :, None, :] * bt_im[d][:, None] + pi_[:, :, None, :] * bt_re[d][:, None]
        return re, im

    pf_re, pf_im = in_mat(0, True)
    pr_re, pr_im = in_mat(1, False)
    pp = jnp.concatenate([pf_re, pr_re, pf_im, pr_im], axis=-1).reshape(G, W, 4 * SSM_STATE)

    qq = jnp.concatenate([cp_re[0][:, :, 1:], cp_re[1][:, :, :0:-1],
                          -cp_im[0][:, :, 1:], -cp_im[1][:, :, :0:-1]], axis=1).reshape(G, 4 * SSM_STATE, W)
    lam16 = jnp.stack([jnp.concatenate([pw_re[0][:, :, C], pw_re[1][:, :, C]], axis=-1),
                       jnp.concatenate([pw_im[0][:, :, C], pw_im[1][:, :, C]], axis=-1)], axis=0)
    return pp.astype(BF16), mm.astype(BF16), qq.astype(BF16), lam16


def _s5_body(u_ref, pp_ref, mm_ref, qq_ref, lam_ref, y_ref,
             zre_ref, zim_ref, are_ref, aim_ref, bre_ref, bim_ref, *, n_chunks):
    GS = S5_GROUPS_PER_STEP
    NS = 2 * SSM_STATE
    for gi in range(GS):
        z = jnp.dot(u_ref[gi], pp_ref[gi], preferred_element_type=F32)
        zre_ref[:, gi, :] = z[:, :NS]
        zim_ref[:, gi, :] = z[:, NS:]

    ar = lam_ref[0]
    ai = lam_ref[1]
    fwd = lax.broadcasted_iota(jnp.int32, (GS, NS), 1) < SSM_STATE
    sre0 = jnp.where(fwd, zre_ref[n_chunks], 0.0)
    sim0 = jnp.where(fwd, zim_ref[n_chunks], 0.0)

    def scan_step(k, carry):
        sre, sim = carry
        kr = n_chunks - 1 - k
        are_ref[k] = sre
        aim_ref[k] = sim
        bre_ref[kr] = sre
        bim_ref[kr] = sim
        zr = jnp.where(fwd, zre_ref[k], zre_ref[kr])
        zi = jnp.where(fwd, zim_ref[k], zim_ref[kr])
        return ar * sre - ai * sim + zr, ar * sim + ai * sre + zi

    lax.fori_loop(0, n_chunks, scan_step, (sre0, sim0))

    fwd_rows = lax.broadcasted_iota(jnp.int32, (n_chunks, NS), 1) < SSM_STATE
    for gi in range(GS):
        s_re = jnp.where(fwd_rows, are_ref[:, gi, :], bre_ref[:, gi, :])
        s_im = jnp.where(fwd_rows, aim_ref[:, gi, :], bim_ref[:, gi, :])
        scat = jnp.concatenate([s_re, s_im], axis=1).astype(BF16)
        y = (jnp.dot(u_ref[gi, :n_chunks, :], mm_ref[gi], preferred_element_type=F32)
             + jnp.dot(scat, qq_ref[gi], preferred_element_type=F32))
        y_ref[gi] = y.astype(BF16)


def _s5(u_chunks, pp, mm, qq, lam16, n_chunks):
    GS = S5_GROUPS_PER_STEP
    n_rows = u_chunks.shape[1]
    W = S5_CHUNK * SSM_GROUP
    NS = 2 * SSM_STATE
    mat = lambda: pl.BlockSpec((GS, W, W), lambda g: (g, 0, 0))
    return pl.pallas_call(
        functools.partial(_s5_body, n_chunks=n_chunks),
        grid=(SSM_GROUPS // GS,),
        in_specs=[
            pl.BlockSpec((GS, n_rows, W), lambda g: (g, 0, 0)),
            mat(), mat(), mat(),
            pl.BlockSpec((2, GS, NS), lambda g: (0, g, 0)),
        ],
        out_specs=pl.BlockSpec((GS, n_chunks, W), lambda g: (g, 0, 0)),
        out_shape=jax.ShapeDtypeStruct((SSM_GROUPS, n_chunks, W), BF16),
        scratch_shapes=[pltpu.VMEM((n_rows, GS, NS), F32), pltpu.VMEM((n_rows, GS, NS), F32)]
        + [pltpu.VMEM((n_chunks, GS, NS), F32) for _ in range(4)],
        compiler_params=pltpu.CompilerParams(
            dimension_semantics=("arbitrary",), vmem_limit_bytes=VMEM_LIMIT),
        name="s5_chunked",
    )(u_chunks, pp, mm, qq, lam16)


def _mix_out_body(x_ref, att_ref, y_ref, u_ref, d_ref, wglu_ref, bglu_ref, sn_ref, wout_ref,
                  nf_ref, rwt_ref, sgu_ref, sd_ref, h_ref, hn_ref, sc_ref):
    y = y_ref[...].astype(F32) + d_ref[...] * u_ref[...].astype(F32)
    y = jax.nn.gelu(y)
    gate = jax.nn.sigmoid(jnp.dot(y.astype(BF16), wglu_ref[...], preferred_element_type=F32) + bglu_ref[...])
    s = y * gate
    ms = jnp.mean(s * s, axis=-1, keepdims=True)
    ssm = (s * lax.rsqrt(ms + EPS) * sn_ref[...]).astype(BF16)
    mixed = jnp.concatenate([att_ref[...], ssm], axis=1)
    h = x_ref[...] + jnp.dot(mixed, wout_ref[...], preferred_element_type=F32)
    ms = jnp.mean(h * h, axis=-1, keepdims=True)
    hn = h * lax.rsqrt(ms + EPS) * nf_ref[...]
    hnb = hn.astype(BF16)
    hn_ref[...] = hnb
    logits_t = lax.dot_general(rwt_ref[...], hn, (((1,), (1,)), ((), ())),
                               precision=lax.Precision.HIGHEST, preferred_element_type=F32)
    sc_ref[...] = jax.nn.sigmoid(logits_t)
    gu = jnp.dot(hnb, sgu_ref[...], preferred_element_type=F32)
    act = (jax.nn.silu(gu[:, :SHARED_HIDDEN]) * gu[:, SHARED_HIDDEN:]).astype(BF16)
    h_ref[...] = h + jnp.dot(act, sd_ref[...], preferred_element_type=F32)


def _mix_out(x2, att, y, u, d_skip, w_glu, b_glu, ssm_norm, w_out, norm_ffn, router_wt,
             sh_gate_up, sh_down):
    seq = x2.shape[0]
    R = MIX_ROW_BLOCK
    res = lambda shape: _resident(shape, lambda i: (0, 0))
    return pl.pallas_call(
        _mix_out_body,
        grid=(seq // R,),
        in_specs=[
            pl.BlockSpec((R, D_MODEL), lambda i: (i, 0)),
            pl.BlockSpec((R, ATT_WIDTH), lambda i: (i, 0)),
            pl.BlockSpec((R, SSM_WIDTH), lambda i: (i, 0)),
            pl.BlockSpec((R, SSM_WIDTH), lambda i: (i, 0)),
            res((1, SSM_WIDTH)), res((SSM_WIDTH, SSM_WIDTH)), res((1, SSM_WIDTH)), res((1, SSM_WIDTH)),
            res((D_MODEL, D_MODEL)), res((1, D_MODEL)), res((N_EXPERTS, D_MODEL)),
            res((D_MODEL, 2 * SHARED_HIDDEN)), res((SHARED_HIDDEN, D_MODEL)),
        ],
        out_specs=[
            pl.BlockSpec((R, D_MODEL), lambda i: (i, 0)),
            pl.BlockSpec((R, D_MODEL), lambda i: (i, 0)),
            pl.BlockSpec((N_EXPERTS, R), lambda i: (0, i)),
        ],
        out_shape=[
            jax.ShapeDtypeStruct((seq, D_MODEL), F32),
            jax.ShapeDtypeStruct((seq, D_MODEL), BF16),
            jax.ShapeDtypeStruct((N_EXPERTS, seq), F32),
        ],
        compiler_params=pltpu.CompilerParams(
            dimension_semantics=("arbitrary",), vmem_limit_bytes=VMEM_LIMIT),
        name="mix_out_shared",
    )(x2, att, y, u, d_skip, w_glu, b_glu, ssm_norm, w_out, norm_ffn, router_wt,
      sh_gate_up, sh_down)


def _route_body(sc_ref, rb_ref, tri_ref, w_ref, slot_ref, cnt_ref):
    scores = sc_ref[...]
    R = scores.shape[1]
    per_group = N_EXPERTS // N_EXPERT_GROUPS
    choice = scores + rb_ref[...]
    c3 = choice.reshape(N_EXPERT_GROUPS, per_group, R)
    within = lax.broadcasted_iota(jnp.int32, c3.shape, 1)
    m1 = jnp.max(c3, axis=1, keepdims=True)
    first = jnp.min(jnp.where(c3 == m1, within, per_group), axis=1, keepdims=True)
    m2 = jnp.max(jnp.where(within == first, -jnp.inf, c3), axis=1, keepdims=True)
    grp = (m1 + m2).reshape(N_EXPERT_GROUPS, R)
    gidx = lax.broadcasted_iota(jnp.int32, grp.shape, 0)
    grank = jnp.zeros(grp.shape, jnp.int32)
    for b in range(N_EXPERT_GROUPS):
        gb = grp[b:b + 1, :]
        grank += ((gb > grp) | ((gb == grp) & (b < gidx))).astype(jnp.int32)
    gmask = grank < TOPK_GROUPS
    emask = jnp.broadcast_to(gmask[:, None, :], c3.shape).reshape(N_EXPERTS, R)
    val = jnp.where(emask, choice, -jnp.inf)
    eidx = lax.broadcasted_iota(jnp.int32, val.shape, 0)
    rank = jnp.zeros(val.shape, jnp.int32)
    for e in range(N_EXPERTS):
        ve = val[e:e + 1, :]
        rank += ((ve > val) | ((ve == val) & (e < eidx))).astype(jnp.int32)
    sel = rank < TOP_K
    w = jnp.where(sel, scores, 0.0)
    w_ref[...] = w / jnp.sum(w, axis=0, keepdims=True) * ROUTED_SCALE
    cum = jnp.dot(jnp.where(sel, 1.0, 0.0).astype(BF16), tri_ref[...], preferred_element_type=F32)
    slot_ref[...] = jnp.where(sel, cum - 1.0, NOT_ROUTED)
    cnt_ref[0] = jnp.broadcast_to(cum[:, R - 1:R], (N_EXPERTS, LANES))


def _route(scores_t, router_bias, tri):
    seq = scores_t.shape[1]
    R = MOE_TOKEN_BLOCK
    n_blk = seq // R
    blk = pl.BlockSpec((N_EXPERTS, R), lambda i: (0, i))
    return pl.pallas_call(
        _route_body,
        grid=(n_blk,),
        in_specs=[blk, _resident((N_EXPERTS, 1), lambda i: (0, 0)), _resident((R, R), lambda i: (0, 0))],
        out_specs=[blk, blk, pl.BlockSpec((1, N_EXPERTS, LANES), lambda i: (i, 0, 0))],
        out_shape=[jax.ShapeDtypeStruct((N_EXPERTS, seq), F32),
                   jax.ShapeDtypeStruct((N_EXPERTS, seq), F32),
                   jax.ShapeDtypeStruct((n_blk, N_EXPERTS, LANES), F32)],
        compiler_params=pltpu.CompilerParams(dimension_semantics=("arbitrary",)),
        name="route",
    )(scores_t, router_bias, tri)


def _moe_tables(cnt):
    n_blk = cnt.shape[0]
    U, RB, SB = MOE_UNIT, MOE_BLOCK_ROWS, MOE_SLOT_BLOCK
    NU = SB // U
    E = N_EXPERTS
    pc = (cnt + U - 1) // U * U
    off = jnp.cumsum(pc, axis=1) - pc
    upc_t = (pc // U).T
    cum_t = jnp.cumsum(upc_t, axis=1)
    units_e = cum_t[:, -1]
    nblk_e = (units_e + NU - 1) // NU
    blk_end = jnp.cumsum(nblk_e)
    blk_start = blk_end - nblk_e
    n_act = blk_end[-1]
    max_blocks = n_blk * (RB // SB) + E
    i = jnp.arange(max_blocks, dtype=jnp.int32)
    active = i < n_act
    count_le = lambda edges, v: jnp.sum((edges <= v[..., None]).astype(jnp.int32), axis=-1)
    last_e = jnp.minimum(count_le(blk_end, n_act - 1), E - 1)
    be = jnp.where(active, jnp.minimum(count_le(blk_end[None, :], i), E - 1), last_e)
    oh_e = be[:, None] == jnp.arange(E, dtype=jnp.int32)[None, :]
    pick_e = lambda v: jnp.sum(jnp.where(oh_e, v[None, :], 0), axis=1)
    pick_e2 = lambda m: jnp.sum(jnp.where(oh_e[:, :, None], m[None, :, :], 0), axis=1)
    bstart_i = pick_e(blk_start)
    first = active & (i == bstart_i)
    local = (i - bstart_i)[:, None] * NU + jnp.arange(NU, dtype=jnp.int32)[None, :]
    valid = active[:, None] & (local < pick_e(units_e)[:, None])
    cum_i = pick_e2(cum_t)
    b_of = jnp.minimum(count_le(cum_i[:, None, :], local), n_blk - 1)
    oh_b = b_of[:, :, None] == jnp.arange(n_blk, dtype=jnp.int32)[None, None, :]
    pick_b = lambda m: jnp.sum(jnp.where(oh_b, m[:, None, :], 0), axis=2)
    seg_start = pick_b(cum_i) - pick_b(pick_e2(upc_t))
    unit = (b_of * RB + pick_b(pick_e2(off.T))) // U + (local - seg_start)
    spare = n_blk * RB // U
    src = jnp.where(valid, unit, spare)
    dst = jnp.where(valid, unit,
                    spare + (1 + i % 2)[:, None] * NU + jnp.arange(NU, dtype=jnp.int32)[None, :])
    short = active & (pick_e(units_e) - (i - bstart_i) * NU <= NU // 2)
    i32 = lambda a: a.astype(jnp.int32)
    flags = i32(first) + 2 * i32(short)
    return (off, pc, i32(src.reshape(-1)), i32(dst.reshape(-1)), i32(be), flags,
            i32(n_act.reshape(1)))


def _split_hi_lo(pos):
    hi = jnp.floor(pos * (1.0 / 64.0))
    return hi.astype(BF16), (pos - 64.0 * hi).astype(BF16)


def _dispatch_body(hn_ref, slot_ref, offc_ref, offl_ref, pcl_ref, xs_ref, *, n_blk):
    b = pl.program_id(0)
    RB, TB = MOE_BLOCK_ROWS, MOE_TOKEN_BLOCK

    @pl.when(b == n_blk)
    def _():
        xs_ref[...] = jnp.zeros(xs_ref.shape, BF16)

    @pl.when(b < n_blk)
    def _():
        pos = slot_ref[...] + offc_ref[0]
        hi, lo = _split_hi_lo(pos)
        r = lax.broadcasted_iota(jnp.int32, (RB, N_EXPERTS), 0).astype(F32)
        off = offl_ref[0]
        owner = jnp.where((r >= off) & (r < off + pcl_ref[0]), 1.0, 0.0).astype(BF16)
        p = (64.0 * jnp.dot(owner, hi, preferred_element_type=F32)
             + jnp.dot(owner, lo, preferred_element_type=F32))
        rr = lax.broadcasted_iota(jnp.int32, (RB, TB), 0).astype(F32)
        onehot = jnp.where(jnp.abs(p - rr) < 0.5, 1.0, 0.0).astype(BF16)
        x = hn_ref[...]
        C = MOE_SLOT_BLOCK
        for c in range(RB // C):
            xs_ref[0, c * C:(c + 1) * C, :] = jnp.dot(
                onehot[c * C:(c + 1) * C], x, preferred_element_type=F32).astype(BF16)


def _dispatch(hn, slots, off, pc):
    n_blk = off.shape[0]
    RB, TB = MOE_BLOCK_ROWS, MOE_TOKEN_BLOCK
    clamp = lambda b: jnp.minimum(b, n_blk - 1)
    return pl.pallas_call(
        functools.partial(_dispatch_body, n_blk=n_blk),
        grid=(n_blk + 1,),
        in_specs=[
            pl.BlockSpec((TB, D_MODEL), lambda b: (clamp(b), 0)),
            pl.BlockSpec((N_EXPERTS, TB), lambda b: (0, clamp(b))),
            pl.BlockSpec((1, N_EXPERTS, 1), lambda b: (clamp(b), 0, 0)),
            pl.BlockSpec((1, 1, N_EXPERTS), lambda b: (clamp(b), 0, 0)),
            pl.BlockSpec((1, 1, N_EXPERTS), lambda b: (clamp(b), 0, 0)),
        ],
        out_specs=pl.BlockSpec((1, RB, D_MODEL), lambda b: (b, 0, 0)),
        out_shape=jax.ShapeDtypeStruct((n_blk + 1, RB, D_MODEL), BF16),
        compiler_params=pltpu.CompilerParams(
            dimension_semantics=("arbitrary",), vmem_limit_bytes=VMEM_LIMIT),
        name="moe_dispatch",
    )(hn, slots, off[:, :, None], off[:, None, :], pc[:, None, :])


def _expert_body(src_ref, dst_ref, bexp_ref, first_ref, nact_ref,
                 xs_hbm, wg_ref, wu_ref, wd_ref, ys_hbm,
                 xbuf, ybuf, wgub, wdb, sem_in, sem_out):
    del bexp_ref
    i = pl.program_id(0)
    n_act = nact_ref[0]
    cur = lax.rem(i, 2)
    U = MOE_UNIT
    NU = MOE_SLOT_BLOCK // U

    def in_copy(blk, buf, u):
        return pltpu.make_async_copy(xs_hbm.at[src_ref[blk * NU + u]],
                                     xbuf.at[buf, pl.ds(u * U, U)], sem_in.at[buf])

    def out_copy(blk, buf, u):
        return pltpu.make_async_copy(ybuf.at[buf, pl.ds(u * U, U)],
                                     ys_hbm.at[dst_ref[blk * NU + u]], sem_out.at[buf])

    @pl.when(i == 0)
    def _():
        for u in range(NU):
            in_copy(0, 0, u).start()

    @pl.when(i < n_act)
    def _():
        @pl.when(i + 1 < n_act)
        def _():
            for u in range(NU):
                in_copy(i + 1, 1 - cur, u).start()

        @pl.when(first_ref[i] % 2 == 1)
        def _():
            wgub[:, :EXPERT_HIDDEN] = wg_ref[0].astype(BF16)
            wgub[:, EXPERT_HIDDEN:] = wu_ref[0].astype(BF16)
            wdb[...] = wd_ref[0].astype(BF16)

        for u in range(NU):
            in_copy(i, cur, u).wait()

        @pl.when(i >= 2)
        def _():
            for u in range(NU):
                out_copy(i - 2, cur, u).wait()

        def swiglu_rows(rows):
            x = xbuf[cur, :rows]
            gu = jnp.dot(x, wgub[...], preferred_element_type=F32)
            act = (jax.nn.silu(gu[:, :EXPERT_HIDDEN]) * gu[:, EXPERT_HIDDEN:]).astype(BF16)
            ybuf[cur, :rows] = jnp.dot(act, wdb[...], preferred_element_type=F32).astype(BF16)

        half = MOE_SLOT_BLOCK // 2

        @pl.when(first_ref[i] < 2)
        def _():
            swiglu_rows(MOE_SLOT_BLOCK)

        @pl.when(first_ref[i] >= 2)
        def _():
            swiglu_rows(half)
            ybuf[cur, half:] = jnp.zeros((half, D_MODEL), BF16)

        for u in range(NU):
            out_copy(i, cur, u).start()

        @pl.when(i == n_act - 1)
        def _():
            for u in range(NU):
                out_copy(i, cur, u).wait()

            @pl.when(i >= 1)
            def _():
                for u in range(NU):
                    out_copy(i - 1, 1 - cur, u).wait()


def _experts(xs, src, dst, bexp, first, n_act, wg, wu, wd):
    U, SB = MOE_UNIT, MOE_SLOT_BLOCK
    n_units = xs.shape[0] * xs.shape[1] // U
    max_blocks = bexp.shape[0]
    unit_view = lambda a: a.reshape(n_units, U, D_MODEL)
    wspec = lambda shape: pl.BlockSpec((1,) + shape, lambda i, src, dst, bexp, first, nact: (bexp[i], 0, 0))
    grid_spec = pltpu.PrefetchScalarGridSpec(
        num_scalar_prefetch=5,
        grid=(max_blocks,),
        in_specs=[
            pl.BlockSpec(memory_space=pl.ANY),
            wspec((D_MODEL, EXPERT_HIDDEN)), wspec((D_MODEL, EXPERT_HIDDEN)), wspec((EXPERT_HIDDEN, D_MODEL)),
        ],
        out_specs=pl.BlockSpec(memory_space=pl.ANY),
        scratch_shapes=[
            pltpu.VMEM((2, SB, D_MODEL), BF16), pltpu.VMEM((2, SB, D_MODEL), BF16),
            pltpu.VMEM((D_MODEL, 2 * EXPERT_HIDDEN), BF16), pltpu.VMEM((EXPERT_HIDDEN, D_MODEL), BF16),
            pltpu.SemaphoreType.DMA((2,)), pltpu.SemaphoreType.DMA((2,)),
        ],
    )
    ys = pl.pallas_call(
        _expert_body,
        grid_spec=grid_spec,
        out_shape=jax.ShapeDtypeStruct((n_units, U, D_MODEL), BF16),
        input_output_aliases={5: 0},
        compiler_params=pltpu.CompilerParams(
            dimension_semantics=("arbitrary",), vmem_limit_bytes=VMEM_LIMIT),
        name="moe_experts",
    )(src, dst, bexp, first, n_act, unit_view(xs), wg, wu, wd)
    return ys.reshape(xs.shape)


def _combine_body(h_ref, ys_ref, slot_ref, w_ref, offl_ref, offc_ref, pcc_ref, o_ref):
    RB, TB = MOE_BLOCK_ROWS, MOE_TOKEN_BLOCK
    pos = slot_ref[...] + offl_ref[0]
    hi, lo = _split_hi_lo(pos)
    r = lax.broadcasted_iota(jnp.int32, (N_EXPERTS, RB), 1).astype(F32)
    off = offc_ref[0]
    owner = jnp.where((r >= off) & (r < off + pcc_ref[0]), 1.0, 0.0).astype(BF16)
    p = (64.0 * jnp.dot(hi, owner, preferred_element_type=F32)
         + jnp.dot(lo, owner, preferred_element_type=F32))
    wr = jnp.dot(w_ref[...].astype(BF16), owner, preferred_element_type=F32)
    rr = lax.broadcasted_iota(jnp.int32, (TB, RB), 1).astype(F32)
    gather_w = jnp.where(jnp.abs(p - rr) < 0.5, wr, 0.0).astype(BF16)
    o_ref[...] = h_ref[...] + jnp.dot(gather_w, ys_ref[0], preferred_element_type=F32)


def _combine(h, ys, slots_tok, w_tok, off, pc):
    n_blk = off.shape[0]
    RB, TB = MOE_BLOCK_ROWS, MOE_TOKEN_BLOCK
    seq = h.shape[0]
    return pl.pallas_call(
        _combine_body,
        grid=(n_blk,),
        in_specs=[
            pl.BlockSpec((TB, D_MODEL), lambda b: (b, 0)),
            pl.BlockSpec((1, RB, D_MODEL), lambda b: (b, 0, 0)),
            pl.BlockSpec((TB, N_EXPERTS), lambda b: (b, 0)),
            pl.BlockSpec((TB, N_EXPERTS), lambda b: (b, 0)),
            pl.BlockSpec((1, 1, N_EXPERTS), lambda b: (b, 0, 0)),
            pl.BlockSpec((1, N_EXPERTS, 1), lambda b: (b, 0, 0)),
            pl.BlockSpec((1, N_EXPERTS, 1), lambda b: (b, 0, 0)),
        ],
        out_specs=pl.BlockSpec((TB, D_MODEL), lambda b: (b, 0)),
        out_shape=jax.ShapeDtypeStruct((seq, D_MODEL), F32),
        compiler_params=pltpu.CompilerParams(
            dimension_semantics=("arbitrary",), vmem_limit_bytes=VMEM_LIMIT),
        name="moe_combine",
    )(h, ys, slots_tok, w_tok, off[:, None, :], off[:, :, None], pc[:, :, None])


def kernel(x, meta_tokens, rel_bias, norm_mix, w_in, q_norm, k_norm, lam_q1, lam_k1, lam_q2, lam_k2, subln, ssm_a_re, ssm_a_im, ssm_log_step, ssm_b_re, ssm_b_im, ssm_c_re, ssm_c_im, ssm_d, w_glu, b_glu, ssm_norm, w_out, norm_ffn, router_w, router_bias, w_gate, w_up, w_down, shared_gate, shared_up, shared_down):
    batch, seq, d = x.shape
    assert batch == 1 and d == D_MODEL and seq % ROW_BLOCK == 0 and seq % ATT_BLOCK == 0
    assert norm_mix.shape[0] == 1, "single layer"
    x2 = x.reshape(seq, d)
    meta_pad = jnp.zeros((ROW_BLOCK, d), F32).at[:N_META].set(meta_tokens.astype(F32))
    seg = jnp.kron(jnp.eye(QK_WIDTH // HEAD_DIM, dtype=F32),
                   jnp.full((HEAD_DIM, HEAD_DIM), 1.0 / HEAD_DIM, F32)).astype(BF16)
    qg = jnp.tile(q_norm[0].astype(F32), QK_WIDTH // HEAD_DIM)[None] * (HEAD_DIM ** -0.5)
    kg = jnp.tile(k_norm[0].astype(F32), QK_WIDTH // HEAD_DIM)[None]

    proj, u = _inproj(x2, meta_pad, norm_mix[0][None], w_in[0].astype(BF16), seg, qg, kg)

    score_bound = (BOUND_MARGIN * HEAD_DIM ** 0.5 * jnp.max(jnp.abs(q_norm[0].astype(F32)))
                   * jnp.max(jnp.abs(k_norm[0].astype(F32)))
                   + jnp.max(jnp.abs(rel_bias.astype(F32)))).reshape(1)
    att = _attention(proj, rel_bias.astype(F32), score_bound, lam_q1[0][None], lam_k1[0][None],
                     lam_q2[0][None], lam_k2[0][None], subln[0][None], seq)

    n_rows = proj.shape[0] // S5_CHUNK
    n_chunks = seq // S5_CHUNK
    perm = _lane_regroup_matrix()
    u_chunks = _to_chunks(u.reshape(n_rows, S5_CHUNK, SSM_WIDTH), perm)
    pp, mm, qq, lam16 = _s5_prep(ssm_a_re[0].astype(F32), ssm_a_im[0].astype(F32),
                                 ssm_log_step[0].astype(F32), ssm_b_re[0].astype(F32),
                                 ssm_b_im[0].astype(F32), ssm_c_re[0].astype(F32),
                                 ssm_c_im[0].astype(F32))
    y_chunks = _s5(u_chunks, pp, mm, qq, lam16, n_chunks)
    y = _from_chunks(y_chunks, perm.T).reshape(seq, SSM_WIDTH)

    h, hn, scores_t = _mix_out(
        x2, att, y, u, ssm_d[0][None].astype(F32), w_glu[0].astype(BF16), b_glu[0][None].astype(F32),
        ssm_norm[0][None].astype(F32), w_out[0].astype(BF16), norm_ffn[0][None].astype(F32),
        router_w[0].astype(F32).T,
        jnp.concatenate([shared_gate[0], shared_up[0]], axis=1).astype(BF16), shared_down[0].astype(BF16))

    tb = MOE_TOKEN_BLOCK
    tri = (jnp.arange(tb)[:, None] <= jnp.arange(tb)[None, :]).astype(BF16)
    wts_t, slots_t, cnt = _route(scores_t, router_bias[0].astype(F32)[:, None], tri)
    off, pc, src, dst, bexp, first, n_act = _moe_tables(cnt[:, :, 0].astype(jnp.int32))
    off_f, pc_f = off.astype(F32), pc.astype(F32)
    xs = _dispatch(hn, slots_t, off_f, pc_f)
    ys = _experts(xs, src, dst, bexp, first, n_act, w_gate[0], w_up[0], w_down[0])
    out = _combine(h, ys, slots_t.T, wts_t.T, off_f, pc_f)
    return out.reshape(batch, seq, d)
```

```python
import functools
import math

import jax
import jax.numpy as jnp
from jax import lax
from jax.experimental import pallas as pl
from jax.experimental.pallas import tpu as pltpu

F32 = jnp.float32
BF16 = jnp.bfloat16

D_MODEL = 2048
N_META = 16
ATT_WIDTH = 1024
SSM_WIDTH = 1024
HEAD_DIM = 64
V_DIM = 128
HEADS = 8
QK_WIDTH = 1024
IN_WIDTH = 4096
V_OFFSET = 2 * QK_WIDTH
PROJ_WIDTH = V_OFFSET + 2 * ATT_WIDTH
SSM_GROUP = 16
SSM_GROUPS = 64
SSM_STATE = 64
N_BUCKETS = 32
MAX_DISTANCE = 128
N_EXPERTS = 64
TOP_K = 8
N_EXPERT_GROUPS = 8
TOPK_GROUPS = 4
EXPERT_HIDDEN = 512
SHARED_HIDDEN = 512
ROUTED_SCALE = 2.5
EPS = 1e-6
LAMBDA_INIT = 0.8 - 0.6 * math.exp(-0.3 * 0)

ROW_BLOCK = 512
ATT_BLOCK = 512
S5_CHUNK = 16
S5_GROUPS_PER_STEP = 8
S5_OCTET_WIDTH = S5_GROUPS_PER_STEP * S5_CHUNK * SSM_GROUP
assert S5_GROUPS_PER_STEP * SSM_GROUP == 128
MOE_TOKEN_BLOCK = 256
MOE_UNIT = 16
MOE_SLOT_BLOCK = 512
MOE_BLOCK_ROWS = -(-(MOE_TOKEN_BLOCK * TOP_K + N_EXPERTS * (MOE_UNIT - 1)) // MOE_SLOT_BLOCK) * MOE_SLOT_BLOCK
NOT_ROUTED = -1e6
MIX_ROW_BLOCK = 256
NEG_BIG = -1e30
MAX_EXP_RANGE = 80.0
BOUND_MARGIN = 1.02
LANES = 128
VMEM_LIMIT = 56 * 2 ** 20


def _resident(shape, index_map):
    return pl.BlockSpec(shape, index_map, pipeline_mode=pl.Buffered(1))


def _inproj_body(x_ref, meta_ref, g_ref, w_ref, seg_ref, qg_ref, kg_ref, o_ref, u_ref, *, n_xblk):
    i = pl.program_id(0)

    def run(src_ref):
        xv = src_ref[...]
        ms = jnp.mean(xv * xv, axis=-1, keepdims=True)
        hn = (xv * lax.rsqrt(ms + EPS) * g_ref[...]).astype(BF16)
        sec = IN_WIDTH // 4
        for s in range(4):
            ps = jnp.dot(hn, w_ref[:, s * sec:(s + 1) * sec], preferred_element_type=F32)
            if s < 2:
                gain = qg_ref if s == 0 else kg_ref
                msq = jnp.dot((ps * ps).astype(BF16), seg_ref[...], preferred_element_type=F32)
                ps = ps * lax.rsqrt(msq + EPS) * gain[...]
            if s == 3:
                u_ref[...] = ps
                continue
            pb = ps.astype(BF16)
            if s < 2:
                o_ref[:, s * sec:(s + 1) * sec] = pb
            else:
                lane = lax.broadcasted_iota(jnp.int32, (pb.shape[0], V_DIM), 1)
                ones_col = jnp.where(lane == 0, 1.0, 0.0).astype(BF16)
                for hh in range(HEADS):
                    base = V_OFFSET + hh * 2 * V_DIM
                    o_ref[:, base:base + V_DIM] = pb[:, hh * V_DIM:(hh + 1) * V_DIM]
                    o_ref[:, base + V_DIM:base + 2 * V_DIM] = ones_col

    @pl.when(i < n_xblk)
    def _():
        run(x_ref)

    @pl.when(i == n_xblk)
    def _():
        run(meta_ref)


def _inproj(x2, meta_pad, gain, w_bf, seg, qg, kg):
    seq = x2.shape[0]
    n_xblk = seq // ROW_BLOCK
    rows = seq + ROW_BLOCK
    return pl.pallas_call(
        functools.partial(_inproj_body, n_xblk=n_xblk),
        grid=(n_xblk + 1,),
        in_specs=[
            pl.BlockSpec((ROW_BLOCK, D_MODEL), lambda i: (jnp.minimum(i, n_xblk - 1), 0)),
            _resident((ROW_BLOCK, D_MODEL), lambda i: (0, 0)),
            _resident((1, D_MODEL), lambda i: (0, 0)),
            _resident((D_MODEL, IN_WIDTH), lambda i: (0, 0)),
            _resident((QK_WIDTH, QK_WIDTH), lambda i: (0, 0)),
            _resident((1, QK_WIDTH), lambda i: (0, 0)),
            _resident((1, QK_WIDTH), lambda i: (0, 0)),
        ],
        out_specs=[pl.BlockSpec((ROW_BLOCK, PROJ_WIDTH), lambda i: (i, 0)),
                   pl.BlockSpec((ROW_BLOCK, SSM_WIDTH), lambda i: (i, 0))],
        out_shape=[jax.ShapeDtypeStruct((rows, PROJ_WIDTH), BF16),
                   jax.ShapeDtypeStruct((rows, SSM_WIDTH), F32)],
        compiler_params=pltpu.CompilerParams(
            dimension_semantics=("arbitrary",), vmem_limit_bytes=VMEM_LIMIT),
        name="inproj",
    )(x2, meta_pad, gain, w_bf, seg, qg, kg)


def _t5_bias(rel, tab_ref, h):
    half = N_BUCKETS // 2
    exact = half // 2
    n = jnp.abs(rel)
    nf = jnp.maximum(n, 1).astype(F32)
    large = exact + (jnp.log(nf / exact) / math.log(MAX_DISTANCE / exact) * (half - exact)).astype(jnp.int32)
    large = jnp.minimum(large, half - 1)
    bucket = jnp.where(rel > 0, half, 0) + jnp.where(n < exact, n, large)
    out = jnp.zeros(rel.shape, F32)
    for b in range(N_BUCKETS):
        out = jnp.where(bucket == b, tab_ref[b, h], out)
    return out


def _attn_body(tab_ref, bound_ref, q_ref, k_ref, v_ref, lq1_ref, lk1_ref, lq2_ref, lk2_ref,
               subln_ref, o_ref, bias_ref, acc1_ref, acc2_ref, m1_ref, m2_ref, *, n_main):
    T = ATT_BLOCK
    h = pl.program_id(0)
    qi = pl.program_id(1)
    bound = bound_ref[0]

    @pl.when(qi == 0)
    def _():
        offsets = (-T, 0, T, -N_META, -N_META - T, -2 * T, 2 * T)
        for kind, off in enumerate(offsets):
            masked = kind in (3, 4)

            def rows(rc, carry, off=off, masked=masked, kind=kind):
                r0 = pl.multiple_of(rc * 8, 8)
                r = r0 + lax.broadcasted_iota(jnp.int32, (8, T), 0)
                c = lax.broadcasted_iota(jnp.int32, (8, T), 1)
                b = _t5_bias(off + c - r, tab_ref, h) - bound
                if masked:
                    b = jnp.where(c < N_META, b, NEG_BIG)
                bias_ref[kind, pl.ds(r0, 8), :] = b
                return carry

            lax.fori_loop(0, T // 8, rows, 0)

    acc1_ref[...] = jnp.zeros(acc1_ref.shape, F32)
    acc2_ref[...] = jnp.zeros(acc2_ref.shape, F32)

    q = q_ref[...]
    q1 = q[:, :HEAD_DIM]
    q2 = q[:, HEAD_DIM:]
    nt = (((1,), (1,)), ((), ()))

    def tile(ki):
        koff = pl.multiple_of(ki * T, T)
        kb = k_ref[pl.ds(koff, T), :]
        va = v_ref[pl.ds(koff, T), :]
        d = ki - qi
        kind = jnp.where(ki == n_main, jnp.where(qi == 0, 3, 4),
                         jnp.where(d <= -2, 5, jnp.where(d >= 2, 6, d + 1)))
        return kb, va, bias_ref[kind]

    def bounded_step(ki, carry):
        kb, va, bias = tile(ki)
        s1 = lax.dot_general(q1, kb[:, :HEAD_DIM], nt, preferred_element_type=F32) + bias
        acc1_ref[...] += jnp.dot(jnp.exp(s1).astype(BF16), va, preferred_element_type=F32)
        s2 = lax.dot_general(q2, kb[:, HEAD_DIM:], nt, preferred_element_type=F32) + bias
        acc2_ref[...] += jnp.dot(jnp.exp(s2).astype(BF16), va, preferred_element_type=F32)
        return carry

    def online_map(s, va, m_ref, acc_ref):
        m_old = m_ref[...]
        m_new = jnp.maximum(m_old, jnp.max(s, axis=-1, keepdims=True))
        p = jnp.exp(s - m_new).astype(BF16)
        acc_ref[...] = (jnp.exp(m_old - m_new) * acc_ref[...]
                        + jnp.dot(p, va, preferred_element_type=F32))
        m_ref[...] = m_new

    def online_step(ki, carry):
        kb, va, bias = tile(ki)
        s1 = lax.dot_general(q1, kb[:, :HEAD_DIM], nt, preferred_element_type=F32) + bias
        online_map(s1, va, m1_ref, acc1_ref)
        s2 = lax.dot_general(q2, kb[:, HEAD_DIM:], nt, preferred_element_type=F32) + bias
        online_map(s2, va, m2_ref, acc2_ref)
        return carry

    no_running_max = 2.0 * bound <= MAX_EXP_RANGE

    @pl.when(no_running_max)
    def _():
        unroll = next(u for u in (33, 11, 3, 2, 1) if (n_main + 1) % u == 0)
        lax.fori_loop(0, n_main + 1, bounded_step, 0, unroll=unroll)

    @pl.when(jnp.logical_not(no_running_max))
    def _():
        m1_ref[...] = jnp.full(m1_ref.shape, -jnp.inf, F32)
        m2_ref[...] = jnp.full(m2_ref.shape, -jnp.inf, F32)
        lax.fori_loop(0, n_main + 1, online_step, 0)

    lam = (jnp.exp(jnp.sum(lq1_ref[...] * lk1_ref[...], axis=-1, keepdims=True))
           - jnp.exp(jnp.sum(lq2_ref[...] * lk2_ref[...], axis=-1, keepdims=True))
           + LAMBDA_INIT)
    a1 = acc1_ref[...]
    a2 = acc2_ref[...]
    o = (a1[:, :V_DIM] / a1[:, V_DIM:V_DIM + 1]
         - lam * (a2[:, :V_DIM] / a2[:, V_DIM:V_DIM + 1]))
    ms = jnp.mean(o * o, axis=-1, keepdims=True)
    o = o * lax.rsqrt(ms + EPS) * subln_ref[...] * (1.0 - LAMBDA_INIT)
    o_ref[...] = o.astype(BF16)


def _attention(proj, rel_bias, score_bound, lq1, lk1, lq2, lk2, subln, seq):
    T = ATT_BLOCK
    n_main = seq // T
    rows = proj.shape[0]
    vec64 = lambda: _resident((1, HEAD_DIM), lambda h, qi: (0, 0))
    return pl.pallas_call(
        functools.partial(_attn_body, n_main=n_main),
        grid=(HEADS, n_main),
        in_specs=[
            pl.BlockSpec(memory_space=pltpu.SMEM),
            pl.BlockSpec(memory_space=pltpu.SMEM),
            pl.BlockSpec((T, 2 * HEAD_DIM), lambda h, qi: (qi, h)),
            pl.BlockSpec((rows, 2 * HEAD_DIM), lambda h, qi: (0, HEADS + h)),
            pl.BlockSpec((rows, 2 * V_DIM), lambda h, qi: (0, V_OFFSET // (2 * V_DIM) + h)),
            vec64(), vec64(), vec64(), vec64(),
            _resident((1, V_DIM), lambda h, qi: (0, 0)),
        ],
        out_specs=pl.BlockSpec((T, V_DIM), lambda h, qi: (qi, h)),
        out_shape=jax.ShapeDtypeStruct((seq, ATT_WIDTH), BF16),
        scratch_shapes=[
            pltpu.VMEM((7, T, T), F32),
            pltpu.VMEM((T, 2 * V_DIM), F32), pltpu.VMEM((T, 2 * V_DIM), F32),
            pltpu.VMEM((T, 1), F32), pltpu.VMEM((T, 1), F32),
        ],
        compiler_params=pltpu.CompilerParams(
            dimension_semantics=("arbitrary", "arbitrary"), vmem_limit_bytes=VMEM_LIMIT),
        name="diff_attention",
    )(rel_bias, score_bound, proj, proj, proj, lq1, lk1, lq2, lk2, subln)


def _lane_regroup_matrix():
    out_lane = jnp.arange(S5_OCTET_WIDTH, dtype=jnp.int32)
    g8 = out_lane // (S5_CHUNK * SSM_GROUP)
    j = (out_lane % (S5_CHUNK * SSM_GROUP)) // SSM_GROUP
    p = out_lane % SSM_GROUP
    in_lane = j * LANES + g8 * SSM_GROUP + p
    return (jnp.arange(S5_OCTET_WIDTH, dtype=jnp.int32)[:, None] == in_lane[None, :]).astype(BF16)


def _to_chunks_body(u_ref, perm_ref, o_ref):
    x = jnp.concatenate([u_ref[:, j, :].astype(BF16) for j in range(S5_CHUNK)], axis=1)
    r = jnp.dot(x, perm_ref[...], preferred_element_type=F32)
    w = S5_CHUNK * SSM_GROUP
    for g8 in range(S5_GROUPS_PER_STEP):
        o_ref[g8] = r[:, g8 * w:(g8 + 1) * w].astype(BF16)


def _to_chunks(u3, perm):
    n_rows = u3.shape[0]
    w = S5_CHUNK * SSM_GROUP
    return pl.pallas_call(
        _to_chunks_body,
        grid=(SSM_GROUPS // S5_GROUPS_PER_STEP,),
        in_specs=[pl.BlockSpec((n_rows, S5_CHUNK, LANES), lambda o: (0, 0, o)),
                  _resident((S5_OCTET_WIDTH, S5_OCTET_WIDTH), lambda o: (0, 0))],
        out_specs=pl.BlockSpec((S5_GROUPS_PER_STEP, n_rows, w), lambda o: (o, 0, 0)),
        out_shape=jax.ShapeDtypeStruct((SSM_GROUPS, n_rows, w), BF16),
        compiler_params=pltpu.CompilerParams(
            dimension_semantics=("arbitrary",), vmem_limit_bytes=VMEM_LIMIT),
        name="s5_to_chunks",
    )(u3, perm)


def _from_chunks_body(y_ref, perm_t_ref, o_ref):
    x = jnp.concatenate([y_ref[g8] for g8 in range(S5_GROUPS_PER_STEP)], axis=1)
    r = jnp.dot(x, perm_t_ref[...], preferred_element_type=F32)
    for j in range(S5_CHUNK):
        o_ref[:, j, :] = r[:, j * LANES:(j + 1) * LANES]


def _from_chunks(y_chunks, perm_t):
    n_chunks = y_chunks.shape[1]
    w = S5_CHUNK * SSM_GROUP
    return pl.pallas_call(
        _from_chunks_body,
        grid=(SSM_GROUPS // S5_GROUPS_PER_STEP,),
        in_specs=[pl.BlockSpec((S5_GROUPS_PER_STEP, n_chunks, w), lambda o: (o, 0, 0)),
                  _resident((S5_OCTET_WIDTH, S5_OCTET_WIDTH), lambda o: (0, 0))],
        out_specs=pl.BlockSpec((n_chunks, S5_CHUNK, LANES), lambda o: (0, 0, o)),
        out_shape=jax.ShapeDtypeStruct((n_chunks, S5_CHUNK, SSM_WIDTH), F32),
        compiler_params=pltpu.CompilerParams(
            dimension_semantics=("arbitrary",), vmem_limit_bytes=VMEM_LIMIT),
        name="s5_from_chunks",
    )(y_chunks, perm_t)


def _s5_prep(a_re, a_im, log_step, b_re, b_im, c_re, c_im):
    C = S5_CHUNK
    dt = jnp.exp(log_step)[..., None]
    decay = jnp.exp(a_re * dt)
    ab_re = decay * jnp.cos(a_im * dt)
    ab_im = decay * jnp.sin(a_im * dt)
    den = a_re * a_re + a_im * a_im
    zr = ab_re - 1.0
    f_re = (zr * a_re + ab_im * a_im) / den
    f_im = (ab_im * a_re - zr * a_im) / den
    bb_re = f_re[..., None] * b_re - f_im[..., None] * b_im
    bb_im = f_re[..., None] * b_im + f_im[..., None] * b_re
    pr, pi = jnp.ones_like(ab_re), jnp.zeros_like(ab_re)
    pw_re, pw_im = [pr], [pi]
    for _ in range(C):
        pr, pi = pr * ab_re - pi * ab_im, pr * ab_im + pi * ab_re
        pw_re.append(pr)
        pw_im.append(pi)
    G, P, W = SSM_GROUPS, SSM_GROUP, C * SSM_GROUP
    pw_re = jnp.stack(pw_re, axis=-1)
    pw_im = jnp.stack(pw_im, axis=-1)
    ct_re = c_re.transpose(0, 1, 3, 2)
    ct_im = c_im.transpose(0, 1, 3, 2)
    cp_re = ct_re[:, :, :, None, :] * pw_re[..., None] - ct_im[:, :, :, None, :] * pw_im[..., None]
    cp_im = ct_re[:, :, :, None, :] * pw_im[..., None] + ct_im[:, :, :, None, :] * pw_re[..., None]
    bt_re = bb_re.transpose(0, 1, 3, 2)
    bt_im = bb_im.transpose(0, 1, 3, 2)

    def response(d):
        prod = (cp_re[d][:, None, :, :C, :] * bt_re[d][:, :, :, None, None]
                - cp_im[d][:, None, :, :C, :] * bt_im[d][:, :, :, None, None])
        return jnp.sum(prod, axis=2)

    ext_f = jnp.pad(response(0).reshape(G, P, W), ((0, 0), (0, 0), (W, 0)))
    ext_r = jnp.pad(response(1)[:, :, ::-1, :].reshape(G, P, W), ((0, 0), (0, 0), (0, W)))
    mm = jnp.stack([ext_f[:, :, W - P * j:2 * W - P * j] + ext_r[:, :, (C - 1 - j) * P:(C - 1 - j) * P + W]
                    for j in range(C)], axis=1).reshape(G, W, W)

    def in_mat(d, reverse_powers):
        pr_ = pw_re[d][:, :, :C].transpose(0, 2, 1)
        pi_ = pw_im[d][:, :, :C].transpose(0, 2, 1)
        if reverse_powers:
            pr_, pi_ = pr_[:, ::-1], pi_[:, ::-1]
        re = pr_[:, :, None, :] * bt_re[d][:, None] - pi_[:, :, None, :] * bt_im[d][:, None]
        im = pr_[:, :, None, :] * bt_im[d][:, None] + pi_[:, :, None, :] * bt_re[d][:, None]
        return re, im

    pf_re, pf_im = in_mat(0, True)
    pr_re, pr_im = in_mat(1, False)
    pp = jnp.concatenate([pf_re, pr_re, pf_im, pr_im], axis=-1).reshape(G, W, 4 * SSM_STATE)

    qq = jnp.concatenate([cp_re[0][:, :, 1:], cp_re[1][:, :, :0:-1],
                          -cp_im[0][:, :, 1:], -cp_im[1][:, :, :0:-1]], axis=1).reshape(G, 4 * SSM_STATE, W)
    lam16 = jnp.stack([jnp.concatenate([pw_re[0][:, :, C], pw_re[1][:, :, C]], axis=-1),
                       jnp.concatenate([pw_im[0][:, :, C], pw_im[1][:, :, C]], axis=-1)], axis=0)
    return pp.astype(BF16), mm.astype(BF16), qq.astype(BF16), lam16


def _s5_body(u_ref, pp_ref, mm_ref, qq_ref, lam_ref, y_ref,
             zre_ref, zim_ref, are_ref, aim_ref, bre_ref, bim_ref, *, n_chunks):
    GS = S5_GROUPS_PER_STEP
    NS = 2 * SSM_STATE
    for gi in range(GS):
        z = jnp.dot(u_ref[gi], pp_ref[gi], preferred_element_type=F32)
        zre_ref[:, gi, :] = z[:, :NS]
        zim_ref[:, gi, :] = z[:, NS:]

    ar = lam_ref[0]
    ai = lam_ref[1]
    fwd = lax.broadcasted_iota(jnp.int32, (GS, NS), 1) < SSM_STATE
    sre0 = jnp.where(fwd, zre_ref[n_chunks], 0.0)
    sim0 = jnp.where(fwd, zim_ref[n_chunks], 0.0)

    def scan_step(k, carry):
        sre, sim = carry
        kr = n_chunks - 1 - k
        are_ref[k] = sre
        aim_ref[k] = sim
        bre_ref[kr] = sre
        bim_ref[kr] = sim
        zr = jnp.where(fwd, zre_ref[k], zre_ref[kr])
        zi = jnp.where(fwd, zim_ref[k], zim_ref[kr])
        return ar * sre - ai * sim + zr, ar * sim + ai * sre + zi

    lax.fori_loop(0, n_chunks, scan_step, (sre0, sim0))

    fwd_rows = lax.broadcasted_iota(jnp.int32, (n_chunks, NS), 1) < SSM_STATE
    for gi in range(GS):
        s_re = jnp.where(fwd_rows, are_ref[:, gi, :], bre_ref[:, gi, :])
        s_im = jnp.where(fwd_rows, aim_ref[:, gi, :], bim_ref[:, gi, :])
        scat = jnp.concatenate([s_re, s_im], axis=1).astype(BF16)
        y = (jnp.dot(u_ref[gi, :n_chunks, :], mm_ref[gi], preferred_element_type=F32)
             + jnp.dot(scat, qq_ref[gi], preferred_element_type=F32))
        y_ref[gi] = y.astype(BF16)


def _s5(u_chunks, pp, mm, qq, lam16, n_chunks):
    GS = S5_GROUPS_PER_STEP
    n_rows = u_chunks.shape[1]
    W = S5_CHUNK * SSM_GROUP
    NS = 2 * SSM_STATE
    mat = lambda: pl.BlockSpec((GS, W, W), lambda g: (g, 0, 0))
    return pl.pallas_call(
        functools.partial(_s5_body, n_chunks=n_chunks),
        grid=(SSM_GROUPS // GS,),
        in_specs=[
            pl.BlockSpec((GS, n_rows, W), lambda g: (g, 0, 0)),
            mat(), mat(), mat(),
            pl.BlockSpec((2, GS, NS), lambda g: (0, g, 0)),
        ],
        out_specs=pl.BlockSpec((GS, n_chunks, W), lambda g: (g, 0, 0)),
        out_shape=jax.ShapeDtypeStruct((SSM_GROUPS, n_chunks, W), BF16),
        scratch_shapes=[pltpu.VMEM((n_rows, GS, NS), F32), pltpu.VMEM((n_rows, GS, NS), F32)]
        + [pltpu.VMEM((n_chunks, GS, NS), F32) for _ in range(4)],
        compiler_params=pltpu.CompilerParams(
            dimension_semantics=("arbitrary",), vmem_limit_bytes=VMEM_LIMIT),
        name="s5_chunked",
    )(u_chunks, pp, mm, qq, lam16)


def _mix_out_body(x_ref, att_ref, y_ref, u_ref, d_ref, wglu_ref, bglu_ref, sn_ref, wout_ref,
                  nf_ref, rwt_ref, sgu_ref, sd_ref, h_ref, hn_ref, sc_ref):
    y = y_ref[...].astype(F32) + d_ref[...] * u_ref[...].astype(F32)
    y = jax.nn.gelu(y)
    gate = jax.nn.sigmoid(jnp.dot(y.astype(BF16), wglu_ref[...], preferred_element_type=F32) + bglu_ref[...])
    s = y * gate
    ms = jnp.mean(s * s, axis=-1, keepdims=True)
    ssm = (s * lax.rsqrt(ms + EPS) * sn_ref[...]).astype(BF16)
    mixed = jnp.concatenate([att_ref[...], ssm], axis=1)
    h = x_ref[...] + jnp.dot(mixed, wout_ref[...], preferred_element_type=F32)
    ms = jnp.mean(h * h, axis=-1, keepdims=True)
    hn = h * lax.rsqrt(ms + EPS) * nf_ref[...]
    hnb = hn.astype(BF16)
    hn_ref[...] = hnb
    logits_t = lax.dot_general(rwt_ref[...], hn, (((1,), (1,)), ((), ())),
                               precision=lax.Precision.HIGHEST, preferred_element_type=F32)
    sc_ref[...] = jax.nn.sigmoid(logits_t)
    gu = jnp.dot(hnb, sgu_ref[...], preferred_element_type=F32)
    act = (jax.nn.silu(gu[:, :SHARED_HIDDEN]) * gu[:, SHARED_HIDDEN:]).astype(BF16)
    h_ref[...] = h + jnp.dot(act, sd_ref[...], preferred_element_type=F32)


def _mix_out(x2, att, y, u, d_skip, w_glu, b_glu, ssm_norm, w_out, norm_ffn, router_wt,
             sh_gate_up, sh_down):
    seq = x2.shape[0]
    R = MIX_ROW_BLOCK
    res = lambda shape: _resident(shape, lambda i: (0, 0))
    return pl.pallas_call(
        _mix_out_body,
        grid=(seq // R,),
        in_specs=[
            pl.BlockSpec((R, D_MODEL), lambda i: (i, 0)),
            pl.BlockSpec((R, ATT_WIDTH), lambda i: (i, 0)),
            pl.BlockSpec((R, SSM_WIDTH), lambda i: (i, 0)),
            pl.BlockSpec((R, SSM_WIDTH), lambda i: (i, 0)),
            res((1, SSM_WIDTH)), res((SSM_WIDTH, SSM_WIDTH)), res((1, SSM_WIDTH)), res((1, SSM_WIDTH)),
            res((D_MODEL, D_MODEL)), res((1, D_MODEL)), res((N_EXPERTS, D_MODEL)),
            res((D_MODEL, 2 * SHARED_HIDDEN)), res((SHARED_HIDDEN, D_MODEL)),
        ],
        out_specs=[
            pl.BlockSpec((R, D_MODEL), lambda i: (i, 0)),
            pl.BlockSpec((R, D_MODEL), lambda i: (i, 0)),
            pl.BlockSpec((N_EXPERTS, R), lambda i: (0, i)),
        ],
        out_shape=[
            jax.ShapeDtypeStruct((seq, D_MODEL), F32),
            jax.ShapeDtypeStruct((seq, D_MODEL), BF16),
            jax.ShapeDtypeStruct((N_EXPERTS, seq), F32),
        ],
        compiler_params=pltpu.CompilerParams(
            dimension_semantics=("arbitrary",), vmem_limit_bytes=VMEM_LIMIT),
        name="mix_out_shared",
    )(x2, att, y, u, d_skip, w_glu, b_glu, ssm_norm, w_out, norm_ffn, router_wt,
      sh_gate_up, sh_down)


def _route_body(sc_ref, rb_ref, tri_ref, w_ref, slot_ref, cnt_ref):
    scores = sc_ref[...]
    R = scores.shape[1]
    per_group = N_EXPERTS // N_EXPERT_GROUPS
    choice = scores + rb_ref[...]
    c3 = choice.reshape(N_EXPERT_GROUPS, per_group, R)
    within = lax.broadcasted_iota(jnp.int32, c3.shape, 1)
    m1 = jnp.max(c3, axis=1, keepdims=True)
    first = jnp.min(jnp.where(c3 == m1, within, per_group), axis=1, keepdims=True)
    m2 = jnp.max(jnp.where(within == first, -jnp.inf, c3), axis=1, keepdims=True)
    grp = (m1 + m2).reshape(N_EXPERT_GROUPS, R)
    gidx = lax.broadcasted_iota(jnp.int32, grp.shape, 0)
    grank = jnp.zeros(grp.shape, jnp.int32)
    for b in range(N_EXPERT_GROUPS):
        gb = grp[b:b + 1, :]
        grank += ((gb > grp) | ((gb == grp) & (b < gidx))).astype(jnp.int32)
    gmask = grank < TOPK_GROUPS
    emask = jnp.broadcast_to(gmask[:, None, :], c3.shape).reshape(N_EXPERTS, R)
    val = jnp.where(emask, choice, -jnp.inf)
    eidx = lax.broadcasted_iota(jnp.int32, val.shape, 0)
    rank = jnp.zeros(val.shape, jnp.int32)
    for e in range(N_EXPERTS):
        ve = val[e:e + 1, :]
        rank += ((ve > val) | ((ve == val) & (e < eidx))).astype(jnp.int32)
    sel = rank < TOP_K
    w = jnp.where(sel, scores, 0.0)
    w_ref[...] = w / jnp.sum(w, axis=0, keepdims=True) * ROUTED_SCALE
    cum = jnp.dot(jnp.where(sel, 1.0, 0.0).astype(BF16), tri_ref[...], preferred_element_type=F32)
    slot_ref[...] = jnp.where(sel, cum - 1.0, NOT_ROUTED)
    cnt_ref[0] = jnp.broadcast_to(cum[:, R - 1:R], (N_EXPERTS, LANES))


def _route(scores_t, router_bias, tri):
    seq = scores_t.shape[1]
    R = MOE_TOKEN_BLOCK
    n_blk = seq // R
    blk = pl.BlockSpec((N_EXPERTS, R), lambda i: (0, i))
    return pl.pallas_call(
        _route_body,
        grid=(n_blk,),
        in_specs=[blk, _resident((N_EXPERTS, 1), lambda i: (0, 0)), _resident((R, R), lambda i: (0, 0))],
        out_specs=[blk, blk, pl.BlockSpec((1, N_EXPERTS, LANES), lambda i: (i, 0, 0))],
        out_shape=[jax.ShapeDtypeStruct((N_EXPERTS, seq), F32),
                   jax.ShapeDtypeStruct((N_EXPERTS, seq), F32),
                   jax.ShapeDtypeStruct((n_blk, N_EXPERTS, LANES), F32)],
        compiler_params=pltpu.CompilerParams(dimension_semantics=("arbitrary",)),
        name="route",
    )(scores_t, router_bias, tri)


def _moe_tables(cnt):
    n_blk = cnt.shape[0]
    U, RB, SB = MOE_UNIT, MOE_BLOCK_ROWS, MOE_SLOT_BLOCK
    NU = SB // U
    E = N_EXPERTS
    pc = (cnt + U - 1) // U * U
    off = jnp.cumsum(pc, axis=1) - pc
    upc_t = (pc // U).T
    cum_t = jnp.cumsum(upc_t, axis=1)
    units_e = cum_t[:, -1]
    nblk_e = (units_e + NU - 1) // NU
    blk_end = jnp.cumsum(nblk_e)
    blk_start = blk_end - nblk_e
    n_act = blk_end[-1]
    max_blocks = n_blk * (RB // SB) + E
    i = jnp.arange(max_blocks, dtype=jnp.int32)
    active = i < n_act
    count_le = lambda edges, v: jnp.sum((edges <= v[..., None]).astype(jnp.int32), axis=-1)
    last_e = jnp.minimum(count_le(blk_end, n_act - 1), E - 1)
    be = jnp.where(active, jnp.minimum(count_le(blk_end[None, :], i), E - 1), last_e)
    oh_e = be[:, None] == jnp.arange(E, dtype=jnp.int32)[None, :]
    pick_e = lambda v: jnp.sum(jnp.where(oh_e, v[None, :], 0), axis=1)
    pick_e2 = lambda m: jnp.sum(jnp.where(oh_e[:, :, None], m[None, :, :], 0), axis=1)
    bstart_i = pick_e(blk_start)
    first = active & (i == bstart_i)
    local = (i - bstart_i)[:, None] * NU + jnp.arange(NU, dtype=jnp.int32)[None, :]
    valid = active[:, None] & (local < pick_e(units_e)[:, None])
    cum_i = pick_e2(cum_t)
    b_of = jnp.minimum(count_le(cum_i[:, None, :], local), n_blk - 1)
    oh_b = b_of[:, :, None] == jnp.arange(n_blk, dtype=jnp.int32)[None, None, :]
    pick_b = lambda m: jnp.sum(jnp.where(oh_b, m[:, None, :], 0), axis=2)
    seg_start = pick_b(cum_i) - pick_b(pick_e2(upc_t))
    unit = (b_of * RB + pick_b(pick_e2(off.T))) // U + (local - seg_start)
    spare = n_blk * RB // U
    src = jnp.where(valid, unit, spare)
    dst = jnp.where(valid, unit,
                    spare + (1 + i % 2)[:, None] * NU + jnp.arange(NU, dtype=jnp.int32)[None, :])
    short = active & (pick_e(units_e) - (i - bstart_i) * NU <= NU // 2)
    i32 = lambda a: a.astype(jnp.int32)
    flags = i32(first) + 2 * i32(short)
    return (off, pc, i32(src.reshape(-1)), i32(dst.reshape(-1)), i32(be), flags,
            i32(n_act.reshape(1)))


def _split_hi_lo(pos):
    hi = jnp.floor(pos * (1.0 / 64.0))
    return hi.astype(BF16), (pos - 64.0 * hi).astype(BF16)


def _dispatch_body(used_ref, hn_ref, slot_ref, offc_ref, offl_ref, pcl_ref, xs_ref, *, n_blk):
    b = pl.program_id(0)
    used = used_ref[jnp.minimum(b, n_blk - 1)]
    RB, TB = MOE_BLOCK_ROWS, MOE_TOKEN_BLOCK

    @pl.when(b == n_blk)
    def _():
        xs_ref[...] = jnp.zeros(xs_ref.shape, BF16)

    @pl.when(b < n_blk)
    def _():
        pos = slot_ref[...] + offc_ref[0]
        hi, lo = _split_hi_lo(pos)
        r = lax.broadcasted_iota(jnp.int32, (RB, N_EXPERTS), 0).astype(F32)
        off = offl_ref[0]
        owner = jnp.where((r >= off) & (r < off + pcl_ref[0]), 1.0, 0.0).astype(BF16)
        p = (64.0 * jnp.dot(owner, hi, preferred_element_type=F32)
             + jnp.dot(owner, lo, preferred_element_type=F32))
        rr = lax.broadcasted_iota(jnp.int32, (RB, TB), 0).astype(F32)
        onehot = jnp.where(jnp.abs(p - rr) < 0.5, 1.0, 0.0).astype(BF16)
        x = hn_ref[...]
        C = MOE_SLOT_BLOCK
        for c in range(RB // C):
            @pl.when(c * C < used)
            def _(c=c):
                xs_ref[0, c * C:(c + 1) * C, :] = jnp.dot(
                    onehot[c * C:(c + 1) * C], x, preferred_element_type=F32).astype(BF16)

            @pl.when(c * C >= used)
            def _(c=c):
                xs_ref[0, c * C:(c + 1) * C, :] = jnp.zeros((C, D_MODEL), BF16)


def _dispatch(used, hn, slots, off, pc):
    n_blk = off.shape[0]
    RB, TB = MOE_BLOCK_ROWS, MOE_TOKEN_BLOCK
    clamp = lambda b: jnp.minimum(b, n_blk - 1)
    return pl.pallas_call(
        functools.partial(_dispatch_body, n_blk=n_blk),
        grid=(n_blk + 1,),
        in_specs=[
            pl.BlockSpec(memory_space=pltpu.SMEM),
            pl.BlockSpec((TB, D_MODEL), lambda b: (clamp(b), 0)),
            pl.BlockSpec((N_EXPERTS, TB), lambda b: (0, clamp(b))),
            pl.BlockSpec((1, N_EXPERTS, 1), lambda b: (clamp(b), 0, 0)),
            pl.BlockSpec((1, 1, N_EXPERTS), lambda b: (clamp(b), 0, 0)),
            pl.BlockSpec((1, 1, N_EXPERTS), lambda b: (clamp(b), 0, 0)),
        ],
        out_specs=pl.BlockSpec((1, RB, D_MODEL), lambda b: (b, 0, 0)),
        out_shape=jax.ShapeDtypeStruct((n_blk + 1, RB, D_MODEL), BF16),
        compiler_params=pltpu.CompilerParams(
            dimension_semantics=("arbitrary",), vmem_limit_bytes=VMEM_LIMIT),
        name="moe_dispatch",
    )(used, hn, slots, off[:, :, None], off[:, None, :], pc[:, None, :])


def _expert_body(src_ref, dst_ref, bexp_ref, first_ref, nact_ref,
                 xs_hbm, wg_ref, wu_ref, wd_ref, ys_hbm,
                 xbuf, ybuf, wgub, wdb, sem_in, sem_out):
    del bexp_ref
    i = pl.program_id(0)
    n_act = nact_ref[0]
    cur = lax.rem(i, 2)
    U = MOE_UNIT
    NU = MOE_SLOT_BLOCK // U

    def in_copy(blk, buf, u):
        return pltpu.make_async_copy(xs_hbm.at[src_ref[blk * NU + u]],
                                     xbuf.at[buf, pl.ds(u * U, U)], sem_in.at[buf])

    def out_copy(blk, buf, u):
        return pltpu.make_async_copy(ybuf.at[buf, pl.ds(u * U, U)],
                                     ys_hbm.at[dst_ref[blk * NU + u]], sem_out.at[buf])

    @pl.when(i == 0)
    def _():
        for u in range(NU):
            in_copy(0, 0, u).start()

    @pl.when(i < n_act)
    def _():
        @pl.when(i + 1 < n_act)
        def _():
            for u in range(NU):
                in_copy(i + 1, 1 - cur, u).start()

        @pl.when(first_ref[i] % 2 == 1)
        def _():
            wgub[:, :EXPERT_HIDDEN] = wg_ref[0].astype(BF16)
            wgub[:, EXPERT_HIDDEN:] = wu_ref[0].astype(BF16)
            wdb[...] = wd_ref[0].astype(BF16)

        for u in range(NU):
            in_copy(i, cur, u).wait()

        @pl.when(i >= 2)
        def _():
            for u in range(NU):
                out_copy(i - 2, cur, u).wait()

        def swiglu_rows(rows):
            x = xbuf[cur, :rows]
            gu = jnp.dot(x, wgub[...], preferred_element_type=F32)
            act = (jax.nn.silu(gu[:, :EXPERT_HIDDEN]) * gu[:, EXPERT_HIDDEN:]).astype(BF16)
            ybuf[cur, :rows] = jnp.dot(act, wdb[...], preferred_element_type=F32).astype(BF16)

        half = MOE_SLOT_BLOCK // 2

        @pl.when(first_ref[i] < 2)
        def _():
            swiglu_rows(MOE_SLOT_BLOCK)

        @pl.when(first_ref[i] >= 2)
        def _():
            swiglu_rows(half)
            ybuf[cur, half:] = jnp.zeros((half, D_MODEL), BF16)

        for u in range(NU):
            out_copy(i, cur, u).start()

        @pl.when(i == n_act - 1)
        def _():
            for u in range(NU):
                out_copy(i, cur, u).wait()

            @pl.when(i >= 1)
            def _():
                for u in range(NU):
                    out_copy(i - 1, 1 - cur, u).wait()


def _experts(xs, src, dst, bexp, first, n_act, wg, wu, wd):
    U, SB = MOE_UNIT, MOE_SLOT_BLOCK
    n_units = xs.shape[0] * xs.shape[1] // U
    max_blocks = bexp.shape[0]
    unit_view = lambda a: a.reshape(n_units, U, D_MODEL)
    wspec = lambda shape: pl.BlockSpec((1,) + shape, lambda i, src, dst, bexp, first, nact: (bexp[i], 0, 0))
    grid_spec = pltpu.PrefetchScalarGridSpec(
        num_scalar_prefetch=5,
        grid=(max_blocks,),
        in_specs=[
            pl.BlockSpec(memory_space=pl.ANY),
            wspec((D_MODEL, EXPERT_HIDDEN)), wspec((D_MODEL, EXPERT_HIDDEN)), wspec((EXPERT_HIDDEN, D_MODEL)),
        ],
        out_specs=pl.BlockSpec(memory_space=pl.ANY),
        scratch_shapes=[
            pltpu.VMEM((2, SB, D_MODEL), BF16), pltpu.VMEM((2, SB, D_MODEL), BF16),
            pltpu.VMEM((D_MODEL, 2 * EXPERT_HIDDEN), BF16), pltpu.VMEM((EXPERT_HIDDEN, D_MODEL), BF16),
            pltpu.SemaphoreType.DMA((2,)), pltpu.SemaphoreType.DMA((2,)),
        ],
    )
    ys = pl.pallas_call(
        _expert_body,
        grid_spec=grid_spec,
        out_shape=jax.ShapeDtypeStruct((n_units, U, D_MODEL), BF16),
        input_output_aliases={5: 0},
        compiler_params=pltpu.CompilerParams(
            dimension_semantics=("arbitrary",), vmem_limit_bytes=VMEM_LIMIT),
        name="moe_experts",
    )(src, dst, bexp, first, n_act, unit_view(xs), wg, wu, wd)
    return ys.reshape(xs.shape)


def _combine_body(used_ref, h_ref, ys_ref, slot_ref, w_ref, offl_ref, offc_ref, pcc_ref, o_ref):
    RB, TB = MOE_BLOCK_ROWS, MOE_TOKEN_BLOCK
    used = used_ref[pl.program_id(0)]
    pos = slot_ref[...] + offl_ref[0]
    hi, lo = _split_hi_lo(pos)
    r = lax.broadcasted_iota(jnp.int32, (N_EXPERTS, RB), 1).astype(F32)
    off = offc_ref[0]
    owner = jnp.where((r >= off) & (r < off + pcc_ref[0]), 1.0, 0.0).astype(BF16)
    p = (64.0 * jnp.dot(hi, owner, preferred_element_type=F32)
         + jnp.dot(lo, owner, preferred_element_type=F32))
    wr = jnp.dot(w_ref[...].astype(BF16), owner, preferred_element_type=F32)
    rr = lax.broadcasted_iota(jnp.int32, (TB, RB), 1).astype(F32)
    gather_w = jnp.where(jnp.abs(p - rr) < 0.5, wr, 0.0).astype(BF16)
    o_ref[...] = h_ref[...]
    C = MOE_SLOT_BLOCK
    for c in range(RB // C):
        @pl.when(c * C < used)
        def _(c=c):
            o_ref[...] += jnp.dot(gather_w[:, c * C:(c + 1) * C], ys_ref[0, c * C:(c + 1) * C, :],
                                  preferred_element_type=F32)


def _combine(used, h, ys, slots_tok, w_tok, off, pc):
    n_blk = off.shape[0]
    RB, TB = MOE_BLOCK_ROWS, MOE_TOKEN_BLOCK
    seq = h.shape[0]
    return pl.pallas_call(
        _combine_body,
        grid=(n_blk,),
        in_specs=[
            pl.BlockSpec(memory_space=pltpu.SMEM),
            pl.BlockSpec((TB, D_MODEL), lambda b: (b, 0)),
            pl.BlockSpec((1, RB, D_MODEL), lambda b: (b, 0, 0)),
            pl.BlockSpec((TB, N_EXPERTS), lambda b: (b, 0)),
            pl.BlockSpec((TB, N_EXPERTS), lambda b: (b, 0)),
            pl.BlockSpec((1, 1, N_EXPERTS), lambda b: (b, 0, 0)),
            pl.BlockSpec((1, N_EXPERTS, 1), lambda b: (b, 0, 0)),
            pl.BlockSpec((1, N_EXPERTS, 1), lambda b: (b, 0, 0)),
        ],
        out_specs=pl.BlockSpec((TB, D_MODEL), lambda b: (b, 0)),
        out_shape=jax.ShapeDtypeStruct((seq, D_MODEL), F32),
        compiler_params=pltpu.CompilerParams(
            dimension_semantics=("arbitrary",), vmem_limit_bytes=VMEM_LIMIT),
        name="moe_combine",
    )(used, h, ys, slots_tok, w_tok, off[:, None, :], off[:, :, None], pc[:, :, None])


def kernel(x, meta_tokens, rel_bias, norm_mix, w_in, q_norm, k_norm, lam_q1, lam_k1, lam_q2, lam_k2, subln, ssm_a_re, ssm_a_im, ssm_log_step, ssm_b_re, ssm_b_im, ssm_c_re, ssm_c_im, ssm_d, w_glu, b_glu, ssm_norm, w_out, norm_ffn, router_w, router_bias, w_gate, w_up, w_down, shared_gate, shared_up, shared_down):
    batch, seq, d = x.shape
    assert batch == 1 and d == D_MODEL and seq % ROW_BLOCK == 0 and seq % ATT_BLOCK == 0
    assert norm_mix.shape[0] == 1, "single layer"
    x2 = x.reshape(seq, d)
    meta_pad = jnp.zeros((ROW_BLOCK, d), F32).at[:N_META].set(meta_tokens.astype(F32))
    seg = jnp.kron(jnp.eye(QK_WIDTH // HEAD_DIM, dtype=F32),
                   jnp.full((HEAD_DIM, HEAD_DIM), 1.0 / HEAD_DIM, F32)).astype(BF16)
    qg = jnp.tile(q_norm[0].astype(F32), QK_WIDTH // HEAD_DIM)[None] * (HEAD_DIM ** -0.5)
    kg = jnp.tile(k_norm[0].astype(F32), QK_WIDTH // HEAD_DIM)[None]

    proj, u = _inproj(x2, meta_pad, norm_mix[0][None], w_in[0].astype(BF16), seg, qg, kg)

    score_bound = (BOUND_MARGIN * HEAD_DIM ** 0.5 * jnp.max(jnp.abs(q_norm[0].astype(F32)))
                   * jnp.max(jnp.abs(k_norm[0].astype(F32)))
                   + jnp.max(jnp.abs(rel_bias.astype(F32)))).reshape(1)
    att = _attention(proj, rel_bias.astype(F32), score_bound, lam_q1[0][None], lam_k1[0][None],
                     lam_q2[0][None], lam_k2[0][None], subln[0][None], seq)

    n_rows = proj.shape[0] // S5_CHUNK
    n_chunks = seq // S5_CHUNK
    perm = _lane_regroup_matrix()
    u_chunks = _to_chunks(u.reshape(n_rows, S5_CHUNK, SSM_WIDTH), perm)
    pp, mm, qq, lam16 = _s5_prep(ssm_a_re[0].astype(F32), ssm_a_im[0].astype(F32),
                                 ssm_log_step[0].astype(F32), ssm_b_re[0].astype(F32),
                                 ssm_b_im[0].astype(F32), ssm_c_re[0].astype(F32),
                                 ssm_c_im[0].astype(F32))
    y_chunks = _s5(u_chunks, pp, mm, qq, lam16, n_chunks)
    y = _from_chunks(y_chunks, perm.T).reshape(seq, SSM_WIDTH)

    h, hn, scores_t = _mix_out(
        x2, att, y, u, ssm_d[0][None].astype(F32), w_glu[0].astype(BF16), b_glu[0][None].astype(F32),
        ssm_norm[0][None].astype(F32), w_out[0].astype(BF16), norm_ffn[0][None].astype(F32),
        router_w[0].astype(F32).T,
        jnp.concatenate([shared_gate[0], shared_up[0]], axis=1).astype(BF16), shared_down[0].astype(BF16))

    tb = MOE_TOKEN_BLOCK
    tri = (jnp.arange(tb)[:, None] <= jnp.arange(tb)[None, :]).astype(BF16)
    wts_t, slots_t, cnt = _route(scores_t, router_bias[0].astype(F32)[:, None], tri)
    off, pc, src, dst, bexp, first, n_act = _moe_tables(cnt[:, :, 0].astype(jnp.int32))
    off_f, pc_f = off.astype(F32), pc.astype(F32)
    used = jnp.sum(pc, axis=1).astype(jnp.int32)
    xs = _dispatch(used, hn, slots_t, off_f, pc_f)
    ys = _experts(xs, src, dst, bexp, first, n_act, w_gate[0], w_up[0], w_down[0])
    out = _combine(used, h, ys, slots_t.T, wts_t.T, off_f, pc_f)
    return out.reshape(batch, seq, d)
```

```python
import functools
import math

import jax
import jax.numpy as jnp
from jax import lax
from jax.experimental import pallas as pl
from jax.experimental.pallas import tpu as pltpu

F32 = jnp.float32
BF16 = jnp.bfloat16

D_MODEL = 2048
N_META = 16
ATT_WIDTH = 1024
SSM_WIDTH = 1024
HEAD_DIM = 64
V_DIM = 128
HEADS = 8
QK_WIDTH = 1024
IN_WIDTH = 4096
V_OFFSET = 2 * QK_WIDTH
PROJ_WIDTH = V_OFFSET + 2 * ATT_WIDTH
SSM_GROUP = 16
SSM_GROUPS = 64
SSM_STATE = 64
N_BUCKETS = 32
MAX_DISTANCE = 128
N_EXPERTS = 64
TOP_K = 8
N_EXPERT_GROUPS = 8
TOPK_GROUPS = 4
EXPERT_HIDDEN = 512
SHARED_HIDDEN = 512
ROUTED_SCALE = 2.5
EPS = 1e-6
LAMBDA_INIT = 0.8 - 0.6 * math.exp(-0.3 * 0)

ROW_BLOCK = 512
ATT_BLOCK = 512
S5_CHUNK = 16
S5_GROUPS_PER_STEP = 8
S5_OCTET_WIDTH = S5_GROUPS_PER_STEP * S5_CHUNK * SSM_GROUP
assert S5_GROUPS_PER_STEP * SSM_GROUP == 128
MOE_TOKEN_BLOCK = 256
MOE_UNIT = 16
MOE_SLOT_BLOCK = 512
MOE_BLOCK_ROWS = -(-(MOE_TOKEN_BLOCK * TOP_K + N_EXPERTS * (MOE_UNIT - 1)) // MOE_SLOT_BLOCK) * MOE_SLOT_BLOCK
NOT_ROUTED = -1e6
MIX_ROW_BLOCK = 256
NEG_BIG = -1e30
MAX_EXP_RANGE = 80.0
BOUND_MARGIN = 1.02
LANES = 128
VMEM_LIMIT = 56 * 2 ** 20


def _resident(shape, index_map):
    return pl.BlockSpec(shape, index_map, pipeline_mode=pl.Buffered(1))


def _inproj_body(x_ref, meta_ref, g_ref, w_ref, seg_ref, qg_ref, kg_ref, o_ref, u_ref, *, n_xblk):
    i = pl.program_id(0)

    def run(src_ref):
        xv = src_ref[...]
        ms = jnp.mean(xv * xv, axis=-1, keepdims=True)
        hn = (xv * lax.rsqrt(ms + EPS) * g_ref[...]).astype(BF16)
        sec = IN_WIDTH // 4
        for s in range(4):
            ps = jnp.dot(hn, w_ref[:, s * sec:(s + 1) * sec], preferred_element_type=F32)
            if s < 2:
                gain = qg_ref if s == 0 else kg_ref
                msq = jnp.dot((ps * ps).astype(BF16), seg_ref[...], preferred_element_type=F32)
                ps = ps * lax.rsqrt(msq + EPS) * gain[...]
            if s == 3:
                u_ref[...] = ps
                continue
            pb = ps.astype(BF16)
            if s < 2:
                o_ref[:, s * sec:(s + 1) * sec] = pb
            else:
                lane = lax.broadcasted_iota(jnp.int32, (pb.shape[0], V_DIM), 1)
                ones_col = jnp.where(lane == 0, 1.0, 0.0).astype(BF16)
                for hh in range(HEADS):
                    base = V_OFFSET + hh * 2 * V_DIM
                    o_ref[:, base:base + V_DIM] = pb[:, hh * V_DIM:(hh + 1) * V_DIM]
                    o_ref[:, base + V_DIM:base + 2 * V_DIM] = ones_col

    @pl.when(i < n_xblk)
    def _():
        run(x_ref)

    @pl.when(i == n_xblk)
    def _():
        run(meta_ref)


def _inproj(x2, meta_pad, gain, w_bf, seg, qg, kg):
    seq = x2.shape[0]
    n_xblk = seq // ROW_BLOCK
    rows = seq + ROW_BLOCK
    return pl.pallas_call(
        functools.partial(_inproj_body, n_xblk=n_xblk),
        grid=(n_xblk + 1,),
        in_specs=[
            pl.BlockSpec((ROW_BLOCK, D_MODEL), lambda i: (jnp.minimum(i, n_xblk - 1), 0)),
            _resident((ROW_BLOCK, D_MODEL), lambda i: (0, 0)),
            _resident((1, D_MODEL), lambda i: (0, 0)),
            _resident((D_MODEL, IN_WIDTH), lambda i: (0, 0)),
            _resident((QK_WIDTH, QK_WIDTH), lambda i: (0, 0)),
            _resident((1, QK_WIDTH), lambda i: (0, 0)),
            _resident((1, QK_WIDTH), lambda i: (0, 0)),
        ],
        out_specs=[pl.BlockSpec((ROW_BLOCK, PROJ_WIDTH), lambda i: (i, 0)),
                   pl.BlockSpec((ROW_BLOCK, SSM_WIDTH), lambda i: (i, 0))],
        out_shape=[jax.ShapeDtypeStruct((rows, PROJ_WIDTH), BF16),
                   jax.ShapeDtypeStruct((rows, SSM_WIDTH), F32)],
        compiler_params=pltpu.CompilerParams(
            dimension_semantics=("arbitrary",), vmem_limit_bytes=VMEM_LIMIT),
        name="inproj",
    )(x2, meta_pad, gain, w_bf, seg, qg, kg)


def _t5_bias(rel, tab_ref, h):
    half = N_BUCKETS // 2
    exact = half // 2
    n = jnp.abs(rel)
    nf = jnp.maximum(n, 1).astype(F32)
    large = exact + (jnp.log(nf / exact) / math.log(MAX_DISTANCE / exact) * (half - exact)).astype(jnp.int32)
    large = jnp.minimum(large, half - 1)
    bucket = jnp.where(rel > 0, half, 0) + jnp.where(n < exact, n, large)
    out = jnp.zeros(rel.shape, F32)
    for b in range(N_BUCKETS):
        out = jnp.where(bucket == b, tab_ref[b, h], out)
    return out


def _attn_body(tab_ref, bound_ref, q_ref, k_ref, v_ref, lq1_ref, lk1_ref, lq2_ref, lk2_ref,
               subln_ref, o_ref, bias_ref, acc1_ref, acc2_ref, m1_ref, m2_ref, *, n_main):
    T = ATT_BLOCK
    h = pl.program_id(0)
    qi = pl.program_id(1)
    bound = bound_ref[0]

    @pl.when(qi == 0)
    def _():
        offsets = (-T, 0, T, -N_META, -N_META - T, -2 * T, 2 * T)
        for kind, off in enumerate(offsets):
            masked = kind in (3, 4)

            def rows(rc, carry, off=off, masked=masked, kind=kind):
                r0 = pl.multiple_of(rc * 8, 8)
                r = r0 + lax.broadcasted_iota(jnp.int32, (8, T), 0)
                c = lax.broadcasted_iota(jnp.int32, (8, T), 1)
                b = _t5_bias(off + c - r, tab_ref, h) - bound
                if masked:
                    b = jnp.where(c < N_META, b, NEG_BIG)
                bias_ref[kind, pl.ds(r0, 8), :] = b
                return carry

            lax.fori_loop(0, T // 8, rows, 0)

    acc1_ref[...] = jnp.zeros(acc1_ref.shape, F32)
    acc2_ref[...] = jnp.zeros(acc2_ref.shape, F32)

    q = q_ref[...]
    q1 = q[:, :HEAD_DIM]
    q2 = q[:, HEAD_DIM:]
    nt = (((1,), (1,)), ((), ()))

    def tile(ki):
        koff = pl.multiple_of(ki * T, T)
        kb = k_ref[pl.ds(koff, T), :]
        va = v_ref[pl.ds(koff, T), :]
        d = ki - qi
        kind = jnp.where(ki == n_main, jnp.where(qi == 0, 3, 4),
                         jnp.where(d <= -2, 5, jnp.where(d >= 2, 6, d + 1)))
        return kb, va, bias_ref[kind]

    def bounded_step(ki, carry):
        kb, va, bias = tile(ki)
        s1 = lax.dot_general(q1, kb[:, :HEAD_DIM], nt, preferred_element_type=F32) + bias
        acc1_ref[...] += jnp.dot(jnp.exp(s1).astype(BF16), va, preferred_element_type=F32)
        s2 = lax.dot_general(q2, kb[:, HEAD_DIM:], nt, preferred_element_type=F32) + bias
        acc2_ref[...] += jnp.dot(jnp.exp(s2).astype(BF16), va, preferred_element_type=F32)
        return carry

    def online_map(s, va, m_ref, acc_ref):
        m_old = m_ref[...]
        m_new = jnp.maximum(m_old, jnp.max(s, axis=-1, keepdims=True))
        p = jnp.exp(s - m_new).astype(BF16)
        acc_ref[...] = (jnp.exp(m_old - m_new) * acc_ref[...]
                        + jnp.dot(p, va, preferred_element_type=F32))
        m_ref[...] = m_new

    def online_step(ki, carry):
        kb, va, bias = tile(ki)
        s1 = lax.dot_general(q1, kb[:, :HEAD_DIM], nt, preferred_element_type=F32) + bias
        online_map(s1, va, m1_ref, acc1_ref)
        s2 = lax.dot_general(q2, kb[:, HEAD_DIM:], nt, preferred_element_type=F32) + bias
        online_map(s2, va, m2_ref, acc2_ref)
        return carry

    no_running_max = 2.0 * bound <= MAX_EXP_RANGE

    @pl.when(no_running_max)
    def _():
        unroll = next(u for u in (33, 11, 3, 2, 1) if (n_main + 1) % u == 0)
        lax.fori_loop(0, n_main + 1, bounded_step, 0, unroll=unroll)

    @pl.when(jnp.logical_not(no_running_max))
    def _():
        m1_ref[...] = jnp.full(m1_ref.shape, -jnp.inf, F32)
        m2_ref[...] = jnp.full(m2_ref.shape, -jnp.inf, F32)
        lax.fori_loop(0, n_main + 1, online_step, 0)

    lam = (jnp.exp(jnp.sum(lq1_ref[...] * lk1_ref[...], axis=-1, keepdims=True))
           - jnp.exp(jnp.sum(lq2_ref[...] * lk2_ref[...], axis=-1, keepdims=True))
           + LAMBDA_INIT)
    a1 = acc1_ref[...]
    a2 = acc2_ref[...]
    o = (a1[:, :V_DIM] / a1[:, V_DIM:V_DIM + 1]
         - lam * (a2[:, :V_DIM] / a2[:, V_DIM:V_DIM + 1]))
    ms = jnp.mean(o * o, axis=-1, keepdims=True)
    o = o * lax.rsqrt(ms + EPS) * subln_ref[...] * (1.0 - LAMBDA_INIT)
    o_ref[...] = o.astype(BF16)


def _attention(proj, rel_bias, score_bound, lq1, lk1, lq2, lk2, subln, seq):
    T = ATT_BLOCK
    n_main = seq // T
    rows = proj.shape[0]
    vec64 = lambda: _resident((1, HEAD_DIM), lambda h, qi: (0, 0))
    return pl.pallas_call(
        functools.partial(_attn_body, n_main=n_main),
        grid=(HEADS, n_main),
        in_specs=[
            pl.BlockSpec(memory_space=pltpu.SMEM),
            pl.BlockSpec(memory_space=pltpu.SMEM),
            pl.BlockSpec((T, 2 * HEAD_DIM), lambda h, qi: (qi, h)),
            pl.BlockSpec((rows, 2 * HEAD_DIM), lambda h, qi: (0, HEADS + h)),
            pl.BlockSpec((rows, 2 * V_DIM), lambda h, qi: (0, V_OFFSET // (2 * V_DIM) + h)),
            vec64(), vec64(), vec64(), vec64(),
            _resident((1, V_DIM), lambda h, qi: (0, 0)),
        ],
        out_specs=pl.BlockSpec((T, V_DIM), lambda h, qi: (qi, h)),
        out_shape=jax.ShapeDtypeStruct((seq, ATT_WIDTH), BF16),
        scratch_shapes=[
            pltpu.VMEM((7, T, T), F32),
            pltpu.VMEM((T, 2 * V_DIM), F32), pltpu.VMEM((T, 2 * V_DIM), F32),
            pltpu.VMEM((T, 1), F32), pltpu.VMEM((T, 1), F32),
        ],
        compiler_params=pltpu.CompilerParams(
            dimension_semantics=("arbitrary", "arbitrary"), vmem_limit_bytes=VMEM_LIMIT),
        name="diff_attention",
    )(rel_bias, score_bound, proj, proj, proj, lq1, lk1, lq2, lk2, subln)


def _lane_regroup_matrix():
    out_lane = jnp.arange(S5_OCTET_WIDTH, dtype=jnp.int32)
    g8 = out_lane // (S5_CHUNK * SSM_GROUP)
    j = (out_lane % (S5_CHUNK * SSM_GROUP)) // SSM_GROUP
    p = out_lane % SSM_GROUP
    in_lane = j * LANES + g8 * SSM_GROUP + p
    return (jnp.arange(S5_OCTET_WIDTH, dtype=jnp.int32)[:, None] == in_lane[None, :]).astype(BF16)


def _to_chunks_body(u_ref, perm_ref, o_ref):
    x = jnp.concatenate([u_ref[:, j, :].astype(BF16) for j in range(S5_CHUNK)], axis=1)
    r = jnp.dot(x, perm_ref[...], preferred_element_type=F32)
    w = S5_CHUNK * SSM_GROUP
    for g8 in range(S5_GROUPS_PER_STEP):
        o_ref[g8] = r[:, g8 * w:(g8 + 1) * w].astype(BF16)


def _to_chunks(u3, perm):
    n_rows = u3.shape[0]
    w = S5_CHUNK * SSM_GROUP
    return pl.pallas_call(
        _to_chunks_body,
        grid=(SSM_GROUPS // S5_GROUPS_PER_STEP,),
        in_specs=[pl.BlockSpec((n_rows, S5_CHUNK, LANES), lambda o: (0, 0, o)),
                  _resident((S5_OCTET_WIDTH, S5_OCTET_WIDTH), lambda o: (0, 0))],
        out_specs=pl.BlockSpec((S5_GROUPS_PER_STEP, n_rows, w), lambda o: (o, 0, 0)),
        out_shape=jax.ShapeDtypeStruct((SSM_GROUPS, n_rows, w), BF16),
        compiler_params=pltpu.CompilerParams(
            dimension_semantics=("arbitrary",), vmem_limit_bytes=VMEM_LIMIT),
        name="s5_to_chunks",
    )(u3, perm)


def _from_chunks_body(y_ref, perm_t_ref, o_ref):
    x = jnp.concatenate([y_ref[g8] for g8 in range(S5_GROUPS_PER_STEP)], axis=1)
    r = jnp.dot(x, perm_t_ref[...], preferred_element_type=F32)
    for j in range(S5_CHUNK):
        o_ref[:, j, :] = r[:, j * LANES:(j + 1) * LANES]


def _from_chunks(y_chunks, perm_t):
    n_chunks = y_chunks.shape[1]
    w = S5_CHUNK * SSM_GROUP
    return pl.pallas_call(
        _from_chunks_body,
        grid=(SSM_GROUPS // S5_GROUPS_PER_STEP,),
        in_specs=[pl.BlockSpec((S5_GROUPS_PER_STEP, n_chunks, w), lambda o: (o, 0, 0)),
                  _resident((S5_OCTET_WIDTH, S5_OCTET_WIDTH), lambda o: (0, 0))],
        out_specs=pl.BlockSpec((n_chunks, S5_CHUNK, LANES), lambda o: (0, 0, o)),
        out_shape=jax.ShapeDtypeStruct((n_chunks, S5_CHUNK, SSM_WIDTH), F32),
        compiler_params=pltpu.CompilerParams(
            dimension_semantics=("arbitrary",), vmem_limit_bytes=VMEM_LIMIT),
        name="s5_from_chunks",
    )(y_chunks, perm_t)


def _s5_prep(a_re, a_im, log_step, b_re, b_im, c_re, c_im):
    C = S5_CHUNK
    dt = jnp.exp(log_step)[..., None]
    decay = jnp.exp(a_re * dt)
    ab_re = decay * jnp.cos(a_im * dt)
    ab_im = decay * jnp.sin(a_im * dt)
    den = a_re * a_re + a_im * a_im
    zr = ab_re - 1.0
    f_re = (zr * a_re + ab_im * a_im) / den
    f_im = (ab_im * a_re - zr * a_im) / den
    bb_re = f_re[..., None] * b_re - f_im[..., None] * b_im
    bb_im = f_re[..., None] * b_im + f_im[..., None] * b_re
    pr, pi = jnp.ones_like(ab_re), jnp.zeros_like(ab_re)
    pw_re, pw_im = [pr], [pi]
    for _ in range(C):
        pr, pi = pr * ab_re - pi * ab_im, pr * ab_im + pi * ab_re
        pw_re.append(pr)
        pw_im.append(pi)
    G, P, W = SSM_GROUPS, SSM_GROUP, C * SSM_GROUP
    pw_re = jnp.stack(pw_re, axis=-1)
    pw_im = jnp.stack(pw_im, axis=-1)
    ct_re = c_re.transpose(0, 1, 3, 2)
    ct_im = c_im.transpose(0, 1, 3, 2)
    cp_re = ct_re[:, :, :, None, :] * pw_re[..., None] - ct_im[:, :, :, None, :] * pw_im[..., None]
    cp_im = ct_re[:, :, :, None, :] * pw_im[..., None] + ct_im[:, :, :, None, :] * pw_re[..., None]
    bt_re = bb_re.transpose(0, 1, 3, 2)
    bt_im = bb_im.transpose(0, 1, 3, 2)

    def response(d):
        prod = (cp_re[d][:, None, :, :C, :] * bt_re[d][:, :, :, None, None]
                - cp_im[d][:, None, :, :C, :] * bt_im[d][:, :, :, None, None])
        return jnp.sum(prod, axis=2)

    ext_f = jnp.pad(response(0).reshape(G, P, W), ((0, 0), (0, 0), (W, 0)))
    ext_r = jnp.pad(response(1)[:, :, ::-1, :].reshape(G, P, W), ((0, 0), (0, 0), (0, W)))
    mm = jnp.stack([ext_f[:, :, W - P * j:2 * W - P * j] + ext_r[:, :, (C - 1 - j) * P:(C - 1 - j) * P + W]
                    for j in range(C)], axis=1).reshape(G, W, W)

    def in_mat(d, reverse_powers):
        pr_ = pw_re[d][:, :, :C].transpose(0, 2, 1)
        pi_ = pw_im[d][:, :, :C].transpose(0, 2, 1)
        if reverse_powers:
            pr_, pi_ = pr_[:, ::-1], pi_[:, ::-1]
        re = pr_[:, :, None, :] * bt_re[d][:, None] - pi_[:, :, None, :] * bt_im[d][:, None]
        im = pr_[:, :, None, :] * bt_im[d][:, None] + pi_[:, :, None, :] * bt_re[d][:, None]
        return re, im

    pf_re, pf_im = in_mat(0, True)
    pr_re, pr_im = in_mat(1, False)
    pp = jnp.concatenate([pf_re, pr_re, pf_im, pr_im], axis=-1).reshape(G, W, 4 * SSM_STATE)

    qq = jnp.concatenate([cp_re[0][:, :, 1:], cp_re[1][:, :, :0:-1],
                          -cp_im[0][:, :, 1:], -cp_im[1][:, :, :0:-1]], axis=1).reshape(G, 4 * SSM_STATE, W)
    lam16 = jnp.stack([jnp.concatenate([pw_re[0][:, :, C], pw_re[1][:, :, C]], axis=-1),
                       jnp.concatenate([pw_im[0][:, :, C], pw_im[1][:, :, C]], axis=-1)], axis=0)
    return pp.astype(BF16), mm.astype(BF16), qq.astype(BF16), lam16


def _s5_body(u_ref, pp_ref, mm_ref, qq_ref, lam_ref, y_ref,
             zre_ref, zim_ref, are_ref, aim_ref, bre_ref, bim_ref, *, n_chunks):
    GS = S5_GROUPS_PER_STEP
    NS = 2 * SSM_STATE
    for gi in range(GS):
        z = jnp.dot(u_ref[gi], pp_ref[gi], preferred_element_type=F32)
        zre_ref[:, gi, :] = z[:, :NS]
        zim_ref[:, gi, :] = z[:, NS:]

    ar = lam_ref[0]
    ai = lam_ref[1]
    fwd = lax.broadcasted_iota(jnp.int32, (GS, NS), 1) < SSM_STATE
    sre0 = jnp.where(fwd, zre_ref[n_chunks], 0.0)
    sim0 = jnp.where(fwd, zim_ref[n_chunks], 0.0)

    def scan_step(k, carry):
        sre, sim = carry
        kr = n_chunks - 1 - k
        are_ref[k] = sre
        aim_ref[k] = sim
        bre_ref[kr] = sre
        bim_ref[kr] = sim
        zr = jnp.where(fwd, zre_ref[k], zre_ref[kr])
        zi = jnp.where(fwd, zim_ref[k], zim_ref[kr])
        return ar * sre - ai * sim + zr, ar * sim + ai * sre + zi

    lax.fori_loop(0, n_chunks, scan_step, (sre0, sim0))

    fwd_rows = lax.broadcasted_iota(jnp.int32, (n_chunks, NS), 1) < SSM_STATE
    for gi in range(GS):
        s_re = jnp.where(fwd_rows, are_ref[:, gi, :], bre_ref[:, gi, :])
        s_im = jnp.where(fwd_rows, aim_ref[:, gi, :], bim_ref[:, gi, :])
        scat = jnp.concatenate([s_re, s_im], axis=1).astype(BF16)
        y = (jnp.dot(u_ref[gi, :n_chunks, :], mm_ref[gi], preferred_element_type=F32)
             + jnp.dot(scat, qq_ref[gi], preferred_element_type=F32))
        y_ref[gi] = y.astype(BF16)


def _s5(u_chunks, pp, mm, qq, lam16, n_chunks):
    GS = S5_GROUPS_PER_STEP
    n_rows = u_chunks.shape[1]
    W = S5_CHUNK * SSM_GROUP
    NS = 2 * SSM_STATE
    mat = lambda: pl.BlockSpec((GS, W, W), lambda g: (g, 0, 0))
    return pl.pallas_call(
        functools.partial(_s5_body, n_chunks=n_chunks),
        grid=(SSM_GROUPS // GS,),
        in_specs=[
            pl.BlockSpec((GS, n_rows, W), lambda g: (g, 0, 0)),
            mat(), mat(), mat(),
            pl.BlockSpec((2, GS, NS), lambda g: (0, g, 0)),
        ],
        out_specs=pl.BlockSpec((GS, n_chunks, W), lambda g: (g, 0, 0)),
        out_shape=jax.ShapeDtypeStruct((SSM_GROUPS, n_chunks, W), BF16),
        scratch_shapes=[pltpu.VMEM((n_rows, GS, NS), F32), pltpu.VMEM((n_rows, GS, NS), F32)]
        + [pltpu.VMEM((n_chunks, GS, NS), F32) for _ in range(4)],
        compiler_params=pltpu.CompilerParams(
            dimension_semantics=("arbitrary",), vmem_limit_bytes=VMEM_LIMIT),
        name="s5_chunked",
    )(u_chunks, pp, mm, qq, lam16)


def _mix_out_body(x_ref, att_ref, y_ref, u_ref, d_ref, wglu_ref, bglu_ref, sn_ref, wout_ref,
                  nf_ref, rwt_ref, rb_ref, tri_ref, sgu_ref, sd_ref, h_ref, hn_ref, w_ref, slot_ref, cnt_ref):
    y = y_ref[...].astype(F32) + d_ref[...] * u_ref[...].astype(F32)
    y = jax.nn.gelu(y)
    gate = jax.nn.sigmoid(jnp.dot(y.astype(BF16), wglu_ref[...], preferred_element_type=F32) + bglu_ref[...])
    s = y * gate
    ms = jnp.mean(s * s, axis=-1, keepdims=True)
    ssm = (s * lax.rsqrt(ms + EPS) * sn_ref[...]).astype(BF16)
    mixed = jnp.concatenate([att_ref[...], ssm], axis=1)
    h = x_ref[...] + jnp.dot(mixed, wout_ref[...], preferred_element_type=F32)
    ms = jnp.mean(h * h, axis=-1, keepdims=True)
    hn = h * lax.rsqrt(ms + EPS) * nf_ref[...]
    hnb = hn.astype(BF16)
    hn_ref[...] = hnb
    logits_t = lax.dot_general(rwt_ref[...], hn, (((1,), (1,)), ((), ())),
                               precision=lax.Precision.HIGHEST, preferred_element_type=F32)
    w_ref[...], slot_ref[...], cnt_ref[0] = _route_block(jax.nn.sigmoid(logits_t), rb_ref[...], tri_ref[...])
    gu = jnp.dot(hnb, sgu_ref[...], preferred_element_type=F32)
    act = (jax.nn.silu(gu[:, :SHARED_HIDDEN]) * gu[:, SHARED_HIDDEN:]).astype(BF16)
    h_ref[...] = h + jnp.dot(act, sd_ref[...], preferred_element_type=F32)


def _mix_out(x2, att, y, u, d_skip, w_glu, b_glu, ssm_norm, w_out, norm_ffn, router_wt, router_bias, tri,
             sh_gate_up, sh_down):
    seq = x2.shape[0]
    R = MIX_ROW_BLOCK
    assert R == MOE_TOKEN_BLOCK
    res = lambda shape: _resident(shape, lambda i: (0, 0))
    return pl.pallas_call(
        _mix_out_body,
        grid=(seq // R,),
        in_specs=[
            pl.BlockSpec((R, D_MODEL), lambda i: (i, 0)),
            pl.BlockSpec((R, ATT_WIDTH), lambda i: (i, 0)),
            pl.BlockSpec((R, SSM_WIDTH), lambda i: (i, 0)),
            pl.BlockSpec((R, SSM_WIDTH), lambda i: (i, 0)),
            res((1, SSM_WIDTH)), res((SSM_WIDTH, SSM_WIDTH)), res((1, SSM_WIDTH)), res((1, SSM_WIDTH)),
            res((D_MODEL, D_MODEL)), res((1, D_MODEL)), res((N_EXPERTS, D_MODEL)),
            res((N_EXPERTS, 1)), res((R, R)),
            res((D_MODEL, 2 * SHARED_HIDDEN)), res((SHARED_HIDDEN, D_MODEL)),
        ],
        out_specs=[
            pl.BlockSpec((R, D_MODEL), lambda i: (i, 0)),
            pl.BlockSpec((R, D_MODEL), lambda i: (i, 0)),
            pl.BlockSpec((N_EXPERTS, R), lambda i: (0, i)),
            pl.BlockSpec((N_EXPERTS, R), lambda i: (0, i)),
            pl.BlockSpec((1, N_EXPERTS, LANES), lambda i: (i, 0, 0)),
        ],
        out_shape=[
            jax.ShapeDtypeStruct((seq, D_MODEL), F32),
            jax.ShapeDtypeStruct((seq, D_MODEL), BF16),
            jax.ShapeDtypeStruct((N_EXPERTS, seq), F32),
            jax.ShapeDtypeStruct((N_EXPERTS, seq), F32),
            jax.ShapeDtypeStruct((seq // R, N_EXPERTS, LANES), F32),
        ],
        compiler_params=pltpu.CompilerParams(
            dimension_semantics=("arbitrary",), vmem_limit_bytes=VMEM_LIMIT),
        name="mix_out_shared",
    )(x2, att, y, u, d_skip, w_glu, b_glu, ssm_norm, w_out, norm_ffn, router_wt, router_bias, tri,
      sh_gate_up, sh_down)


def _route_block(scores, router_bias, tri):
    R = scores.shape[1]
    per_group = N_EXPERTS // N_EXPERT_GROUPS
    choice = scores + router_bias
    c3 = choice.reshape(N_EXPERT_GROUPS, per_group, R)
    within = lax.broadcasted_iota(jnp.int32, c3.shape, 1)
    m1 = jnp.max(c3, axis=1, keepdims=True)
    first = jnp.min(jnp.where(c3 == m1, within, per_group), axis=1, keepdims=True)
    m2 = jnp.max(jnp.where(within == first, -jnp.inf, c3), axis=1, keepdims=True)
    grp = (m1 + m2).reshape(N_EXPERT_GROUPS, R)
    gidx = lax.broadcasted_iota(jnp.int32, grp.shape, 0)
    grank = jnp.zeros(grp.shape, jnp.int32)
    for b in range(N_EXPERT_GROUPS):
        gb = grp[b:b + 1, :]
        grank += ((gb > grp) | ((gb == grp) & (b < gidx))).astype(jnp.int32)
    gmask = grank < TOPK_GROUPS
    emask = jnp.broadcast_to(gmask[:, None, :], c3.shape).reshape(N_EXPERTS, R)
    val = jnp.where(emask, choice, -jnp.inf)
    eidx = lax.broadcasted_iota(jnp.int32, val.shape, 0)
    rank = jnp.zeros(val.shape, jnp.int32)
    for e in range(N_EXPERTS):
        ve = val[e:e + 1, :]
        rank += ((ve > val) | ((ve == val) & (e < eidx))).astype(jnp.int32)
    sel = rank < TOP_K
    w = jnp.where(sel, scores, 0.0)
    weights = w / jnp.sum(w, axis=0, keepdims=True) * ROUTED_SCALE
    cum = jnp.dot(jnp.where(sel, 1.0, 0.0).astype(BF16), tri, preferred_element_type=F32)
    slots = jnp.where(sel, cum - 1.0, NOT_ROUTED)
    return weights, slots, jnp.broadcast_to(cum[:, R - 1:R], (N_EXPERTS, LANES))


def _moe_tables(cnt):
    n_blk = cnt.shape[0]
    U, RB, SB = MOE_UNIT, MOE_BLOCK_ROWS, MOE_SLOT_BLOCK
    NU = SB // U
    E = N_EXPERTS
    pc = (cnt + U - 1) // U * U
    off = jnp.cumsum(pc, axis=1) - pc
    upc_t = (pc // U).T
    cum_t = jnp.cumsum(upc_t, axis=1)
    units_e = cum_t[:, -1]
    nblk_e = (units_e + NU - 1) // NU
    blk_end = jnp.cumsum(nblk_e)
    blk_start = blk_end - nblk_e
    n_act = blk_end[-1]
    max_blocks = n_blk * (RB // SB) + E
    i = jnp.arange(max_blocks, dtype=jnp.int32)
    active = i < n_act
    count_le = lambda edges, v: jnp.sum((edges <= v[..., None]).astype(jnp.int32), axis=-1)
    last_e = jnp.minimum(count_le(blk_end, n_act - 1), E - 1)
    be = jnp.where(active, jnp.minimum(count_le(blk_end[None, :], i), E - 1), last_e)
    oh_e = be[:, None] == jnp.arange(E, dtype=jnp.int32)[None, :]
    pick_e = lambda v: jnp.sum(jnp.where(oh_e, v[None, :], 0), axis=1)
    pick_e2 = lambda m: jnp.sum(jnp.where(oh_e[:, :, None], m[None, :, :], 0), axis=1)
    bstart_i = pick_e(blk_start)
    first = active & (i == bstart_i)
    local = (i - bstart_i)[:, None] * NU + jnp.arange(NU, dtype=jnp.int32)[None, :]
    valid = active[:, None] & (local < pick_e(units_e)[:, None])
    cum_i = pick_e2(cum_t)
    b_of = jnp.minimum(count_le(cum_i[:, None, :], local), n_blk - 1)
    oh_b = b_of[:, :, None] == jnp.arange(n_blk, dtype=jnp.int32)[None, None, :]
    pick_b = lambda m: jnp.sum(jnp.where(oh_b, m[:, None, :], 0), axis=2)
    seg_start = pick_b(cum_i) - pick_b(pick_e2(upc_t))
    unit = (b_of * RB + pick_b(pick_e2(off.T))) // U + (local - seg_start)
    spare = n_blk * RB // U
    src = jnp.where(valid, unit, spare)
    dst = jnp.where(valid, unit,
                    spare + (1 + i % 2)[:, None] * NU + jnp.arange(NU, dtype=jnp.int32)[None, :])
    short = active & (pick_e(units_e) - (i - bstart_i) * NU <= NU // 2)
    i32 = lambda a: a.astype(jnp.int32)
    flags = i32(first) + 2 * i32(short)
    return (off, pc, i32(src.reshape(-1)), i32(dst.reshape(-1)), i32(be), flags,
            i32(n_act.reshape(1)))


def _split_hi_lo(pos):
    hi = jnp.floor(pos * (1.0 / 64.0))
    return hi.astype(BF16), (pos - 64.0 * hi).astype(BF16)


def _dispatch_body(hn_ref, slot_ref, offc_ref, offl_ref, pcl_ref, xs_ref, *, n_blk):
    b = pl.program_id(0)
    RB, TB = MOE_BLOCK_ROWS, MOE_TOKEN_BLOCK

    @pl.when(b == n_blk)
    def _():
        xs_ref[...] = jnp.zeros(xs_ref.shape, BF16)

    @pl.when(b < n_blk)
    def _():
        pos = slot_ref[...] + offc_ref[0]
        hi, lo = _split_hi_lo(pos)
        r = lax.broadcasted_iota(jnp.int32, (RB, N_EXPERTS), 0).astype(F32)
        off = offl_ref[0]
        owner = jnp.where((r >= off) & (r < off + pcl_ref[0]), 1.0, 0.0).astype(BF16)
        p = (64.0 * jnp.dot(owner, hi, preferred_element_type=F32)
             + jnp.dot(owner, lo, preferred_element_type=F32))
        rr = lax.broadcasted_iota(jnp.int32, (RB, TB), 0).astype(F32)
        onehot = jnp.where(jnp.abs(p - rr) < 0.5, 1.0, 0.0).astype(BF16)
        x = hn_ref[...]
        C = MOE_SLOT_BLOCK
        for c in range(RB // C):
            xs_ref[0, c * C:(c + 1) * C, :] = jnp.dot(
                onehot[c * C:(c + 1) * C], x, preferred_element_type=F32).astype(BF16)


def _dispatch(hn, slots, off, pc):
    n_blk = off.shape[0]
    RB, TB = MOE_BLOCK_ROWS, MOE_TOKEN_BLOCK
    clamp = lambda b: jnp.minimum(b, n_blk - 1)
    return pl.pallas_call(
        functools.partial(_dispatch_body, n_blk=n_blk),
        grid=(n_blk + 1,),
        in_specs=[
            pl.BlockSpec((TB, D_MODEL), lambda b: (clamp(b), 0)),
            pl.BlockSpec((N_EXPERTS, TB), lambda b: (0, clamp(b))),
            pl.BlockSpec((1, N_EXPERTS, 1), lambda b: (clamp(b), 0, 0)),
            pl.BlockSpec((1, 1, N_EXPERTS), lambda b: (clamp(b), 0, 0)),
            pl.BlockSpec((1, 1, N_EXPERTS), lambda b: (clamp(b), 0, 0)),
        ],
        out_specs=pl.BlockSpec((1, RB, D_MODEL), lambda b: (b, 0, 0)),
        out_shape=jax.ShapeDtypeStruct((n_blk + 1, RB, D_MODEL), BF16),
        compiler_params=pltpu.CompilerParams(
            dimension_semantics=("arbitrary",), vmem_limit_bytes=VMEM_LIMIT),
        name="moe_dispatch",
    )(hn, slots, off[:, :, None], off[:, None, :], pc[:, None, :])


def _expert_body(src_ref, dst_ref, bexp_ref, first_ref, nact_ref,
                 xs_hbm, wg_ref, wu_ref, wd_ref, ys_hbm,
                 xbuf, ybuf, wgub, wdb, sem_in, sem_out):
    del bexp_ref
    i = pl.program_id(0)
    n_act = nact_ref[0]
    cur = lax.rem(i, 2)
    U = MOE_UNIT
    NU = MOE_SLOT_BLOCK // U

    def in_copy(blk, buf, u):
        return pltpu.make_async_copy(xs_hbm.at[src_ref[blk * NU + u]],
                                     xbuf.at[buf, pl.ds(u * U, U)], sem_in.at[buf])

    def out_copy(blk, buf, u):
        return pltpu.make_async_copy(ybuf.at[buf, pl.ds(u * U, U)],
                                     ys_hbm.at[dst_ref[blk * NU + u]], sem_out.at[buf])

    @pl.when(i == 0)
    def _():
        for u in range(NU):
            in_copy(0, 0, u).start()

    @pl.when(i < n_act)
    def _():
        @pl.when(i + 1 < n_act)
        def _():
            for u in range(NU):
                in_copy(i + 1, 1 - cur, u).start()

        @pl.when(first_ref[i] % 2 == 1)
        def _():
            wgub[:, :EXPERT_HIDDEN] = wg_ref[0].astype(BF16)
            wgub[:, EXPERT_HIDDEN:] = wu_ref[0].astype(BF16)
            wdb[...] = wd_ref[0].astype(BF16)

        for u in range(NU):
            in_copy(i, cur, u).wait()

        @pl.when(i >= 2)
        def _():
            for u in range(NU):
                out_copy(i - 2, cur, u).wait()

        def swiglu_rows(rows):
            x = xbuf[cur, :rows]
            gu = jnp.dot(x, wgub[...], preferred_element_type=F32)
            act = (jax.nn.silu(gu[:, :EXPERT_HIDDEN]) * gu[:, EXPERT_HIDDEN:]).astype(BF16)
            ybuf[cur, :rows] = jnp.dot(act, wdb[...], preferred_element_type=F32).astype(BF16)

        half = MOE_SLOT_BLOCK // 2

        @pl.when(first_ref[i] < 2)
        def _():
            swiglu_rows(MOE_SLOT_BLOCK)

        @pl.when(first_ref[i] >= 2)
        def _():
            swiglu_rows(half)
            ybuf[cur, half:] = jnp.zeros((half, D_MODEL), BF16)

        for u in range(NU):
            out_copy(i, cur, u).start()

        @pl.when(i == n_act - 1)
        def _():
            for u in range(NU):
                out_copy(i, cur, u).wait()

            @pl.when(i >= 1)
            def _():
                for u in range(NU):
                    out_copy(i - 1, 1 - cur, u).wait()


def _experts(xs, src, dst, bexp, first, n_act, wg, wu, wd):
    U, SB = MOE_UNIT, MOE_SLOT_BLOCK
    n_units = xs.shape[0] * xs.shape[1] // U
    max_blocks = bexp.shape[0]
    unit_view = lambda a: a.reshape(n_units, U, D_MODEL)
    wspec = lambda shape: pl.BlockSpec((1,) + shape, lambda i, src, dst, bexp, first, nact: (bexp[i], 0, 0))
    grid_spec = pltpu.PrefetchScalarGridSpec(
        num_scalar_prefetch=5,
        grid=(max_blocks,),
        in_specs=[
            pl.BlockSpec(memory_space=pl.ANY),
            wspec((D_MODEL, EXPERT_HIDDEN)), wspec((D_MODEL, EXPERT_HIDDEN)), wspec((EXPERT_HIDDEN, D_MODEL)),
        ],
        out_specs=pl.BlockSpec(memory_space=pl.ANY),
        scratch_shapes=[
            pltpu.VMEM((2, SB, D_MODEL), BF16), pltpu.VMEM((2, SB, D_MODEL), BF16),
            pltpu.VMEM((D_MODEL, 2 * EXPERT_HIDDEN), BF16), pltpu.VMEM((EXPERT_HIDDEN, D_MODEL), BF16),
            pltpu.SemaphoreType.DMA((2,)), pltpu.SemaphoreType.DMA((2,)),
        ],
    )
    ys = pl.pallas_call(
        _expert_body,
        grid_spec=grid_spec,
        out_shape=jax.ShapeDtypeStruct((n_units, U, D_MODEL), BF16),
        input_output_aliases={5: 0},
        compiler_params=pltpu.CompilerParams(
            dimension_semantics=("arbitrary",), vmem_limit_bytes=VMEM_LIMIT),
        name="moe_experts",
    )(src, dst, bexp, first, n_act, unit_view(xs), wg, wu, wd)
    return ys.reshape(xs.shape)


def _combine_body(h_ref, ys_ref, slot_ref, w_ref, offl_ref, offc_ref, pcc_ref, o_ref):
    RB, TB = MOE_BLOCK_ROWS, MOE_TOKEN_BLOCK
    pos = slot_ref[...] + offl_ref[0]
    hi, lo = _split_hi_lo(pos)
    r = lax.broadcasted_iota(jnp.int32, (N_EXPERTS, RB), 1).astype(F32)
    off = offc_ref[0]
    owner = jnp.where((r >= off) & (r < off + pcc_ref[0]), 1.0, 0.0).astype(BF16)
    p = (64.0 * jnp.dot(hi, owner, preferred_element_type=F32)
         + jnp.dot(lo, owner, preferred_element_type=F32))
    wr = jnp.dot(w_ref[...].astype(BF16), owner, preferred_element_type=F32)
    rr = lax.broadcasted_iota(jnp.int32, (TB, RB), 1).astype(F32)
    gather_w = jnp.where(jnp.abs(p - rr) < 0.5, wr, 0.0).astype(BF16)
    o_ref[...] = h_ref[...] + jnp.dot(gather_w, ys_ref[0], preferred_element_type=F32)


def _combine(h, ys, slots_tok, w_tok, off, pc):
    n_blk = off.shape[0]
    RB, TB = MOE_BLOCK_ROWS, MOE_TOKEN_BLOCK
    seq = h.shape[0]
    return pl.pallas_call(
        _combine_body,
        grid=(n_blk,),
        in_specs=[
            pl.BlockSpec((TB, D_MODEL), lambda b: (b, 0)),
            pl.BlockSpec((1, RB, D_MODEL), lambda b: (b, 0, 0)),
            pl.BlockSpec((TB, N_EXPERTS), lambda b: (b, 0)),
            pl.BlockSpec((TB, N_EXPERTS), lambda b: (b, 0)),
            pl.BlockSpec((1, 1, N_EXPERTS), lambda b: (b, 0, 0)),
            pl.BlockSpec((1, N_EXPERTS, 1), lambda b: (b, 0, 0)),
            pl.BlockSpec((1, N_EXPERTS, 1), lambda b: (b, 0, 0)),
        ],
        out_specs=pl.BlockSpec((TB, D_MODEL), lambda b: (b, 0)),
        out_shape=jax.ShapeDtypeStruct((seq, D_MODEL), F32),
        compiler_params=pltpu.CompilerParams(
            dimension_semantics=("arbitrary",), vmem_limit_bytes=VMEM_LIMIT),
        name="moe_combine",
    )(h, ys, slots_tok, w_tok, off[:, None, :], off[:, :, None], pc[:, :, None])


def kernel(x, meta_tokens, rel_bias, norm_mix, w_in, q_norm, k_norm, lam_q1, lam_k1, lam_q2, lam_k2, subln, ssm_a_re, ssm_a_im, ssm_log_step, ssm_b_re, ssm_b_im, ssm_c_re, ssm_c_im, ssm_d, w_glu, b_glu, ssm_norm, w_out, norm_ffn, router_w, router_bias, w_gate, w_up, w_down, shared_gate, shared_up, shared_down):
    batch, seq, d = x.shape
    assert batch == 1 and d == D_MODEL and seq % ROW_BLOCK == 0 and seq % ATT_BLOCK == 0
    assert norm_mix.shape[0] == 1, "single layer"
    x2 = x.reshape(seq, d)
    meta_pad = jnp.zeros((ROW_BLOCK, d), F32).at[:N_META].set(meta_tokens.astype(F32))
    seg = jnp.kron(jnp.eye(QK_WIDTH // HEAD_DIM, dtype=F32),
                   jnp.full((HEAD_DIM, HEAD_DIM), 1.0 / HEAD_DIM, F32)).astype(BF16)
    qg = jnp.tile(q_norm[0].astype(F32), QK_WIDTH // HEAD_DIM)[None] * (HEAD_DIM ** -0.5)
    kg = jnp.tile(k_norm[0].astype(F32), QK_WIDTH // HEAD_DIM)[None]

    proj, u = _inproj(x2, meta_pad, norm_mix[0][None], w_in[0].astype(BF16), seg, qg, kg)

    score_bound = (BOUND_MARGIN * HEAD_DIM ** 0.5 * jnp.max(jnp.abs(q_norm[0].astype(F32)))
                   * jnp.max(jnp.abs(k_norm[0].astype(F32)))
                   + jnp.max(jnp.abs(rel_bias.astype(F32)))).reshape(1)
    att = _attention(proj, rel_bias.astype(F32), score_bound, lam_q1[0][None], lam_k1[0][None],
                     lam_q2[0][None], lam_k2[0][None], subln[0][None], seq)

    n_rows = proj.shape[0] // S5_CHUNK
    n_chunks = seq // S5_CHUNK
    perm = _lane_regroup_matrix()
    u_chunks = _to_chunks(u.reshape(n_rows, S5_CHUNK, SSM_WIDTH), perm)
    pp, mm, qq, lam16 = _s5_prep(ssm_a_re[0].astype(F32), ssm_a_im[0].astype(F32),
                                 ssm_log_step[0].astype(F32), ssm_b_re[0].astype(F32),
                                 ssm_b_im[0].astype(F32), ssm_c_re[0].astype(F32),
                                 ssm_c_im[0].astype(F32))
    y_chunks = _s5(u_chunks, pp, mm, qq, lam16, n_chunks)
    y = _from_chunks(y_chunks, perm.T).reshape(seq, SSM_WIDTH)

    tb = MOE_TOKEN_BLOCK
    tri = (jnp.arange(tb)[:, None] <= jnp.arange(tb)[None, :]).astype(BF16)
    h, hn, wts_t, slots_t, cnt = _mix_out(
        x2, att, y, u, ssm_d[0][None].astype(F32), w_glu[0].astype(BF16), b_glu[0][None].astype(F32),
        ssm_norm[0][None].astype(F32), w_out[0].astype(BF16), norm_ffn[0][None].astype(F32),
        router_w[0].astype(F32).T, router_bias[0].astype(F32)[:, None], tri,
        jnp.concatenate([shared_gate[0], shared_up[0]], axis=1).astype(BF16), shared_down[0].astype(BF16))

    off, pc, src, dst, bexp, first, n_act = _moe_tables(cnt[:, :, 0].astype(jnp.int32))
    off_f, pc_f = off.astype(F32), pc.astype(F32)
    xs = _dispatch(hn, slots_t, off_f, pc_f)
    ys = _experts(xs, src, dst, bexp, first, n_act, w_gate[0], w_up[0], w_down[0])
    out = _combine(h, ys, slots_t.T, wts_t.T, off_f, pc_f)
    return out.reshape(batch, seq, d)
```

```python
import functools
import math

import jax
import jax.numpy as jnp
from jax import lax
from jax.experimental import pallas as pl
from jax.experimental.pallas import tpu as pltpu

F32 = jnp.float32
BF16 = jnp.bfloat16

D_MODEL = 2048
N_META = 16
ATT_WIDTH = 1024
SSM_WIDTH = 1024
HEAD_DIM = 64
V_DIM = 128
HEADS = 8
QK_WIDTH = 1024
IN_WIDTH = 4096
V_OFFSET = 2 * QK_WIDTH
PROJ_WIDTH = V_OFFSET + 2 * ATT_WIDTH
SSM_GROUP = 16
SSM_GROUPS = 64
SSM_STATE = 64
N_BUCKETS = 32
MAX_DISTANCE = 128
N_EXPERTS = 64
TOP_K = 8
N_EXPERT_GROUPS = 8
TOPK_GROUPS = 4
EXPERT_HIDDEN = 512
SHARED_HIDDEN = 512
ROUTED_SCALE = 2.5
EPS = 1e-6
LAMBDA_INIT = 0.8 - 0.6 * math.exp(-0.3 * 0)

ROW_BLOCK = 512
ATT_BLOCK = 512
S5_CHUNK = 16
S5_GROUPS_PER_STEP = 8
S5_OCTET_WIDTH = S5_GROUPS_PER_STEP * S5_CHUNK * SSM_GROUP
assert S5_GROUPS_PER_STEP * SSM_GROUP == 128
MOE_TOKEN_BLOCK = 256
MOE_UNIT = 16
MOE_SLOT_BLOCK = 512
MOE_BLOCK_ROWS = -(-(MOE_TOKEN_BLOCK * TOP_K + N_EXPERTS * (MOE_UNIT - 1)) // MOE_SLOT_BLOCK) * MOE_SLOT_BLOCK
NOT_ROUTED = -1e6
MIX_ROW_BLOCK = 256
NEG_BIG = -1e30
MAX_EXP_RANGE = 80.0
BOUND_MARGIN = 1.02
LANES = 128
SUBLANES = 8
VMEM_LIMIT = 56 * 2 ** 20


def _resident(shape, index_map):
    return pl.BlockSpec(shape, index_map, pipeline_mode=pl.Buffered(1))


def _inproj_body(x_ref, meta_ref, g_ref, w_ref, seg_ref, qg_ref, kg_ref, o_ref, u_ref, *, n_xblk):
    i = pl.program_id(0)

    def run(src_ref):
        xv = src_ref[...]
        ms = jnp.mean(xv * xv, axis=-1, keepdims=True)
        hn = (xv * lax.rsqrt(ms + EPS) * g_ref[...]).astype(BF16)
        sec = IN_WIDTH // 4
        for s in range(4):
            ps = jnp.dot(hn, w_ref[:, s * sec:(s + 1) * sec], preferred_element_type=F32)
            if s < 2:
                gain = qg_ref if s == 0 else kg_ref
                msq = jnp.dot((ps * ps).astype(BF16), seg_ref[...], preferred_element_type=F32)
                ps = ps * lax.rsqrt(msq + EPS) * gain[...]
            if s == 3:
                u_ref[...] = ps
                continue
            pb = ps.astype(BF16)
            if s < 2:
                o_ref[:, s * sec:(s + 1) * sec] = pb
            else:
                lane = lax.broadcasted_iota(jnp.int32, (pb.shape[0], V_DIM), 1)
                ones_col = jnp.where(lane == 0, 1.0, 0.0).astype(BF16)
                for hh in range(HEADS):
                    base = V_OFFSET + hh * 2 * V_DIM
                    o_ref[:, base:base + V_DIM] = pb[:, hh * V_DIM:(hh + 1) * V_DIM]
                    o_ref[:, base + V_DIM:base + 2 * V_DIM] = ones_col

    @pl.when(i < n_xblk)
    def _():
        run(x_ref)

    @pl.when(i == n_xblk)
    def _():
        run(meta_ref)


def _inproj(x2, meta_pad, gain, w_bf, seg, qg, kg):
    seq = x2.shape[0]
    n_xblk = seq // ROW_BLOCK
    rows = seq + ROW_BLOCK
    return pl.pallas_call(
        functools.partial(_inproj_body, n_xblk=n_xblk),
        grid=(n_xblk + 1,),
        in_specs=[
            pl.BlockSpec((ROW_BLOCK, D_MODEL), lambda i: (jnp.minimum(i, n_xblk - 1), 0)),
            _resident((ROW_BLOCK, D_MODEL), lambda i: (0, 0)),
            _resident((1, D_MODEL), lambda i: (0, 0)),
            _resident((D_MODEL, IN_WIDTH), lambda i: (0, 0)),
            _resident((QK_WIDTH, QK_WIDTH), lambda i: (0, 0)),
            _resident((1, QK_WIDTH), lambda i: (0, 0)),
            _resident((1, QK_WIDTH), lambda i: (0, 0)),
        ],
        out_specs=[pl.BlockSpec((ROW_BLOCK, PROJ_WIDTH), lambda i: (i, 0)),
                   pl.BlockSpec((ROW_BLOCK, SSM_WIDTH), lambda i: (i, 0))],
        out_shape=[jax.ShapeDtypeStruct((rows, PROJ_WIDTH), BF16),
                   jax.ShapeDtypeStruct((rows, SSM_WIDTH), F32)],
        compiler_params=pltpu.CompilerParams(
            dimension_semantics=("arbitrary",), vmem_limit_bytes=VMEM_LIMIT),
        name="inproj",
    )(x2, meta_pad, gain, w_bf, seg, qg, kg)


def _t5_bias(rel, tab_ref, h):
    half = N_BUCKETS // 2
    exact = half // 2
    n = jnp.abs(rel)
    nf = jnp.maximum(n, 1).astype(F32)
    large = exact + (jnp.log(nf / exact) / math.log(MAX_DISTANCE / exact) * (half - exact)).astype(jnp.int32)
    large = jnp.minimum(large, half - 1)
    bucket = jnp.where(rel > 0, half, 0) + jnp.where(n < exact, n, large)
    out = jnp.zeros(rel.shape, F32)
    for b in range(N_BUCKETS):
        out = jnp.where(bucket == b, tab_ref[b, h], out)
    return out


def _attn_body(tab_ref, bound_ref, q_ref, k_ref, v_ref, lq1_ref, lk1_ref, lq2_ref, lk2_ref,
               subln_ref, o_ref, bias_ref, acc1_ref, acc2_ref, m1_ref, m2_ref, *, n_main):
    T = ATT_BLOCK
    h = pl.program_id(0)
    qi = pl.program_id(1)
    bound = bound_ref[0]

    @pl.when(qi == 0)
    def _():
        offsets = (-T, 0, T, -N_META, -N_META - T, -2 * T, 2 * T)
        for kind, off in enumerate(offsets):
            masked = kind in (3, 4)

            def rows(rc, carry, off=off, masked=masked, kind=kind):
                r0 = pl.multiple_of(rc * SUBLANES, SUBLANES)
                r = r0 + lax.broadcasted_iota(jnp.int32, (SUBLANES, T), 0)
                c = lax.broadcasted_iota(jnp.int32, (SUBLANES, T), 1)
                b = _t5_bias(off + c - r, tab_ref, h) - bound
                if masked:
                    b = jnp.where(c < N_META, b, NEG_BIG)
                bias_ref[kind, pl.ds(r0, SUBLANES), :] = b
                return carry

            lax.fori_loop(0, T // SUBLANES, rows, 0)

    acc1_ref[...] = jnp.zeros(acc1_ref.shape, F32)
    acc2_ref[...] = jnp.zeros(acc2_ref.shape, F32)

    q = q_ref[...]
    q1 = q[:, :HEAD_DIM]
    q2 = q[:, HEAD_DIM:]
    nt = (((1,), (1,)), ((), ()))

    def tile(ki):
        koff = pl.multiple_of(ki * T, T)
        kb = k_ref[pl.ds(koff, T), :]
        va = v_ref[pl.ds(koff, T), :]
        d = ki - qi
        kind = jnp.where(ki == n_main, jnp.where(qi == 0, 3, 4),
                         jnp.where(d <= -2, 5, jnp.where(d >= 2, 6, d + 1)))
        return kb, va, bias_ref[kind]

    def bounded_step(ki, carry):
        kb, va, bias = tile(ki)
        s1 = lax.dot_general(q1, kb[:, :HEAD_DIM], nt, preferred_element_type=F32) + bias
        acc1_ref[...] += jnp.dot(jnp.exp(s1).astype(BF16), va, preferred_element_type=F32)
        s2 = lax.dot_general(q2, kb[:, HEAD_DIM:], nt, preferred_element_type=F32) + bias
        acc2_ref[...] += jnp.dot(jnp.exp(s2).astype(BF16), va, preferred_element_type=F32)
        return carry

    def online_map(s, va, m_ref, acc_ref):
        m_old = m_ref[...]
        m_new = jnp.maximum(m_old, jnp.max(s, axis=-1, keepdims=True))
        p = jnp.exp(s - m_new).astype(BF16)
        acc_ref[...] = (jnp.exp(m_old - m_new) * acc_ref[...]
                        + jnp.dot(p, va, preferred_element_type=F32))
        m_ref[...] = m_new

    def online_step(ki, carry):
        kb, va, bias = tile(ki)
        s1 = lax.dot_general(q1, kb[:, :HEAD_DIM], nt, preferred_element_type=F32) + bias
        online_map(s1, va, m1_ref, acc1_ref)
        s2 = lax.dot_general(q2, kb[:, HEAD_DIM:], nt, preferred_element_type=F32) + bias
        online_map(s2, va, m2_ref, acc2_ref)
        return carry

    no_running_max = 2.0 * bound <= MAX_EXP_RANGE

    @pl.when(no_running_max)
    def _():
        unroll = next(u for u in (33, 11, 3, 2, 1) if (n_main + 1) % u == 0)
        lax.fori_loop(0, n_main + 1, bounded_step, 0, unroll=unroll)

    @pl.when(jnp.logical_not(no_running_max))
    def _():
        m1_ref[...] = jnp.full(m1_ref.shape, -jnp.inf, F32)
        m2_ref[...] = jnp.full(m2_ref.shape, -jnp.inf, F32)
        lax.fori_loop(0, n_main + 1, online_step, 0)

    lam = (jnp.exp(jnp.sum(lq1_ref[...] * lk1_ref[...], axis=-1, keepdims=True))
           - jnp.exp(jnp.sum(lq2_ref[...] * lk2_ref[...], axis=-1, keepdims=True))
           + LAMBDA_INIT)
    a1 = acc1_ref[...]
    a2 = acc2_ref[...]
    o = (a1[:, :V_DIM] / a1[:, V_DIM:V_DIM + 1]
         - lam * (a2[:, :V_DIM] / a2[:, V_DIM:V_DIM + 1]))
    ms = jnp.mean(o * o, axis=-1, keepdims=True)
    o = o * lax.rsqrt(ms + EPS) * subln_ref[...] * (1.0 - LAMBDA_INIT)
    o_ref[...] = o.astype(BF16)


def _attention(proj, rel_bias, score_bound, lq1, lk1, lq2, lk2, subln, seq):
    T = ATT_BLOCK
    n_main = seq // T
    rows = proj.shape[0]
    vec64 = lambda: _resident((1, HEAD_DIM), lambda h, qi: (0, 0))
    return pl.pallas_call(
        functools.partial(_attn_body, n_main=n_main),
        grid=(HEADS, n_main),
        in_specs=[
            pl.BlockSpec(memory_space=pltpu.SMEM),
            pl.BlockSpec(memory_space=pltpu.SMEM),
            pl.BlockSpec((T, 2 * HEAD_DIM), lambda h, qi: (qi, h)),
            pl.BlockSpec((rows, 2 * HEAD_DIM), lambda h, qi: (0, HEADS + h)),
            pl.BlockSpec((rows, 2 * V_DIM), lambda h, qi: (0, V_OFFSET // (2 * V_DIM) + h)),
            vec64(), vec64(), vec64(), vec64(),
            _resident((1, V_DIM), lambda h, qi: (0, 0)),
        ],
        out_specs=pl.BlockSpec((T, V_DIM), lambda h, qi: (qi, h)),
        out_shape=jax.ShapeDtypeStruct((seq, ATT_WIDTH), BF16),
        scratch_shapes=[
            pltpu.VMEM((7, T, T), F32),
            pltpu.VMEM((T, 2 * V_DIM), F32), pltpu.VMEM((T, 2 * V_DIM), F32),
            pltpu.VMEM((T, 1), F32), pltpu.VMEM((T, 1), F32),
        ],
        compiler_params=pltpu.CompilerParams(
            dimension_semantics=("arbitrary", "arbitrary"), vmem_limit_bytes=VMEM_LIMIT),
        name="diff_attention",
    )(rel_bias, score_bound, proj, proj, proj, lq1, lk1, lq2, lk2, subln)


def _lane_regroup_matrix():
    out_lane = jnp.arange(S5_OCTET_WIDTH, dtype=jnp.int32)
    g8 = out_lane // (S5_CHUNK * SSM_GROUP)
    j = (out_lane % (S5_CHUNK * SSM_GROUP)) // SSM_GROUP
    p = out_lane % SSM_GROUP
    in_lane = j * LANES + g8 * SSM_GROUP + p
    return (jnp.arange(S5_OCTET_WIDTH, dtype=jnp.int32)[:, None] == in_lane[None, :]).astype(BF16)


def _to_chunks_body(u_ref, perm_ref, o_ref):
    x = jnp.concatenate([u_ref[:, j, :].astype(BF16) for j in range(S5_CHUNK)], axis=1)
    r = jnp.dot(x, perm_ref[...], preferred_element_type=F32)
    w = S5_CHUNK * SSM_GROUP
    for g8 in range(S5_GROUPS_PER_STEP):
        o_ref[g8] = r[:, g8 * w:(g8 + 1) * w].astype(BF16)


def _to_chunks(u3, perm):
    n_rows = u3.shape[0]
    w = S5_CHUNK * SSM_GROUP
    return pl.pallas_call(
        _to_chunks_body,
        grid=(SSM_GROUPS // S5_GROUPS_PER_STEP,),
        in_specs=[pl.BlockSpec((n_rows, S5_CHUNK, LANES), lambda o: (0, 0, o)),
                  _resident((S5_OCTET_WIDTH, S5_OCTET_WIDTH), lambda o: (0, 0))],
        out_specs=pl.BlockSpec((S5_GROUPS_PER_STEP, n_rows, w), lambda o: (o, 0, 0)),
        out_shape=jax.ShapeDtypeStruct((SSM_GROUPS, n_rows, w), BF16),
        compiler_params=pltpu.CompilerParams(
            dimension_semantics=("arbitrary",), vmem_limit_bytes=VMEM_LIMIT),
        name="s5_to_chunks",
    )(u3, perm)


def _from_chunks_body(y_ref, perm_t_ref, o_ref):
    x = jnp.concatenate([y_ref[g8] for g8 in range(S5_GROUPS_PER_STEP)], axis=1)
    r = jnp.dot(x, perm_t_ref[...], preferred_element_type=F32)
    for j in range(S5_CHUNK):
        o_ref[:, j, :] = r[:, j * LANES:(j + 1) * LANES]


def _from_chunks(y_chunks, perm_t):
    n_chunks = y_chunks.shape[1]
    w = S5_CHUNK * SSM_GROUP
    return pl.pallas_call(
        _from_chunks_body,
        grid=(SSM_GROUPS // S5_GROUPS_PER_STEP,),
        in_specs=[pl.BlockSpec((S5_GROUPS_PER_STEP, n_chunks, w), lambda o: (o, 0, 0)),
                  _resident((S5_OCTET_WIDTH, S5_OCTET_WIDTH), lambda o: (0, 0))],
        out_specs=pl.BlockSpec((n_chunks, S5_CHUNK, LANES), lambda o: (0, 0, o)),
        out_shape=jax.ShapeDtypeStruct((n_chunks, S5_CHUNK, SSM_WIDTH), F32),
        compiler_params=pltpu.CompilerParams(
            dimension_semantics=("arbitrary",), vmem_limit_bytes=VMEM_LIMIT),
        name="s5_from_chunks",
    )(y_chunks, perm_t)


def _s5_prep(a_re, a_im, log_step, b_re, b_im, c_re, c_im):
    C = S5_CHUNK
    dt = jnp.exp(log_step)[..., None]
    decay = jnp.exp(a_re * dt)
    ab_re = decay * jnp.cos(a_im * dt)
    ab_im = decay * jnp.sin(a_im * dt)
    den = a_re * a_re + a_im * a_im
    zr = ab_re - 1.0
    f_re = (zr * a_re + ab_im * a_im) / den
    f_im = (ab_im * a_re - zr * a_im) / den
    bb_re = f_re[..., None] * b_re - f_im[..., None] * b_im
    bb_im = f_re[..., None] * b_im + f_im[..., None] * b_re
    pr, pi = jnp.ones_like(ab_re), jnp.zeros_like(ab_re)
    pw_re, pw_im = [pr], [pi]
    for _ in range(C):
        pr, pi = pr * ab_re - pi * ab_im, pr * ab_im + pi * ab_re
        pw_re.append(pr)
        pw_im.append(pi)
    G, P, W = SSM_GROUPS, SSM_GROUP, C * SSM_GROUP
    pw_re = jnp.stack(pw_re, axis=-1)
    pw_im = jnp.stack(pw_im, axis=-1)
    ct_re = c_re.transpose(0, 1, 3, 2)
    ct_im = c_im.transpose(0, 1, 3, 2)
    cp_re = ct_re[:, :, :, None, :] * pw_re[..., None] - ct_im[:, :, :, None, :] * pw_im[..., None]
    cp_im = ct_re[:, :, :, None, :] * pw_im[..., None] + ct_im[:, :, :, None, :] * pw_re[..., None]
    bt_re = bb_re.transpose(0, 1, 3, 2)
    bt_im = bb_im.transpose(0, 1, 3, 2)

    def response(d):
        prod = (cp_re[d][:, None, :, :C, :] * bt_re[d][:, :, :, None, None]
                - cp_im[d][:, None, :, :C, :] * bt_im[d][:, :, :, None, None])
        return jnp.sum(prod, axis=2)

    ext_f = jnp.pad(response(0).reshape(G, P, W), ((0, 0), (0, 0), (W, 0)))
    ext_r = jnp.pad(response(1)[:, :, ::-1, :].reshape(G, P, W), ((0, 0), (0, 0), (0, W)))
    mm = jnp.stack([ext_f[:, :, W - P * j:2 * W - P * j] + ext_r[:, :, (C - 1 - j) * P:(C - 1 - j) * P + W]
                    for j in range(C)], axis=1).reshape(G, W, W)

    def in_mat(d, reverse_powers):
        pr_ = pw_re[d][:, :, :C].transpose(0, 2, 1)
        pi_ = pw_im[d][:, :, :C].transpose(0, 2, 1)
        if reverse_powers:
            pr_, pi_ = pr_[:, ::-1], pi_[:, ::-1]
        re = pr_[:, :, None, :] * bt_re[d][:, None] - pi_[:, :, None, :] * bt_im[d][:, None]
        im = pr_[:, :, None, :] * bt_im[d][:, None] + pi_[:, :, None, :] * bt_re[d][:, None]
        return re, im

    pf_re, pf_im = in_mat(0, True)
    pr_re, pr_im = in_mat(1, False)
    pp = jnp.concatenate([pf_re, pr_re, pf_im, pr_im], axis=-1).reshape(G, W, 4 * SSM_STATE)

    qq = jnp.concatenate([cp_re[0][:, :, 1:], cp_re[1][:, :, :0:-1],
                          -cp_im[0][:, :, 1:], -cp_im[1][:, :, :0:-1]], axis=1).reshape(G, 4 * SSM_STATE, W)
    lam16 = jnp.stack([jnp.concatenate([pw_re[0][:, :, C], pw_re[1][:, :, C]], axis=-1),
                       jnp.concatenate([pw_im[0][:, :, C], pw_im[1][:, :, C]], axis=-1)], axis=0)
    return pp.astype(BF16), mm.astype(BF16), qq.astype(BF16), lam16


def _s5_body(u_ref, pp_ref, mm_ref, qq_ref, lam_ref, y_ref,
             zre_ref, zim_ref, are_ref, aim_ref, bre_ref, bim_ref, *, n_chunks):
    GS = S5_GROUPS_PER_STEP
    NS = 2 * SSM_STATE
    for gi in range(GS):
        z = jnp.dot(u_ref[gi], pp_ref[gi], preferred_element_type=F32)
        zre_ref[:, gi, :] = z[:, :NS]
        zim_ref[:, gi, :] = z[:, NS:]

    ar = lam_ref[0]
    ai = lam_ref[1]
    fwd = lax.broadcasted_iota(jnp.int32, (GS, NS), 1) < SSM_STATE
    sre0 = jnp.where(fwd, zre_ref[n_chunks], 0.0)
    sim0 = jnp.where(fwd, zim_ref[n_chunks], 0.0)

    def scan_step(k, carry):
        sre, sim = carry
        kr = n_chunks - 1 - k
        are_ref[k] = sre
        aim_ref[k] = sim
        bre_ref[kr] = sre
        bim_ref[kr] = sim
        zr = jnp.where(fwd, zre_ref[k], zre_ref[kr])
        zi = jnp.where(fwd, zim_ref[k], zim_ref[kr])
        return ar * sre - ai * sim + zr, ar * sim + ai * sre + zi

    lax.fori_loop(0, n_chunks, scan_step, (sre0, sim0))

    fwd_rows = lax.broadcasted_iota(jnp.int32, (n_chunks, NS), 1) < SSM_STATE
    for gi in range(GS):
        s_re = jnp.where(fwd_rows, are_ref[:, gi, :], bre_ref[:, gi, :])
        s_im = jnp.where(fwd_rows, aim_ref[:, gi, :], bim_ref[:, gi, :])
        scat = jnp.concatenate([s_re, s_im], axis=1).astype(BF16)
        y = (jnp.dot(u_ref[gi, :n_chunks, :], mm_ref[gi], preferred_element_type=F32)
             + jnp.dot(scat, qq_ref[gi], preferred_element_type=F32))
        y_ref[gi] = y.astype(BF16)


def _s5(u_chunks, pp, mm, qq, lam16, n_chunks):
    GS = S5_GROUPS_PER_STEP
    n_rows = u_chunks.shape[1]
    W = S5_CHUNK * SSM_GROUP
    NS = 2 * SSM_STATE
    mat = lambda: pl.BlockSpec((GS, W, W), lambda g: (g, 0, 0))
    return pl.pallas_call(
        functools.partial(_s5_body, n_chunks=n_chunks),
        grid=(SSM_GROUPS // GS,),
        in_specs=[
            pl.BlockSpec((GS, n_rows, W), lambda g: (g, 0, 0)),
            mat(), mat(), mat(),
            pl.BlockSpec((2, GS, NS), lambda g: (0, g, 0)),
        ],
        out_specs=pl.BlockSpec((GS, n_chunks, W), lambda g: (g, 0, 0)),
        out_shape=jax.ShapeDtypeStruct((SSM_GROUPS, n_chunks, W), BF16),
        scratch_shapes=[pltpu.VMEM((n_rows, GS, NS), F32), pltpu.VMEM((n_rows, GS, NS), F32)]
        + [pltpu.VMEM((n_chunks, GS, NS), F32) for _ in range(4)],
        compiler_params=pltpu.CompilerParams(
            dimension_semantics=("arbitrary",), vmem_limit_bytes=VMEM_LIMIT),
        name="s5_chunked",
    )(u_chunks, pp, mm, qq, lam16)


def _mix_out_body(x_ref, att_ref, y_ref, u_ref, d_ref, wglu_ref, bglu_ref, sn_ref, wout_ref,
                  nf_ref, rwt_ref, sgu_ref, sd_ref, h_ref, hn_ref, sc_ref):
    y = y_ref[...].astype(F32) + d_ref[...] * u_ref[...].astype(F32)
    y = jax.nn.gelu(y)
    gate = jax.nn.sigmoid(jnp.dot(y.astype(BF16), wglu_ref[...], preferred_element_type=F32) + bglu_ref[...])
    s = y * gate
    ms = jnp.mean(s * s, axis=-1, keepdims=True)
    ssm = (s * lax.rsqrt(ms + EPS) * sn_ref[...]).astype(BF16)
    mixed = jnp.concatenate([att_ref[...], ssm], axis=1)
    h = x_ref[...] + jnp.dot(mixed, wout_ref[...], preferred_element_type=F32)
    ms = jnp.mean(h * h, axis=-1, keepdims=True)
    hn = h * lax.rsqrt(ms + EPS) * nf_ref[...]
    hnb = hn.astype(BF16)
    hn_ref[...] = hnb
    logits_t = lax.dot_general(rwt_ref[...], hn, (((1,), (1,)), ((), ())),
                               precision=lax.Precision.HIGHEST, preferred_element_type=F32)
    sc_ref[...] = jax.nn.sigmoid(logits_t)
    gu = jnp.dot(hnb, sgu_ref[...], preferred_element_type=F32)
    act = (jax.nn.silu(gu[:, :SHARED_HIDDEN]) * gu[:, SHARED_HIDDEN:]).astype(BF16)
    h_ref[...] = h + jnp.dot(act, sd_ref[...], preferred_element_type=F32)


def _mix_out(x2, att, y, u, d_skip, w_glu, b_glu, ssm_norm, w_out, norm_ffn, router_wt,
             sh_gate_up, sh_down):
    seq = x2.shape[0]
    R = MIX_ROW_BLOCK
    res = lambda shape: _resident(shape, lambda i: (0, 0))
    return pl.pallas_call(
        _mix_out_body,
        grid=(seq // R,),
        in_specs=[
            pl.BlockSpec((R, D_MODEL), lambda i: (i, 0)),
            pl.BlockSpec((R, ATT_WIDTH), lambda i: (i, 0)),
            pl.BlockSpec((R, SSM_WIDTH), lambda i: (i, 0)),
            pl.BlockSpec((R, SSM_WIDTH), lambda i: (i, 0)),
            res((1, SSM_WIDTH)), res((SSM_WIDTH, SSM_WIDTH)), res((1, SSM_WIDTH)), res((1, SSM_WIDTH)),
            res((D_MODEL, D_MODEL)), res((1, D_MODEL)), res((N_EXPERTS, D_MODEL)),
            res((D_MODEL, 2 * SHARED_HIDDEN)), res((SHARED_HIDDEN, D_MODEL)),
        ],
        out_specs=[
            pl.BlockSpec((R, D_MODEL), lambda i: (i, 0)),
            pl.BlockSpec((R, D_MODEL), lambda i: (i, 0)),
            pl.BlockSpec((N_EXPERTS, R), lambda i: (0, i)),
        ],
        out_shape=[
            jax.ShapeDtypeStruct((seq, D_MODEL), F32),
            jax.ShapeDtypeStruct((seq, D_MODEL), BF16),
            jax.ShapeDtypeStruct((N_EXPERTS, seq), F32),
        ],
        compiler_params=pltpu.CompilerParams(
            dimension_semantics=("arbitrary",), vmem_limit_bytes=VMEM_LIMIT),
        name="mix_out_shared",
    )(x2, att, y, u, d_skip, w_glu, b_glu, ssm_norm, w_out, norm_ffn, router_wt,
      sh_gate_up, sh_down)


def _route_body(sc_ref, rb_ref, tri_ref, w_ref, slot_ref, cnt_ref):
    scores = sc_ref[...]
    R = scores.shape[1]
    per_group = N_EXPERTS // N_EXPERT_GROUPS
    choice = scores + rb_ref[...]
    c3 = choice.reshape(N_EXPERT_GROUPS, per_group, R)
    within = lax.broadcasted_iota(jnp.int32, c3.shape, 1)
    m1 = jnp.max(c3, axis=1, keepdims=True)
    first = jnp.min(jnp.where(c3 == m1, within, per_group), axis=1, keepdims=True)
    m2 = jnp.max(jnp.where(within == first, -jnp.inf, c3), axis=1, keepdims=True)
    grp = (m1 + m2).reshape(N_EXPERT_GROUPS, R)
    gidx = lax.broadcasted_iota(jnp.int32, grp.shape, 0)
    grank = jnp.zeros(grp.shape, jnp.int32)
    for b in range(N_EXPERT_GROUPS):
        gb = grp[b:b + 1, :]
        grank += ((gb > grp) | ((gb == grp) & (b < gidx))).astype(jnp.int32)
    gmask = grank < TOPK_GROUPS
    emask = jnp.broadcast_to(gmask[:, None, :], c3.shape).reshape(N_EXPERTS, R)
    val = jnp.where(emask, choice, -jnp.inf)
    eidx = lax.broadcasted_iota(jnp.int32, val.shape, 0)
    rank = jnp.zeros(val.shape, jnp.int32)
    for e in range(N_EXPERTS):
        ve = val[e:e + 1, :]
        rank += ((ve > val) | ((ve == val) & (e < eidx))).astype(jnp.int32)
    sel = rank < TOP_K
    w = jnp.where(sel, scores, 0.0)
    w_ref[...] = w / jnp.sum(w, axis=0, keepdims=True) * ROUTED_SCALE
    cum = jnp.dot(jnp.where(sel, 1.0, 0.0).astype(BF16), tri_ref[...], preferred_element_type=F32)
    slot_ref[...] = jnp.where(sel, cum - 1.0, NOT_ROUTED)
    cnt_ref[0] = jnp.broadcast_to(cum[:, R - 1:R], (N_EXPERTS, LANES))


def _route(scores_t, router_bias, tri):
    seq = scores_t.shape[1]
    R = MOE_TOKEN_BLOCK
    n_blk = seq // R
    blk = pl.BlockSpec((N_EXPERTS, R), lambda i: (0, i))
    return pl.pallas_call(
        _route_body,
        grid=(n_blk,),
        in_specs=[blk, _resident((N_EXPERTS, 1), lambda i: (0, 0)), _resident((R, R), lambda i: (0, 0))],
        out_specs=[blk, blk, pl.BlockSpec((1, N_EXPERTS, LANES), lambda i: (i, 0, 0))],
        out_shape=[jax.ShapeDtypeStruct((N_EXPERTS, seq), F32),
                   jax.ShapeDtypeStruct((N_EXPERTS, seq), F32),
                   jax.ShapeDtypeStruct((n_blk, N_EXPERTS, LANES), F32)],
        compiler_params=pltpu.CompilerParams(dimension_semantics=("arbitrary",)),
        name="route",
    )(scores_t, router_bias, tri)


def _moe_tables(cnt):
    n_blk = cnt.shape[0]
    U, RB, SB = MOE_UNIT, MOE_BLOCK_ROWS, MOE_SLOT_BLOCK
    NU = SB // U
    E = N_EXPERTS
    pc = (cnt + U - 1) // U * U
    off = jnp.cumsum(pc, axis=1) - pc
    upc_t = (pc // U).T
    cum_t = jnp.cumsum(upc_t, axis=1)
    units_e = cum_t[:, -1]
    nblk_e = (units_e + NU - 1) // NU
    blk_end = jnp.cumsum(nblk_e)
    blk_start = blk_end - nblk_e
    n_act = blk_end[-1]
    max_blocks = n_blk * (RB // SB) + E
    i = jnp.arange(max_blocks, dtype=jnp.int32)
    active = i < n_act
    count_le = lambda edges, v: jnp.sum((edges <= v[..., None]).astype(jnp.int32), axis=-1)
    last_e = jnp.minimum(count_le(blk_end, n_act - 1), E - 1)
    be = jnp.where(active, jnp.minimum(count_le(blk_end[None, :], i), E - 1), last_e)
    oh_e = be[:, None] == jnp.arange(E, dtype=jnp.int32)[None, :]
    pick_e = lambda v: jnp.sum(jnp.where(oh_e, v[None, :], 0), axis=1)
    pick_e2 = lambda m: jnp.sum(jnp.where(oh_e[:, :, None], m[None, :, :], 0), axis=1)
    bstart_i = pick_e(blk_start)
    first = active & (i == bstart_i)
    local = (i - bstart_i)[:, None] * NU + jnp.arange(NU, dtype=jnp.int32)[None, :]
    valid = active[:, None] & (local < pick_e(units_e)[:, None])
    cum_i = pick_e2(cum_t)
    b_of = jnp.minimum(count_le(cum_i[:, None, :], local), n_blk - 1)
    oh_b = b_of[:, :, None] == jnp.arange(n_blk, dtype=jnp.int32)[None, None, :]
    pick_b = lambda m: jnp.sum(jnp.where(oh_b, m[:, None, :], 0), axis=2)
    seg_start = pick_b(cum_i) - pick_b(pick_e2(upc_t))
    unit = (b_of * RB + pick_b(pick_e2(off.T))) // U + (local - seg_start)
    spare = n_blk * RB // U
    src = jnp.where(valid, unit, spare)
    dst = jnp.where(valid, unit,
                    spare + (1 + i % 2)[:, None] * NU + jnp.arange(NU, dtype=jnp.int32)[None, :])
    short = active & (pick_e(units_e) - (i - bstart_i) * NU <= NU // 2)
    i32 = lambda a: a.astype(jnp.int32)
    flags = i32(first) + 2 * i32(short)
    return (off, pc, i32(src.reshape(-1)), i32(dst.reshape(-1)), i32(be), flags,
            i32(n_act.reshape(1)))


def _split_hi_lo(pos):
    hi = jnp.floor(pos * (1.0 / 64.0))
    return hi.astype(BF16), (pos - 64.0 * hi).astype(BF16)


def _dispatch_body(hn_ref, slot_ref, offc_ref, offl_ref, pcl_ref, xs_ref, *, n_blk):
    b = pl.program_id(0)
    RB, TB = MOE_BLOCK_ROWS, MOE_TOKEN_BLOCK

    @pl.when(b == n_blk)
    def _():
        xs_ref[...] = jnp.zeros(xs_ref.shape, BF16)

    @pl.when(b < n_blk)
    def _():
        pos = slot_ref[...] + offc_ref[0]
        hi, lo = _split_hi_lo(pos)
        r = lax.broadcasted_iota(jnp.int32, (RB, N_EXPERTS), 0).astype(F32)
        off = offl_ref[0]
        owner = jnp.where((r >= off) & (r < off + pcl_ref[0]), 1.0, 0.0).astype(BF16)
        p = (64.0 * jnp.dot(owner, hi, preferred_element_type=F32)
             + jnp.dot(owner, lo, preferred_element_type=F32))
        rr = lax.broadcasted_iota(jnp.int32, (RB, TB), 0).astype(F32)
        onehot = jnp.where(jnp.abs(p - rr) < 0.5, 1.0, 0.0).astype(BF16)
        x = hn_ref[...]
        C = MOE_SLOT_BLOCK
        for c in range(RB // C):
            xs_ref[0, c * C:(c + 1) * C, :] = jnp.dot(
                onehot[c * C:(c + 1) * C], x, preferred_element_type=F32).astype(BF16)


def _dispatch(hn, slots, off, pc):
    n_blk = off.shape[0]
    RB, TB = MOE_BLOCK_ROWS, MOE_TOKEN_BLOCK
    clamp = lambda b: jnp.minimum(b, n_blk - 1)
    return pl.pallas_call(
        functools.partial(_dispatch_body, n_blk=n_blk),
        grid=(n_blk + 1,),
        in_specs=[
            pl.BlockSpec((TB, D_MODEL), lambda b: (clamp(b), 0)),
            pl.BlockSpec((N_EXPERTS, TB), lambda b: (0, clamp(b))),
            pl.BlockSpec((1, N_EXPERTS, 1), lambda b: (clamp(b), 0, 0)),
            pl.BlockSpec((1, 1, N_EXPERTS), lambda b: (clamp(b), 0, 0)),
            pl.BlockSpec((1, 1, N_EXPERTS), lambda b: (clamp(b), 0, 0)),
        ],
        out_specs=pl.BlockSpec((1, RB, D_MODEL), lambda b: (b, 0, 0)),
        out_shape=jax.ShapeDtypeStruct((n_blk + 1, RB, D_MODEL), BF16),
        compiler_params=pltpu.CompilerParams(
            dimension_semantics=("arbitrary",), vmem_limit_bytes=VMEM_LIMIT),
        name="moe_dispatch",
    )(hn, slots, off[:, :, None], off[:, None, :], pc[:, None, :])


def _expert_body(src_ref, dst_ref, bexp_ref, first_ref, nact_ref,
                 xs_hbm, wg_ref, wu_ref, wd_ref, ys_hbm,
                 xbuf, ybuf, wgub, wdb, sem_in, sem_out):
    del bexp_ref
    i = pl.program_id(0)
    n_act = nact_ref[0]
    cur = lax.rem(i, 2)
    U = MOE_UNIT
    NU = MOE_SLOT_BLOCK // U

    def in_copy(blk, buf, u):
        return pltpu.make_async_copy(xs_hbm.at[src_ref[blk * NU + u]],
                                     xbuf.at[buf, pl.ds(u * U, U)], sem_in.at[buf])

    def out_copy(blk, buf, u):
        return pltpu.make_async_copy(ybuf.at[buf, pl.ds(u * U, U)],
                                     ys_hbm.at[dst_ref[blk * NU + u]], sem_out.at[buf])

    @pl.when(i == 0)
    def _():
        for u in range(NU):
            in_copy(0, 0, u).start()

    @pl.when(i < n_act)
    def _():
        @pl.when(i + 1 < n_act)
        def _():
            for u in range(NU):
                in_copy(i + 1, 1 - cur, u).start()

        @pl.when(first_ref[i] % 2 == 1)
        def _():
            wgub[:, :EXPERT_HIDDEN] = wg_ref[0].astype(BF16)
            wgub[:, EXPERT_HIDDEN:] = wu_ref[0].astype(BF16)
            wdb[...] = wd_ref[0].astype(BF16)

        for u in range(NU):
            in_copy(i, cur, u).wait()

        @pl.when(i >= 2)
        def _():
            for u in range(NU):
                out_copy(i - 2, cur, u).wait()

        def swiglu_rows(rows):
            x = xbuf[cur, :rows]
            gu = jnp.dot(x, wgub[...], preferred_element_type=F32)
            act = (jax.nn.silu(gu[:, :EXPERT_HIDDEN]) * gu[:, EXPERT_HIDDEN:]).astype(BF16)
            ybuf[cur, :rows] = jnp.dot(act, wdb[...], preferred_element_type=F32).astype(BF16)

        half = MOE_SLOT_BLOCK // 2

        @pl.when(first_ref[i] < 2)
        def _():
            swiglu_rows(MOE_SLOT_BLOCK)

        @pl.when(first_ref[i] >= 2)
        def _():
            swiglu_rows(half)
            ybuf[cur, half:] = jnp.zeros((half, D_MODEL), BF16)

        for u in range(NU):
            out_copy(i, cur, u).start()

        @pl.when(i == n_act - 1)
        def _():
            for u in range(NU):
                out_copy(i, cur, u).wait()

            @pl.when(i >= 1)
            def _():
                for u in range(NU):
                    out_copy(i - 1, 1 - cur, u).wait()


def _experts(xs, src, dst, bexp, first, n_act, wg, wu, wd):
    U, SB = MOE_UNIT, MOE_SLOT_BLOCK
    n_units = xs.shape[0] * xs.shape[1] // U
    max_blocks = bexp.shape[0]
    unit_view = lambda a: a.reshape(n_units, U, D_MODEL)
    wspec = lambda shape: pl.BlockSpec((1,) + shape, lambda i, src, dst, bexp, first, nact: (bexp[i], 0, 0))
    grid_spec = pltpu.PrefetchScalarGridSpec(
        num_scalar_prefetch=5,
        grid=(max_blocks,),
        in_specs=[
            pl.BlockSpec(memory_space=pl.ANY),
            wspec((D_MODEL, EXPERT_HIDDEN)), wspec((D_MODEL, EXPERT_HIDDEN)), wspec((EXPERT_HIDDEN, D_MODEL)),
        ],
        out_specs=pl.BlockSpec(memory_space=pl.ANY),
        scratch_shapes=[
            pltpu.VMEM((2, SB, D_MODEL), BF16), pltpu.VMEM((2, SB, D_MODEL), BF16),
            pltpu.VMEM((D_MODEL, 2 * EXPERT_HIDDEN), BF16), pltpu.VMEM((EXPERT_HIDDEN, D_MODEL), BF16),
            pltpu.SemaphoreType.DMA((2,)), pltpu.SemaphoreType.DMA((2,)),
        ],
    )
    ys = pl.pallas_call(
        _expert_body,
        grid_spec=grid_spec,
        out_shape=jax.ShapeDtypeStruct((n_units, U, D_MODEL), BF16),
        input_output_aliases={5: 0},
        compiler_params=pltpu.CompilerParams(
            dimension_semantics=("arbitrary",), vmem_limit_bytes=VMEM_LIMIT),
        name="moe_experts",
    )(src, dst, bexp, first, n_act, unit_view(xs), wg, wu, wd)
    return ys.reshape(xs.shape)


def _combine_body(h_ref, ys_ref, slot_ref, w_ref, offl_ref, offc_ref, pcc_ref, o_ref):
    RB, TB = MOE_BLOCK_ROWS, MOE_TOKEN_BLOCK
    pos = slot_ref[...] + offl_ref[0]
    hi, lo = _split_hi_lo(pos)
    r = lax.broadcasted_iota(jnp.int32, (N_EXPERTS, RB), 1).astype(F32)
    off = offc_ref[0]
    owner = jnp.where((r >= off) & (r < off + pcc_ref[0]), 1.0, 0.0).astype(BF16)
    p = (64.0 * jnp.dot(hi, owner, preferred_element_type=F32)
         + jnp.dot(lo, owner, preferred_element_type=F32))
    wr = jnp.dot(w_ref[...].astype(BF16), owner, preferred_element_type=F32)
    rr = lax.broadcasted_iota(jnp.int32, (TB, RB), 1).astype(F32)
    gather_w = jnp.where(jnp.abs(p - rr) < 0.5, wr, 0.0).astype(BF16)
    o_ref[...] = h_ref[...] + jnp.dot(gather_w, ys_ref[0], preferred_element_type=F32)


def _combine(h, ys, slots_tok, w_tok, off, pc):
    n_blk = off.shape[0]
    RB, TB = MOE_BLOCK_ROWS, MOE_TOKEN_BLOCK
    seq = h.shape[0]
    return pl.pallas_call(
        _combine_body,
        grid=(n_blk,),
        in_specs=[
            pl.BlockSpec((TB, D_MODEL), lambda b: (b, 0)),
            pl.BlockSpec((1, RB, D_MODEL), lambda b: (b, 0, 0)),
            pl.BlockSpec((TB, N_EXPERTS), lambda b: (b, 0)),
            pl.BlockSpec((TB, N_EXPERTS), lambda b: (b, 0)),
            pl.BlockSpec((1, 1, N_EXPERTS), lambda b: (b, 0, 0)),
            pl.BlockSpec((1, N_EXPERTS, 1), lambda b: (b, 0, 0)),
            pl.BlockSpec((1, N_EXPERTS, 1), lambda b: (b, 0, 0)),
        ],
        out_specs=pl.BlockSpec((TB, D_MODEL), lambda b: (b, 0)),
        out_shape=jax.ShapeDtypeStruct((seq, D_MODEL), F32),
        compiler_params=pltpu.CompilerParams(
            dimension_semantics=("arbitrary",), vmem_limit_bytes=VMEM_LIMIT),
        name="moe_combine",
    )(h, ys, slots_tok, w_tok, off[:, None, :], off[:, :, None], pc[:, :, None])


def kernel(x, meta_tokens, rel_bias, norm_mix, w_in, q_norm, k_norm, lam_q1, lam_k1, lam_q2, lam_k2, subln, ssm_a_re, ssm_a_im, ssm_log_step, ssm_b_re, ssm_b_im, ssm_c_re, ssm_c_im, ssm_d, w_glu, b_glu, ssm_norm, w_out, norm_ffn, router_w, router_bias, w_gate, w_up, w_down, shared_gate, shared_up, shared_down):
    batch, seq, d = x.shape
    assert batch == 1 and d == D_MODEL and seq % ROW_BLOCK == 0 and seq % ATT_BLOCK == 0
    assert norm_mix.shape[0] == 1, "single layer"
    x2 = x.reshape(seq, d)
    meta_pad = jnp.zeros((ROW_BLOCK, d), F32).at[:N_META].set(meta_tokens.astype(F32))
    seg = jnp.kron(jnp.eye(QK_WIDTH // HEAD_DIM, dtype=F32),
                   jnp.full((HEAD_DIM, HEAD_DIM), 1.0 / HEAD_DIM, F32)).astype(BF16)
    qg = jnp.tile(q_norm[0].astype(F32), QK_WIDTH // HEAD_DIM)[None] * (HEAD_DIM ** -0.5)
    kg = jnp.tile(k_norm[0].astype(F32), QK_WIDTH // HEAD_DIM)[None]

    proj, u = _inproj(x2, meta_pad, norm_mix[0][None], w_in[0].astype(BF16), seg, qg, kg)

    score_bound = (BOUND_MARGIN * HEAD_DIM ** 0.5 * jnp.max(jnp.abs(q_norm[0].astype(F32)))
                   * jnp.max(jnp.abs(k_norm[0].astype(F32)))
                   + jnp.max(jnp.abs(rel_bias.astype(F32)))).reshape(1)
    att = _attention(proj, rel_bias.astype(F32), score_bound, lam_q1[0][None], lam_k1[0][None],
                     lam_q2[0][None], lam_k2[0][None], subln[0][None], seq)

    n_rows = proj.shape[0] // S5_CHUNK
    n_chunks = seq // S5_CHUNK
    perm = _lane_regroup_matrix()
    u_chunks = _to_chunks(u.reshape(n_rows, S5_CHUNK, SSM_WIDTH), perm)
    pp, mm, qq, lam16 = _s5_prep(ssm_a_re[0].astype(F32), ssm_a_im[0].astype(F32),
                                 ssm_log_step[0].astype(F32), ssm_b_re[0].astype(F32),
                                 ssm_b_im[0].astype(F32), ssm_c_re[0].astype(F32),
                                 ssm_c_im[0].astype(F32))
    y_chunks = _s5(u_chunks, pp, mm, qq, lam16, n_chunks)
    y = _from_chunks(y_chunks, perm.T).reshape(seq, SSM_WIDTH)

    h, hn, scores_t = _mix_out(
        x2, att, y, u, ssm_d[0][None].astype(F32), w_glu[0].astype(BF16), b_glu[0][None].astype(F32),
        ssm_norm[0][None].astype(F32), w_out[0].astype(BF16), norm_ffn[0][None].astype(F32),
        router_w[0].astype(F32).T,
        jnp.concatenate([shared_gate[0], shared_up[0]], axis=1).astype(BF16), shared_down[0].astype(BF16))

    tb = MOE_TOKEN_BLOCK
    tri = (jnp.arange(tb)[:, None] <= jnp.arange(tb)[None, :]).astype(BF16)
    wts_t, slots_t, cnt = _route(scores_t, router_bias[0].astype(F32)[:, None], tri)
    off, pc, src, dst, bexp, first, n_act = _moe_tables(cnt[:, :, 0].astype(jnp.int32))
    off_f, pc_f = off.astype(F32), pc.astype(F32)
    xs = _dispatch(hn, slots_t, off_f, pc_f)
    ys = _experts(xs, src, dst, bexp, first, n_act, w_gate[0], w_up[0], w_down[0])
    out = _combine(h, ys, slots_t.T, wts_t.T, off_f, pc_f)
    return out.reshape(batch, seq, d)
```

```python
import functools
import math

import jax
import jax.numpy as jnp
from jax import lax
from jax.experimental import pallas as pl
from jax.experimental.pallas import tpu as pltpu

F32 = jnp.float32
BF16 = jnp.bfloat16

D_MODEL = 2048
N_META = 16
ATT_WIDTH = 1024
SSM_WIDTH = 1024
HEAD_DIM = 64
V_DIM = 128
HEADS = 8
QK_WIDTH = 1024
IN_WIDTH = 4096
V_OFFSET = 2 * QK_WIDTH
PROJ_WIDTH = V_OFFSET + 2 * ATT_WIDTH
SSM_GROUP = 16
SSM_GROUPS = 64
SSM_STATE = 64
N_BUCKETS = 32
MAX_DISTANCE = 128
N_EXPERTS = 64
TOP_K = 8
N_EXPERT_GROUPS = 8
TOPK_GROUPS = 4
EXPERT_HIDDEN = 512
SHARED_HIDDEN = 512
ROUTED_SCALE = 2.5
EPS = 1e-6
LAMBDA_INIT = 0.8 - 0.6 * math.exp(-0.3 * 0)

ROW_BLOCK = 512
ATT_BLOCK = 512
S5_CHUNK = 16
S5_GROUPS_PER_STEP = 8
S5_HALF_STEPS = 128 // SSM_GROUP
S5_HALF_WIDTH = S5_GROUPS_PER_STEP * S5_HALF_STEPS * SSM_GROUP
assert S5_GROUPS_PER_STEP * SSM_GROUP == 128 and S5_CHUNK % S5_HALF_STEPS == 0
MOE_TOKEN_BLOCK = 256
MOE_UNIT = 16
MOE_SLOT_BLOCK = 512
MOE_BLOCK_ROWS = -(-(MOE_TOKEN_BLOCK * TOP_K + N_EXPERTS * (MOE_UNIT - 1)) // MOE_SLOT_BLOCK) * MOE_SLOT_BLOCK
NOT_ROUTED = -1e6
MIX_ROW_BLOCK = 256
NEG_BIG = -1e30
MAX_EXP_RANGE = 80.0
BOUND_MARGIN = 1.02
LANES = 128
SUBLANES = 8
VMEM_LIMIT = 56 * 2 ** 20


def _resident(shape, index_map):
    return pl.BlockSpec(shape, index_map, pipeline_mode=pl.Buffered(1))


def _inproj_body(x_ref, meta_ref, g_ref, w_ref, seg_ref, qg_ref, kg_ref, o_ref, u_ref, *, n_xblk):
    i = pl.program_id(0)

    def run(src_ref):
        xv = src_ref[...]
        ms = jnp.mean(xv * xv, axis=-1, keepdims=True)
        hn = (xv * lax.rsqrt(ms + EPS) * g_ref[...]).astype(BF16)
        sec = IN_WIDTH // 4
        for s in range(4):
            ps = jnp.dot(hn, w_ref[:, s * sec:(s + 1) * sec], preferred_element_type=F32)
            if s < 2:
                gain = qg_ref if s == 0 else kg_ref
                msq = jnp.dot((ps * ps).astype(BF16), seg_ref[...], preferred_element_type=F32)
                ps = ps * lax.rsqrt(msq + EPS) * gain[...]
            if s == 3:
                u_ref[...] = ps
                continue
            pb = ps.astype(BF16)
            if s < 2:
                o_ref[:, s * sec:(s + 1) * sec] = pb
            else:
                lane = lax.broadcasted_iota(jnp.int32, (pb.shape[0], V_DIM), 1)
                ones_col = jnp.where(lane == 0, 1.0, 0.0).astype(BF16)
                for hh in range(HEADS):
                    base = V_OFFSET + hh * 2 * V_DIM
                    o_ref[:, base:base + V_DIM] = pb[:, hh * V_DIM:(hh + 1) * V_DIM]
                    o_ref[:, base + V_DIM:base + 2 * V_DIM] = ones_col

    @pl.when(i < n_xblk)
    def _():
        run(x_ref)

    @pl.when(i == n_xblk)
    def _():
        run(meta_ref)


def _inproj(x2, meta_pad, gain, w_bf, seg, qg, kg):
    seq = x2.shape[0]
    n_xblk = seq // ROW_BLOCK
    rows = seq + ROW_BLOCK
    return pl.pallas_call(
        functools.partial(_inproj_body, n_xblk=n_xblk),
        grid=(n_xblk + 1,),
        in_specs=[
            pl.BlockSpec((ROW_BLOCK, D_MODEL), lambda i: (jnp.minimum(i, n_xblk - 1), 0)),
            _resident((ROW_BLOCK, D_MODEL), lambda i: (0, 0)),
            _resident((1, D_MODEL), lambda i: (0, 0)),
            _resident((D_MODEL, IN_WIDTH), lambda i: (0, 0)),
            _resident((QK_WIDTH, QK_WIDTH), lambda i: (0, 0)),
            _resident((1, QK_WIDTH), lambda i: (0, 0)),
            _resident((1, QK_WIDTH), lambda i: (0, 0)),
        ],
        out_specs=[pl.BlockSpec((ROW_BLOCK, PROJ_WIDTH), lambda i: (i, 0)),
                   pl.BlockSpec((ROW_BLOCK, SSM_WIDTH), lambda i: (i, 0))],
        out_shape=[jax.ShapeDtypeStruct((rows, PROJ_WIDTH), BF16),
                   jax.ShapeDtypeStruct((rows, SSM_WIDTH), F32)],
        compiler_params=pltpu.CompilerParams(
            dimension_semantics=("arbitrary",), vmem_limit_bytes=VMEM_LIMIT),
        name="inproj",
    )(x2, meta_pad, gain, w_bf, seg, qg, kg)


def _t5_bias(rel, tab_ref, h):
    half = N_BUCKETS // 2
    exact = half // 2
    n = jnp.abs(rel)
    nf = jnp.maximum(n, 1).astype(F32)
    large = exact + (jnp.log(nf / exact) / math.log(MAX_DISTANCE / exact) * (half - exact)).astype(jnp.int32)
    large = jnp.minimum(large, half - 1)
    bucket = jnp.where(rel > 0, half, 0) + jnp.where(n < exact, n, large)
    out = jnp.zeros(rel.shape, F32)
    for b in range(N_BUCKETS):
        out = jnp.where(bucket == b, tab_ref[b, h], out)
    return out


def _attn_body(tab_ref, bound_ref, q_ref, k_ref, v_ref, lq1_ref, lk1_ref, lq2_ref, lk2_ref,
               subln_ref, o_ref, bias_ref, acc1_ref, acc2_ref, m1_ref, m2_ref, *, n_main):
    T = ATT_BLOCK
    h = pl.program_id(0)
    qi = pl.program_id(1)
    bound = bound_ref[0]

    @pl.when(qi == 0)
    def _():
        offsets = (-T, 0, T, -N_META, -N_META - T, -2 * T, 2 * T)
        for kind, off in enumerate(offsets):
            masked = kind in (3, 4)

            def rows(rc, carry, off=off, masked=masked, kind=kind):
                r0 = pl.multiple_of(rc * SUBLANES, SUBLANES)
                r = r0 + lax.broadcasted_iota(jnp.int32, (SUBLANES, T), 0)
                c = lax.broadcasted_iota(jnp.int32, (SUBLANES, T), 1)
                b = _t5_bias(off + c - r, tab_ref, h) - bound
                if masked:
                    b = jnp.where(c < N_META, b, NEG_BIG)
                bias_ref[kind, pl.ds(r0, SUBLANES), :] = b
                return carry

            lax.fori_loop(0, T // SUBLANES, rows, 0)

    acc1_ref[...] = jnp.zeros(acc1_ref.shape, F32)
    acc2_ref[...] = jnp.zeros(acc2_ref.shape, F32)

    q = q_ref[...]
    q1 = q[:, :HEAD_DIM]
    q2 = q[:, HEAD_DIM:]
    nt = (((1,), (1,)), ((), ()))

    def tile(ki):
        koff = pl.multiple_of(ki * T, T)
        kb = k_ref[pl.ds(koff, T), :]
        va = v_ref[pl.ds(koff, T), :]
        d = ki - qi
        kind = jnp.where(ki == n_main, jnp.where(qi == 0, 3, 4),
                         jnp.where(d <= -2, 5, jnp.where(d >= 2, 6, d + 1)))
        return kb, va, bias_ref[kind]

    def bounded_step(ki, carry):
        kb, va, bias = tile(ki)
        s1 = lax.dot_general(q1, kb[:, :HEAD_DIM], nt, preferred_element_type=F32) + bias
        acc1_ref[...] += jnp.dot(jnp.exp(s1).astype(BF16), va, preferred_element_type=F32)
        s2 = lax.dot_general(q2, kb[:, HEAD_DIM:], nt, preferred_element_type=F32) + bias
        acc2_ref[...] += jnp.dot(jnp.exp(s2).astype(BF16), va, preferred_element_type=F32)
        return carry

    def online_map(s, va, m_ref, acc_ref):
        m_old = m_ref[...]
        m_new = jnp.maximum(m_old, jnp.max(s, axis=-1, keepdims=True))
        p = jnp.exp(s - m_new).astype(BF16)
        acc_ref[...] = (jnp.exp(m_old - m_new) * acc_ref[...]
                        + jnp.dot(p, va, preferred_element_type=F32))
        m_ref[...] = m_new

    def online_step(ki, carry):
        kb, va, bias = tile(ki)
        s1 = lax.dot_general(q1, kb[:, :HEAD_DIM], nt, preferred_element_type=F32) + bias
        online_map(s1, va, m1_ref, acc1_ref)
        s2 = lax.dot_general(q2, kb[:, HEAD_DIM:], nt, preferred_element_type=F32) + bias
        online_map(s2, va, m2_ref, acc2_ref)
        return carry

    no_running_max = 2.0 * bound <= MAX_EXP_RANGE

    @pl.when(no_running_max)
    def _():
        unroll = next(u for u in (33, 11, 3, 2, 1) if (n_main + 1) % u == 0)
        lax.fori_loop(0, n_main + 1, bounded_step, 0, unroll=unroll)

    @pl.when(jnp.logical_not(no_running_max))
    def _():
        m1_ref[...] = jnp.full(m1_ref.shape, -jnp.inf, F32)
        m2_ref[...] = jnp.full(m2_ref.shape, -jnp.inf, F32)
        lax.fori_loop(0, n_main + 1, online_step, 0)

    lam = (jnp.exp(jnp.sum(lq1_ref[...] * lk1_ref[...], axis=-1, keepdims=True))
           - jnp.exp(jnp.sum(lq2_ref[...] * lk2_ref[...], axis=-1, keepdims=True))
           + LAMBDA_INIT)
    a1 = acc1_ref[...]
    a2 = acc2_ref[...]
    o = (a1[:, :V_DIM] / a1[:, V_DIM:V_DIM + 1]
         - lam * (a2[:, :V_DIM] / a2[:, V_DIM:V_DIM + 1]))
    ms = jnp.mean(o * o, axis=-1, keepdims=True)
    o = o * lax.rsqrt(ms + EPS) * subln_ref[...] * (1.0 - LAMBDA_INIT)
    o_ref[...] = o.astype(BF16)


def _attention(proj, rel_bias, score_bound, lq1, lk1, lq2, lk2, subln, seq):
    T = ATT_BLOCK
    n_main = seq // T
    rows = proj.shape[0]
    vec64 = lambda: _resident((1, HEAD_DIM), lambda h, qi: (0, 0))
    return pl.pallas_call(
        functools.partial(_attn_body, n_main=n_main),
        grid=(HEADS, n_main),
        in_specs=[
            pl.BlockSpec(memory_space=pltpu.SMEM),
            pl.BlockSpec(memory_space=pltpu.SMEM),
            pl.BlockSpec((T, 2 * HEAD_DIM), lambda h, qi: (qi, h)),
            pl.BlockSpec((rows, 2 * HEAD_DIM), lambda h, qi: (0, HEADS + h)),
            pl.BlockSpec((rows, 2 * V_DIM), lambda h, qi: (0, V_OFFSET // (2 * V_DIM) + h)),
            vec64(), vec64(), vec64(), vec64(),
            _resident((1, V_DIM), lambda h, qi: (0, 0)),
        ],
        out_specs=pl.BlockSpec((T, V_DIM), lambda h, qi: (qi, h)),
        out_shape=jax.ShapeDtypeStruct((seq, ATT_WIDTH), BF16),
        scratch_shapes=[
            pltpu.VMEM((7, T, T), F32),
            pltpu.VMEM((T, 2 * V_DIM), F32), pltpu.VMEM((T, 2 * V_DIM), F32),
            pltpu.VMEM((T, 1), F32), pltpu.VMEM((T, 1), F32),
        ],
        compiler_params=pltpu.CompilerParams(
            dimension_semantics=("arbitrary", "arbitrary"), vmem_limit_bytes=VMEM_LIMIT),
        name="diff_attention",
    )(rel_bias, score_bound, proj, proj, proj, lq1, lk1, lq2, lk2, subln)


def _lane_regroup_matrix():
    out_lane = jnp.arange(S5_HALF_WIDTH, dtype=jnp.int32)
    g8 = out_lane // LANES
    jj = (out_lane % LANES) // SSM_GROUP
    p = out_lane % SSM_GROUP
    in_lane = jj * LANES + g8 * SSM_GROUP + p
    return (jnp.arange(S5_HALF_WIDTH, dtype=jnp.int32)[:, None] == in_lane[None, :]).astype(BF16)


def _to_chunks_body(u_ref, perm_ref, o_ref):
    parts = []
    for half in range(S5_CHUNK // S5_HALF_STEPS):
        x = jnp.concatenate([u_ref[:, half * S5_HALF_STEPS + jj, :].astype(BF16)
                             for jj in range(S5_HALF_STEPS)], axis=1)
        parts.append(jnp.dot(x, perm_ref[...], preferred_element_type=F32))
    for g8 in range(S5_GROUPS_PER_STEP):
        o_ref[g8] = jnp.concatenate([r[:, g8 * LANES:(g8 + 1) * LANES] for r in parts], axis=1).astype(BF16)


def _to_chunks(u3, perm):
    n_rows = u3.shape[0]
    w = S5_CHUNK * SSM_GROUP
    return pl.pallas_call(
        _to_chunks_body,
        grid=(SSM_GROUPS // S5_GROUPS_PER_STEP,),
        in_specs=[pl.BlockSpec((n_rows, S5_CHUNK, LANES), lambda o: (0, 0, o)),
                  _resident((S5_HALF_WIDTH, S5_HALF_WIDTH), lambda o: (0, 0))],
        out_specs=pl.BlockSpec((S5_GROUPS_PER_STEP, n_rows, w), lambda o: (o, 0, 0)),
        out_shape=jax.ShapeDtypeStruct((SSM_GROUPS, n_rows, w), BF16),
        compiler_params=pltpu.CompilerParams(
            dimension_semantics=("arbitrary",), vmem_limit_bytes=VMEM_LIMIT),
        name="s5_to_chunks",
    )(u3, perm)


def _from_chunks_body(y_ref, perm_t_ref, o_ref):
    for half in range(S5_CHUNK // S5_HALF_STEPS):
        x = jnp.concatenate([y_ref[g8, :, half * LANES:(half + 1) * LANES]
                             for g8 in range(S5_GROUPS_PER_STEP)], axis=1)
        r = jnp.dot(x, perm_t_ref[...], preferred_element_type=F32)
        for jj in range(S5_HALF_STEPS):
            o_ref[:, half * S5_HALF_STEPS + jj, :] = r[:, jj * LANES:(jj + 1) * LANES]


def _from_chunks(y_chunks, perm_t):
    n_chunks = y_chunks.shape[1]
    w = S5_CHUNK * SSM_GROUP
    return pl.pallas_call(
        _from_chunks_body,
        grid=(SSM_GROUPS // S5_GROUPS_PER_STEP,),
        in_specs=[pl.BlockSpec((S5_GROUPS_PER_STEP, n_chunks, w), lambda o: (o, 0, 0)),
                  _resident((S5_HALF_WIDTH, S5_HALF_WIDTH), lambda o: (0, 0))],
        out_specs=pl.BlockSpec((n_chunks, S5_CHUNK, LANES), lambda o: (0, 0, o)),
        out_shape=jax.ShapeDtypeStruct((n_chunks, S5_CHUNK, SSM_WIDTH), F32),
        compiler_params=pltpu.CompilerParams(
            dimension_semantics=("arbitrary",), vmem_limit_bytes=VMEM_LIMIT),
        name="s5_from_chunks",
    )(y_chunks, perm_t)


def _s5_prep(a_re, a_im, log_step, b_re, b_im, c_re, c_im):
    C = S5_CHUNK
    dt = jnp.exp(log_step)[..., None]
    decay = jnp.exp(a_re * dt)
    ab_re = decay * jnp.cos(a_im * dt)
    ab_im = decay * jnp.sin(a_im * dt)
    den = a_re * a_re + a_im * a_im
    zr = ab_re - 1.0
    f_re = (zr * a_re + ab_im * a_im) / den
    f_im = (ab_im * a_re - zr * a_im) / den
    bb_re = f_re[..., None] * b_re - f_im[..., None] * b_im
    bb_im = f_re[..., None] * b_im + f_im[..., None] * b_re
    pr, pi = jnp.ones_like(ab_re), jnp.zeros_like(ab_re)
    pw_re, pw_im = [pr], [pi]
    for _ in range(C):
        pr, pi = pr * ab_re - pi * ab_im, pr * ab_im + pi * ab_re
        pw_re.append(pr)
        pw_im.append(pi)
    G, P, W = SSM_GROUPS, SSM_GROUP, C * SSM_GROUP
    pw_re = jnp.stack(pw_re, axis=-1)
    pw_im = jnp.stack(pw_im, axis=-1)
    ct_re = c_re.transpose(0, 1, 3, 2)
    ct_im = c_im.transpose(0, 1, 3, 2)
    cp_re = ct_re[:, :, :, None, :] * pw_re[..., None] - ct_im[:, :, :, None, :] * pw_im[..., None]
    cp_im = ct_re[:, :, :, None, :] * pw_im[..., None] + ct_im[:, :, :, None, :] * pw_re[..., None]
    bt_re = bb_re.transpose(0, 1, 3, 2)
    bt_im = bb_im.transpose(0, 1, 3, 2)

    def response(d):
        prod = (cp_re[d][:, None, :, :C, :] * bt_re[d][:, :, :, None, None]
                - cp_im[d][:, None, :, :C, :] * bt_im[d][:, :, :, None, None])
        return jnp.sum(prod, axis=2)

    ext_f = jnp.pad(response(0).reshape(G, P, W), ((0, 0), (0, 0), (W, 0)))
    ext_r = jnp.pad(response(1)[:, :, ::-1, :].reshape(G, P, W), ((0, 0), (0, 0), (0, W)))
    mm = jnp.stack([ext_f[:, :, W - P * j:2 * W - P * j] + ext_r[:, :, (C - 1 - j) * P:(C - 1 - j) * P + W]
                    for j in range(C)], axis=1).reshape(G, W, W)

    def in_mat(d, reverse_powers):
        pr_ = pw_re[d][:, :, :C].transpose(0, 2, 1)
        pi_ = pw_im[d][:, :, :C].transpose(0, 2, 1)
        if reverse_powers:
            pr_, pi_ = pr_[:, ::-1], pi_[:, ::-1]
        re = pr_[:, :, None, :] * bt_re[d][:, None] - pi_[:, :, None, :] * bt_im[d][:, None]
        im = pr_[:, :, None, :] * bt_im[d][:, None] + pi_[:, :, None, :] * bt_re[d][:, None]
        return re, im

    pf_re, pf_im = in_mat(0, True)
    pr_re, pr_im = in_mat(1, False)
    pp = jnp.concatenate([pf_re, pr_re, pf_im, pr_im], axis=-1).reshape(G, W, 4 * SSM_STATE)

    qq = jnp.concatenate([cp_re[0][:, :, 1:], cp_re[1][:, :, :0:-1],
                          -cp_im[0][:, :, 1:], -cp_im[1][:, :, :0:-1]], axis=1).reshape(G, 4 * SSM_STATE, W)
    lam16 = jnp.stack([jnp.concatenate([pw_re[0][:, :, C], pw_re[1][:, :, C]], axis=-1),
                       jnp.concatenate([pw_im[0][:, :, C], pw_im[1][:, :, C]], axis=-1)], axis=0)
    return pp.astype(BF16), mm.astype(BF16), qq.astype(BF16), lam16


def _s5_body(u_ref, pp_ref, mm_ref, qq_ref, lam_ref, y_ref,
             zre_ref, zim_ref, are_ref, aim_ref, bre_ref, bim_ref, *, n_chunks):
    GS = S5_GROUPS_PER_STEP
    NS = 2 * SSM_STATE
    for gi in range(GS):
        z = jnp.dot(u_ref[gi], pp_ref[gi], preferred_element_type=F32)
        zre_ref[:, gi, :] = z[:, :NS]
        zim_ref[:, gi, :] = z[:, NS:]

    ar = lam_ref[0]
    ai = lam_ref[1]
    fwd = lax.broadcasted_iota(jnp.int32, (GS, NS), 1) < SSM_STATE
    sre0 = jnp.where(fwd, zre_ref[n_chunks], 0.0)
    sim0 = jnp.where(fwd, zim_ref[n_chunks], 0.0)

    def scan_step(k, carry):
        sre, sim = carry
        kr = n_chunks - 1 - k
        are_ref[k] = sre
        aim_ref[k] = sim
        bre_ref[kr] = sre
        bim_ref[kr] = sim
        zr = jnp.where(fwd, zre_ref[k], zre_ref[kr])
        zi = jnp.where(fwd, zim_ref[k], zim_ref[kr])
        return ar * sre - ai * sim + zr, ar * sim + ai * sre + zi

    lax.fori_loop(0, n_chunks, scan_step, (sre0, sim0))

    fwd_rows = lax.broadcasted_iota(jnp.int32, (n_chunks, NS), 1) < SSM_STATE
    for gi in range(GS):
        s_re = jnp.where(fwd_rows, are_ref[:, gi, :], bre_ref[:, gi, :])
        s_im = jnp.where(fwd_rows, aim_ref[:, gi, :], bim_ref[:, gi, :])
        scat = jnp.concatenate([s_re, s_im], axis=1).astype(BF16)
        y = (jnp.dot(u_ref[gi, :n_chunks, :], mm_ref[gi], preferred_element_type=F32)
             + jnp.dot(scat, qq_ref[gi], preferred_element_type=F32))
        y_ref[gi] = y.astype(BF16)


def _s5(u_chunks, pp, mm, qq, lam16, n_chunks):
    GS = S5_GROUPS_PER_STEP
    n_rows = u_chunks.shape[1]
    W = S5_CHUNK * SSM_GROUP
    NS = 2 * SSM_STATE
    mat = lambda: pl.BlockSpec((GS, W, W), lambda g: (g, 0, 0))
    return pl.pallas_call(
        functools.partial(_s5_body, n_chunks=n_chunks),
        grid=(SSM_GROUPS // GS,),
        in_specs=[
            pl.BlockSpec((GS, n_rows, W), lambda g: (g, 0, 0)),
            mat(), mat(), mat(),
            pl.BlockSpec((2, GS, NS), lambda g: (0, g, 0)),
        ],
        out_specs=pl.BlockSpec((GS, n_chunks, W), lambda g: (g, 0, 0)),
        out_shape=jax.ShapeDtypeStruct((SSM_GROUPS, n_chunks, W), BF16),
        scratch_shapes=[pltpu.VMEM((n_rows, GS, NS), F32), pltpu.VMEM((n_rows, GS, NS), F32)]
        + [pltpu.VMEM((n_chunks, GS, NS), F32) for _ in range(4)],
        compiler_params=pltpu.CompilerParams(
            dimension_semantics=("arbitrary",), vmem_limit_bytes=VMEM_LIMIT),
        name="s5_chunked",
    )(u_chunks, pp, mm, qq, lam16)


def _mix_out_body(x_ref, att_ref, y_ref, u_ref, d_ref, wglu_ref, bglu_ref, sn_ref, wout_ref,
                  nf_ref, rwt_ref, sgu_ref, sd_ref, h_ref, hn_ref, sc_ref):
    y = y_ref[...].astype(F32) + d_ref[...] * u_ref[...].astype(F32)
    y = jax.nn.gelu(y)
    gate = jax.nn.sigmoid(jnp.dot(y.astype(BF16), wglu_ref[...], preferred_element_type=F32) + bglu_ref[...])
    s = y * gate
    ms = jnp.mean(s * s, axis=-1, keepdims=True)
    ssm = (s * lax.rsqrt(ms + EPS) * sn_ref[...]).astype(BF16)
    mixed = jnp.concatenate([att_ref[...], ssm], axis=1)
    h = x_ref[...] + jnp.dot(mixed, wout_ref[...], preferred_element_type=F32)
    ms = jnp.mean(h * h, axis=-1, keepdims=True)
    hn = h * lax.rsqrt(ms + EPS) * nf_ref[...]
    hnb = hn.astype(BF16)
    hn_ref[...] = hnb
    logits_t = lax.dot_general(rwt_ref[...], hn, (((1,), (1,)), ((), ())),
                               precision=lax.Precision.HIGHEST, preferred_element_type=F32)
    sc_ref[...] = jax.nn.sigmoid(logits_t)
    gu = jnp.dot(hnb, sgu_ref[...], preferred_element_type=F32)
    act = (jax.nn.silu(gu[:, :SHARED_HIDDEN]) * gu[:, SHARED_HIDDEN:]).astype(BF16)
    h_ref[...] = h + jnp.dot(act, sd_ref[...], preferred_element_type=F32)


def _mix_out(x2, att, y, u, d_skip, w_glu, b_glu, ssm_norm, w_out, norm_ffn, router_wt,
             sh_gate_up, sh_down):
    seq = x2.shape[0]
    R = MIX_ROW_BLOCK
    res = lambda shape: _resident(shape, lambda i: (0, 0))
    return pl.pallas_call(
        _mix_out_body,
        grid=(seq // R,),
        in_specs=[
            pl.BlockSpec((R, D_MODEL), lambda i: (i, 0)),
            pl.BlockSpec((R, ATT_WIDTH), lambda i: (i, 0)),
            pl.BlockSpec((R, SSM_WIDTH), lambda i: (i, 0)),
            pl.BlockSpec((R, SSM_WIDTH), lambda i: (i, 0)),
            res((1, SSM_WIDTH)), res((SSM_WIDTH, SSM_WIDTH)), res((1, SSM_WIDTH)), res((1, SSM_WIDTH)),
            res((D_MODEL, D_MODEL)), res((1, D_MODEL)), res((N_EXPERTS, D_MODEL)),
            res((D_MODEL, 2 * SHARED_HIDDEN)), res((SHARED_HIDDEN, D_MODEL)),
        ],
        out_specs=[
            pl.BlockSpec((R, D_MODEL), lambda i: (i, 0)),
            pl.BlockSpec((R, D_MODEL), lambda i: (i, 0)),
            pl.BlockSpec((N_EXPERTS, R), lambda i: (0, i)),
        ],
        out_shape=[
            jax.ShapeDtypeStruct((seq, D_MODEL), F32),
            jax.ShapeDtypeStruct((seq, D_MODEL), BF16),
            jax.ShapeDtypeStruct((N_EXPERTS, seq), F32),
        ],
        compiler_params=pltpu.CompilerParams(
            dimension_semantics=("arbitrary",), vmem_limit_bytes=VMEM_LIMIT),
        name="mix_out_shared",
    )(x2, att, y, u, d_skip, w_glu, b_glu, ssm_norm, w_out, norm_ffn, router_wt,
      sh_gate_up, sh_down)


def _route_body(sc_ref, rb_ref, tri_ref, w_ref, slot_ref, cnt_ref):
    scores = sc_ref[...]
    R = scores.shape[1]
    per_group = N_EXPERTS // N_EXPERT_GROUPS
    choice = scores + rb_ref[...]
    c3 = choice.reshape(N_EXPERT_GROUPS, per_group, R)
    within = lax.broadcasted_iota(jnp.int32, c3.shape, 1)
    m1 = jnp.max(c3, axis=1, keepdims=True)
    first = jnp.min(jnp.where(c3 == m1, within, per_group), axis=1, keepdims=True)
    m2 = jnp.max(jnp.where(within == first, -jnp.inf, c3), axis=1, keepdims=True)
    grp = (m1 + m2).reshape(N_EXPERT_GROUPS, R)
    gidx = lax.broadcasted_iota(jnp.int32, grp.shape, 0)
    grank = jnp.zeros(grp.shape, jnp.int32)
    for b in range(N_EXPERT_GROUPS):
        gb = grp[b:b + 1, :]
        grank += ((gb > grp) | ((gb == grp) & (b < gidx))).astype(jnp.int32)
    gmask = grank < TOPK_GROUPS
    emask = jnp.broadcast_to(gmask[:, None, :], c3.shape).reshape(N_EXPERTS, R)
    val = jnp.where(emask, choice, -jnp.inf)
    eidx = lax.broadcasted_iota(jnp.int32, val.shape, 0)
    rank = jnp.zeros(val.shape, jnp.int32)
    for e in range(N_EXPERTS):
        ve = val[e:e + 1, :]
        rank += ((ve > val) | ((ve == val) & (e < eidx))).astype(jnp.int32)
    sel = rank < TOP_K
    w = jnp.where(sel, scores, 0.0)
    w_ref[...] = w / jnp.sum(w, axis=0, keepdims=True) * ROUTED_SCALE
    cum = jnp.dot(jnp.where(sel, 1.0, 0.0).astype(BF16), tri_ref[...], preferred_element_type=F32)
    slot_ref[...] = jnp.where(sel, cum - 1.0, NOT_ROUTED)
    cnt_ref[0] = jnp.broadcast_to(cum[:, R - 1:R], (N_EXPERTS, LANES))


def _route(scores_t, router_bias, tri):
    seq = scores_t.shape[1]
    R = MOE_TOKEN_BLOCK
    n_blk = seq // R
    blk = pl.BlockSpec((N_EXPERTS, R), lambda i: (0, i))
    return pl.pallas_call(
        _route_body,
        grid=(n_blk,),
        in_specs=[blk, _resident((N_EXPERTS, 1), lambda i: (0, 0)), _resident((R, R), lambda i: (0, 0))],
        out_specs=[blk, blk, pl.BlockSpec((1, N_EXPERTS, LANES), lambda i: (i, 0, 0))],
        out_shape=[jax.ShapeDtypeStruct((N_EXPERTS, seq), F32),
                   jax.ShapeDtypeStruct((N_EXPERTS, seq), F32),
                   jax.ShapeDtypeStruct((n_blk, N_EXPERTS, LANES), F32)],
        compiler_params=pltpu.CompilerParams(dimension_semantics=("arbitrary",)),
        name="route",
    )(scores_t, router_bias, tri)


def _moe_tables(cnt):
    n_blk = cnt.shape[0]
    U, RB, SB = MOE_UNIT, MOE_BLOCK_ROWS, MOE_SLOT_BLOCK
    NU = SB // U
    E = N_EXPERTS
    pc = (cnt + U - 1) // U * U
    off = jnp.cumsum(pc, axis=1) - pc
    upc_t = (pc // U).T
    cum_t = jnp.cumsum(upc_t, axis=1)
    units_e = cum_t[:, -1]
    nblk_e = (units_e + NU - 1) // NU
    blk_end = jnp.cumsum(nblk_e)
    blk_start = blk_end - nblk_e
    n_act = blk_end[-1]
    max_blocks = n_blk * (RB // SB) + E
    i = jnp.arange(max_blocks, dtype=jnp.int32)
    active = i < n_act
    count_le = lambda edges, v: jnp.sum((edges <= v[..., None]).astype(jnp.int32), axis=-1)
    last_e = jnp.minimum(count_le(blk_end, n_act - 1), E - 1)
    be = jnp.where(active, jnp.minimum(count_le(blk_end[None, :], i), E - 1), last_e)
    oh_e = be[:, None] == jnp.arange(E, dtype=jnp.int32)[None, :]
    pick_e = lambda v: jnp.sum(jnp.where(oh_e, v[None, :], 0), axis=1)
    pick_e2 = lambda m: jnp.sum(jnp.where(oh_e[:, :, None], m[None, :, :], 0), axis=1)
    bstart_i = pick_e(blk_start)
    first = active & (i == bstart_i)
    local = (i - bstart_i)[:, None] * NU + jnp.arange(NU, dtype=jnp.int32)[None, :]
    valid = active[:, None] & (local < pick_e(units_e)[:, None])
    cum_i = pick_e2(cum_t)
    b_of = jnp.minimum(count_le(cum_i[:, None, :], local), n_blk - 1)
    oh_b = b_of[:, :, None] == jnp.arange(n_blk, dtype=jnp.int32)[None, None, :]
    pick_b = lambda m: jnp.sum(jnp.where(oh_b, m[:, None, :], 0), axis=2)
    seg_start = pick_b(cum_i) - pick_b(pick_e2(upc_t))
    unit = (b_of * RB + pick_b(pick_e2(off.T))) // U + (local - seg_start)
    spare = n_blk * RB // U
    src = jnp.where(valid, unit, spare)
    dst = jnp.where(valid, unit,
                    spare + (1 + i % 2)[:, None] * NU + jnp.arange(NU, dtype=jnp.int32)[None, :])
    short = active & (pick_e(units_e) - (i - bstart_i) * NU <= NU // 2)
    i32 = lambda a: a.astype(jnp.int32)
    flags = i32(first) + 2 * i32(short)
    return (off, pc, i32(src.reshape(-1)), i32(dst.reshape(-1)), i32(be), flags,
            i32(n_act.reshape(1)))


def _split_hi_lo(pos):
    hi = jnp.floor(pos * (1.0 / 64.0))
    return hi.astype(BF16), (pos - 64.0 * hi).astype(BF16)


def _dispatch_body(hn_ref, slot_ref, offc_ref, offl_ref, pcl_ref, xs_ref, *, n_blk):
    b = pl.program_id(0)
    RB, TB = MOE_BLOCK_ROWS, MOE_TOKEN_BLOCK

    @pl.when(b == n_blk)
    def _():
        xs_ref[...] = jnp.zeros(xs_ref.shape, BF16)

    @pl.when(b < n_blk)
    def _():
        pos = slot_ref[...] + offc_ref[0]
        hi, lo = _split_hi_lo(pos)
        r = lax.broadcasted_iota(jnp.int32, (RB, N_EXPERTS), 0).astype(F32)
        off = offl_ref[0]
        owner = jnp.where((r >= off) & (r < off + pcl_ref[0]), 1.0, 0.0).astype(BF16)
        p = (64.0 * jnp.dot(owner, hi, preferred_element_type=F32)
             + jnp.dot(owner, lo, preferred_element_type=F32))
        rr = lax.broadcasted_iota(jnp.int32, (RB, TB), 0).astype(F32)
        onehot = jnp.where(jnp.abs(p - rr) < 0.5, 1.0, 0.0).astype(BF16)
        x = hn_ref[...]
        C = MOE_SLOT_BLOCK
        for c in range(RB // C):
            xs_ref[0, c * C:(c + 1) * C, :] = jnp.dot(
                onehot[c * C:(c + 1) * C], x, preferred_element_type=F32).astype(BF16)


def _dispatch(hn, slots, off, pc):
    n_blk = off.shape[0]
    RB, TB = MOE_BLOCK_ROWS, MOE_TOKEN_BLOCK
    clamp = lambda b: jnp.minimum(b, n_blk - 1)
    return pl.pallas_call(
        functools.partial(_dispatch_body, n_blk=n_blk),
        grid=(n_blk + 1,),
        in_specs=[
            pl.BlockSpec((TB, D_MODEL), lambda b: (clamp(b), 0)),
            pl.BlockSpec((N_EXPERTS, TB), lambda b: (0, clamp(b))),
            pl.BlockSpec((1, N_EXPERTS, 1), lambda b: (clamp(b), 0, 0)),
            pl.BlockSpec((1, 1, N_EXPERTS), lambda b: (clamp(b), 0, 0)),
            pl.BlockSpec((1, 1, N_EXPERTS), lambda b: (clamp(b), 0, 0)),
        ],
        out_specs=pl.BlockSpec((1, RB, D_MODEL), lambda b: (b, 0, 0)),
        out_shape=jax.ShapeDtypeStruct((n_blk + 1, RB, D_MODEL), BF16),
        compiler_params=pltpu.CompilerParams(
            dimension_semantics=("arbitrary",), vmem_limit_bytes=VMEM_LIMIT),
        name="moe_dispatch",
    )(hn, slots, off[:, :, None], off[:, None, :], pc[:, None, :])


def _expert_body(src_ref, dst_ref, bexp_ref, first_ref, nact_ref,
                 xs_hbm, wg_ref, wu_ref, wd_ref, ys_hbm,
                 xbuf, ybuf, wgub, wdb, sem_in, sem_out):
    del bexp_ref
    i = pl.program_id(0)
    n_act = nact_ref[0]
    cur = lax.rem(i, 2)
    U = MOE_UNIT
    NU = MOE_SLOT_BLOCK // U

    def in_copy(blk, buf, u):
        return pltpu.make_async_copy(xs_hbm.at[src_ref[blk * NU + u]],
                                     xbuf.at[buf, pl.ds(u * U, U)], sem_in.at[buf])

    def out_copy(blk, buf, u):
        return pltpu.make_async_copy(ybuf.at[buf, pl.ds(u * U, U)],
                                     ys_hbm.at[dst_ref[blk * NU + u]], sem_out.at[buf])

    @pl.when(i == 0)
    def _():
        for u in range(NU):
            in_copy(0, 0, u).start()

    @pl.when(i < n_act)
    def _():
        @pl.when(i + 1 < n_act)
        def _():
            for u in range(NU):
                in_copy(i + 1, 1 - cur, u).start()

        @pl.when(first_ref[i] % 2 == 1)
        def _():
            wgub[:, :EXPERT_HIDDEN] = wg_ref[0].astype(BF16)
            wgub[:, EXPERT_HIDDEN:] = wu_ref[0].astype(BF16)
            wdb[...] = wd_ref[0].astype(BF16)

        for u in range(NU):
            in_copy(i, cur, u).wait()

        @pl.when(i >= 2)
        def _():
            for u in range(NU):
                out_copy(i - 2, cur, u).wait()

        def swiglu_rows(rows):
            x = xbuf[cur, :rows]
            gu = jnp.dot(x, wgub[...], preferred_element_type=F32)
            act = (jax.nn.silu(gu[:, :EXPERT_HIDDEN]) * gu[:, EXPERT_HIDDEN:]).astype(BF16)
            ybuf[cur, :rows] = jnp.dot(act, wdb[...], preferred_element_type=F32).astype(BF16)

        half = MOE_SLOT_BLOCK // 2

        @pl.when(first_ref[i] < 2)
        def _():
            swiglu_rows(MOE_SLOT_BLOCK)

        @pl.when(first_ref[i] >= 2)
        def _():
            swiglu_rows(half)
            ybuf[cur, half:] = jnp.zeros((half, D_MODEL), BF16)

        for u in range(NU):
            out_copy(i, cur, u).start()

        @pl.when(i == n_act - 1)
        def _():
            for u in range(NU):
                out_copy(i, cur, u).wait()

            @pl.when(i >= 1)
            def _():
                for u in range(NU):
                    out_copy(i - 1, 1 - cur, u).wait()


def _experts(xs, src, dst, bexp, first, n_act, wg, wu, wd):
    U, SB = MOE_UNIT, MOE_SLOT_BLOCK
    n_units = xs.shape[0] * xs.shape[1] // U
    max_blocks = bexp.shape[0]
    unit_view = lambda a: a.reshape(n_units, U, D_MODEL)
    wspec = lambda shape: pl.BlockSpec((1,) + shape, lambda i, src, dst, bexp, first, nact: (bexp[i], 0, 0))
    grid_spec = pltpu.PrefetchScalarGridSpec(
        num_scalar_prefetch=5,
        grid=(max_blocks,),
        in_specs=[
            pl.BlockSpec(memory_space=pl.ANY),
            wspec((D_MODEL, EXPERT_HIDDEN)), wspec((D_MODEL, EXPERT_HIDDEN)), wspec((EXPERT_HIDDEN, D_MODEL)),
        ],
        out_specs=pl.BlockSpec(memory_space=pl.ANY),
        scratch_shapes=[
            pltpu.VMEM((2, SB, D_MODEL), BF16), pltpu.VMEM((2, SB, D_MODEL), BF16),
            pltpu.VMEM((D_MODEL, 2 * EXPERT_HIDDEN), BF16), pltpu.VMEM((EXPERT_HIDDEN, D_MODEL), BF16),
            pltpu.SemaphoreType.DMA((2,)), pltpu.SemaphoreType.DMA((2,)),
        ],
    )
    ys = pl.pallas_call(
        _expert_body,
        grid_spec=grid_spec,
        out_shape=jax.ShapeDtypeStruct((n_units, U, D_MODEL), BF16),
        input_output_aliases={5: 0},
        compiler_params=pltpu.CompilerParams(
            dimension_semantics=("arbitrary",), vmem_limit_bytes=VMEM_LIMIT),
        name="moe_experts",
    )(src, dst, bexp, first, n_act, unit_view(xs), wg, wu, wd)
    return ys.reshape(xs.shape)


def _combine_body(h_ref, ys_ref, slot_ref, w_ref, offl_ref, offc_ref, pcc_ref, o_ref):
    RB, TB = MOE_BLOCK_ROWS, MOE_TOKEN_BLOCK
    pos = slot_ref[...] + offl_ref[0]
    hi, lo = _split_hi_lo(pos)
    r = lax.broadcasted_iota(jnp.int32, (N_EXPERTS, RB), 1).astype(F32)
    off = offc_ref[0]
    owner = jnp.where((r >= off) & (r < off + pcc_ref[0]), 1.0, 0.0).astype(BF16)
    p = (64.0 * jnp.dot(hi, owner, preferred_element_type=F32)
         + jnp.dot(lo, owner, preferred_element_type=F32))
    wr = jnp.dot(w_ref[...].astype(BF16), owner, preferred_element_type=F32)
    rr = lax.broadcasted_iota(jnp.int32, (TB, RB), 1).astype(F32)
    gather_w = jnp.where(jnp.abs(p - rr) < 0.5, wr, 0.0).astype(BF16)
    o_ref[...] = h_ref[...] + jnp.dot(gather_w, ys_ref[0], preferred_element_type=F32)


def _combine(h, ys, slots_tok, w_tok, off, pc):
    n_blk = off.shape[0]
    RB, TB = MOE_BLOCK_ROWS, MOE_TOKEN_BLOCK
    seq = h.shape[0]
    return pl.pallas_call(
        _combine_body,
        grid=(n_blk,),
        in_specs=[
            pl.BlockSpec((TB, D_MODEL), lambda b: (b, 0)),
            pl.BlockSpec((1, RB, D_MODEL), lambda b: (b, 0, 0)),
            pl.BlockSpec((TB, N_EXPERTS), lambda b: (b, 0)),
            pl.BlockSpec((TB, N_EXPERTS), lambda b: (b, 0)),
            pl.BlockSpec((1, 1, N_EXPERTS), lambda b: (b, 0, 0)),
            pl.BlockSpec((1, N_EXPERTS, 1), lambda b: (b, 0, 0)),
            pl.BlockSpec((1, N_EXPERTS, 1), lambda b: (b, 0, 0)),
        ],
        out_specs=pl.BlockSpec((TB, D_MODEL), lambda b: (b, 0)),
        out_shape=jax.ShapeDtypeStruct((seq, D_MODEL), F32),
        compiler_params=pltpu.CompilerParams(
            dimension_semantics=("arbitrary",), vmem_limit_bytes=VMEM_LIMIT),
        name="moe_combine",
    )(h, ys, slots_tok, w_tok, off[:, None, :], off[:, :, None], pc[:, :, None])


def kernel(x, meta_tokens, rel_bias, norm_mix, w_in, q_norm, k_norm, lam_q1, lam_k1, lam_q2, lam_k2, subln, ssm_a_re, ssm_a_im, ssm_log_step, ssm_b_re, ssm_b_im, ssm_c_re, ssm_c_im, ssm_d, w_glu, b_glu, ssm_norm, w_out, norm_ffn, router_w, router_bias, w_gate, w_up, w_down, shared_gate, shared_up, shared_down):
    batch, seq, d = x.shape
    assert batch == 1 and d == D_MODEL and seq % ROW_BLOCK == 0 and seq % ATT_BLOCK == 0
    assert norm_mix.shape[0] == 1, "single layer"
    x2 = x.reshape(seq, d)
    meta_pad = jnp.zeros((ROW_BLOCK, d), F32).at[:N_META].set(meta_tokens.astype(F32))
    seg = jnp.kron(jnp.eye(QK_WIDTH // HEAD_DIM, dtype=F32),
                   jnp.full((HEAD_DIM, HEAD_DIM), 1.0 / HEAD_DIM, F32)).astype(BF16)
    qg = jnp.tile(q_norm[0].astype(F32), QK_WIDTH // HEAD_DIM)[None] * (HEAD_DIM ** -0.5)
    kg = jnp.tile(k_norm[0].astype(F32), QK_WIDTH // HEAD_DIM)[None]

    proj, u = _inproj(x2, meta_pad, norm_mix[0][None], w_in[0].astype(BF16), seg, qg, kg)

    score_bound = (BOUND_MARGIN * HEAD_DIM ** 0.5 * jnp.max(jnp.abs(q_norm[0].astype(F32)))
                   * jnp.max(jnp.abs(k_norm[0].astype(F32)))
                   + jnp.max(jnp.abs(rel_bias.astype(F32)))).reshape(1)
    att = _attention(proj, rel_bias.astype(F32), score_bound, lam_q1[0][None], lam_k1[0][None],
                     lam_q2[0][None], lam_k2[0][None], subln[0][None], seq)

    n_rows = proj.shape[0] // S5_CHUNK
    n_chunks = seq // S5_CHUNK
    perm = _lane_regroup_matrix()
    u_chunks = _to_chunks(u.reshape(n_rows, S5_CHUNK, SSM_WIDTH), perm)
    pp, mm, qq, lam16 = _s5_prep(ssm_a_re[0].astype(F32), ssm_a_im[0].astype(F32),
                                 ssm_log_step[0].astype(F32), ssm_b_re[0].astype(F32),
                                 ssm_b_im[0].astype(F32), ssm_c_re[0].astype(F32),
                                 ssm_c_im[0].astype(F32))
    y_chunks = _s5(u_chunks, pp, mm, qq, lam16, n_chunks)
    y = _from_chunks(y_chunks, perm.T).reshape(seq, SSM_WIDTH)

    h, hn, scores_t = _mix_out(
        x2, att, y, u, ssm_d[0][None].astype(F32), w_glu[0].astype(BF16), b_glu[0][None].astype(F32),
        ssm_norm[0][None].astype(F32), w_out[0].astype(BF16), norm_ffn[0][None].astype(F32),
        router_w[0].astype(F32).T,
        jnp.concatenate([shared_gate[0], shared_up[0]], axis=1).astype(BF16), shared_down[0].astype(BF16))

    tb = MOE_TOKEN_BLOCK
    tri = (jnp.arange(tb)[:, None] <= jnp.arange(tb)[None, :]).astype(BF16)
    wts_t, slots_t, cnt = _route(scores_t, router_bias[0].astype(F32)[:, None], tri)
    off, pc, src, dst, bexp, first, n_act = _moe_tables(cnt[:, :, 0].astype(jnp.int32))
    off_f, pc_f = off.astype(F32), pc.astype(F32)
    xs = _dispatch(hn, slots_t, off_f, pc_f)
    ys = _experts(xs, src, dst, bexp, first, n_act, w_gate[0], w_up[0], w_down[0])
    out = _combine(h, ys, slots_t.T, wts_t.T, off_f, pc_f)
    return out.reshape(batch, seq, d)
```

```python
import functools
import math

import jax
import jax.numpy as jnp
from jax import lax
from jax.experimental import pallas as pl
from jax.experimental.pallas import tpu as pltpu

F32 = jnp.float32
BF16 = jnp.bfloat16

D_MODEL = 2048
N_META = 16
ATT_WIDTH = 1024
SSM_WIDTH = 1024
HEAD_DIM = 64
V_DIM = 128
HEADS = 8
QK_WIDTH = 1024
IN_WIDTH = 4096
V_OFFSET = 2 * QK_WIDTH
PROJ_WIDTH = V_OFFSET + 2 * ATT_WIDTH
SSM_GROUP = 16
SSM_GROUPS = 64
SSM_STATE = 64
N_BUCKETS = 32
MAX_DISTANCE = 128
N_EXPERTS = 64
TOP_K = 8
N_EXPERT_GROUPS = 8
TOPK_GROUPS = 4
EXPERT_HIDDEN = 512
SHARED_HIDDEN = 512
ROUTED_SCALE = 2.5
EPS = 1e-6
LAMBDA_INIT = 0.8 - 0.6 * math.exp(-0.3 * 0)

ROW_BLOCK = 512
ATT_BLOCK = 512
S5_CHUNK = 16
S5_GROUPS_PER_STEP = 8
S5_HALF_STEPS = 128 // SSM_GROUP
S5_HALF_WIDTH = S5_GROUPS_PER_STEP * S5_HALF_STEPS * SSM_GROUP
assert S5_GROUPS_PER_STEP * SSM_GROUP == 128 and S5_CHUNK % S5_HALF_STEPS == 0
MOE_TOKEN_BLOCK = 256
MOE_UNIT = 16
MOE_SLOT_BLOCK = 512
MOE_BLOCK_ROWS = -(-(MOE_TOKEN_BLOCK * TOP_K + N_EXPERTS * (MOE_UNIT - 1)) // MOE_SLOT_BLOCK) * MOE_SLOT_BLOCK
MOE_TYPICAL_ROWS = -(-(MOE_TOKEN_BLOCK * TOP_K + N_EXPERTS * MOE_UNIT // 2) // MOE_SLOT_BLOCK) * MOE_SLOT_BLOCK
NOT_ROUTED = -1e6
MIX_ROW_BLOCK = 256
NEG_BIG = -1e30
MAX_EXP_RANGE = 80.0
BOUND_MARGIN = 1.02
LANES = 128
SUBLANES = 8
VMEM_LIMIT = 56 * 2 ** 20


def _resident(shape, index_map):
    return pl.BlockSpec(shape, index_map, pipeline_mode=pl.Buffered(1))


def _inproj_body(x_ref, meta_ref, g_ref, w_ref, seg_ref, qg_ref, kg_ref, o_ref, u_ref, *, n_xblk):
    i = pl.program_id(0)

    def run(src_ref):
        xv = src_ref[...]
        ms = jnp.mean(xv * xv, axis=-1, keepdims=True)
        hn = (xv * lax.rsqrt(ms + EPS) * g_ref[...]).astype(BF16)
        sec = IN_WIDTH // 4
        for s in range(4):
            ps = jnp.dot(hn, w_ref[:, s * sec:(s + 1) * sec], preferred_element_type=F32)
            if s < 2:
                gain = qg_ref if s == 0 else kg_ref
                msq = jnp.dot((ps * ps).astype(BF16), seg_ref[...], preferred_element_type=F32)
                ps = ps * lax.rsqrt(msq + EPS) * gain[...]
            if s == 3:
                u_ref[...] = ps
                continue
            pb = ps.astype(BF16)
            if s < 2:
                o_ref[:, s * sec:(s + 1) * sec] = pb
            else:
                lane = lax.broadcasted_iota(jnp.int32, (pb.shape[0], V_DIM), 1)
                ones_col = jnp.where(lane == 0, 1.0, 0.0).astype(BF16)
                for hh in range(HEADS):
                    base = V_OFFSET + hh * 2 * V_DIM
                    o_ref[:, base:base + V_DIM] = pb[:, hh * V_DIM:(hh + 1) * V_DIM]
                    o_ref[:, base + V_DIM:base + 2 * V_DIM] = ones_col

    @pl.when(i < n_xblk)
    def _():
        run(x_ref)

    @pl.when(i == n_xblk)
    def _():
        run(meta_ref)


def _inproj(x2, meta_pad, gain, w_bf, seg, qg, kg):
    seq = x2.shape[0]
    n_xblk = seq // ROW_BLOCK
    rows = seq + ROW_BLOCK
    return pl.pallas_call(
        functools.partial(_inproj_body, n_xblk=n_xblk),
        grid=(n_xblk + 1,),
        in_specs=[
            pl.BlockSpec((ROW_BLOCK, D_MODEL), lambda i: (jnp.minimum(i, n_xblk - 1), 0)),
            _resident((ROW_BLOCK, D_MODEL), lambda i: (0, 0)),
            _resident((1, D_MODEL), lambda i: (0, 0)),
            _resident((D_MODEL, IN_WIDTH), lambda i: (0, 0)),
            _resident((QK_WIDTH, QK_WIDTH), lambda i: (0, 0)),
            _resident((1, QK_WIDTH), lambda i: (0, 0)),
            _resident((1, QK_WIDTH), lambda i: (0, 0)),
        ],
        out_specs=[pl.BlockSpec((ROW_BLOCK, PROJ_WIDTH), lambda i: (i, 0)),
                   pl.BlockSpec((ROW_BLOCK, SSM_WIDTH), lambda i: (i, 0))],
        out_shape=[jax.ShapeDtypeStruct((rows, PROJ_WIDTH), BF16),
                   jax.ShapeDtypeStruct((rows, SSM_WIDTH), F32)],
        compiler_params=pltpu.CompilerParams(
            dimension_semantics=("arbitrary",), vmem_limit_bytes=VMEM_LIMIT),
        name="inproj",
    )(x2, meta_pad, gain, w_bf, seg, qg, kg)


def _t5_bias(rel, tab_ref, h):
    half = N_BUCKETS // 2
    exact = half // 2
    n = jnp.abs(rel)
    nf = jnp.maximum(n, 1).astype(F32)
    large = exact + (jnp.log(nf / exact) / math.log(MAX_DISTANCE / exact) * (half - exact)).astype(jnp.int32)
    large = jnp.minimum(large, half - 1)
    bucket = jnp.where(rel > 0, half, 0) + jnp.where(n < exact, n, large)
    out = jnp.zeros(rel.shape, F32)
    for b in range(N_BUCKETS):
        out = jnp.where(bucket == b, tab_ref[b, h], out)
    return out


def _attn_body(tab_ref, bound_ref, q_ref, k_ref, v_ref, lq1_ref, lk1_ref, lq2_ref, lk2_ref,
               subln_ref, o_ref, bias_ref, acc1_ref, acc2_ref, m1_ref, m2_ref, *, n_main):
    T = ATT_BLOCK
    h = pl.program_id(0)
    qi = pl.program_id(1)
    bound = bound_ref[0]

    @pl.when(qi == 0)
    def _():
        offsets = (-T, 0, T, -N_META, -N_META - T, -2 * T, 2 * T)
        for kind, off in enumerate(offsets):
            masked = kind in (3, 4)

            def rows(rc, carry, off=off, masked=masked, kind=kind):
                r0 = pl.multiple_of(rc * SUBLANES, SUBLANES)
                r = r0 + lax.broadcasted_iota(jnp.int32, (SUBLANES, T), 0)
                c = lax.broadcasted_iota(jnp.int32, (SUBLANES, T), 1)
                b = _t5_bias(off + c - r, tab_ref, h) - bound
                if masked:
                    b = jnp.where(c < N_META, b, NEG_BIG)
                bias_ref[kind, pl.ds(r0, SUBLANES), :] = b
                return carry

            lax.fori_loop(0, T // SUBLANES, rows, 0)

    acc1_ref[...] = jnp.zeros(acc1_ref.shape, F32)
    acc2_ref[...] = jnp.zeros(acc2_ref.shape, F32)

    q = q_ref[...]
    q1 = q[:, :HEAD_DIM]
    q2 = q[:, HEAD_DIM:]
    nt = (((1,), (1,)), ((), ()))

    def tile(ki):
        koff = pl.multiple_of(ki * T, T)
        kb = k_ref[pl.ds(koff, T), :]
        va = v_ref[pl.ds(koff, T), :]
        d = ki - qi
        kind = jnp.where(ki == n_main, jnp.where(qi == 0, 3, 4),
                         jnp.where(d <= -2, 5, jnp.where(d >= 2, 6, d + 1)))
        return kb, va, bias_ref[kind]

    def bounded_step(ki, carry):
        kb, va, bias = tile(ki)
        s1 = lax.dot_general(q1, kb[:, :HEAD_DIM], nt, preferred_element_type=F32) + bias
        acc1_ref[...] += jnp.dot(jnp.exp(s1).astype(BF16), va, preferred_element_type=F32)
        s2 = lax.dot_general(q2, kb[:, HEAD_DIM:], nt, preferred_element_type=F32) + bias
        acc2_ref[...] += jnp.dot(jnp.exp(s2).astype(BF16), va, preferred_element_type=F32)
        return carry

    def online_map(s, va, m_ref, acc_ref):
        m_old = m_ref[...]
        m_new = jnp.maximum(m_old, jnp.max(s, axis=-1, keepdims=True))
        p = jnp.exp(s - m_new).astype(BF16)
        acc_ref[...] = (jnp.exp(m_old - m_new) * acc_ref[...]
                        + jnp.dot(p, va, preferred_element_type=F32))
        m_ref[...] = m_new

    def online_step(ki, carry):
        kb, va, bias = tile(ki)
        s1 = lax.dot_general(q1, kb[:, :HEAD_DIM], nt, preferred_element_type=F32) + bias
        online_map(s1, va, m1_ref, acc1_ref)
        s2 = lax.dot_general(q2, kb[:, HEAD_DIM:], nt, preferred_element_type=F32) + bias
        online_map(s2, va, m2_ref, acc2_ref)
        return carry

    no_running_max = 2.0 * bound <= MAX_EXP_RANGE

    @pl.when(no_running_max)
    def _():
        unroll = next(u for u in (33, 11, 3, 2, 1) if (n_main + 1) % u == 0)
        lax.fori_loop(0, n_main + 1, bounded_step, 0, unroll=unroll)

    @pl.when(jnp.logical_not(no_running_max))
    def _():
        m1_ref[...] = jnp.full(m1_ref.shape, -jnp.inf, F32)
        m2_ref[...] = jnp.full(m2_ref.shape, -jnp.inf, F32)
        lax.fori_loop(0, n_main + 1, online_step, 0)

    lam = (jnp.exp(jnp.sum(lq1_ref[...] * lk1_ref[...], axis=-1, keepdims=True))
           - jnp.exp(jnp.sum(lq2_ref[...] * lk2_ref[...], axis=-1, keepdims=True))
           + LAMBDA_INIT)
    a1 = acc1_ref[...]
    a2 = acc2_ref[...]
    o = (a1[:, :V_DIM] / a1[:, V_DIM:V_DIM + 1]
         - lam * (a2[:, :V_DIM] / a2[:, V_DIM:V_DIM + 1]))
    ms = jnp.mean(o * o, axis=-1, keepdims=True)
    o = o * lax.rsqrt(ms + EPS) * subln_ref[...] * (1.0 - LAMBDA_INIT)
    o_ref[...] = o.astype(BF16)


def _attention(proj, rel_bias, score_bound, lq1, lk1, lq2, lk2, subln, seq):
    T = ATT_BLOCK
    n_main = seq // T
    rows = proj.shape[0]
    vec64 = lambda: _resident((1, HEAD_DIM), lambda h, qi: (0, 0))
    return pl.pallas_call(
        functools.partial(_attn_body, n_main=n_main),
        grid=(HEADS, n_main),
        in_specs=[
            pl.BlockSpec(memory_space=pltpu.SMEM),
            pl.BlockSpec(memory_space=pltpu.SMEM),
            pl.BlockSpec((T, 2 * HEAD_DIM), lambda h, qi: (qi, h)),
            pl.BlockSpec((rows, 2 * HEAD_DIM), lambda h, qi: (0, HEADS + h)),
            pl.BlockSpec((rows, 2 * V_DIM), lambda h, qi: (0, V_OFFSET // (2 * V_DIM) + h)),
            vec64(), vec64(), vec64(), vec64(),
            _resident((1, V_DIM), lambda h, qi: (0, 0)),
        ],
        out_specs=pl.BlockSpec((T, V_DIM), lambda h, qi: (qi, h)),
        out_shape=jax.ShapeDtypeStruct((seq, ATT_WIDTH), BF16),
        scratch_shapes=[
            pltpu.VMEM((7, T, T), F32),
            pltpu.VMEM((T, 2 * V_DIM), F32), pltpu.VMEM((T, 2 * V_DIM), F32),
            pltpu.VMEM((T, 1), F32), pltpu.VMEM((T, 1), F32),
        ],
        compiler_params=pltpu.CompilerParams(
            dimension_semantics=("arbitrary", "arbitrary"), vmem_limit_bytes=VMEM_LIMIT),
        name="diff_attention",
    )(rel_bias, score_bound, proj, proj, proj, lq1, lk1, lq2, lk2, subln)


def _lane_regroup_matrix():
    out_lane = jnp.arange(S5_HALF_WIDTH, dtype=jnp.int32)
    g8 = out_lane // LANES
    jj = (out_lane % LANES) // SSM_GROUP
    p = out_lane % SSM_GROUP
    in_lane = jj * LANES + g8 * SSM_GROUP + p
    return (jnp.arange(S5_HALF_WIDTH, dtype=jnp.int32)[:, None] == in_lane[None, :]).astype(BF16)


def _to_chunks_body(u_ref, perm_ref, o_ref):
    parts = []
    for half in range(S5_CHUNK // S5_HALF_STEPS):
        x = jnp.concatenate([u_ref[:, half * S5_HALF_STEPS + jj, :].astype(BF16)
                             for jj in range(S5_HALF_STEPS)], axis=1)
        parts.append(jnp.dot(x, perm_ref[...], preferred_element_type=F32))
    for g8 in range(S5_GROUPS_PER_STEP):
        o_ref[g8] = jnp.concatenate([r[:, g8 * LANES:(g8 + 1) * LANES] for r in parts], axis=1).astype(BF16)


def _to_chunks(u3, perm):
    n_rows = u3.shape[0]
    w = S5_CHUNK * SSM_GROUP
    return pl.pallas_call(
        _to_chunks_body,
        grid=(SSM_GROUPS // S5_GROUPS_PER_STEP,),
        in_specs=[pl.BlockSpec((n_rows, S5_CHUNK, LANES), lambda o: (0, 0, o)),
                  _resident((S5_HALF_WIDTH, S5_HALF_WIDTH), lambda o: (0, 0))],
        out_specs=pl.BlockSpec((S5_GROUPS_PER_STEP, n_rows, w), lambda o: (o, 0, 0)),
        out_shape=jax.ShapeDtypeStruct((SSM_GROUPS, n_rows, w), BF16),
        compiler_params=pltpu.CompilerParams(
            dimension_semantics=("arbitrary",), vmem_limit_bytes=VMEM_LIMIT),
        name="s5_to_chunks",
    )(u3, perm)


def _from_chunks_body(y_ref, perm_t_ref, o_ref):
    for half in range(S5_CHUNK // S5_HALF_STEPS):
        x = jnp.concatenate([y_ref[g8, :, half * LANES:(half + 1) * LANES]
                             for g8 in range(S5_GROUPS_PER_STEP)], axis=1)
        r = jnp.dot(x, perm_t_ref[...], preferred_element_type=F32)
        for jj in range(S5_HALF_STEPS):
            o_ref[:, half * S5_HALF_STEPS + jj, :] = r[:, jj * LANES:(jj + 1) * LANES]


def _from_chunks(y_chunks, perm_t):
    n_chunks = y_chunks.shape[1]
    w = S5_CHUNK * SSM_GROUP
    return pl.pallas_call(
        _from_chunks_body,
        grid=(SSM_GROUPS // S5_GROUPS_PER_STEP,),
        in_specs=[pl.BlockSpec((S5_GROUPS_PER_STEP, n_chunks, w), lambda o: (o, 0, 0)),
                  _resident((S5_HALF_WIDTH, S5_HALF_WIDTH), lambda o: (0, 0))],
        out_specs=pl.BlockSpec((n_chunks, S5_CHUNK, LANES), lambda o: (0, 0, o)),
        out_shape=jax.ShapeDtypeStruct((n_chunks, S5_CHUNK, SSM_WIDTH), F32),
        compiler_params=pltpu.CompilerParams(
            dimension_semantics=("arbitrary",), vmem_limit_bytes=VMEM_LIMIT),
        name="s5_from_chunks",
    )(y_chunks, perm_t)


def _s5_prep(a_re, a_im, log_step, b_re, b_im, c_re, c_im):
    C = S5_CHUNK
    dt = jnp.exp(log_step)[..., None]
    decay = jnp.exp(a_re * dt)
    ab_re = decay * jnp.cos(a_im * dt)
    ab_im = decay * jnp.sin(a_im * dt)
    den = a_re * a_re + a_im * a_im
    zr = ab_re - 1.0
    f_re = (zr * a_re + ab_im * a_im) / den
    f_im = (ab_im * a_re - zr * a_im) / den
    bb_re = f_re[..., None] * b_re - f_im[..., None] * b_im
    bb_im = f_re[..., None] * b_im + f_im[..., None] * b_re
    pr, pi = jnp.ones_like(ab_re), jnp.zeros_like(ab_re)
    pw_re, pw_im = [pr], [pi]
    for _ in range(C):
        pr, pi = pr * ab_re - pi * ab_im, pr * ab_im + pi * ab_re
        pw_re.append(pr)
        pw_im.append(pi)
    G, P, W = SSM_GROUPS, SSM_GROUP, C * SSM_GROUP
    pw_re = jnp.stack(pw_re, axis=-1)
    pw_im = jnp.stack(pw_im, axis=-1)
    ct_re = c_re.transpose(0, 1, 3, 2)
    ct_im = c_im.transpose(0, 1, 3, 2)
    cp_re = ct_re[:, :, :, None, :] * pw_re[..., None] - ct_im[:, :, :, None, :] * pw_im[..., None]
    cp_im = ct_re[:, :, :, None, :] * pw_im[..., None] + ct_im[:, :, :, None, :] * pw_re[..., None]
    bt_re = bb_re.transpose(0, 1, 3, 2)
    bt_im = bb_im.transpose(0, 1, 3, 2)

    def response(d):
        prod = (cp_re[d][:, None, :, :C, :] * bt_re[d][:, :, :, None, None]
                - cp_im[d][:, None, :, :C, :] * bt_im[d][:, :, :, None, None])
        return jnp.sum(prod, axis=2)

    ext_f = jnp.pad(response(0).reshape(G, P, W), ((0, 0), (0, 0), (W, 0)))
    ext_r = jnp.pad(response(1)[:, :, ::-1, :].reshape(G, P, W), ((0, 0), (0, 0), (0, W)))
    mm = jnp.stack([ext_f[:, :, W - P * j:2 * W - P * j] + ext_r[:, :, (C - 1 - j) * P:(C - 1 - j) * P + W]
                    for j in range(C)], axis=1).reshape(G, W, W)

    def in_mat(d, reverse_powers):
        pr_ = pw_re[d][:, :, :C].transpose(0, 2, 1)
        pi_ = pw_im[d][:, :, :C].transpose(0, 2, 1)
        if reverse_powers:
            pr_, pi_ = pr_[:, ::-1], pi_[:, ::-1]
        re = pr_[:, :, None, :] * bt_re[d][:, None] - pi_[:, :, None, :] * bt_im[d][:, None]
        im = pr_[:, :, None, :] * bt_im[d][:, None] + pi_[:, :, None, :] * bt_re[d][:, None]
        return re, im

    pf_re, pf_im = in_mat(0, True)
    pr_re, pr_im = in_mat(1, False)
    pp = jnp.concatenate([pf_re, pr_re, pf_im, pr_im], axis=-1).reshape(G, W, 4 * SSM_STATE)

    qq = jnp.concatenate([cp_re[0][:, :, 1:], cp_re[1][:, :, :0:-1],
                          -cp_im[0][:, :, 1:], -cp_im[1][:, :, :0:-1]], axis=1).reshape(G, 4 * SSM_STATE, W)
    lam16 = jnp.stack([jnp.concatenate([pw_re[0][:, :, C], pw_re[1][:, :, C]], axis=-1),
                       jnp.concatenate([pw_im[0][:, :, C], pw_im[1][:, :, C]], axis=-1)], axis=0)
    return pp.astype(BF16), mm.astype(BF16), qq.astype(BF16), lam16


def _s5_body(u_ref, pp_ref, mm_ref, qq_ref, lam_ref, y_ref,
             zre_ref, zim_ref, are_ref, aim_ref, bre_ref, bim_ref, *, n_chunks):
    GS = S5_GROUPS_PER_STEP
    NS = 2 * SSM_STATE
    for gi in range(GS):
        z = jnp.dot(u_ref[gi], pp_ref[gi], preferred_element_type=F32)
        zre_ref[:, gi, :] = z[:, :NS]
        zim_ref[:, gi, :] = z[:, NS:]

    ar = lam_ref[0]
    ai = lam_ref[1]
    fwd = lax.broadcasted_iota(jnp.int32, (GS, NS), 1) < SSM_STATE
    sre0 = jnp.where(fwd, zre_ref[n_chunks], 0.0)
    sim0 = jnp.where(fwd, zim_ref[n_chunks], 0.0)

    def scan_step(k, carry):
        sre, sim = carry
        kr = n_chunks - 1 - k
        are_ref[k] = sre
        aim_ref[k] = sim
        bre_ref[kr] = sre
        bim_ref[kr] = sim
        zr = jnp.where(fwd, zre_ref[k], zre_ref[kr])
        zi = jnp.where(fwd, zim_ref[k], zim_ref[kr])
        return ar * sre - ai * sim + zr, ar * sim + ai * sre + zi

    lax.fori_loop(0, n_chunks, scan_step, (sre0, sim0))

    fwd_rows = lax.broadcasted_iota(jnp.int32, (n_chunks, NS), 1) < SSM_STATE
    for gi in range(GS):
        s_re = jnp.where(fwd_rows, are_ref[:, gi, :], bre_ref[:, gi, :])
        s_im = jnp.where(fwd_rows, aim_ref[:, gi, :], bim_ref[:, gi, :])
        scat = jnp.concatenate([s_re, s_im], axis=1).astype(BF16)
        y = (jnp.dot(u_ref[gi, :n_chunks, :], mm_ref[gi], preferred_element_type=F32)
             + jnp.dot(scat, qq_ref[gi], preferred_element_type=F32))
        y_ref[gi] = y.astype(BF16)


def _s5(u_chunks, pp, mm, qq, lam16, n_chunks):
    GS = S5_GROUPS_PER_STEP
    n_rows = u_chunks.shape[1]
    W = S5_CHUNK * SSM_GROUP
    NS = 2 * SSM_STATE
    mat = lambda: pl.BlockSpec((GS, W, W), lambda g: (g, 0, 0))
    return pl.pallas_call(
        functools.partial(_s5_body, n_chunks=n_chunks),
        grid=(SSM_GROUPS // GS,),
        in_specs=[
            pl.BlockSpec((GS, n_rows, W), lambda g: (g, 0, 0)),
            mat(), mat(), mat(),
            pl.BlockSpec((2, GS, NS), lambda g: (0, g, 0)),
        ],
        out_specs=pl.BlockSpec((GS, n_chunks, W), lambda g: (g, 0, 0)),
        out_shape=jax.ShapeDtypeStruct((SSM_GROUPS, n_chunks, W), BF16),
        scratch_shapes=[pltpu.VMEM((n_rows, GS, NS), F32), pltpu.VMEM((n_rows, GS, NS), F32)]
        + [pltpu.VMEM((n_chunks, GS, NS), F32) for _ in range(4)],
        compiler_params=pltpu.CompilerParams(
            dimension_semantics=("arbitrary",), vmem_limit_bytes=VMEM_LIMIT),
        name="s5_chunked",
    )(u_chunks, pp, mm, qq, lam16)


def _mix_out_body(x_ref, att_ref, y_ref, u_ref, d_ref, wglu_ref, bglu_ref, sn_ref, wout_ref,
                  nf_ref, rwt_ref, sgu_ref, sd_ref, h_ref, hn_ref, sc_ref):
    y = y_ref[...].astype(F32) + d_ref[...] * u_ref[...].astype(F32)
    y = jax.nn.gelu(y)
    gate = jax.nn.sigmoid(jnp.dot(y.astype(BF16), wglu_ref[...], preferred_element_type=F32) + bglu_ref[...])
    s = y * gate
    ms = jnp.mean(s * s, axis=-1, keepdims=True)
    ssm = (s * lax.rsqrt(ms + EPS) * sn_ref[...]).astype(BF16)
    mixed = jnp.concatenate([att_ref[...], ssm], axis=1)
    h = x_ref[...] + jnp.dot(mixed, wout_ref[...], preferred_element_type=F32)
    ms = jnp.mean(h * h, axis=-1, keepdims=True)
    hn = h * lax.rsqrt(ms + EPS) * nf_ref[...]
    hnb = hn.astype(BF16)
    hn_ref[...] = hnb
    logits_t = lax.dot_general(rwt_ref[...], hn, (((1,), (1,)), ((), ())),
                               precision=lax.Precision.HIGHEST, preferred_element_type=F32)
    sc_ref[...] = jax.nn.sigmoid(logits_t)
    gu = jnp.dot(hnb, sgu_ref[...], preferred_element_type=F32)
    act = (jax.nn.silu(gu[:, :SHARED_HIDDEN]) * gu[:, SHARED_HIDDEN:]).astype(BF16)
    h_ref[...] = h + jnp.dot(act, sd_ref[...], preferred_element_type=F32)


def _mix_out(x2, att, y, u, d_skip, w_glu, b_glu, ssm_norm, w_out, norm_ffn, router_wt,
             sh_gate_up, sh_down):
    seq = x2.shape[0]
    R = MIX_ROW_BLOCK
    res = lambda shape: _resident(shape, lambda i: (0, 0))
    return pl.pallas_call(
        _mix_out_body,
        grid=(seq // R,),
        in_specs=[
            pl.BlockSpec((R, D_MODEL), lambda i: (i, 0)),
            pl.BlockSpec((R, ATT_WIDTH), lambda i: (i, 0)),
            pl.BlockSpec((R, SSM_WIDTH), lambda i: (i, 0)),
            pl.BlockSpec((R, SSM_WIDTH), lambda i: (i, 0)),
            res((1, SSM_WIDTH)), res((SSM_WIDTH, SSM_WIDTH)), res((1, SSM_WIDTH)), res((1, SSM_WIDTH)),
            res((D_MODEL, D_MODEL)), res((1, D_MODEL)), res((N_EXPERTS, D_MODEL)),
            res((D_MODEL, 2 * SHARED_HIDDEN)), res((SHARED_HIDDEN, D_MODEL)),
        ],
        out_specs=[
            pl.BlockSpec((R, D_MODEL), lambda i: (i, 0)),
            pl.BlockSpec((R, D_MODEL), lambda i: (i, 0)),
            pl.BlockSpec((N_EXPERTS, R), lambda i: (0, i)),
        ],
        out_shape=[
            jax.ShapeDtypeStruct((seq, D_MODEL), F32),
            jax.ShapeDtypeStruct((seq, D_MODEL), BF16),
            jax.ShapeDtypeStruct((N_EXPERTS, seq), F32),
        ],
        compiler_params=pltpu.CompilerParams(
            dimension_semantics=("arbitrary",), vmem_limit_bytes=VMEM_LIMIT),
        name="mix_out_shared",
    )(x2, att, y, u, d_skip, w_glu, b_glu, ssm_norm, w_out, norm_ffn, router_wt,
      sh_gate_up, sh_down)


def _route_body(sc_ref, rb_ref, tri_ref, w_ref, slot_ref, cnt_ref):
    scores = sc_ref[...]
    R = scores.shape[1]
    per_group = N_EXPERTS // N_EXPERT_GROUPS
    choice = scores + rb_ref[...]
    c3 = choice.reshape(N_EXPERT_GROUPS, per_group, R)
    within = lax.broadcasted_iota(jnp.int32, c3.shape, 1)
    m1 = jnp.max(c3, axis=1, keepdims=True)
    first = jnp.min(jnp.where(c3 == m1, within, per_group), axis=1, keepdims=True)
    m2 = jnp.max(jnp.where(within == first, -jnp.inf, c3), axis=1, keepdims=True)
    grp = (m1 + m2).reshape(N_EXPERT_GROUPS, R)
    gidx = lax.broadcasted_iota(jnp.int32, grp.shape, 0)
    grank = jnp.zeros(grp.shape, jnp.int32)
    for b in range(N_EXPERT_GROUPS):
        gb = grp[b:b + 1, :]
        grank += ((gb > grp) | ((gb == grp) & (b < gidx))).astype(jnp.int32)
    gmask = grank < TOPK_GROUPS
    emask = jnp.broadcast_to(gmask[:, None, :], c3.shape).reshape(N_EXPERTS, R)
    val = jnp.where(emask, choice, -jnp.inf)
    eidx = lax.broadcasted_iota(jnp.int32, val.shape, 0)
    rank = jnp.zeros(val.shape, jnp.int32)
    for e in range(N_EXPERTS):
        ve = val[e:e + 1, :]
        rank += ((ve > val) | ((ve == val) & (e < eidx))).astype(jnp.int32)
    sel = rank < TOP_K
    w = jnp.where(sel, scores, 0.0)
    w_ref[...] = w / jnp.sum(w, axis=0, keepdims=True) * ROUTED_SCALE
    cum = jnp.dot(jnp.where(sel, 1.0, 0.0).astype(BF16), tri_ref[...], preferred_element_type=F32)
    slot_ref[...] = jnp.where(sel, cum - 1.0, NOT_ROUTED)
    cnt_ref[0] = jnp.broadcast_to(cum[:, R - 1:R], (N_EXPERTS, LANES))


def _route(scores_t, router_bias, tri):
    seq = scores_t.shape[1]
    R = MOE_TOKEN_BLOCK
    n_blk = seq // R
    blk = pl.BlockSpec((N_EXPERTS, R), lambda i: (0, i))
    return pl.pallas_call(
        _route_body,
        grid=(n_blk,),
        in_specs=[blk, _resident((N_EXPERTS, 1), lambda i: (0, 0)), _resident((R, R), lambda i: (0, 0))],
        out_specs=[blk, blk, pl.BlockSpec((1, N_EXPERTS, LANES), lambda i: (i, 0, 0))],
        out_shape=[jax.ShapeDtypeStruct((N_EXPERTS, seq), F32),
                   jax.ShapeDtypeStruct((N_EXPERTS, seq), F32),
                   jax.ShapeDtypeStruct((n_blk, N_EXPERTS, LANES), F32)],
        compiler_params=pltpu.CompilerParams(dimension_semantics=("arbitrary",)),
        name="route",
    )(scores_t, router_bias, tri)


def _moe_tables(cnt):
    n_blk = cnt.shape[0]
    U, RB, SB = MOE_UNIT, MOE_BLOCK_ROWS, MOE_SLOT_BLOCK
    NU = SB // U
    E = N_EXPERTS
    pc = (cnt + U - 1) // U * U
    off = jnp.cumsum(pc, axis=1) - pc
    upc_t = (pc // U).T
    cum_t = jnp.cumsum(upc_t, axis=1)
    units_e = cum_t[:, -1]
    nblk_e = (units_e + NU - 1) // NU
    blk_end = jnp.cumsum(nblk_e)
    blk_start = blk_end - nblk_e
    n_act = blk_end[-1]
    max_blocks = n_blk * (RB // SB) + E
    i = jnp.arange(max_blocks, dtype=jnp.int32)
    active = i < n_act
    count_le = lambda edges, v: jnp.sum((edges <= v[..., None]).astype(jnp.int32), axis=-1)
    last_e = jnp.minimum(count_le(blk_end, n_act - 1), E - 1)
    be = jnp.where(active, jnp.minimum(count_le(blk_end[None, :], i), E - 1), last_e)
    oh_e = be[:, None] == jnp.arange(E, dtype=jnp.int32)[None, :]
    pick_e = lambda v: jnp.sum(jnp.where(oh_e, v[None, :], 0), axis=1)
    pick_e2 = lambda m: jnp.sum(jnp.where(oh_e[:, :, None], m[None, :, :], 0), axis=1)
    bstart_i = pick_e(blk_start)
    first = active & (i == bstart_i)
    local = (i - bstart_i)[:, None] * NU + jnp.arange(NU, dtype=jnp.int32)[None, :]
    valid = active[:, None] & (local < pick_e(units_e)[:, None])
    cum_i = pick_e2(cum_t)
    b_of = jnp.minimum(count_le(cum_i[:, None, :], local), n_blk - 1)
    oh_b = b_of[:, :, None] == jnp.arange(n_blk, dtype=jnp.int32)[None, None, :]
    pick_b = lambda m: jnp.sum(jnp.where(oh_b, m[:, None, :], 0), axis=2)
    seg_start = pick_b(cum_i) - pick_b(pick_e2(upc_t))
    unit = (b_of * RB + pick_b(pick_e2(off.T))) // U + (local - seg_start)
    spare = n_blk * RB // U
    src = jnp.where(valid, unit, spare)
    dst = jnp.where(valid, unit,
                    spare + (1 + i % 2)[:, None] * NU + jnp.arange(NU, dtype=jnp.int32)[None, :])
    short = active & (pick_e(units_e) - (i - bstart_i) * NU <= NU // 2)
    i32 = lambda a: a.astype(jnp.int32)
    flags = i32(first) + 2 * i32(short)
    return (off, pc, i32(src.reshape(-1)), i32(dst.reshape(-1)), i32(be), flags,
            i32(n_act.reshape(1)))


def _split_hi_lo(pos):
    hi = jnp.floor(pos * (1.0 / 64.0))
    return hi.astype(BF16), (pos - 64.0 * hi).astype(BF16)


def _dispatch_body(used_ref, hn_ref, slot_ref, offc_ref, offl_ref, pcl_ref, xs_ref, *, n_blk):
    b = pl.program_id(0)
    RB, TB = MOE_BLOCK_ROWS, MOE_TOKEN_BLOCK
    used = used_ref[jnp.minimum(b, n_blk - 1)]

    @pl.when(b == n_blk)
    def _():
        xs_ref[...] = jnp.zeros(xs_ref.shape, BF16)

    def scatter_rows(rows):
        pos = slot_ref[...] + offc_ref[0]
        hi, lo = _split_hi_lo(pos)
        r = lax.broadcasted_iota(jnp.int32, (rows, N_EXPERTS), 0).astype(F32)
        off = offl_ref[0]
        owner = jnp.where((r >= off) & (r < off + pcl_ref[0]), 1.0, 0.0).astype(BF16)
        p = (64.0 * jnp.dot(owner, hi, preferred_element_type=F32)
             + jnp.dot(owner, lo, preferred_element_type=F32))
        rr = lax.broadcasted_iota(jnp.int32, (rows, TB), 0).astype(F32)
        onehot = jnp.where(jnp.abs(p - rr) < 0.5, 1.0, 0.0).astype(BF16)
        x = hn_ref[...]
        C = MOE_SLOT_BLOCK
        for c in range(rows // C):
            xs_ref[0, c * C:(c + 1) * C, :] = jnp.dot(
                onehot[c * C:(c + 1) * C], x, preferred_element_type=F32).astype(BF16)
        if rows < RB:
            xs_ref[0, rows:, :] = jnp.zeros((RB - rows, D_MODEL), BF16)

    @pl.when((b < n_blk) & (used <= MOE_TYPICAL_ROWS))
    def _():
        scatter_rows(MOE_TYPICAL_ROWS)

    @pl.when((b < n_blk) & (used > MOE_TYPICAL_ROWS))
    def _():
        scatter_rows(RB)


def _dispatch(used, hn, slots, off, pc):
    n_blk = off.shape[0]
    RB, TB = MOE_BLOCK_ROWS, MOE_TOKEN_BLOCK
    clamp = lambda b: jnp.minimum(b, n_blk - 1)
    return pl.pallas_call(
        functools.partial(_dispatch_body, n_blk=n_blk),
        grid=(n_blk + 1,),
        in_specs=[
            pl.BlockSpec(memory_space=pltpu.SMEM),
            pl.BlockSpec((TB, D_MODEL), lambda b: (clamp(b), 0)),
            pl.BlockSpec((N_EXPERTS, TB), lambda b: (0, clamp(b))),
            pl.BlockSpec((1, N_EXPERTS, 1), lambda b: (clamp(b), 0, 0)),
            pl.BlockSpec((1, 1, N_EXPERTS), lambda b: (clamp(b), 0, 0)),
            pl.BlockSpec((1, 1, N_EXPERTS), lambda b: (clamp(b), 0, 0)),
        ],
        out_specs=pl.BlockSpec((1, RB, D_MODEL), lambda b: (b, 0, 0)),
        out_shape=jax.ShapeDtypeStruct((n_blk + 1, RB, D_MODEL), BF16),
        compiler_params=pltpu.CompilerParams(
            dimension_semantics=("arbitrary",), vmem_limit_bytes=VMEM_LIMIT),
        name="moe_dispatch",
    )(used, hn, slots, off[:, :, None], off[:, None, :], pc[:, None, :])


def _expert_body(src_ref, dst_ref, bexp_ref, first_ref, nact_ref,
                 xs_hbm, wg_ref, wu_ref, wd_ref, ys_hbm,
                 xbuf, ybuf, wgub, wdb, sem_in, sem_out):
    del bexp_ref
    i = pl.program_id(0)
    n_act = nact_ref[0]
    cur = lax.rem(i, 2)
    U = MOE_UNIT
    NU = MOE_SLOT_BLOCK // U

    def in_copy(blk, buf, u):
        return pltpu.make_async_copy(xs_hbm.at[src_ref[blk * NU + u]],
                                     xbuf.at[buf, pl.ds(u * U, U)], sem_in.at[buf])

    def out_copy(blk, buf, u):
        return pltpu.make_async_copy(ybuf.at[buf, pl.ds(u * U, U)],
                                     ys_hbm.at[dst_ref[blk * NU + u]], sem_out.at[buf])

    @pl.when(i == 0)
    def _():
        for u in range(NU):
            in_copy(0, 0, u).start()

    @pl.when(i < n_act)
    def _():
        @pl.when(i + 1 < n_act)
        def _():
            for u in range(NU):
                in_copy(i + 1, 1 - cur, u).start()

        @pl.when(first_ref[i] % 2 == 1)
        def _():
            wgub[:, :EXPERT_HIDDEN] = wg_ref[0].astype(BF16)
            wgub[:, EXPERT_HIDDEN:] = wu_ref[0].astype(BF16)
            wdb[...] = wd_ref[0].astype(BF16)

        for u in range(NU):
            in_copy(i, cur, u).wait()

        @pl.when(i >= 2)
        def _():
            for u in range(NU):
                out_copy(i - 2, cur, u).wait()

        def swiglu_rows(rows):
            x = xbuf[cur, :rows]
            gu = jnp.dot(x, wgub[...], preferred_element_type=F32)
            act = (jax.nn.silu(gu[:, :EXPERT_HIDDEN]) * gu[:, EXPERT_HIDDEN:]).astype(BF16)
            ybuf[cur, :rows] = jnp.dot(act, wdb[...], preferred_element_type=F32).astype(BF16)

        half = MOE_SLOT_BLOCK // 2

        @pl.when(first_ref[i] < 2)
        def _():
            swiglu_rows(MOE_SLOT_BLOCK)

        @pl.when(first_ref[i] >= 2)
        def _():
            swiglu_rows(half)
            ybuf[cur, half:] = jnp.zeros((half, D_MODEL), BF16)

        for u in range(NU):
            out_copy(i, cur, u).start()

        @pl.when(i == n_act - 1)
        def _():
            for u in range(NU):
                out_copy(i, cur, u).wait()

            @pl.when(i >= 1)
            def _():
                for u in range(NU):
                    out_copy(i - 1, 1 - cur, u).wait()


def _experts(xs, src, dst, bexp, first, n_act, wg, wu, wd):
    U, SB = MOE_UNIT, MOE_SLOT_BLOCK
    n_units = xs.shape[0] * xs.shape[1] // U
    max_blocks = bexp.shape[0]
    unit_view = lambda a: a.reshape(n_units, U, D_MODEL)
    wspec = lambda shape: pl.BlockSpec((1,) + shape, lambda i, src, dst, bexp, first, nact: (bexp[i], 0, 0))
    grid_spec = pltpu.PrefetchScalarGridSpec(
        num_scalar_prefetch=5,
        grid=(max_blocks,),
        in_specs=[
            pl.BlockSpec(memory_space=pl.ANY),
            wspec((D_MODEL, EXPERT_HIDDEN)), wspec((D_MODEL, EXPERT_HIDDEN)), wspec((EXPERT_HIDDEN, D_MODEL)),
        ],
        out_specs=pl.BlockSpec(memory_space=pl.ANY),
        scratch_shapes=[
            pltpu.VMEM((2, SB, D_MODEL), BF16), pltpu.VMEM((2, SB, D_MODEL), BF16),
            pltpu.VMEM((D_MODEL, 2 * EXPERT_HIDDEN), BF16), pltpu.VMEM((EXPERT_HIDDEN, D_MODEL), BF16),
            pltpu.SemaphoreType.DMA((2,)), pltpu.SemaphoreType.DMA((2,)),
        ],
    )
    ys = pl.pallas_call(
        _expert_body,
        grid_spec=grid_spec,
        out_shape=jax.ShapeDtypeStruct((n_units, U, D_MODEL), BF16),
        input_output_aliases={5: 0},
        compiler_params=pltpu.CompilerParams(
            dimension_semantics=("arbitrary",), vmem_limit_bytes=VMEM_LIMIT),
        name="moe_experts",
    )(src, dst, bexp, first, n_act, unit_view(xs), wg, wu, wd)
    return ys.reshape(xs.shape)


def _combine_body(used_ref, h_ref, ys_ref, slot_ref, w_ref, offl_ref, offc_ref, pcc_ref, o_ref):
    RB, TB = MOE_BLOCK_ROWS, MOE_TOKEN_BLOCK
    used = used_ref[pl.program_id(0)]

    def gather_rows(rows):
        pos = slot_ref[...] + offl_ref[0]
        hi, lo = _split_hi_lo(pos)
        r = lax.broadcasted_iota(jnp.int32, (N_EXPERTS, rows), 1).astype(F32)
        off = offc_ref[0]
        owner = jnp.where((r >= off) & (r < off + pcc_ref[0]), 1.0, 0.0).astype(BF16)
        p = (64.0 * jnp.dot(hi, owner, preferred_element_type=F32)
             + jnp.dot(lo, owner, preferred_element_type=F32))
        wr = jnp.dot(w_ref[...].astype(BF16), owner, preferred_element_type=F32)
        rr = lax.broadcasted_iota(jnp.int32, (TB, rows), 1).astype(F32)
        gather_w = jnp.where(jnp.abs(p - rr) < 0.5, wr, 0.0).astype(BF16)
        o_ref[...] = h_ref[...] + jnp.dot(gather_w, ys_ref[0, :rows, :], preferred_element_type=F32)

    @pl.when(used <= MOE_TYPICAL_ROWS)
    def _():
        gather_rows(MOE_TYPICAL_ROWS)

    @pl.when(used > MOE_TYPICAL_ROWS)
    def _():
        gather_rows(RB)


def _combine(used, h, ys, slots_tok, w_tok, off, pc):
    n_blk = off.shape[0]
    RB, TB = MOE_BLOCK_ROWS, MOE_TOKEN_BLOCK
    seq = h.shape[0]
    return pl.pallas_call(
        _combine_body,
        grid=(n_blk,),
        in_specs=[
            pl.BlockSpec(memory_space=pltpu.SMEM),
            pl.BlockSpec((TB, D_MODEL), lambda b: (b, 0)),
            pl.BlockSpec((1, RB, D_MODEL), lambda b: (b, 0, 0)),
            pl.BlockSpec((TB, N_EXPERTS), lambda b: (b, 0)),
            pl.BlockSpec((TB, N_EXPERTS), lambda b: (b, 0)),
            pl.BlockSpec((1, 1, N_EXPERTS), lambda b: (b, 0, 0)),
            pl.BlockSpec((1, N_EXPERTS, 1), lambda b: (b, 0, 0)),
            pl.BlockSpec((1, N_EXPERTS, 1), lambda b: (b, 0, 0)),
        ],
        out_specs=pl.BlockSpec((TB, D_MODEL), lambda b: (b, 0)),
        out_shape=jax.ShapeDtypeStruct((seq, D_MODEL), F32),
        compiler_params=pltpu.CompilerParams(
            dimension_semantics=("arbitrary",), vmem_limit_bytes=VMEM_LIMIT),
        name="moe_combine",
    )(used, h, ys, slots_tok, w_tok, off[:, None, :], off[:, :, None], pc[:, :, None])


def kernel(x, meta_tokens, rel_bias, norm_mix, w_in, q_norm, k_norm, lam_q1, lam_k1, lam_q2, lam_k2, subln, ssm_a_re, ssm_a_im, ssm_log_step, ssm_b_re, ssm_b_im, ssm_c_re, ssm_c_im, ssm_d, w_glu, b_glu, ssm_norm, w_out, norm_ffn, router_w, router_bias, w_gate, w_up, w_down, shared_gate, shared_up, shared_down):
    batch, seq, d = x.shape
    assert batch == 1 and d == D_MODEL and seq % ROW_BLOCK == 0 and seq % ATT_BLOCK == 0
    assert norm_mix.shape[0] == 1, "single layer"
    x2 = x.reshape(seq, d)
    meta_pad = jnp.zeros((ROW_BLOCK, d), F32).at[:N_META].set(meta_tokens.astype(F32))
    seg = jnp.kron(jnp.eye(QK_WIDTH // HEAD_DIM, dtype=F32),
                   jnp.full((HEAD_DIM, HEAD_DIM), 1.0 / HEAD_DIM, F32)).astype(BF16)
    qg = jnp.tile(q_norm[0].astype(F32), QK_WIDTH // HEAD_DIM)[None] * (HEAD_DIM ** -0.5)
    kg = jnp.tile(k_norm[0].astype(F32), QK_WIDTH // HEAD_DIM)[None]

    proj, u = _inproj(x2, meta_pad, norm_mix[0][None], w_in[0].astype(BF16), seg, qg, kg)

    score_bound = (BOUND_MARGIN * HEAD_DIM ** 0.5 * jnp.max(jnp.abs(q_norm[0].astype(F32)))
                   * jnp.max(jnp.abs(k_norm[0].astype(F32)))
                   + jnp.max(jnp.abs(rel_bias.astype(F32)))).reshape(1)
    att = _attention(proj, rel_bias.astype(F32), score_bound, lam_q1[0][None], lam_k1[0][None],
                     lam_q2[0][None], lam_k2[0][None], subln[0][None], seq)

    n_rows = proj.shape[0] // S5_CHUNK
    n_chunks = seq // S5_CHUNK
    perm = _lane_regroup_matrix()
    u_chunks = _to_chunks(u.reshape(n_rows, S5_CHUNK, SSM_WIDTH), perm)
    pp, mm, qq, lam16 = _s5_prep(ssm_a_re[0].astype(F32), ssm_a_im[0].astype(F32),
                                 ssm_log_step[0].astype(F32), ssm_b_re[0].astype(F32),
                                 ssm_b_im[0].astype(F32), ssm_c_re[0].astype(F32),
                                 ssm_c_im[0].astype(F32))
    y_chunks = _s5(u_chunks, pp, mm, qq, lam16, n_chunks)
    y = _from_chunks(y_chunks, perm.T).reshape(seq, SSM_WIDTH)

    h, hn, scores_t = _mix_out(
        x2, att, y, u, ssm_d[0][None].astype(F32), w_glu[0].astype(BF16), b_glu[0][None].astype(F32),
        ssm_norm[0][None].astype(F32), w_out[0].astype(BF16), norm_ffn[0][None].astype(F32),
        router_w[0].astype(F32).T,
        jnp.concatenate([shared_gate[0], shared_up[0]], axis=1).astype(BF16), shared_down[0].astype(BF16))

    tb = MOE_TOKEN_BLOCK
    tri = (jnp.arange(tb)[:, None] <= jnp.arange(tb)[None, :]).astype(BF16)
    wts_t, slots_t, cnt = _route(scores_t, router_bias[0].astype(F32)[:, None], tri)
    off, pc, src, dst, bexp, first, n_act = _moe_tables(cnt[:, :, 0].astype(jnp.int32))
    off_f, pc_f = off.astype(F32), pc.astype(F32)
    used = jnp.sum(pc, axis=1).astype(jnp.int32)
    xs = _dispatch(used, hn, slots_t, off_f, pc_f)
    ys = _experts(xs, src, dst, bexp, first, n_act, w_gate[0], w_up[0], w_down[0])
    out = _combine(used, h, ys, slots_t.T, wts_t.T, off_f, pc_f)
    return out.reshape(batch, seq, d)
```

```python
import functools
import math

import jax
import jax.numpy as jnp
from jax import lax
from jax.experimental import pallas as pl
from jax.experimental.pallas import tpu as pltpu

F32 = jnp.float32
BF16 = jnp.bfloat16

D_MODEL = 2048
N_META = 16
ATT_WIDTH = 1024
SSM_WIDTH = 1024
HEAD_DIM = 64
V_DIM = 128
HEADS = 8
QK_WIDTH = 1024
IN_WIDTH = 4096
V_OFFSET = 2 * QK_WIDTH
PROJ_WIDTH = V_OFFSET + 2 * ATT_WIDTH
SSM_GROUP = 16
SSM_GROUPS = 64
SSM_STATE = 64
N_BUCKETS = 32
MAX_DISTANCE = 128
N_EXPERTS = 64
TOP_K = 8
N_EXPERT_GROUPS = 8
TOPK_GROUPS = 4
EXPERT_HIDDEN = 512
SHARED_HIDDEN = 512
ROUTED_SCALE = 2.5
EPS = 1e-6
LAMBDA_INIT = 0.8 - 0.6 * math.exp(-0.3 * 0)

ROW_BLOCK = 512
ATT_BLOCK = 512
S5_CHUNK = 16
S5_GROUPS_PER_STEP = 8
S5_HALF_STEPS = 128 // SSM_GROUP
S5_HALF_WIDTH = S5_GROUPS_PER_STEP * S5_HALF_STEPS * SSM_GROUP
assert S5_GROUPS_PER_STEP * SSM_GROUP == 128 and S5_CHUNK % S5_HALF_STEPS == 0
MOE_TOKEN_BLOCK = 256
MOE_UNIT = 16
MOE_SLOT_BLOCK = 512
MOE_BLOCK_ROWS = -(-(MOE_TOKEN_BLOCK * TOP_K + N_EXPERTS * (MOE_UNIT - 1)) // MOE_SLOT_BLOCK) * MOE_SLOT_BLOCK
MOE_TYPICAL_ROWS = -(-(MOE_TOKEN_BLOCK * TOP_K + N_EXPERTS * MOE_UNIT // 2) // MOE_SLOT_BLOCK) * MOE_SLOT_BLOCK
NOT_ROUTED = -1e6
MIX_ROW_BLOCK = 256
NEG_BIG = -1e30
MAX_EXP_RANGE = 80.0
BOUND_MARGIN = 1.02
LANES = 128
SUBLANES = 8
VMEM_LIMIT = 56 * 2 ** 20


def _resident(shape, index_map):
    return pl.BlockSpec(shape, index_map, pipeline_mode=pl.Buffered(1))


def _inproj_body(x_ref, meta_ref, g_ref, w_ref, seg_ref, qg_ref, kg_ref, o_ref, u_ref, *, n_xblk):
    i = pl.program_id(0)

    def run(src_ref):
        xv = src_ref[...]
        ms = jnp.mean(xv * xv, axis=-1, keepdims=True)
        hn = (xv * lax.rsqrt(ms + EPS) * g_ref[...]).astype(BF16)
        sec = IN_WIDTH // 4
        for s in range(4):
            ps = jnp.dot(hn, w_ref[:, s * sec:(s + 1) * sec], preferred_element_type=F32)
            if s < 2:
                gain = qg_ref if s == 0 else kg_ref
                msq = jnp.dot((ps * ps).astype(BF16), seg_ref[...], preferred_element_type=F32)
                inv = lax.rsqrt(msq + EPS)
                inv_full = jnp.concatenate(
                    [jnp.broadcast_to(inv[:, c:c + 1], (inv.shape[0], HEAD_DIM))
                     for c in range(QK_WIDTH // HEAD_DIM)], axis=1)
                ps = ps * inv_full * gain[...]
            if s == 3:
                u_ref[...] = ps
                continue
            pb = ps.astype(BF16)
            if s < 2:
                o_ref[:, s * sec:(s + 1) * sec] = pb
            else:
                lane = lax.broadcasted_iota(jnp.int32, (pb.shape[0], V_DIM), 1)
                ones_col = jnp.where(lane == 0, 1.0, 0.0).astype(BF16)
                for hh in range(HEADS):
                    base = V_OFFSET + hh * 2 * V_DIM
                    o_ref[:, base:base + V_DIM] = pb[:, hh * V_DIM:(hh + 1) * V_DIM]
                    o_ref[:, base + V_DIM:base + 2 * V_DIM] = ones_col

    @pl.when(i < n_xblk)
    def _():
        run(x_ref)

    @pl.when(i == n_xblk)
    def _():
        run(meta_ref)


def _inproj(x2, meta_pad, gain, w_bf, seg, qg, kg):
    seq = x2.shape[0]
    n_xblk = seq // ROW_BLOCK
    rows = seq + ROW_BLOCK
    return pl.pallas_call(
        functools.partial(_inproj_body, n_xblk=n_xblk),
        grid=(n_xblk + 1,),
        in_specs=[
            pl.BlockSpec((ROW_BLOCK, D_MODEL), lambda i: (jnp.minimum(i, n_xblk - 1), 0)),
            _resident((ROW_BLOCK, D_MODEL), lambda i: (0, 0)),
            _resident((1, D_MODEL), lambda i: (0, 0)),
            _resident((D_MODEL, IN_WIDTH), lambda i: (0, 0)),
            _resident((QK_WIDTH, LANES), lambda i: (0, 0)),
            _resident((1, QK_WIDTH), lambda i: (0, 0)),
            _resident((1, QK_WIDTH), lambda i: (0, 0)),
        ],
        out_specs=[pl.BlockSpec((ROW_BLOCK, PROJ_WIDTH), lambda i: (i, 0)),
                   pl.BlockSpec((ROW_BLOCK, SSM_WIDTH), lambda i: (i, 0))],
        out_shape=[jax.ShapeDtypeStruct((rows, PROJ_WIDTH), BF16),
                   jax.ShapeDtypeStruct((rows, SSM_WIDTH), F32)],
        compiler_params=pltpu.CompilerParams(
            dimension_semantics=("arbitrary",), vmem_limit_bytes=VMEM_LIMIT),
        name="inproj",
    )(x2, meta_pad, gain, w_bf, seg, qg, kg)


def _t5_bias(rel, tab_ref, h):
    half = N_BUCKETS // 2
    exact = half // 2
    n = jnp.abs(rel)
    nf = jnp.maximum(n, 1).astype(F32)
    large = exact + (jnp.log(nf / exact) / math.log(MAX_DISTANCE / exact) * (half - exact)).astype(jnp.int32)
    large = jnp.minimum(large, half - 1)
    bucket = jnp.where(rel > 0, half, 0) + jnp.where(n < exact, n, large)
    out = jnp.zeros(rel.shape, F32)
    for b in range(N_BUCKETS):
        out = jnp.where(bucket == b, tab_ref[b, h], out)
    return out


def _attn_body(tab_ref, bound_ref, q_ref, k_ref, v_ref, lq1_ref, lk1_ref, lq2_ref, lk2_ref,
               subln_ref, o_ref, bias_ref, acc1_ref, acc2_ref, m1_ref, m2_ref, *, n_main):
    T = ATT_BLOCK
    h = pl.program_id(0)
    qi = pl.program_id(1)
    bound = bound_ref[0]

    @pl.when(qi == 0)
    def _():
        offsets = (-T, 0, T, -N_META, -N_META - T, -2 * T, 2 * T)
        for kind, off in enumerate(offsets):
            masked = kind in (3, 4)

            def rows(rc, carry, off=off, masked=masked, kind=kind):
                r0 = pl.multiple_of(rc * SUBLANES, SUBLANES)
                r = r0 + lax.broadcasted_iota(jnp.int32, (SUBLANES, T), 0)
                c = lax.broadcasted_iota(jnp.int32, (SUBLANES, T), 1)
                b = _t5_bias(off + c - r, tab_ref, h) - bound
                if masked:
                    b = jnp.where(c < N_META, b, NEG_BIG)
                bias_ref[kind, pl.ds(r0, SUBLANES), :] = b
                return carry

            lax.fori_loop(0, T // SUBLANES, rows, 0)

    acc1_ref[...] = jnp.zeros(acc1_ref.shape, F32)
    acc2_ref[...] = jnp.zeros(acc2_ref.shape, F32)

    q = q_ref[...]
    q1 = q[:, :HEAD_DIM]
    q2 = q[:, HEAD_DIM:]
    nt = (((1,), (1,)), ((), ()))

    def tile(ki):
        koff = pl.multiple_of(ki * T, T)
        kb = k_ref[pl.ds(koff, T), :]
        va = v_ref[pl.ds(koff, T), :]
        d = ki - qi
        kind = jnp.where(ki == n_main, jnp.where(qi == 0, 3, 4),
                         jnp.where(d <= -2, 5, jnp.where(d >= 2, 6, d + 1)))
        return kb, va, bias_ref[kind]

    def bounded_step(ki, carry):
        kb, va, bias = tile(ki)
        s1 = lax.dot_general(q1, kb[:, :HEAD_DIM], nt, preferred_element_type=F32) + bias
        acc1_ref[...] += jnp.dot(jnp.exp(s1).astype(BF16), va, preferred_element_type=F32)
        s2 = lax.dot_general(q2, kb[:, HEAD_DIM:], nt, preferred_element_type=F32) + bias
        acc2_ref[...] += jnp.dot(jnp.exp(s2).astype(BF16), va, preferred_element_type=F32)
        return carry

    def online_map(s, va, m_ref, acc_ref):
        m_old = m_ref[...]
        m_new = jnp.maximum(m_old, jnp.max(s, axis=-1, keepdims=True))
        p = jnp.exp(s - m_new).astype(BF16)
        acc_ref[...] = (jnp.exp(m_old - m_new) * acc_ref[...]
                        + jnp.dot(p, va, preferred_element_type=F32))
        m_ref[...] = m_new

    def online_step(ki, carry):
        kb, va, bias = tile(ki)
        s1 = lax.dot_general(q1, kb[:, :HEAD_DIM], nt, preferred_element_type=F32) + bias
        online_map(s1, va, m1_ref, acc1_ref)
        s2 = lax.dot_general(q2, kb[:, HEAD_DIM:], nt, preferred_element_type=F32) + bias
        online_map(s2, va, m2_ref, acc2_ref)
        return carry

    no_running_max = 2.0 * bound <= MAX_EXP_RANGE

    @pl.when(no_running_max)
    def _():
        unroll = next(u for u in (33, 11, 3, 2, 1) if (n_main + 1) % u == 0)
        lax.fori_loop(0, n_main + 1, bounded_step, 0, unroll=unroll)

    @pl.when(jnp.logical_not(no_running_max))
    def _():
        m1_ref[...] = jnp.full(m1_ref.shape, -jnp.inf, F32)
        m2_ref[...] = jnp.full(m2_ref.shape, -jnp.inf, F32)
        lax.fori_loop(0, n_main + 1, online_step, 0)

    lam = (jnp.exp(jnp.sum(lq1_ref[...] * lk1_ref[...], axis=-1, keepdims=True))
           - jnp.exp(jnp.sum(lq2_ref[...] * lk2_ref[...], axis=-1, keepdims=True))
           + LAMBDA_INIT)
    a1 = acc1_ref[...]
    a2 = acc2_ref[...]
    o = (a1[:, :V_DIM] / a1[:, V_DIM:V_DIM + 1]
         - lam * (a2[:, :V_DIM] / a2[:, V_DIM:V_DIM + 1]))
    ms = jnp.mean(o * o, axis=-1, keepdims=True)
    o = o * lax.rsqrt(ms + EPS) * subln_ref[...] * (1.0 - LAMBDA_INIT)
    o_ref[...] = o.astype(BF16)


def _attention(proj, rel_bias, score_bound, lq1, lk1, lq2, lk2, subln, seq):
    T = ATT_BLOCK
    n_main = seq // T
    rows = proj.shape[0]
    vec64 = lambda: _resident((1, HEAD_DIM), lambda h, qi: (0, 0))
    return pl.pallas_call(
        functools.partial(_attn_body, n_main=n_main),
        grid=(HEADS, n_main),
        in_specs=[
            pl.BlockSpec(memory_space=pltpu.SMEM),
            pl.BlockSpec(memory_space=pltpu.SMEM),
            pl.BlockSpec((T, 2 * HEAD_DIM), lambda h, qi: (qi, h)),
            pl.BlockSpec((rows, 2 * HEAD_DIM), lambda h, qi: (0, HEADS + h)),
            pl.BlockSpec((rows, 2 * V_DIM), lambda h, qi: (0, V_OFFSET // (2 * V_DIM) + h)),
            vec64(), vec64(), vec64(), vec64(),
            _resident((1, V_DIM), lambda h, qi: (0, 0)),
        ],
        out_specs=pl.BlockSpec((T, V_DIM), lambda h, qi: (qi, h)),
        out_shape=jax.ShapeDtypeStruct((seq, ATT_WIDTH), BF16),
        scratch_shapes=[
            pltpu.VMEM((7, T, T), F32),
            pltpu.VMEM((T, 2 * V_DIM), F32), pltpu.VMEM((T, 2 * V_DIM), F32),
            pltpu.VMEM((T, 1), F32), pltpu.VMEM((T, 1), F32),
        ],
        compiler_params=pltpu.CompilerParams(
            dimension_semantics=("arbitrary", "arbitrary"), vmem_limit_bytes=VMEM_LIMIT),
        name="diff_attention",
    )(rel_bias, score_bound, proj, proj, proj, lq1, lk1, lq2, lk2, subln)


def _lane_regroup_matrix():
    out_lane = jnp.arange(S5_HALF_WIDTH, dtype=jnp.int32)
    g8 = out_lane // LANES
    jj = (out_lane % LANES) // SSM_GROUP
    p = out_lane % SSM_GROUP
    in_lane = jj * LANES + g8 * SSM_GROUP + p
    return (jnp.arange(S5_HALF_WIDTH, dtype=jnp.int32)[:, None] == in_lane[None, :]).astype(BF16)


def _to_chunks_body(u_ref, perm_ref, o_ref):
    parts = []
    for half in range(S5_CHUNK // S5_HALF_STEPS):
        x = jnp.concatenate([u_ref[:, half * S5_HALF_STEPS + jj, :].astype(BF16)
                             for jj in range(S5_HALF_STEPS)], axis=1)
        parts.append(jnp.dot(x, perm_ref[...], preferred_element_type=F32))
    for g8 in range(S5_GROUPS_PER_STEP):
        o_ref[g8] = jnp.concatenate([r[:, g8 * LANES:(g8 + 1) * LANES] for r in parts], axis=1).astype(BF16)


def _to_chunks(u3, perm):
    n_rows = u3.shape[0]
    w = S5_CHUNK * SSM_GROUP
    return pl.pallas_call(
        _to_chunks_body,
        grid=(SSM_GROUPS // S5_GROUPS_PER_STEP,),
        in_specs=[pl.BlockSpec((n_rows, S5_CHUNK, LANES), lambda o: (0, 0, o)),
                  _resident((S5_HALF_WIDTH, S5_HALF_WIDTH), lambda o: (0, 0))],
        out_specs=pl.BlockSpec((S5_GROUPS_PER_STEP, n_rows, w), lambda o: (o, 0, 0)),
        out_shape=jax.ShapeDtypeStruct((SSM_GROUPS, n_rows, w), BF16),
        compiler_params=pltpu.CompilerParams(
            dimension_semantics=("arbitrary",), vmem_limit_bytes=VMEM_LIMIT),
        name="s5_to_chunks",
    )(u3, perm)


def _from_chunks_body(y_ref, perm_t_ref, o_ref):
    for half in range(S5_CHUNK // S5_HALF_STEPS):
        x = jnp.concatenate([y_ref[g8, :, half * LANES:(half + 1) * LANES]
                             for g8 in range(S5_GROUPS_PER_STEP)], axis=1)
        r = jnp.dot(x, perm_t_ref[...], preferred_element_type=F32)
        for jj in range(S5_HALF_STEPS):
            o_ref[:, half * S5_HALF_STEPS + jj, :] = r[:, jj * LANES:(jj + 1) * LANES]


def _from_chunks(y_chunks, perm_t):
    n_chunks = y_chunks.shape[1]
    w = S5_CHUNK * SSM_GROUP
    return pl.pallas_call(
        _from_chunks_body,
        grid=(SSM_GROUPS // S5_GROUPS_PER_STEP,),
        in_specs=[pl.BlockSpec((S5_GROUPS_PER_STEP, n_chunks, w), lambda o: (o, 0, 0)),
                  _resident((S5_HALF_WIDTH, S5_HALF_WIDTH), lambda o: (0, 0))],
        out_specs=pl.BlockSpec((n_chunks, S5_CHUNK, LANES), lambda o: (0, 0, o)),
        out_shape=jax.ShapeDtypeStruct((n_chunks, S5_CHUNK, SSM_WIDTH), F32),
        compiler_params=pltpu.CompilerParams(
            dimension_semantics=("arbitrary",), vmem_limit_bytes=VMEM_LIMIT),
        name="s5_from_chunks",
    )(y_chunks, perm_t)


def _s5_prep(a_re, a_im, log_step, b_re, b_im, c_re, c_im):
    C = S5_CHUNK
    dt = jnp.exp(log_step)[..., None]
    decay = jnp.exp(a_re * dt)
    ab_re = decay * jnp.cos(a_im * dt)
    ab_im = decay * jnp.sin(a_im * dt)
    den = a_re * a_re + a_im * a_im
    zr = ab_re - 1.0
    f_re = (zr * a_re + ab_im * a_im) / den
    f_im = (ab_im * a_re - zr * a_im) / den
    bb_re = f_re[..., None] * b_re - f_im[..., None] * b_im
    bb_im = f_re[..., None] * b_im + f_im[..., None] * b_re
    pr, pi = jnp.ones_like(ab_re), jnp.zeros_like(ab_re)
    pw_re, pw_im = [pr], [pi]
    for _ in range(C):
        pr, pi = pr * ab_re - pi * ab_im, pr * ab_im + pi * ab_re
        pw_re.append(pr)
        pw_im.append(pi)
    G, P, W = SSM_GROUPS, SSM_GROUP, C * SSM_GROUP
    pw_re = jnp.stack(pw_re, axis=-1)
    pw_im = jnp.stack(pw_im, axis=-1)
    ct_re = c_re.transpose(0, 1, 3, 2)
    ct_im = c_im.transpose(0, 1, 3, 2)
    cp_re = ct_re[:, :, :, None, :] * pw_re[..., None] - ct_im[:, :, :, None, :] * pw_im[..., None]
    cp_im = ct_re[:, :, :, None, :] * pw_im[..., None] + ct_im[:, :, :, None, :] * pw_re[..., None]
    bt_re = bb_re.transpose(0, 1, 3, 2)
    bt_im = bb_im.transpose(0, 1, 3, 2)

    def response(d):
        prod = (cp_re[d][:, None, :, :C, :] * bt_re[d][:, :, :, None, None]
                - cp_im[d][:, None, :, :C, :] * bt_im[d][:, :, :, None, None])
        return jnp.sum(prod, axis=2)

    ext_f = jnp.pad(response(0).reshape(G, P, W), ((0, 0), (0, 0), (W, 0)))
    ext_r = jnp.pad(response(1)[:, :, ::-1, :].reshape(G, P, W), ((0, 0), (0, 0), (0, W)))
    mm = jnp.stack([ext_f[:, :, W - P * j:2 * W - P * j] + ext_r[:, :, (C - 1 - j) * P:(C - 1 - j) * P + W]
                    for j in range(C)], axis=1).reshape(G, W, W)

    def in_mat(d, reverse_powers):
        pr_ = pw_re[d][:, :, :C].transpose(0, 2, 1)
        pi_ = pw_im[d][:, :, :C].transpose(0, 2, 1)
        if reverse_powers:
            pr_, pi_ = pr_[:, ::-1], pi_[:, ::-1]
        re = pr_[:, :, None, :] * bt_re[d][:, None] - pi_[:, :, None, :] * bt_im[d][:, None]
        im = pr_[:, :, None, :] * bt_im[d][:, None] + pi_[:, :, None, :] * bt_re[d][:, None]
        return re, im

    pf_re, pf_im = in_mat(0, True)
    pr_re, pr_im = in_mat(1, False)
    pp = jnp.concatenate([pf_re, pr_re, pf_im, pr_im], axis=-1).reshape(G, W, 4 * SSM_STATE)

    qq = jnp.concatenate([cp_re[0][:, :, 1:], cp_re[1][:, :, :0:-1],
                          -cp_im[0][:, :, 1:], -cp_im[1][:, :, :0:-1]], axis=1).reshape(G, 4 * SSM_STATE, W)
    lam16 = jnp.stack([jnp.concatenate([pw_re[0][:, :, C], pw_re[1][:, :, C]], axis=-1),
                       jnp.concatenate([pw_im[0][:, :, C], pw_im[1][:, :, C]], axis=-1)], axis=0)
    return pp.astype(BF16), mm.astype(BF16), qq.astype(BF16), lam16


def _s5_body(u_ref, pp_ref, mm_ref, qq_ref, lam_ref, y_ref,
             zre_ref, zim_ref, are_ref, aim_ref, bre_ref, bim_ref, *, n_chunks):
    GS = S5_GROUPS_PER_STEP
    NS = 2 * SSM_STATE
    for gi in range(GS):
        z = jnp.dot(u_ref[gi], pp_ref[gi], preferred_element_type=F32)
        zre_ref[:, gi, :] = z[:, :NS]
        zim_ref[:, gi, :] = z[:, NS:]

    ar = lam_ref[0]
    ai = lam_ref[1]
    fwd = lax.broadcasted_iota(jnp.int32, (GS, NS), 1) < SSM_STATE
    sre0 = jnp.where(fwd, zre_ref[n_chunks], 0.0)
    sim0 = jnp.where(fwd, zim_ref[n_chunks], 0.0)

    def scan_step(k, carry):
        sre, sim = carry
        kr = n_chunks - 1 - k
        are_ref[k] = sre
        aim_ref[k] = sim
        bre_ref[kr] = sre
        bim_ref[kr] = sim
        zr = jnp.where(fwd, zre_ref[k], zre_ref[kr])
        zi = jnp.where(fwd, zim_ref[k], zim_ref[kr])
        return ar * sre - ai * sim + zr, ar * sim + ai * sre + zi

    lax.fori_loop(0, n_chunks, scan_step, (sre0, sim0))

    fwd_rows = lax.broadcasted_iota(jnp.int32, (n_chunks, NS), 1) < SSM_STATE
    for gi in range(GS):
        s_re = jnp.where(fwd_rows, are_ref[:, gi, :], bre_ref[:, gi, :])
        s_im = jnp.where(fwd_rows, aim_ref[:, gi, :], bim_ref[:, gi, :])
        scat = jnp.concatenate([s_re, s_im], axis=1).astype(BF16)
        y = (jnp.dot(u_ref[gi, :n_chunks, :], mm_ref[gi], preferred_element_type=F32)
             + jnp.dot(scat, qq_ref[gi], preferred_element_type=F32))
        y_ref[gi] = y.astype(BF16)


def _s5(u_chunks, pp, mm, qq, lam16, n_chunks):
    GS = S5_GROUPS_PER_STEP
    n_rows = u_chunks.shape[1]
    W = S5_CHUNK * SSM_GROUP
    NS = 2 * SSM_STATE
    mat = lambda: pl.BlockSpec((GS, W, W), lambda g: (g, 0, 0))
    return pl.pallas_call(
        functools.partial(_s5_body, n_chunks=n_chunks),
        grid=(SSM_GROUPS // GS,),
        in_specs=[
            pl.BlockSpec((GS, n_rows, W), lambda g: (g, 0, 0)),
            mat(), mat(), mat(),
            pl.BlockSpec((2, GS, NS), lambda g: (0, g, 0)),
        ],
        out_specs=pl.BlockSpec((GS, n_chunks, W), lambda g: (g, 0, 0)),
        out_shape=jax.ShapeDtypeStruct((SSM_GROUPS, n_chunks, W), BF16),
        scratch_shapes=[pltpu.VMEM((n_rows, GS, NS), F32), pltpu.VMEM((n_rows, GS, NS), F32)]
        + [pltpu.VMEM((n_chunks, GS, NS), F32) for _ in range(4)],
        compiler_params=pltpu.CompilerParams(
            dimension_semantics=("arbitrary",), vmem_limit_bytes=VMEM_LIMIT),
        name="s5_chunked",
    )(u_chunks, pp, mm, qq, lam16)


def _mix_out_body(x_ref, att_ref, y_ref, u_ref, d_ref, wglu_ref, bglu_ref, sn_ref, wout_ref,
                  nf_ref, rwt_ref, sgu_ref, sd_ref, h_ref, hn_ref, sc_ref):
    y = y_ref[...].astype(F32) + d_ref[...] * u_ref[...].astype(F32)
    y = jax.nn.gelu(y)
    gate = jax.nn.sigmoid(jnp.dot(y.astype(BF16), wglu_ref[...], preferred_element_type=F32) + bglu_ref[...])
    s = y * gate
    ms = jnp.mean(s * s, axis=-1, keepdims=True)
    ssm = (s * lax.rsqrt(ms + EPS) * sn_ref[...]).astype(BF16)
    mixed = jnp.concatenate([att_ref[...], ssm], axis=1)
    h = x_ref[...] + jnp.dot(mixed, wout_ref[...], preferred_element_type=F32)
    ms = jnp.mean(h * h, axis=-1, keepdims=True)
    hn = h * lax.rsqrt(ms + EPS) * nf_ref[...]
    hnb = hn.astype(BF16)
    hn_ref[...] = hnb
    logits_t = lax.dot_general(rwt_ref[...], hn, (((1,), (1,)), ((), ())),
                               precision=lax.Precision.HIGHEST, preferred_element_type=F32)
    sc_ref[...] = jax.nn.sigmoid(logits_t)
    gu = jnp.dot(hnb, sgu_ref[...], preferred_element_type=F32)
    act = (jax.nn.silu(gu[:, :SHARED_HIDDEN]) * gu[:, SHARED_HIDDEN:]).astype(BF16)
    h_ref[...] = h + jnp.dot(act, sd_ref[...], preferred_element_type=F32)


def _mix_out(x2, att, y, u, d_skip, w_glu, b_glu, ssm_norm, w_out, norm_ffn, router_wt,
             sh_gate_up, sh_down):
    seq = x2.shape[0]
    R = MIX_ROW_BLOCK
    res = lambda shape: _resident(shape, lambda i: (0, 0))
    return pl.pallas_call(
        _mix_out_body,
        grid=(seq // R,),
        in_specs=[
            pl.BlockSpec((R, D_MODEL), lambda i: (i, 0)),
            pl.BlockSpec((R, ATT_WIDTH), lambda i: (i, 0)),
            pl.BlockSpec((R, SSM_WIDTH), lambda i: (i, 0)),
            pl.BlockSpec((R, SSM_WIDTH), lambda i: (i, 0)),
            res((1, SSM_WIDTH)), res((SSM_WIDTH, SSM_WIDTH)), res((1, SSM_WIDTH)), res((1, SSM_WIDTH)),
            res((D_MODEL, D_MODEL)), res((1, D_MODEL)), res((N_EXPERTS, D_MODEL)),
            res((D_MODEL, 2 * SHARED_HIDDEN)), res((SHARED_HIDDEN, D_MODEL)),
        ],
        out_specs=[
            pl.BlockSpec((R, D_MODEL), lambda i: (i, 0)),
            pl.BlockSpec((R, D_MODEL), lambda i: (i, 0)),
            pl.BlockSpec((N_EXPERTS, R), lambda i: (0, i)),
        ],
        out_shape=[
            jax.ShapeDtypeStruct((seq, D_MODEL), F32),
            jax.ShapeDtypeStruct((seq, D_MODEL), BF16),
            jax.ShapeDtypeStruct((N_EXPERTS, seq), F32),
        ],
        compiler_params=pltpu.CompilerParams(
            dimension_semantics=("arbitrary",), vmem_limit_bytes=VMEM_LIMIT),
        name="mix_out_shared",
    )(x2, att, y, u, d_skip, w_glu, b_glu, ssm_norm, w_out, norm_ffn, router_wt,
      sh_gate_up, sh_down)


def _route_body(sc_ref, rb_ref, tri_ref, w_ref, slot_ref, cnt_ref):
    scores = sc_ref[...]
    R = scores.shape[1]
    per_group = N_EXPERTS // N_EXPERT_GROUPS
    choice = scores + rb_ref[...]
    c3 = choice.reshape(N_EXPERT_GROUPS, per_group, R)
    within = lax.broadcasted_iota(jnp.int32, c3.shape, 1)
    m1 = jnp.max(c3, axis=1, keepdims=True)
    first = jnp.min(jnp.where(c3 == m1, within, per_group), axis=1, keepdims=True)
    m2 = jnp.max(jnp.where(within == first, -jnp.inf, c3), axis=1, keepdims=True)
    grp = (m1 + m2).reshape(N_EXPERT_GROUPS, R)
    gidx = lax.broadcasted_iota(jnp.int32, grp.shape, 0)
    grank = jnp.zeros(grp.shape, jnp.int32)
    for b in range(N_EXPERT_GROUPS):
        gb = grp[b:b + 1, :]
        grank += ((gb > grp) | ((gb == grp) & (b < gidx))).astype(jnp.int32)
    gmask = grank < TOPK_GROUPS
    emask = jnp.broadcast_to(gmask[:, None, :], c3.shape).reshape(N_EXPERTS, R)
    val = jnp.where(emask, choice, -jnp.inf)
    eidx = lax.broadcasted_iota(jnp.int32, val.shape, 0)
    rank = jnp.zeros(val.shape, jnp.int32)
    for e in range(N_EXPERTS):
        ve = val[e:e + 1, :]
        rank += ((ve > val) | ((ve == val) & (e < eidx))).astype(jnp.int32)
    sel = rank < TOP_K
    w = jnp.where(sel, scores, 0.0)
    w_ref[...] = w / jnp.sum(w, axis=0, keepdims=True) * ROUTED_SCALE
    cum = jnp.dot(jnp.where(sel, 1.0, 0.0).astype(BF16), tri_ref[...], preferred_element_type=F32)
    slot_ref[...] = jnp.where(sel, cum - 1.0, NOT_ROUTED)
    cnt_ref[0] = jnp.broadcast_to(cum[:, R - 1:R], (N_EXPERTS, LANES))


def _route(scores_t, router_bias, tri):
    seq = scores_t.shape[1]
    R = MOE_TOKEN_BLOCK
    n_blk = seq // R
    blk = pl.BlockSpec((N_EXPERTS, R), lambda i: (0, i))
    return pl.pallas_call(
        _route_body,
        grid=(n_blk,),
        in_specs=[blk, _resident((N_EXPERTS, 1), lambda i: (0, 0)), _resident((R, R), lambda i: (0, 0))],
        out_specs=[blk, blk, pl.BlockSpec((1, N_EXPERTS, LANES), lambda i: (i, 0, 0))],
        out_shape=[jax.ShapeDtypeStruct((N_EXPERTS, seq), F32),
                   jax.ShapeDtypeStruct((N_EXPERTS, seq), F32),
                   jax.ShapeDtypeStruct((n_blk, N_EXPERTS, LANES), F32)],
        compiler_params=pltpu.CompilerParams(dimension_semantics=("arbitrary",)),
        name="route",
    )(scores_t, router_bias, tri)


def _moe_tables(cnt):
    n_blk = cnt.shape[0]
    U, RB, SB = MOE_UNIT, MOE_BLOCK_ROWS, MOE_SLOT_BLOCK
    NU = SB // U
    E = N_EXPERTS
    pc = (cnt + U - 1) // U * U
    off = jnp.cumsum(pc, axis=1) - pc
    upc_t = (pc // U).T
    cum_t = jnp.cumsum(upc_t, axis=1)
    units_e = cum_t[:, -1]
    nblk_e = (units_e + NU - 1) // NU
    blk_end = jnp.cumsum(nblk_e)
    blk_start = blk_end - nblk_e
    n_act = blk_end[-1]
    max_blocks = n_blk * (RB // SB) + E
    i = jnp.arange(max_blocks, dtype=jnp.int32)
    active = i < n_act
    count_le = lambda edges, v: jnp.sum((edges <= v[..., None]).astype(jnp.int32), axis=-1)
    last_e = jnp.minimum(count_le(blk_end, n_act - 1), E - 1)
    be = jnp.where(active, jnp.minimum(count_le(blk_end[None, :], i), E - 1), last_e)
    oh_e = be[:, None] == jnp.arange(E, dtype=jnp.int32)[None, :]
    pick_e = lambda v: jnp.sum(jnp.where(oh_e, v[None, :], 0), axis=1)
    pick_e2 = lambda m: jnp.sum(jnp.where(oh_e[:, :, None], m[None, :, :], 0), axis=1)
    bstart_i = pick_e(blk_start)
    first = active & (i == bstart_i)
    local = (i - bstart_i)[:, None] * NU + jnp.arange(NU, dtype=jnp.int32)[None, :]
    valid = active[:, None] & (local < pick_e(units_e)[:, None])
    cum_i = pick_e2(cum_t)
    b_of = jnp.minimum(count_le(cum_i[:, None, :], local), n_blk - 1)
    oh_b = b_of[:, :, None] == jnp.arange(n_blk, dtype=jnp.int32)[None, None, :]
    pick_b = lambda m: jnp.sum(jnp.where(oh_b, m[:, None, :], 0), axis=2)
    seg_start = pick_b(cum_i) - pick_b(pick_e2(upc_t))
    unit = (b_of * RB + pick_b(pick_e2(off.T))) // U + (local - seg_start)
    spare = n_blk * RB // U
    src = jnp.where(valid, unit, spare)
    dst = jnp.where(valid, unit,
                    spare + (1 + i % 2)[:, None] * NU + jnp.arange(NU, dtype=jnp.int32)[None, :])
    short = active & (pick_e(units_e) - (i - bstart_i) * NU <= NU // 2)
    i32 = lambda a: a.astype(jnp.int32)
    flags = i32(first) + 2 * i32(short)
    return (off, pc, i32(src.reshape(-1)), i32(dst.reshape(-1)), i32(be), flags,
            i32(n_act.reshape(1)))


def _split_hi_lo(pos):
    hi = jnp.floor(pos * (1.0 / 64.0))
    return hi.astype(BF16), (pos - 64.0 * hi).astype(BF16)


def _dispatch_body(used_ref, hn_ref, slot_ref, offc_ref, offl_ref, pcl_ref, xs_ref, *, n_blk):
    b = pl.program_id(0)
    RB, TB = MOE_BLOCK_ROWS, MOE_TOKEN_BLOCK
    used = used_ref[jnp.minimum(b, n_blk - 1)]

    @pl.when(b == n_blk)
    def _():
        xs_ref[...] = jnp.zeros(xs_ref.shape, BF16)

    def scatter_rows(rows):
        pos = slot_ref[...] + offc_ref[0]
        hi, lo = _split_hi_lo(pos)
        r = lax.broadcasted_iota(jnp.int32, (rows, N_EXPERTS), 0).astype(F32)
        off = offl_ref[0]
        owner = jnp.where((r >= off) & (r < off + pcl_ref[0]), 1.0, 0.0).astype(BF16)
        p = (64.0 * jnp.dot(owner, hi, preferred_element_type=F32)
             + jnp.dot(owner, lo, preferred_element_type=F32))
        rr = lax.broadcasted_iota(jnp.int32, (rows, TB), 0).astype(F32)
        onehot = jnp.where(jnp.abs(p - rr) < 0.5, 1.0, 0.0).astype(BF16)
        x = hn_ref[...]
        C = MOE_SLOT_BLOCK
        for c in range(rows // C):
            xs_ref[0, c * C:(c + 1) * C, :] = jnp.dot(
                onehot[c * C:(c + 1) * C], x, preferred_element_type=F32).astype(BF16)
        if rows < RB:
            xs_ref[0, rows:, :] = jnp.zeros((RB - rows, D_MODEL), BF16)

    @pl.when((b < n_blk) & (used <= MOE_TYPICAL_ROWS))
    def _():
        scatter_rows(MOE_TYPICAL_ROWS)

    @pl.when((b < n_blk) & (used > MOE_TYPICAL_ROWS))
    def _():
        scatter_rows(RB)


def _dispatch(used, hn, slots, off, pc):
    n_blk = off.shape[0]
    RB, TB = MOE_BLOCK_ROWS, MOE_TOKEN_BLOCK
    clamp = lambda b: jnp.minimum(b, n_blk - 1)
    return pl.pallas_call(
        functools.partial(_dispatch_body, n_blk=n_blk),
        grid=(n_blk + 1,),
        in_specs=[
            pl.BlockSpec(memory_space=pltpu.SMEM),
            pl.BlockSpec((TB, D_MODEL), lambda b: (clamp(b), 0)),
            pl.BlockSpec((N_EXPERTS, TB), lambda b: (0, clamp(b))),
            pl.BlockSpec((1, N_EXPERTS, 1), lambda b: (clamp(b), 0, 0)),
            pl.BlockSpec((1, 1, N_EXPERTS), lambda b: (clamp(b), 0, 0)),
            pl.BlockSpec((1, 1, N_EXPERTS), lambda b: (clamp(b), 0, 0)),
        ],
        out_specs=pl.BlockSpec((1, RB, D_MODEL), lambda b: (b, 0, 0)),
        out_shape=jax.ShapeDtypeStruct((n_blk + 1, RB, D_MODEL), BF16),
        compiler_params=pltpu.CompilerParams(
            dimension_semantics=("arbitrary",), vmem_limit_bytes=VMEM_LIMIT),
        name="moe_dispatch",
    )(used, hn, slots, off[:, :, None], off[:, None, :], pc[:, None, :])


def _expert_body(src_ref, dst_ref, bexp_ref, first_ref, nact_ref,
                 xs_hbm, wg_ref, wu_ref, wd_ref, ys_hbm,
                 xbuf, ybuf, wgub, wdb, sem_in, sem_out):
    del bexp_ref
    i = pl.program_id(0)
    n_act = nact_ref[0]
    cur = lax.rem(i, 2)
    U = MOE_UNIT
    NU = MOE_SLOT_BLOCK // U

    def in_copy(blk, buf, u):
        return pltpu.make_async_copy(xs_hbm.at[src_ref[blk * NU + u]],
                                     xbuf.at[buf, pl.ds(u * U, U)], sem_in.at[buf])

    def out_copy(blk, buf, u):
        return pltpu.make_async_copy(ybuf.at[buf, pl.ds(u * U, U)],
                                     ys_hbm.at[dst_ref[blk * NU + u]], sem_out.at[buf])

    @pl.when(i == 0)
    def _():
        for u in range(NU):
            in_copy(0, 0, u).start()

    @pl.when(i < n_act)
    def _():
        @pl.when(i + 1 < n_act)
        def _():
            for u in range(NU):
                in_copy(i + 1, 1 - cur, u).start()

        @pl.when(first_ref[i] % 2 == 1)
        def _():
            wgub[:, :EXPERT_HIDDEN] = wg_ref[0].astype(BF16)
            wgub[:, EXPERT_HIDDEN:] = wu_ref[0].astype(BF16)
            wdb[...] = wd_ref[0].astype(BF16)

        for u in range(NU):
            in_copy(i, cur, u).wait()

        @pl.when(i >= 2)
        def _():
            for u in range(NU):
                out_copy(i - 2, cur, u).wait()

        def swiglu_rows(rows):
            x = xbuf[cur, :rows]
            gu = jnp.dot(x, wgub[...], preferred_element_type=F32)
            act = (jax.nn.silu(gu[:, :EXPERT_HIDDEN]) * gu[:, EXPERT_HIDDEN:]).astype(BF16)
            ybuf[cur, :rows] = jnp.dot(act, wdb[...], preferred_element_type=F32).astype(BF16)

        half = MOE_SLOT_BLOCK // 2

        @pl.when(first_ref[i] < 2)
        def _():
            swiglu_rows(MOE_SLOT_BLOCK)

        @pl.when(first_ref[i] >= 2)
        def _():
            swiglu_rows(half)
            ybuf[cur, half:] = jnp.zeros((half, D_MODEL), BF16)

        for u in range(NU):
            out_copy(i, cur, u).start()

        @pl.when(i == n_act - 1)
        def _():
            for u in range(NU):
                out_copy(i, cur, u).wait()

            @pl.when(i >= 1)
            def _():
                for u in range(NU):
                    out_copy(i - 1, 1 - cur, u).wait()


def _experts(xs, src, dst, bexp, first, n_act, wg, wu, wd):
    U, SB = MOE_UNIT, MOE_SLOT_BLOCK
    n_units = xs.shape[0] * xs.shape[1] // U
    max_blocks = bexp.shape[0]
    unit_view = lambda a: a.reshape(n_units, U, D_MODEL)
    wspec = lambda shape: pl.BlockSpec((1,) + shape, lambda i, src, dst, bexp, first, nact: (bexp[i], 0, 0))
    grid_spec = pltpu.PrefetchScalarGridSpec(
        num_scalar_prefetch=5,
        grid=(max_blocks,),
        in_specs=[
            pl.BlockSpec(memory_space=pl.ANY),
            wspec((D_MODEL, EXPERT_HIDDEN)), wspec((D_MODEL, EXPERT_HIDDEN)), wspec((EXPERT_HIDDEN, D_MODEL)),
        ],
        out_specs=pl.BlockSpec(memory_space=pl.ANY),
        scratch_shapes=[
            pltpu.VMEM((2, SB, D_MODEL), BF16), pltpu.VMEM((2, SB, D_MODEL), BF16),
            pltpu.VMEM((D_MODEL, 2 * EXPERT_HIDDEN), BF16), pltpu.VMEM((EXPERT_HIDDEN, D_MODEL), BF16),
            pltpu.SemaphoreType.DMA((2,)), pltpu.SemaphoreType.DMA((2,)),
        ],
    )
    ys = pl.pallas_call(
        _expert_body,
        grid_spec=grid_spec,
        out_shape=jax.ShapeDtypeStruct((n_units, U, D_MODEL), BF16),
        input_output_aliases={5: 0},
        compiler_params=pltpu.CompilerParams(
            dimension_semantics=("arbitrary",), vmem_limit_bytes=VMEM_LIMIT),
        name="moe_experts",
    )(src, dst, bexp, first, n_act, unit_view(xs), wg, wu, wd)
    return ys.reshape(xs.shape)


def _combine_body(used_ref, h_ref, ys_ref, slot_ref, w_ref, offl_ref, offc_ref, pcc_ref, o_ref):
    RB, TB = MOE_BLOCK_ROWS, MOE_TOKEN_BLOCK
    used = used_ref[pl.program_id(0)]

    def gather_rows(rows):
        pos = slot_ref[...] + offl_ref[0]
        hi, lo = _split_hi_lo(pos)
        r = lax.broadcasted_iota(jnp.int32, (N_EXPERTS, rows), 1).astype(F32)
        off = offc_ref[0]
        owner = jnp.where((r >= off) & (r < off + pcc_ref[0]), 1.0, 0.0).astype(BF16)
        p = (64.0 * jnp.dot(hi, owner, preferred_element_type=F32)
             + jnp.dot(lo, owner, preferred_element_type=F32))
        wr = jnp.dot(w_ref[...].astype(BF16), owner, preferred_element_type=F32)
        rr = lax.broadcasted_iota(jnp.int32, (TB, rows), 1).astype(F32)
        gather_w = jnp.where(jnp.abs(p - rr) < 0.5, wr, 0.0).astype(BF16)
        o_ref[...] = h_ref[...] + jnp.dot(gather_w, ys_ref[0, :rows, :], preferred_element_type=F32)

    @pl.when(used <= MOE_TYPICAL_ROWS)
    def _():
        gather_rows(MOE_TYPICAL_ROWS)

    @pl.when(used > MOE_TYPICAL_ROWS)
    def _():
        gather_rows(RB)


def _combine(used, h, ys, slots_tok, w_tok, off, pc):
    n_blk = off.shape[0]
    RB, TB = MOE_BLOCK_ROWS, MOE_TOKEN_BLOCK
    seq = h.shape[0]
    return pl.pallas_call(
        _combine_body,
        grid=(n_blk,),
        in_specs=[
            pl.BlockSpec(memory_space=pltpu.SMEM),
            pl.BlockSpec((TB, D_MODEL), lambda b: (b, 0)),
            pl.BlockSpec((1, RB, D_MODEL), lambda b: (b, 0, 0)),
            pl.BlockSpec((TB, N_EXPERTS), lambda b: (b, 0)),
            pl.BlockSpec((TB, N_EXPERTS), lambda b: (b, 0)),
            pl.BlockSpec((1, 1, N_EXPERTS), lambda b: (b, 0, 0)),
            pl.BlockSpec((1, N_EXPERTS, 1), lambda b: (b, 0, 0)),
            pl.BlockSpec((1, N_EXPERTS, 1), lambda b: (b, 0, 0)),
        ],
        out_specs=pl.BlockSpec((TB, D_MODEL), lambda b: (b, 0)),
        out_shape=jax.ShapeDtypeStruct((seq, D_MODEL), F32),
        compiler_params=pltpu.CompilerParams(
            dimension_semantics=("arbitrary",), vmem_limit_bytes=VMEM_LIMIT),
        name="moe_combine",
    )(used, h, ys, slots_tok, w_tok, off[:, None, :], off[:, :, None], pc[:, :, None])


def kernel(x, meta_tokens, rel_bias, norm_mix, w_in, q_norm, k_norm, lam_q1, lam_k1, lam_q2, lam_k2, subln, ssm_a_re, ssm_a_im, ssm_log_step, ssm_b_re, ssm_b_im, ssm_c_re, ssm_c_im, ssm_d, w_glu, b_glu, ssm_norm, w_out, norm_ffn, router_w, router_bias, w_gate, w_up, w_down, shared_gate, shared_up, shared_down):
    batch, seq, d = x.shape
    assert batch == 1 and d == D_MODEL and seq % ROW_BLOCK == 0 and seq % ATT_BLOCK == 0
    assert norm_mix.shape[0] == 1, "single layer"
    x2 = x.reshape(seq, d)
    meta_pad = jnp.zeros((ROW_BLOCK, d), F32).at[:N_META].set(meta_tokens.astype(F32))
    seg = ((jnp.arange(QK_WIDTH)[:, None] // HEAD_DIM == jnp.arange(LANES)[None, :])
           .astype(F32) / HEAD_DIM).astype(BF16)
    qg = jnp.tile(q_norm[0].astype(F32), QK_WIDTH // HEAD_DIM)[None] * (HEAD_DIM ** -0.5)
    kg = jnp.tile(k_norm[0].astype(F32), QK_WIDTH // HEAD_DIM)[None]

    proj, u = _inproj(x2, meta_pad, norm_mix[0][None], w_in[0].astype(BF16), seg, qg, kg)

    score_bound = (BOUND_MARGIN * HEAD_DIM ** 0.5 * jnp.max(jnp.abs(q_norm[0].astype(F32)))
                   * jnp.max(jnp.abs(k_norm[0].astype(F32)))
                   + jnp.max(jnp.abs(rel_bias.astype(F32)))).reshape(1)
    att = _attention(proj, rel_bias.astype(F32), score_bound, lam_q1[0][None], lam_k1[0][None],
                     lam_q2[0][None], lam_k2[0][None], subln[0][None], seq)

    n_rows = proj.shape[0] // S5_CHUNK
    n_chunks = seq // S5_CHUNK
    perm = _lane_regroup_matrix()
    u_chunks = _to_chunks(u.reshape(n_rows, S5_CHUNK, SSM_WIDTH), perm)
    pp, mm, qq, lam16 = _s5_prep(ssm_a_re[0].astype(F32), ssm_a_im[0].astype(F32),
                                 ssm_log_step[0].astype(F32), ssm_b_re[0].astype(F32),
                                 ssm_b_im[0].astype(F32), ssm_c_re[0].astype(F32),
                                 ssm_c_im[0].astype(F32))
    y_chunks = _s5(u_chunks, pp, mm, qq, lam16, n_chunks)
    y = _from_chunks(y_chunks, perm.T).reshape(seq, SSM_WIDTH)

    h, hn, scores_t = _mix_out(
        x2, att, y, u, ssm_d[0][None].astype(F32), w_glu[0].astype(BF16), b_glu[0][None].astype(F32),
        ssm_norm[0][None].astype(F32), w_out[0].astype(BF16), norm_ffn[0][None].astype(F32),
        router_w[0].astype(F32).T,
        jnp.concatenate([shared_gate[0], shared_up[0]], axis=1).astype(BF16), shared_down[0].astype(BF16))

    tb = MOE_TOKEN_BLOCK
    tri = (jnp.arange(tb)[:, None] <= jnp.arange(tb)[None, :]).astype(BF16)
    wts_t, slots_t, cnt = _route(scores_t, router_bias[0].astype(F32)[:, None], tri)
    off, pc, src, dst, bexp, first, n_act = _moe_tables(cnt[:, :, 0].astype(jnp.int32))
    off_f, pc_f = off.astype(F32), pc.astype(F32)
    used = jnp.sum(pc, axis=1).astype(jnp.int32)
    xs = _dispatch(used, hn, slots_t, off_f, pc_f)
    ys = _experts(xs, src, dst, bexp, first, n_act, w_gate[0], w_up[0], w_down[0])
    out = _combine(used, h, ys, slots_t.T, wts_t.T, off_f, pc_f)
    return out.reshape(batch, seq, d)
```

```python
import functools
import math

import jax
import jax.numpy as jnp
from jax import lax
from jax.experimental import pallas as pl
from jax.experimental.pallas import tpu as pltpu

F32 = jnp.float32
BF16 = jnp.bfloat16

D_MODEL = 2048
N_META = 16
ATT_WIDTH = 1024
SSM_WIDTH = 1024
HEAD_DIM = 64
V_DIM = 128
HEADS = 8
QK_WIDTH = 1024
IN_WIDTH = 4096
V_OFFSET = 2 * QK_WIDTH
PROJ_WIDTH = V_OFFSET + 2 * ATT_WIDTH
SSM_GROUP = 16
SSM_GROUPS = 64
SSM_STATE = 64
N_BUCKETS = 32
MAX_DISTANCE = 128
N_EXPERTS = 64
TOP_K = 8
N_EXPERT_GROUPS = 8
TOPK_GROUPS = 4
EXPERT_HIDDEN = 512
SHARED_HIDDEN = 512
ROUTED_SCALE = 2.5
EPS = 1e-6
LAMBDA_INIT = 0.8 - 0.6 * math.exp(-0.3 * 0)

ROW_BLOCK = 512
ATT_BLOCK = 512
S5_CHUNK = 16
S5_GROUPS_PER_STEP = 8
S5_HALF_STEPS = 128 // SSM_GROUP
S5_HALF_WIDTH = S5_GROUPS_PER_STEP * S5_HALF_STEPS * SSM_GROUP
assert S5_GROUPS_PER_STEP * SSM_GROUP == 128 and S5_CHUNK % S5_HALF_STEPS == 0
MOE_TOKEN_BLOCK = 256
MOE_UNIT = 16
MOE_SLOT_BLOCK = 512
MOE_BLOCK_ROWS = -(-(MOE_TOKEN_BLOCK * TOP_K + N_EXPERTS * (MOE_UNIT - 1)) // MOE_SLOT_BLOCK) * MOE_SLOT_BLOCK
MOE_TYPICAL_ROWS = -(-(MOE_TOKEN_BLOCK * TOP_K + N_EXPERTS * MOE_UNIT // 2) // MOE_SLOT_BLOCK) * MOE_SLOT_BLOCK
NOT_ROUTED = -1e6
MIX_ROW_BLOCK = 256
NEG_BIG = -1e30
MAX_EXP_RANGE = 80.0
BOUND_MARGIN = 1.02
LANES = 128
SUBLANES = 8
VMEM_LIMIT = 56 * 2 ** 20


def _resident(shape, index_map):
    return pl.BlockSpec(shape, index_map, pipeline_mode=pl.Buffered(1))


def _inproj_body(x_ref, meta_ref, g_ref, w_ref, seg_ref, qg_ref, kg_ref, o_ref, u_ref, *, n_xblk):
    i = pl.program_id(0)

    def run(src_ref):
        xv = src_ref[...]
        ms = jnp.mean(xv * xv, axis=-1, keepdims=True)
        hn = (xv * lax.rsqrt(ms + EPS) * g_ref[...]).astype(BF16)
        sec = IN_WIDTH // 4
        for s in range(4):
            ps = jnp.dot(hn, w_ref[:, s * sec:(s + 1) * sec], preferred_element_type=F32)
            if s < 2:
                gain = qg_ref if s == 0 else kg_ref
                msq = jnp.dot((ps * ps).astype(BF16), seg_ref[...], preferred_element_type=F32)
                inv = lax.rsqrt(msq + EPS)
                inv_full = jnp.concatenate(
                    [jnp.broadcast_to(inv[:, c:c + 1], (inv.shape[0], HEAD_DIM))
                     for c in range(QK_WIDTH // HEAD_DIM)], axis=1)
                ps = ps * inv_full * gain[...]
            if s == 3:
                u_ref[...] = ps
                continue
            pb = ps.astype(BF16)
            if s < 2:
                o_ref[:, s * sec:(s + 1) * sec] = pb
            else:
                lane = lax.broadcasted_iota(jnp.int32, (pb.shape[0], V_DIM), 1)
                ones_col = jnp.where(lane == 0, 1.0, 0.0).astype(BF16)
                for hh in range(HEADS):
                    base = V_OFFSET + hh * 2 * V_DIM
                    o_ref[:, base:base + V_DIM] = pb[:, hh * V_DIM:(hh + 1) * V_DIM]
                    o_ref[:, base + V_DIM:base + 2 * V_DIM] = ones_col

    @pl.when(i < n_xblk)
    def _():
        run(x_ref)

    @pl.when(i == n_xblk)
    def _():
        run(meta_ref)


def _inproj(x2, meta_pad, gain, w_bf, seg, qg, kg):
    seq = x2.shape[0]
    n_xblk = seq // ROW_BLOCK
    rows = seq + ROW_BLOCK
    return pl.pallas_call(
        functools.partial(_inproj_body, n_xblk=n_xblk),
        grid=(n_xblk + 1,),
        in_specs=[
            pl.BlockSpec((ROW_BLOCK, D_MODEL), lambda i: (jnp.minimum(i, n_xblk - 1), 0)),
            _resident((ROW_BLOCK, D_MODEL), lambda i: (0, 0)),
            _resident((1, D_MODEL), lambda i: (0, 0)),
            _resident((D_MODEL, IN_WIDTH), lambda i: (0, 0)),
            _resident((QK_WIDTH, LANES), lambda i: (0, 0)),
            _resident((1, QK_WIDTH), lambda i: (0, 0)),
            _resident((1, QK_WIDTH), lambda i: (0, 0)),
        ],
        out_specs=[pl.BlockSpec((ROW_BLOCK, PROJ_WIDTH), lambda i: (i, 0)),
                   pl.BlockSpec((ROW_BLOCK, SSM_WIDTH), lambda i: (i, 0))],
        out_shape=[jax.ShapeDtypeStruct((rows, PROJ_WIDTH), BF16),
                   jax.ShapeDtypeStruct((rows, SSM_WIDTH), F32)],
        compiler_params=pltpu.CompilerParams(
            dimension_semantics=("arbitrary",), vmem_limit_bytes=VMEM_LIMIT),
        name="inproj",
    )(x2, meta_pad, gain, w_bf, seg, qg, kg)


def _t5_bias(rel, tab_ref, h):
    half = N_BUCKETS // 2
    exact = half // 2
    n = jnp.abs(rel)
    nf = jnp.maximum(n, 1).astype(F32)
    large = exact + (jnp.log(nf / exact) / math.log(MAX_DISTANCE / exact) * (half - exact)).astype(jnp.int32)
    large = jnp.minimum(large, half - 1)
    bucket = jnp.where(rel > 0, half, 0) + jnp.where(n < exact, n, large)
    out = jnp.zeros(rel.shape, F32)
    for b in range(N_BUCKETS):
        out = jnp.where(bucket == b, tab_ref[b, h], out)
    return out


def _attn_body(tab_ref, bound_ref, q_ref, k_ref, v_ref, lq1_ref, lk1_ref, lq2_ref, lk2_ref,
               subln_ref, o_ref, bias_ref, acc1_ref, acc2_ref, m1_ref, m2_ref, *, n_main):
    T = ATT_BLOCK
    h = pl.program_id(0)
    qi = pl.program_id(1)
    bound = bound_ref[0]

    @pl.when(qi == 0)
    def _():
        offsets = (-T, 0, T, -N_META, -N_META - T, -2 * T, 2 * T)
        for kind, off in enumerate(offsets):
            masked = kind in (3, 4)

            def rows(rc, carry, off=off, masked=masked, kind=kind):
                r0 = pl.multiple_of(rc * SUBLANES, SUBLANES)
                r = r0 + lax.broadcasted_iota(jnp.int32, (SUBLANES, T), 0)
                c = lax.broadcasted_iota(jnp.int32, (SUBLANES, T), 1)
                b = _t5_bias(off + c - r, tab_ref, h) - bound
                if masked:
                    b = jnp.where(c < N_META, b, NEG_BIG)
                bias_ref[kind, pl.ds(r0, SUBLANES), :] = b
                return carry

            lax.fori_loop(0, T // SUBLANES, rows, 0)

    acc1_ref[...] = jnp.zeros(acc1_ref.shape, F32)
    acc2_ref[...] = jnp.zeros(acc2_ref.shape, F32)

    q = q_ref[...]
    q1 = q[:, :HEAD_DIM]
    q2 = q[:, HEAD_DIM:]
    nt = (((1,), (1,)), ((), ()))

    def tile(ki):
        koff = pl.multiple_of(ki * T, T)
        kb = k_ref[pl.ds(koff, T), :]
        va = v_ref[pl.ds(koff, T), :]
        d = ki - qi
        kind = jnp.where(ki == n_main, jnp.where(qi == 0, 3, 4),
                         jnp.where(d <= -2, 5, jnp.where(d >= 2, 6, d + 1)))
        return kb, va, bias_ref[kind]

    def bounded_step(ki, carry):
        kb, va, bias = tile(ki)
        s1 = lax.dot_general(q1, kb[:, :HEAD_DIM], nt, preferred_element_type=F32) + bias
        acc1_ref[...] += jnp.dot(jnp.exp(s1).astype(BF16), va, preferred_element_type=F32)
        s2 = lax.dot_general(q2, kb[:, HEAD_DIM:], nt, preferred_element_type=F32) + bias
        acc2_ref[...] += jnp.dot(jnp.exp(s2).astype(BF16), va, preferred_element_type=F32)
        return carry

    def online_map(s, va, m_ref, acc_ref):
        m_old = m_ref[...]
        m_new = jnp.maximum(m_old, jnp.max(s, axis=-1, keepdims=True))
        p = jnp.exp(s - m_new).astype(BF16)
        acc_ref[...] = (jnp.exp(m_old - m_new) * acc_ref[...]
                        + jnp.dot(p, va, preferred_element_type=F32))
        m_ref[...] = m_new

    def online_step(ki, carry):
        kb, va, bias = tile(ki)
        s1 = lax.dot_general(q1, kb[:, :HEAD_DIM], nt, preferred_element_type=F32) + bias
        online_map(s1, va, m1_ref, acc1_ref)
        s2 = lax.dot_general(q2, kb[:, HEAD_DIM:], nt, preferred_element_type=F32) + bias
        online_map(s2, va, m2_ref, acc2_ref)
        return carry

    no_running_max = 2.0 * bound <= MAX_EXP_RANGE

    @pl.when(no_running_max)
    def _():
        unroll = next(u for u in (33, 11, 3, 2, 1) if (n_main + 1) % u == 0)
        lax.fori_loop(0, n_main + 1, bounded_step, 0, unroll=unroll)

    @pl.when(jnp.logical_not(no_running_max))
    def _():
        m1_ref[...] = jnp.full(m1_ref.shape, -jnp.inf, F32)
        m2_ref[...] = jnp.full(m2_ref.shape, -jnp.inf, F32)
        lax.fori_loop(0, n_main + 1, online_step, 0)

    lam = (jnp.exp(jnp.sum(lq1_ref[...] * lk1_ref[...], axis=-1, keepdims=True))
           - jnp.exp(jnp.sum(lq2_ref[...] * lk2_ref[...], axis=-1, keepdims=True))
           + LAMBDA_INIT)
    a1 = acc1_ref[...]
    a2 = acc2_ref[...]
    o = (a1[:, :V_DIM] / a1[:, V_DIM:V_DIM + 1]
         - lam * (a2[:, :V_DIM] / a2[:, V_DIM:V_DIM + 1]))
    ms = jnp.mean(o * o, axis=-1, keepdims=True)
    o = o * lax.rsqrt(ms + EPS) * subln_ref[...] * (1.0 - LAMBDA_INIT)
    o_ref[...] = o.astype(BF16)


def _attention(proj, rel_bias, score_bound, lq1, lk1, lq2, lk2, subln, seq):
    T = ATT_BLOCK
    n_main = seq // T
    rows = proj.shape[0]
    vec64 = lambda: _resident((1, HEAD_DIM), lambda h, qi: (0, 0))
    return pl.pallas_call(
        functools.partial(_attn_body, n_main=n_main),
        grid=(HEADS, n_main),
        in_specs=[
            pl.BlockSpec(memory_space=pltpu.SMEM),
            pl.BlockSpec(memory_space=pltpu.SMEM),
            pl.BlockSpec((T, 2 * HEAD_DIM), lambda h, qi: (qi, h)),
            pl.BlockSpec((rows, 2 * HEAD_DIM), lambda h, qi: (0, HEADS + h)),
            pl.BlockSpec((rows, 2 * V_DIM), lambda h, qi: (0, V_OFFSET // (2 * V_DIM) + h)),
            vec64(), vec64(), vec64(), vec64(),
            _resident((1, V_DIM), lambda h, qi: (0, 0)),
        ],
        out_specs=pl.BlockSpec((T, V_DIM), lambda h, qi: (qi, h)),
        out_shape=jax.ShapeDtypeStruct((seq, ATT_WIDTH), BF16),
        scratch_shapes=[
            pltpu.VMEM((7, T, T), F32),
            pltpu.VMEM((T, 2 * V_DIM), F32), pltpu.VMEM((T, 2 * V_DIM), F32),
            pltpu.VMEM((T, 1), F32), pltpu.VMEM((T, 1), F32),
        ],
        compiler_params=pltpu.CompilerParams(
            dimension_semantics=("arbitrary", "arbitrary"), vmem_limit_bytes=VMEM_LIMIT),
        name="diff_attention",
    )(rel_bias, score_bound, proj, proj, proj, lq1, lk1, lq2, lk2, subln)


def _lane_regroup_matrix():
    out_lane = jnp.arange(S5_HALF_WIDTH, dtype=jnp.int32)
    g8 = out_lane // LANES
    jj = (out_lane % LANES) // SSM_GROUP
    p = out_lane % SSM_GROUP
    in_lane = jj * LANES + g8 * SSM_GROUP + p
    return (jnp.arange(S5_HALF_WIDTH, dtype=jnp.int32)[:, None] == in_lane[None, :]).astype(BF16)


def _to_chunks_body(u_ref, perm_ref, o_ref):
    parts = []
    for half in range(S5_CHUNK // S5_HALF_STEPS):
        x = jnp.concatenate([u_ref[:, half * S5_HALF_STEPS + jj, :].astype(BF16)
                             for jj in range(S5_HALF_STEPS)], axis=1)
        parts.append(jnp.dot(x, perm_ref[...], preferred_element_type=F32))
    for g8 in range(S5_GROUPS_PER_STEP):
        o_ref[g8] = jnp.concatenate([r[:, g8 * LANES:(g8 + 1) * LANES] for r in parts], axis=1).astype(BF16)


def _to_chunks(u3, perm):
    n_rows = u3.shape[0]
    w = S5_CHUNK * SSM_GROUP
    return pl.pallas_call(
        _to_chunks_body,
        grid=(SSM_GROUPS // S5_GROUPS_PER_STEP,),
        in_specs=[pl.BlockSpec((n_rows, S5_CHUNK, LANES), lambda o: (0, 0, o)),
                  _resident((S5_HALF_WIDTH, S5_HALF_WIDTH), lambda o: (0, 0))],
        out_specs=pl.BlockSpec((S5_GROUPS_PER_STEP, n_rows, w), lambda o: (o, 0, 0)),
        out_shape=jax.ShapeDtypeStruct((SSM_GROUPS, n_rows, w), BF16),
        compiler_params=pltpu.CompilerParams(
            dimension_semantics=("arbitrary",), vmem_limit_bytes=VMEM_LIMIT),
        name="s5_to_chunks",
    )(u3, perm)


def _from_chunks_body(y_ref, perm_t_ref, o_ref):
    for half in range(S5_CHUNK // S5_HALF_STEPS):
        x = jnp.concatenate([y_ref[g8, :, half * LANES:(half + 1) * LANES]
                             for g8 in range(S5_GROUPS_PER_STEP)], axis=1)
        r = jnp.dot(x, perm_t_ref[...], preferred_element_type=F32)
        for jj in range(S5_HALF_STEPS):
            o_ref[:, half * S5_HALF_STEPS + jj, :] = r[:, jj * LANES:(jj + 1) * LANES]


def _from_chunks(y_chunks, perm_t):
    n_chunks = y_chunks.shape[1]
    w = S5_CHUNK * SSM_GROUP
    return pl.pallas_call(
        _from_chunks_body,
        grid=(SSM_GROUPS // S5_GROUPS_PER_STEP,),
        in_specs=[pl.BlockSpec((S5_GROUPS_PER_STEP, n_chunks, w), lambda o: (o, 0, 0)),
                  _resident((S5_HALF_WIDTH, S5_HALF_WIDTH), lambda o: (0, 0))],
        out_specs=pl.BlockSpec((n_chunks, S5_CHUNK, LANES), lambda o: (0, 0, o)),
        out_shape=jax.ShapeDtypeStruct((n_chunks, S5_CHUNK, SSM_WIDTH), F32),
        compiler_params=pltpu.CompilerParams(
            dimension_semantics=("arbitrary",), vmem_limit_bytes=VMEM_LIMIT),
        name="s5_from_chunks",
    )(y_chunks, perm_t)


def _s5_prep(a_re, a_im, log_step, b_re, b_im, c_re, c_im):
    C = S5_CHUNK
    dt = jnp.exp(log_step)[..., None]
    decay = jnp.exp(a_re * dt)
    ab_re = decay * jnp.cos(a_im * dt)
    ab_im = decay * jnp.sin(a_im * dt)
    den = a_re * a_re + a_im * a_im
    zr = ab_re - 1.0
    f_re = (zr * a_re + ab_im * a_im) / den
    f_im = (ab_im * a_re - zr * a_im) / den
    bb_re = f_re[..., None] * b_re - f_im[..., None] * b_im
    bb_im = f_re[..., None] * b_im + f_im[..., None] * b_re
    pr, pi = jnp.ones_like(ab_re), jnp.zeros_like(ab_re)
    pw_re, pw_im = [pr], [pi]
    for _ in range(C):
        pr, pi = pr * ab_re - pi * ab_im, pr * ab_im + pi * ab_re
        pw_re.append(pr)
        pw_im.append(pi)
    G, P, W = SSM_GROUPS, SSM_GROUP, C * SSM_GROUP
    pw_re = jnp.stack(pw_re, axis=-1)
    pw_im = jnp.stack(pw_im, axis=-1)
    ct_re = c_re.transpose(0, 1, 3, 2)
    ct_im = c_im.transpose(0, 1, 3, 2)
    cp_re = ct_re[:, :, :, None, :] * pw_re[..., None] - ct_im[:, :, :, None, :] * pw_im[..., None]
    cp_im = ct_re[:, :, :, None, :] * pw_im[..., None] + ct_im[:, :, :, None, :] * pw_re[..., None]
    bt_re = bb_re.transpose(0, 1, 3, 2)
    bt_im = bb_im.transpose(0, 1, 3, 2)

    def response(d):
        prod = (cp_re[d][:, None, :, :C, :] * bt_re[d][:, :, :, None, None]
                - cp_im[d][:, None, :, :C, :] * bt_im[d][:, :, :, None, None])
        return jnp.sum(prod, axis=2)

    ext_f = jnp.pad(response(0).reshape(G, P, W), ((0, 0), (0, 0), (W, 0)))
    ext_r = jnp.pad(response(1)[:, :, ::-1, :].reshape(G, P, W), ((0, 0), (0, 0), (0, W)))
    mm = jnp.stack([ext_f[:, :, W - P * j:2 * W - P * j] + ext_r[:, :, (C - 1 - j) * P:(C - 1 - j) * P + W]
                    for j in range(C)], axis=1).reshape(G, W, W)

    def in_mat(d, reverse_powers):
        pr_ = pw_re[d][:, :, :C].transpose(0, 2, 1)
        pi_ = pw_im[d][:, :, :C].transpose(0, 2, 1)
        if reverse_powers:
            pr_, pi_ = pr_[:, ::-1], pi_[:, ::-1]
        re = pr_[:, :, None, :] * bt_re[d][:, None] - pi_[:, :, None, :] * bt_im[d][:, None]
        im = pr_[:, :, None, :] * bt_im[d][:, None] + pi_[:, :, None, :] * bt_re[d][:, None]
        return re, im

    pf_re, pf_im = in_mat(0, True)
    pr_re, pr_im = in_mat(1, False)
    pp = jnp.concatenate([pf_re, pr_re, pf_im, pr_im], axis=-1).reshape(G, W, 4 * SSM_STATE)

    qq = jnp.concatenate([cp_re[0][:, :, 1:], cp_re[1][:, :, :0:-1],
                          -cp_im[0][:, :, 1:], -cp_im[1][:, :, :0:-1]], axis=1).reshape(G, 4 * SSM_STATE, W)
    lam16 = jnp.stack([jnp.concatenate([pw_re[0][:, :, C], pw_re[1][:, :, C]], axis=-1),
                       jnp.concatenate([pw_im[0][:, :, C], pw_im[1][:, :, C]], axis=-1)], axis=0)
    return pp.astype(BF16), mm.astype(BF16), qq.astype(BF16), lam16


def _s5_body(u_ref, pp_ref, mm_ref, qq_ref, lam_ref, y_ref,
             zre_ref, zim_ref, are_ref, aim_ref, bre_ref, bim_ref, *, n_chunks):
    GS = S5_GROUPS_PER_STEP
    NS = 2 * SSM_STATE
    for gi in range(GS):
        z = jnp.dot(u_ref[gi], pp_ref[gi], preferred_element_type=F32)
        zre_ref[:, gi, :] = z[:, :NS]
        zim_ref[:, gi, :] = z[:, NS:]

    ar = lam_ref[0]
    ai = lam_ref[1]
    fwd = lax.broadcasted_iota(jnp.int32, (GS, NS), 1) < SSM_STATE
    sre0 = jnp.where(fwd, zre_ref[n_chunks], 0.0)
    sim0 = jnp.where(fwd, zim_ref[n_chunks], 0.0)

    def scan_step(k, carry):
        sre, sim = carry
        kr = n_chunks - 1 - k
        are_ref[k] = sre
        aim_ref[k] = sim
        bre_ref[kr] = sre
        bim_ref[kr] = sim
        zr = jnp.where(fwd, zre_ref[k], zre_ref[kr])
        zi = jnp.where(fwd, zim_ref[k], zim_ref[kr])
        return ar * sre - ai * sim + zr, ar * sim + ai * sre + zi

    lax.fori_loop(0, n_chunks, scan_step, (sre0, sim0))

    fwd_rows = lax.broadcasted_iota(jnp.int32, (n_chunks, NS), 1) < SSM_STATE
    for gi in range(GS):
        s_re = jnp.where(fwd_rows, are_ref[:, gi, :], bre_ref[:, gi, :])
        s_im = jnp.where(fwd_rows, aim_ref[:, gi, :], bim_ref[:, gi, :])
        scat = jnp.concatenate([s_re, s_im], axis=1).astype(BF16)
        y = (jnp.dot(u_ref[gi, :n_chunks, :], mm_ref[gi], preferred_element_type=F32)
             + jnp.dot(scat, qq_ref[gi], preferred_element_type=F32))
        y_ref[gi] = y.astype(BF16)


def _s5(u_chunks, pp, mm, qq, lam16, n_chunks):
    GS = S5_GROUPS_PER_STEP
    n_rows = u_chunks.shape[1]
    W = S5_CHUNK * SSM_GROUP
    NS = 2 * SSM_STATE
    mat = lambda: pl.BlockSpec((GS, W, W), lambda g: (g, 0, 0))
    return pl.pallas_call(
        functools.partial(_s5_body, n_chunks=n_chunks),
        grid=(SSM_GROUPS // GS,),
        in_specs=[
            pl.BlockSpec((GS, n_rows, W), lambda g: (g, 0, 0)),
            mat(), mat(), mat(),
            pl.BlockSpec((2, GS, NS), lambda g: (0, g, 0)),
        ],
        out_specs=pl.BlockSpec((GS, n_chunks, W), lambda g: (g, 0, 0)),
        out_shape=jax.ShapeDtypeStruct((SSM_GROUPS, n_chunks, W), BF16),
        scratch_shapes=[pltpu.VMEM((n_rows, GS, NS), F32), pltpu.VMEM((n_rows, GS, NS), F32)]
        + [pltpu.VMEM((n_chunks, GS, NS), F32) for _ in range(4)],
        compiler_params=pltpu.CompilerParams(
            dimension_semantics=("arbitrary",), vmem_limit_bytes=VMEM_LIMIT),
        name="s5_chunked",
    )(u_chunks, pp, mm, qq, lam16)


def _mix_out_body(x_ref, att_ref, y_ref, u_ref, d_ref, wglu_ref, bglu_ref, sn_ref, wout_ref,
                  nf_ref, rwh_ref, rwl_ref, sgu_ref, sd_ref, h_ref, hn_ref, sc_ref):
    y = y_ref[...].astype(F32) + d_ref[...] * u_ref[...].astype(F32)
    y = jax.nn.gelu(y)
    gate = jax.nn.sigmoid(jnp.dot(y.astype(BF16), wglu_ref[...], preferred_element_type=F32) + bglu_ref[...])
    s = y * gate
    ms = jnp.mean(s * s, axis=-1, keepdims=True)
    ssm = (s * lax.rsqrt(ms + EPS) * sn_ref[...]).astype(BF16)
    mixed = jnp.concatenate([att_ref[...], ssm], axis=1)
    h = x_ref[...] + jnp.dot(mixed, wout_ref[...], preferred_element_type=F32)
    ms = jnp.mean(h * h, axis=-1, keepdims=True)
    hn = h * lax.rsqrt(ms + EPS) * nf_ref[...]
    hnb = hn.astype(BF16)
    hn_ref[...] = hnb
    hn_lo = (hn - hnb.astype(F32)).astype(BF16)
    nt = (((1,), (1,)), ((), ()))
    logits_t = (lax.dot_general(rwh_ref[...], hnb, nt, preferred_element_type=F32)
                + lax.dot_general(rwl_ref[...], hnb, nt, preferred_element_type=F32)
                + lax.dot_general(rwh_ref[...], hn_lo, nt, preferred_element_type=F32))
    sc_ref[...] = jax.nn.sigmoid(logits_t)
    gu = jnp.dot(hnb, sgu_ref[...], preferred_element_type=F32)
    act = (jax.nn.silu(gu[:, :SHARED_HIDDEN]) * gu[:, SHARED_HIDDEN:]).astype(BF16)
    h_ref[...] = h + jnp.dot(act, sd_ref[...], preferred_element_type=F32)


def _mix_out(x2, att, y, u, d_skip, w_glu, b_glu, ssm_norm, w_out, norm_ffn, router_wt_hi, router_wt_lo,
             sh_gate_up, sh_down):
    seq = x2.shape[0]
    R = MIX_ROW_BLOCK
    res = lambda shape: _resident(shape, lambda i: (0, 0))
    return pl.pallas_call(
        _mix_out_body,
        grid=(seq // R,),
        in_specs=[
            pl.BlockSpec((R, D_MODEL), lambda i: (i, 0)),
            pl.BlockSpec((R, ATT_WIDTH), lambda i: (i, 0)),
            pl.BlockSpec((R, SSM_WIDTH), lambda i: (i, 0)),
            pl.BlockSpec((R, SSM_WIDTH), lambda i: (i, 0)),
            res((1, SSM_WIDTH)), res((SSM_WIDTH, SSM_WIDTH)), res((1, SSM_WIDTH)), res((1, SSM_WIDTH)),
            res((D_MODEL, D_MODEL)), res((1, D_MODEL)), res((N_EXPERTS, D_MODEL)), res((N_EXPERTS, D_MODEL)),
            res((D_MODEL, 2 * SHARED_HIDDEN)), res((SHARED_HIDDEN, D_MODEL)),
        ],
        out_specs=[
            pl.BlockSpec((R, D_MODEL), lambda i: (i, 0)),
            pl.BlockSpec((R, D_MODEL), lambda i: (i, 0)),
            pl.BlockSpec((N_EXPERTS, R), lambda i: (0, i)),
        ],
        out_shape=[
            jax.ShapeDtypeStruct((seq, D_MODEL), F32),
            jax.ShapeDtypeStruct((seq, D_MODEL), BF16),
            jax.ShapeDtypeStruct((N_EXPERTS, seq), F32),
        ],
        compiler_params=pltpu.CompilerParams(
            dimension_semantics=("arbitrary",), vmem_limit_bytes=VMEM_LIMIT),
        name="mix_out_shared",
    )(x2, att, y, u, d_skip, w_glu, b_glu, ssm_norm, w_out, norm_ffn, router_wt_hi, router_wt_lo,
      sh_gate_up, sh_down)


def _route_body(sc_ref, rb_ref, tri_ref, w_ref, slot_ref, cnt_ref):
    scores = sc_ref[...]
    R = scores.shape[1]
    per_group = N_EXPERTS // N_EXPERT_GROUPS
    choice = scores + rb_ref[...]
    c3 = choice.reshape(N_EXPERT_GROUPS, per_group, R)
    within = lax.broadcasted_iota(jnp.int32, c3.shape, 1)
    m1 = jnp.max(c3, axis=1, keepdims=True)
    first = jnp.min(jnp.where(c3 == m1, within, per_group), axis=1, keepdims=True)
    m2 = jnp.max(jnp.where(within == first, -jnp.inf, c3), axis=1, keepdims=True)
    grp = (m1 + m2).reshape(N_EXPERT_GROUPS, R)
    gidx = lax.broadcasted_iota(jnp.int32, grp.shape, 0)
    grank = jnp.zeros(grp.shape, jnp.int32)
    for b in range(N_EXPERT_GROUPS):
        gb = grp[b:b + 1, :]
        grank += ((gb > grp) | ((gb == grp) & (b < gidx))).astype(jnp.int32)
    gmask = grank < TOPK_GROUPS
    emask = jnp.broadcast_to(gmask[:, None, :], c3.shape).reshape(N_EXPERTS, R)
    val = jnp.where(emask, choice, -jnp.inf)
    eidx = lax.broadcasted_iota(jnp.int32, val.shape, 0)
    rank = jnp.zeros(val.shape, jnp.int32)
    for e in range(N_EXPERTS):
        ve = val[e:e + 1, :]
        rank += ((ve > val) | ((ve == val) & (e < eidx))).astype(jnp.int32)
    sel = rank < TOP_K
    w = jnp.where(sel, scores, 0.0)
    w_ref[...] = w / jnp.sum(w, axis=0, keepdims=True) * ROUTED_SCALE
    cum = jnp.dot(jnp.where(sel, 1.0, 0.0).astype(BF16), tri_ref[...], preferred_element_type=F32)
    slot_ref[...] = jnp.where(sel, cum - 1.0, NOT_ROUTED)
    cnt_ref[0] = jnp.broadcast_to(cum[:, R - 1:R], (N_EXPERTS, LANES))


def _route(scores_t, router_bias, tri):
    seq = scores_t.shape[1]
    R = MOE_TOKEN_BLOCK
    n_blk = seq // R
    blk = pl.BlockSpec((N_EXPERTS, R), lambda i: (0, i))
    return pl.pallas_call(
        _route_body,
        grid=(n_blk,),
        in_specs=[blk, _resident((N_EXPERTS, 1), lambda i: (0, 0)), _resident((R, R), lambda i: (0, 0))],
        out_specs=[blk, blk, pl.BlockSpec((1, N_EXPERTS, LANES), lambda i: (i, 0, 0))],
        out_shape=[jax.ShapeDtypeStruct((N_EXPERTS, seq), F32),
                   jax.ShapeDtypeStruct((N_EXPERTS, seq), F32),
                   jax.ShapeDtypeStruct((n_blk, N_EXPERTS, LANES), F32)],
        compiler_params=pltpu.CompilerParams(dimension_semantics=("arbitrary",)),
        name="route",
    )(scores_t, router_bias, tri)


def _moe_tables(cnt):
    n_blk = cnt.shape[0]
    U, RB, SB = MOE_UNIT, MOE_BLOCK_ROWS, MOE_SLOT_BLOCK
    NU = SB // U
    E = N_EXPERTS
    pc = (cnt + U - 1) // U * U
    off = jnp.cumsum(pc, axis=1) - pc
    upc_t = (pc // U).T
    cum_t = jnp.cumsum(upc_t, axis=1)
    units_e = cum_t[:, -1]
    nblk_e = (units_e + NU - 1) // NU
    blk_end = jnp.cumsum(nblk_e)
    blk_start = blk_end - nblk_e
    n_act = blk_end[-1]
    max_blocks = n_blk * (RB // SB) + E
    i = jnp.arange(max_blocks, dtype=jnp.int32)
    active = i < n_act
    count_le = lambda edges, v: jnp.sum((edges <= v[..., None]).astype(jnp.int32), axis=-1)
    last_e = jnp.minimum(count_le(blk_end, n_act - 1), E - 1)
    be = jnp.where(active, jnp.minimum(count_le(blk_end[None, :], i), E - 1), last_e)
    oh_e = be[:, None] == jnp.arange(E, dtype=jnp.int32)[None, :]
    pick_e = lambda v: jnp.sum(jnp.where(oh_e, v[None, :], 0), axis=1)
    pick_e2 = lambda m: jnp.sum(jnp.where(oh_e[:, :, None], m[None, :, :], 0), axis=1)
    bstart_i = pick_e(blk_start)
    first = active & (i == bstart_i)
    local = (i - bstart_i)[:, None] * NU + jnp.arange(NU, dtype=jnp.int32)[None, :]
    valid = active[:, None] & (local < pick_e(units_e)[:, None])
    cum_i = pick_e2(cum_t)
    b_of = jnp.minimum(count_le(cum_i[:, None, :], local), n_blk - 1)
    oh_b = b_of[:, :, None] == jnp.arange(n_blk, dtype=jnp.int32)[None, None, :]
    pick_b = lambda m: jnp.sum(jnp.where(oh_b, m[:, None, :], 0), axis=2)
    seg_start = pick_b(cum_i) - pick_b(pick_e2(upc_t))
    unit = (b_of * RB + pick_b(pick_e2(off.T))) // U + (local - seg_start)
    spare = n_blk * RB // U
    src = jnp.where(valid, unit, spare)
    dst = jnp.where(valid, unit,
                    spare + (1 + i % 2)[:, None] * NU + jnp.arange(NU, dtype=jnp.int32)[None, :])
    short = active & (pick_e(units_e) - (i - bstart_i) * NU <= NU // 2)
    i32 = lambda a: a.astype(jnp.int32)
    flags = i32(first) + 2 * i32(short)
    return (off, pc, i32(src.reshape(-1)), i32(dst.reshape(-1)), i32(be), flags,
            i32(n_act.reshape(1)))


def _split_hi_lo(pos):
    hi = jnp.floor(pos * (1.0 / 64.0))
    return hi.astype(BF16), (pos - 64.0 * hi).astype(BF16)


def _dispatch_body(used_ref, hn_ref, slot_ref, offc_ref, offl_ref, pcl_ref, xs_ref, *, n_blk):
    b = pl.program_id(0)
    RB, TB = MOE_BLOCK_ROWS, MOE_TOKEN_BLOCK
    used = used_ref[jnp.minimum(b, n_blk - 1)]

    @pl.when(b == n_blk)
    def _():
        xs_ref[...] = jnp.zeros(xs_ref.shape, BF16)

    def scatter_rows(rows):
        pos = slot_ref[...] + offc_ref[0]
        hi, lo = _split_hi_lo(pos)
        r = lax.broadcasted_iota(jnp.int32, (rows, N_EXPERTS), 0).astype(F32)
        off = offl_ref[0]
        owner = jnp.where((r >= off) & (r < off + pcl_ref[0]), 1.0, 0.0).astype(BF16)
        p = (64.0 * jnp.dot(owner, hi, preferred_element_type=F32)
             + jnp.dot(owner, lo, preferred_element_type=F32))
        rr = lax.broadcasted_iota(jnp.int32, (rows, TB), 0).astype(F32)
        onehot = jnp.where(jnp.abs(p - rr) < 0.5, 1.0, 0.0).astype(BF16)
        x = hn_ref[...]
        C = MOE_SLOT_BLOCK
        for c in range(rows // C):
            xs_ref[0, c * C:(c + 1) * C, :] = jnp.dot(
                onehot[c * C:(c + 1) * C], x, preferred_element_type=F32).astype(BF16)
        if rows < RB:
            xs_ref[0, rows:, :] = jnp.zeros((RB - rows, D_MODEL), BF16)

    @pl.when((b < n_blk) & (used <= MOE_TYPICAL_ROWS))
    def _():
        scatter_rows(MOE_TYPICAL_ROWS)

    @pl.when((b < n_blk) & (used > MOE_TYPICAL_ROWS))
    def _():
        scatter_rows(RB)


def _dispatch(used, hn, slots, off, pc):
    n_blk = off.shape[0]
    RB, TB = MOE_BLOCK_ROWS, MOE_TOKEN_BLOCK
    clamp = lambda b: jnp.minimum(b, n_blk - 1)
    return pl.pallas_call(
        functools.partial(_dispatch_body, n_blk=n_blk),
        grid=(n_blk + 1,),
        in_specs=[
            pl.BlockSpec(memory_space=pltpu.SMEM),
            pl.BlockSpec((TB, D_MODEL), lambda b: (clamp(b), 0)),
            pl.BlockSpec((N_EXPERTS, TB), lambda b: (0, clamp(b))),
            pl.BlockSpec((1, N_EXPERTS, 1), lambda b: (clamp(b), 0, 0)),
            pl.BlockSpec((1, 1, N_EXPERTS), lambda b: (clamp(b), 0, 0)),
            pl.BlockSpec((1, 1, N_EXPERTS), lambda b: (clamp(b), 0, 0)),
        ],
        out_specs=pl.BlockSpec((1, RB, D_MODEL), lambda b: (b, 0, 0)),
        out_shape=jax.ShapeDtypeStruct((n_blk + 1, RB, D_MODEL), BF16),
        compiler_params=pltpu.CompilerParams(
            dimension_semantics=("arbitrary",), vmem_limit_bytes=VMEM_LIMIT),
        name="moe_dispatch",
    )(used, hn, slots, off[:, :, None], off[:, None, :], pc[:, None, :])


def _expert_body(src_ref, dst_ref, bexp_ref, first_ref, nact_ref,
                 xs_hbm, wg_ref, wu_ref, wd_ref, ys_hbm,
                 xbuf, ybuf, wgub, wdb, sem_in, sem_out):
    del bexp_ref
    i = pl.program_id(0)
    n_act = nact_ref[0]
    cur = lax.rem(i, 2)
    U = MOE_UNIT
    NU = MOE_SLOT_BLOCK // U

    def in_copy(blk, buf, u):
        return pltpu.make_async_copy(xs_hbm.at[src_ref[blk * NU + u]],
                                     xbuf.at[buf, pl.ds(u * U, U)], sem_in.at[buf])

    def out_copy(blk, buf, u):
        return pltpu.make_async_copy(ybuf.at[buf, pl.ds(u * U, U)],
                                     ys_hbm.at[dst_ref[blk * NU + u]], sem_out.at[buf])

    @pl.when(i == 0)
    def _():
        for u in range(NU):
            in_copy(0, 0, u).start()

    @pl.when(i < n_act)
    def _():
        @pl.when(i + 1 < n_act)
        def _():
            for u in range(NU):
                in_copy(i + 1, 1 - cur, u).start()

        @pl.when(first_ref[i] % 2 == 1)
        def _():
            wgub[:, :EXPERT_HIDDEN] = wg_ref[0].astype(BF16)
            wgub[:, EXPERT_HIDDEN:] = wu_ref[0].astype(BF16)
            wdb[...] = wd_ref[0].astype(BF16)

        for u in range(NU):
            in_copy(i, cur, u).wait()

        @pl.when(i >= 2)
        def _():
            for u in range(NU):
                out_copy(i - 2, cur, u).wait()

        def swiglu_rows(rows):
            x = xbuf[cur, :rows]
            gu = jnp.dot(x, wgub[...], preferred_element_type=F32)
            act = (jax.nn.silu(gu[:, :EXPERT_HIDDEN]) * gu[:, EXPERT_HIDDEN:]).astype(BF16)
            ybuf[cur, :rows] = jnp.dot(act, wdb[...], preferred_element_type=F32).astype(BF16)

        half = MOE_SLOT_BLOCK // 2

        @pl.when(first_ref[i] < 2)
        def _():
            swiglu_rows(MOE_SLOT_BLOCK)

        @pl.when(first_ref[i] >= 2)
        def _():
            swiglu_rows(half)
            ybuf[cur, half:] = jnp.zeros((half, D_MODEL), BF16)

        for u in range(NU):
            out_copy(i, cur, u).start()

        @pl.when(i == n_act - 1)
        def _():
            for u in range(NU):
                out_copy(i, cur, u).wait()

            @pl.when(i >= 1)
            def _():
                for u in range(NU):
                    out_copy(i - 1, 1 - cur, u).wait()


def _experts(xs, src, dst, bexp, first, n_act, wg, wu, wd):
    U, SB = MOE_UNIT, MOE_SLOT_BLOCK
    n_units = xs.shape[0] * xs.shape[1] // U
    max_blocks = bexp.shape[0]
    unit_view = lambda a: a.reshape(n_units, U, D_MODEL)
    wspec = lambda shape: pl.BlockSpec((1,) + shape, lambda i, src, dst, bexp, first, nact: (bexp[i], 0, 0))
    grid_spec = pltpu.PrefetchScalarGridSpec(
        num_scalar_prefetch=5,
        grid=(max_blocks,),
        in_specs=[
            pl.BlockSpec(memory_space=pl.ANY),
            wspec((D_MODEL, EXPERT_HIDDEN)), wspec((D_MODEL, EXPERT_HIDDEN)), wspec((EXPERT_HIDDEN, D_MODEL)),
        ],
        out_specs=pl.BlockSpec(memory_space=pl.ANY),
        scratch_shapes=[
            pltpu.VMEM((2, SB, D_MODEL), BF16), pltpu.VMEM((2, SB, D_MODEL), BF16),
            pltpu.VMEM((D_MODEL, 2 * EXPERT_HIDDEN), BF16), pltpu.VMEM((EXPERT_HIDDEN, D_MODEL), BF16),
            pltpu.SemaphoreType.DMA((2,)), pltpu.SemaphoreType.DMA((2,)),
        ],
    )
    ys = pl.pallas_call(
        _expert_body,
        grid_spec=grid_spec,
        out_shape=jax.ShapeDtypeStruct((n_units, U, D_MODEL), BF16),
        input_output_aliases={5: 0},
        compiler_params=pltpu.CompilerParams(
            dimension_semantics=("arbitrary",), vmem_limit_bytes=VMEM_LIMIT),
        name="moe_experts",
    )(src, dst, bexp, first, n_act, unit_view(xs), wg, wu, wd)
    return ys.reshape(xs.shape)


def _combine_body(used_ref, h_ref, ys_ref, slot_ref, w_ref, offl_ref, offc_ref, pcc_ref, o_ref):
    RB, TB = MOE_BLOCK_ROWS, MOE_TOKEN_BLOCK
    used = used_ref[pl.program_id(0)]

    def gather_rows(rows):
        pos = slot_ref[...] + offl_ref[0]
        hi, lo = _split_hi_lo(pos)
        r = lax.broadcasted_iota(jnp.int32, (N_EXPERTS, rows), 1).astype(F32)
        off = offc_ref[0]
        owner = jnp.where((r >= off) & (r < off + pcc_ref[0]), 1.0, 0.0).astype(BF16)
        p = (64.0 * jnp.dot(hi, owner, preferred_element_type=F32)
             + jnp.dot(lo, owner, preferred_element_type=F32))
        wr = jnp.dot(w_ref[...].astype(BF16), owner, preferred_element_type=F32)
        rr = lax.broadcasted_iota(jnp.int32, (TB, rows), 1).astype(F32)
        gather_w = jnp.where(jnp.abs(p - rr) < 0.5, wr, 0.0).astype(BF16)
        o_ref[...] = h_ref[...] + jnp.dot(gather_w, ys_ref[0, :rows, :], preferred_element_type=F32)

    @pl.when(used <= MOE_TYPICAL_ROWS)
    def _():
        gather_rows(MOE_TYPICAL_ROWS)

    @pl.when(used > MOE_TYPICAL_ROWS)
    def _():
        gather_rows(RB)


def _combine(used, h, ys, slots_tok, w_tok, off, pc):
    n_blk = off.shape[0]
    RB, TB = MOE_BLOCK_ROWS, MOE_TOKEN_BLOCK
    seq = h.shape[0]
    return pl.pallas_call(
        _combine_body,
        grid=(n_blk,),
        in_specs=[
            pl.BlockSpec(memory_space=pltpu.SMEM),
            pl.BlockSpec((TB, D_MODEL), lambda b: (b, 0)),
            pl.BlockSpec((1, RB, D_MODEL), lambda b: (b, 0, 0)),
            pl.BlockSpec((TB, N_EXPERTS), lambda b: (b, 0)),
            pl.BlockSpec((TB, N_EXPERTS), lambda b: (b, 0)),
            pl.BlockSpec((1, 1, N_EXPERTS), lambda b: (b, 0, 0)),
            pl.BlockSpec((1, N_EXPERTS, 1), lambda b: (b, 0, 0)),
            pl.BlockSpec((1, N_EXPERTS, 1), lambda b: (b, 0, 0)),
        ],
        out_specs=pl.BlockSpec((TB, D_MODEL), lambda b: (b, 0)),
        out_shape=jax.ShapeDtypeStruct((seq, D_MODEL), F32),
        compiler_params=pltpu.CompilerParams(
            dimension_semantics=("arbitrary",), vmem_limit_bytes=VMEM_LIMIT),
        name="moe_combine",
    )(used, h, ys, slots_tok, w_tok, off[:, None, :], off[:, :, None], pc[:, :, None])


def kernel(x, meta_tokens, rel_bias, norm_mix, w_in, q_norm, k_norm, lam_q1, lam_k1, lam_q2, lam_k2, subln, ssm_a_re, ssm_a_im, ssm_log_step, ssm_b_re, ssm_b_im, ssm_c_re, ssm_c_im, ssm_d, w_glu, b_glu, ssm_norm, w_out, norm_ffn, router_w, router_bias, w_gate, w_up, w_down, shared_gate, shared_up, shared_down):
    batch, seq, d = x.shape
    assert batch == 1 and d == D_MODEL and seq % ROW_BLOCK == 0 and seq % ATT_BLOCK == 0
    assert norm_mix.shape[0] == 1, "single layer"
    x2 = x.reshape(seq, d)
    meta_pad = jnp.zeros((ROW_BLOCK, d), F32).at[:N_META].set(meta_tokens.astype(F32))
    seg = ((jnp.arange(QK_WIDTH)[:, None] // HEAD_DIM == jnp.arange(LANES)[None, :])
           .astype(F32) / HEAD_DIM).astype(BF16)
    qg = jnp.tile(q_norm[0].astype(F32), QK_WIDTH // HEAD_DIM)[None] * (HEAD_DIM ** -0.5)
    kg = jnp.tile(k_norm[0].astype(F32), QK_WIDTH // HEAD_DIM)[None]

    proj, u = _inproj(x2, meta_pad, norm_mix[0][None], w_in[0].astype(BF16), seg, qg, kg)

    score_bound = (BOUND_MARGIN * HEAD_DIM ** 0.5 * jnp.max(jnp.abs(q_norm[0].astype(F32)))
                   * jnp.max(jnp.abs(k_norm[0].astype(F32)))
                   + jnp.max(jnp.abs(rel_bias.astype(F32)))).reshape(1)
    att = _attention(proj, rel_bias.astype(F32), score_bound, lam_q1[0][None], lam_k1[0][None],
                     lam_q2[0][None], lam_k2[0][None], subln[0][None], seq)

    n_rows = proj.shape[0] // S5_CHUNK
    n_chunks = seq // S5_CHUNK
    perm = _lane_regroup_matrix()
    u_chunks = _to_chunks(u.reshape(n_rows, S5_CHUNK, SSM_WIDTH), perm)
    pp, mm, qq, lam16 = _s5_prep(ssm_a_re[0].astype(F32), ssm_a_im[0].astype(F32),
                                 ssm_log_step[0].astype(F32), ssm_b_re[0].astype(F32),
                                 ssm_b_im[0].astype(F32), ssm_c_re[0].astype(F32),
                                 ssm_c_im[0].astype(F32))
    y_chunks = _s5(u_chunks, pp, mm, qq, lam16, n_chunks)
    y = _from_chunks(y_chunks, perm.T).reshape(seq, SSM_WIDTH)

    rw_t = router_w[0].astype(F32).T
    rw_hi = rw_t.astype(BF16)
    rw_lo = (rw_t - rw_hi.astype(F32)).astype(BF16)
    h, hn, scores_t = _mix_out(
        x2, att, y, u, ssm_d[0][None].astype(F32), w_glu[0].astype(BF16), b_glu[0][None].astype(F32),
        ssm_norm[0][None].astype(F32), w_out[0].astype(BF16), norm_ffn[0][None].astype(F32),
        rw_hi, rw_lo,
        jnp.concatenate([shared_gate[0], shared_up[0]], axis=1).astype(BF16), shared_down[0].astype(BF16))

    tb = MOE_TOKEN_BLOCK
    tri = (jnp.arange(tb)[:, None] <= jnp.arange(tb)[None, :]).astype(BF16)
    wts_t, slots_t, cnt = _route(scores_t, router_bias[0].astype(F32)[:, None], tri)
    off, pc, src, dst, bexp, first, n_act = _moe_tables(cnt[:, :, 0].astype(jnp.int32))
    off_f, pc_f = off.astype(F32), pc.astype(F32)
    used = jnp.sum(pc, axis=1).astype(jnp.int32)
    xs = _dispatch(used, hn, slots_t, off_f, pc_f)
    ys = _experts(xs, src, dst, bexp, first, n_act, w_gate[0], w_up[0], w_down[0])
    out = _combine(used, h, ys, slots_t.T, wts_t.T, off_f, pc_f)
    return out.reshape(batch, seq, d)
```

```python
import functools
import math

import jax
import jax.numpy as jnp
from jax import lax
from jax.experimental import pallas as pl
from jax.experimental.pallas import tpu as pltpu

F32 = jnp.float32
BF16 = jnp.bfloat16

D_MODEL = 2048
N_META = 16
ATT_WIDTH = 1024
SSM_WIDTH = 1024
HEAD_DIM = 64
V_DIM = 128
HEADS = 8
QK_WIDTH = 1024
IN_WIDTH = 4096
V_OFFSET = 2 * QK_WIDTH
PROJ_WIDTH = V_OFFSET + 2 * ATT_WIDTH
SSM_GROUP = 16
SSM_GROUPS = 64
SSM_STATE = 64
N_BUCKETS = 32
MAX_DISTANCE = 128
N_EXPERTS = 64
TOP_K = 8
N_EXPERT_GROUPS = 8
TOPK_GROUPS = 4
EXPERT_HIDDEN = 512
SHARED_HIDDEN = 512
ROUTED_SCALE = 2.5
EPS = 1e-6
LAMBDA_INIT = 0.8 - 0.6 * math.exp(-0.3 * 0)

ROW_BLOCK = 512
ATT_BLOCK = 512
S5_CHUNK = 16
S5_GROUPS_PER_STEP = 8
S5_HALF_STEPS = 128 // SSM_GROUP
S5_HALF_WIDTH = S5_GROUPS_PER_STEP * S5_HALF_STEPS * SSM_GROUP
assert S5_GROUPS_PER_STEP * SSM_GROUP == 128 and S5_CHUNK % S5_HALF_STEPS == 0
MOE_TOKEN_BLOCK = 256
MOE_UNIT = 16
MOE_SLOT_BLOCK = 512
MOE_BLOCK_ROWS = -(-(MOE_TOKEN_BLOCK * TOP_K + N_EXPERTS * (MOE_UNIT - 1)) // MOE_SLOT_BLOCK) * MOE_SLOT_BLOCK
MOE_TYPICAL_ROWS = -(-(MOE_TOKEN_BLOCK * TOP_K + N_EXPERTS * MOE_UNIT // 2) // MOE_SLOT_BLOCK) * MOE_SLOT_BLOCK
NOT_ROUTED = -1e6
MIX_ROW_BLOCK = 256
NEG_BIG = -1e30
MAX_EXP_RANGE = 80.0
BOUND_MARGIN = 1.02
LANES = 128
SUBLANES = 8
VMEM_LIMIT = 56 * 2 ** 20


def _resident(shape, index_map):
    return pl.BlockSpec(shape, index_map, pipeline_mode=pl.Buffered(1))


def _inproj_body(x_ref, meta_ref, g_ref, w_ref, seg_ref, qg_ref, kg_ref, o_ref, u_ref, *, n_xblk):
    i = pl.program_id(0)

    def run(src_ref):
        xv = src_ref[...]
        ms = jnp.mean(xv * xv, axis=-1, keepdims=True)
        hn = (xv * lax.rsqrt(ms + EPS) * g_ref[...]).astype(BF16)
        sec = IN_WIDTH // 4
        for s in range(4):
            ps = jnp.dot(hn, w_ref[:, s * sec:(s + 1) * sec], preferred_element_type=F32)
            if s < 2:
                gain = qg_ref if s == 0 else kg_ref
                msq = jnp.dot((ps * ps).astype(BF16), seg_ref[...], preferred_element_type=F32)
                inv = lax.rsqrt(msq + EPS)
                inv_full = jnp.concatenate(
                    [jnp.broadcast_to(inv[:, c:c + 1], (inv.shape[0], HEAD_DIM))
                     for c in range(QK_WIDTH // HEAD_DIM)], axis=1)
                ps = ps * inv_full * gain[...]
            if s == 3:
                u_ref[...] = ps
                continue
            pb = ps.astype(BF16)
            if s < 2:
                o_ref[:, s * sec:(s + 1) * sec] = pb
            else:
                lane = lax.broadcasted_iota(jnp.int32, (pb.shape[0], V_DIM), 1)
                ones_col = jnp.where(lane == 0, 1.0, 0.0).astype(BF16)
                for hh in range(HEADS):
                    base = V_OFFSET + hh * 2 * V_DIM
                    o_ref[:, base:base + V_DIM] = pb[:, hh * V_DIM:(hh + 1) * V_DIM]
                    o_ref[:, base + V_DIM:base + 2 * V_DIM] = ones_col

    @pl.when(i < n_xblk)
    def _():
        run(x_ref)

    @pl.when(i == n_xblk)
    def _():
        run(meta_ref)


def _inproj(x2, meta_pad, gain, w_bf, seg, qg, kg):
    seq = x2.shape[0]
    n_xblk = seq // ROW_BLOCK
    rows = seq + ROW_BLOCK
    return pl.pallas_call(
        functools.partial(_inproj_body, n_xblk=n_xblk),
        grid=(n_xblk + 1,),
        in_specs=[
            pl.BlockSpec((ROW_BLOCK, D_MODEL), lambda i: (jnp.minimum(i, n_xblk - 1), 0)),
            _resident((ROW_BLOCK, D_MODEL), lambda i: (0, 0)),
            _resident((1, D_MODEL), lambda i: (0, 0)),
            _resident((D_MODEL, IN_WIDTH), lambda i: (0, 0)),
            _resident((QK_WIDTH, LANES), lambda i: (0, 0)),
            _resident((1, QK_WIDTH), lambda i: (0, 0)),
            _resident((1, QK_WIDTH), lambda i: (0, 0)),
        ],
        out_specs=[pl.BlockSpec((ROW_BLOCK, PROJ_WIDTH), lambda i: (i, 0)),
                   pl.BlockSpec((ROW_BLOCK, SSM_WIDTH), lambda i: (i, 0))],
        out_shape=[jax.ShapeDtypeStruct((rows, PROJ_WIDTH), BF16),
                   jax.ShapeDtypeStruct((rows, SSM_WIDTH), F32)],
        compiler_params=pltpu.CompilerParams(
            dimension_semantics=("arbitrary",), vmem_limit_bytes=VMEM_LIMIT),
        name="inproj",
    )(x2, meta_pad, gain, w_bf, seg, qg, kg)


def _t5_bias(rel, tab_ref, h):
    half = N_BUCKETS // 2
    exact = half // 2
    n = jnp.abs(rel)
    nf = jnp.maximum(n, 1).astype(F32)
    large = exact + (jnp.log(nf / exact) / math.log(MAX_DISTANCE / exact) * (half - exact)).astype(jnp.int32)
    large = jnp.minimum(large, half - 1)
    bucket = jnp.where(rel > 0, half, 0) + jnp.where(n < exact, n, large)
    out = jnp.zeros(rel.shape, F32)
    for b in range(N_BUCKETS):
        out = jnp.where(bucket == b, tab_ref[b, h], out)
    return out


def _attn_body(tab_ref, bound_ref, q_ref, k_ref, v_ref, lq1_ref, lk1_ref, lq2_ref, lk2_ref,
               subln_ref, o_ref, bias_ref, acc1_ref, acc2_ref, m1_ref, m2_ref, *, n_main):
    T = ATT_BLOCK
    h = pl.program_id(0)
    qi = pl.program_id(1)
    bound = bound_ref[0]

    @pl.when(qi == 0)
    def _():
        offsets = (-T, 0, T, -N_META, -N_META - T, -2 * T, 2 * T)
        for kind, off in enumerate(offsets):
            masked = kind in (3, 4)

            def rows(rc, carry, off=off, masked=masked, kind=kind):
                r0 = pl.multiple_of(rc * SUBLANES, SUBLANES)
                r = r0 + lax.broadcasted_iota(jnp.int32, (SUBLANES, T), 0)
                c = lax.broadcasted_iota(jnp.int32, (SUBLANES, T), 1)
                b = _t5_bias(off + c - r, tab_ref, h) - bound
                if masked:
                    b = jnp.where(c < N_META, b, NEG_BIG)
                bias_ref[kind, pl.ds(r0, SUBLANES), :] = b
                return carry

            lax.fori_loop(0, T // SUBLANES, rows, 0)

    acc1_ref[...] = jnp.zeros(acc1_ref.shape, F32)
    acc2_ref[...] = jnp.zeros(acc2_ref.shape, F32)

    q = q_ref[...]
    q1 = q[:, :HEAD_DIM]
    q2 = q[:, HEAD_DIM:]
    nt = (((1,), (1,)), ((), ()))

    def tile(ki):
        koff = pl.multiple_of(ki * T, T)
        kb = k_ref[pl.ds(koff, T), :]
        va = v_ref[pl.ds(koff, T), :]
        d = ki - qi
        kind = jnp.where(ki == n_main, jnp.where(qi == 0, 3, 4),
                         jnp.where(d <= -2, 5, jnp.where(d >= 2, 6, d + 1)))
        return kb, va, bias_ref[kind]

    def bounded_step(ki, carry):
        kb, va, bias = tile(ki)
        s1 = lax.dot_general(q1, kb[:, :HEAD_DIM], nt, preferred_element_type=F32) + bias
        acc1_ref[...] += jnp.dot(jnp.exp(s1).astype(BF16), va, preferred_element_type=F32)
        s2 = lax.dot_general(q2, kb[:, HEAD_DIM:], nt, preferred_element_type=F32) + bias
        acc2_ref[...] += jnp.dot(jnp.exp(s2).astype(BF16), va, preferred_element_type=F32)
        return carry

    def online_map(s, va, m_ref, acc_ref):
        m_old = m_ref[...]
        m_new = jnp.maximum(m_old, jnp.max(s, axis=-1, keepdims=True))
        p = jnp.exp(s - m_new).astype(BF16)
        acc_ref[...] = (jnp.exp(m_old - m_new) * acc_ref[...]
                        + jnp.dot(p, va, preferred_element_type=F32))
        m_ref[...] = m_new

    def online_step(ki, carry):
        kb, va, bias = tile(ki)
        s1 = lax.dot_general(q1, kb[:, :HEAD_DIM], nt, preferred_element_type=F32) + bias
        online_map(s1, va, m1_ref, acc1_ref)
        s2 = lax.dot_general(q2, kb[:, HEAD_DIM:], nt, preferred_element_type=F32) + bias
        online_map(s2, va, m2_ref, acc2_ref)
        return carry

    no_running_max = 2.0 * bound <= MAX_EXP_RANGE

    @pl.when(no_running_max)
    def _():
        unroll = next(u for u in (33, 11, 3, 2, 1) if (n_main + 1) % u == 0)
        lax.fori_loop(0, n_main + 1, bounded_step, 0, unroll=unroll)

    @pl.when(jnp.logical_not(no_running_max))
    def _():
        m1_ref[...] = jnp.full(m1_ref.shape, -jnp.inf, F32)
        m2_ref[...] = jnp.full(m2_ref.shape, -jnp.inf, F32)
        lax.fori_loop(0, n_main + 1, online_step, 0)

    lam = (jnp.exp(jnp.sum(lq1_ref[...] * lk1_ref[...], axis=-1, keepdims=True))
           - jnp.exp(jnp.sum(lq2_ref[...] * lk2_ref[...], axis=-1, keepdims=True))
           + LAMBDA_INIT)
    a1 = acc1_ref[...]
    a2 = acc2_ref[...]
    o = (a1[:, :V_DIM] / a1[:, V_DIM:V_DIM + 1]
         - lam * (a2[:, :V_DIM] / a2[:, V_DIM:V_DIM + 1]))
    ms = jnp.mean(o * o, axis=-1, keepdims=True)
    o = o * lax.rsqrt(ms + EPS) * subln_ref[...] * (1.0 - LAMBDA_INIT)
    o_ref[...] = o.astype(BF16)


def _attention(proj, rel_bias, score_bound, lq1, lk1, lq2, lk2, subln, seq):
    T = ATT_BLOCK
    n_main = seq // T
    rows = proj.shape[0]
    vec64 = lambda: _resident((1, HEAD_DIM), lambda h, qi: (0, 0))
    return pl.pallas_call(
        functools.partial(_attn_body, n_main=n_main),
        grid=(HEADS, n_main),
        in_specs=[
            pl.BlockSpec(memory_space=pltpu.SMEM),
            pl.BlockSpec(memory_space=pltpu.SMEM),
            pl.BlockSpec((T, 2 * HEAD_DIM), lambda h, qi: (qi, h)),
            pl.BlockSpec((rows, 2 * HEAD_DIM), lambda h, qi: (0, HEADS + h)),
            pl.BlockSpec((rows, 2 * V_DIM), lambda h, qi: (0, V_OFFSET // (2 * V_DIM) + h)),
            vec64(), vec64(), vec64(), vec64(),
            _resident((1, V_DIM), lambda h, qi: (0, 0)),
        ],
        out_specs=pl.BlockSpec((T, V_DIM), lambda h, qi: (qi, h)),
        out_shape=jax.ShapeDtypeStruct((seq, ATT_WIDTH), BF16),
        scratch_shapes=[
            pltpu.VMEM((7, T, T), F32),
            pltpu.VMEM((T, 2 * V_DIM), F32), pltpu.VMEM((T, 2 * V_DIM), F32),
            pltpu.VMEM((T, 1), F32), pltpu.VMEM((T, 1), F32),
        ],
        compiler_params=pltpu.CompilerParams(
            dimension_semantics=("arbitrary", "arbitrary"), vmem_limit_bytes=VMEM_LIMIT),
        name="diff_attention",
    )(rel_bias, score_bound, proj, proj, proj, lq1, lk1, lq2, lk2, subln)


def _lane_regroup_matrix():
    out_lane = jnp.arange(S5_HALF_WIDTH, dtype=jnp.int32)
    g8 = out_lane // LANES
    jj = (out_lane % LANES) // SSM_GROUP
    p = out_lane % SSM_GROUP
    in_lane = jj * LANES + g8 * SSM_GROUP + p
    return (jnp.arange(S5_HALF_WIDTH, dtype=jnp.int32)[:, None] == in_lane[None, :]).astype(BF16)


def _to_chunks_body(u_ref, perm_ref, o_ref):
    parts = []
    for half in range(S5_CHUNK // S5_HALF_STEPS):
        x = jnp.concatenate([u_ref[:, half * S5_HALF_STEPS + jj, :].astype(BF16)
                             for jj in range(S5_HALF_STEPS)], axis=1)
        parts.append(jnp.dot(x, perm_ref[...], preferred_element_type=F32))
    for g8 in range(S5_GROUPS_PER_STEP):
        o_ref[g8] = jnp.concatenate([r[:, g8 * LANES:(g8 + 1) * LANES] for r in parts], axis=1).astype(BF16)


def _to_chunks(u3, perm):
    n_rows = u3.shape[0]
    w = S5_CHUNK * SSM_GROUP
    return pl.pallas_call(
        _to_chunks_body,
        grid=(SSM_GROUPS // S5_GROUPS_PER_STEP,),
        in_specs=[pl.BlockSpec((n_rows, S5_CHUNK, LANES), lambda o: (0, 0, o)),
                  _resident((S5_HALF_WIDTH, S5_HALF_WIDTH), lambda o: (0, 0))],
        out_specs=pl.BlockSpec((S5_GROUPS_PER_STEP, n_rows, w), lambda o: (o, 0, 0)),
        out_shape=jax.ShapeDtypeStruct((SSM_GROUPS, n_rows, w), BF16),
        compiler_params=pltpu.CompilerParams(
            dimension_semantics=("arbitrary",), vmem_limit_bytes=VMEM_LIMIT),
        name="s5_to_chunks",
    )(u3, perm)


def _from_chunks_body(y_ref, perm_t_ref, o_ref):
    for half in range(S5_CHUNK // S5_HALF_STEPS):
        x = jnp.concatenate([y_ref[g8, :, half * LANES:(half + 1) * LANES]
                             for g8 in range(S5_GROUPS_PER_STEP)], axis=1)
        r = jnp.dot(x, perm_t_ref[...], preferred_element_type=F32)
        for jj in range(S5_HALF_STEPS):
            o_ref[:, half * S5_HALF_STEPS + jj, :] = r[:, jj * LANES:(jj + 1) * LANES]


def _from_chunks(y_chunks, perm_t):
    n_chunks = y_chunks.shape[1]
    w = S5_CHUNK * SSM_GROUP
    return pl.pallas_call(
        _from_chunks_body,
        grid=(SSM_GROUPS // S5_GROUPS_PER_STEP,),
        in_specs=[pl.BlockSpec((S5_GROUPS_PER_STEP, n_chunks, w), lambda o: (o, 0, 0)),
                  _resident((S5_HALF_WIDTH, S5_HALF_WIDTH), lambda o: (0, 0))],
        out_specs=pl.BlockSpec((n_chunks, S5_CHUNK, LANES), lambda o: (0, 0, o)),
        out_shape=jax.ShapeDtypeStruct((n_chunks, S5_CHUNK, SSM_WIDTH), F32),
        compiler_params=pltpu.CompilerParams(
            dimension_semantics=("arbitrary",), vmem_limit_bytes=VMEM_LIMIT),
        name="s5_from_chunks",
    )(y_chunks, perm_t)


def _s5_prep(a_re, a_im, log_step, b_re, b_im, c_re, c_im):
    C = S5_CHUNK
    dt = jnp.exp(log_step)[..., None]
    decay = jnp.exp(a_re * dt)
    ab_re = decay * jnp.cos(a_im * dt)
    ab_im = decay * jnp.sin(a_im * dt)
    den = a_re * a_re + a_im * a_im
    zr = ab_re - 1.0
    f_re = (zr * a_re + ab_im * a_im) / den
    f_im = (ab_im * a_re - zr * a_im) / den
    bb_re = f_re[..., None] * b_re - f_im[..., None] * b_im
    bb_im = f_re[..., None] * b_im + f_im[..., None] * b_re
    pr, pi = jnp.ones_like(ab_re), jnp.zeros_like(ab_re)
    pw_re, pw_im = [pr], [pi]
    for _ in range(C):
        pr, pi = pr * ab_re - pi * ab_im, pr * ab_im + pi * ab_re
        pw_re.append(pr)
        pw_im.append(pi)
    G, P, W = SSM_GROUPS, SSM_GROUP, C * SSM_GROUP
    pw_re = jnp.stack(pw_re, axis=-1)
    pw_im = jnp.stack(pw_im, axis=-1)
    ct_re = c_re.transpose(0, 1, 3, 2)
    ct_im = c_im.transpose(0, 1, 3, 2)
    cp_re = ct_re[:, :, :, None, :] * pw_re[..., None] - ct_im[:, :, :, None, :] * pw_im[..., None]
    cp_im = ct_re[:, :, :, None, :] * pw_im[..., None] + ct_im[:, :, :, None, :] * pw_re[..., None]
    bt_re = bb_re.transpose(0, 1, 3, 2)
    bt_im = bb_im.transpose(0, 1, 3, 2)

    def response(d):
        prod = (cp_re[d][:, None, :, :C, :] * bt_re[d][:, :, :, None, None]
                - cp_im[d][:, None, :, :C, :] * bt_im[d][:, :, :, None, None])
        return jnp.sum(prod, axis=2)

    ext_f = jnp.pad(response(0).reshape(G, P, W), ((0, 0), (0, 0), (W, 0)))
    ext_r = jnp.pad(response(1)[:, :, ::-1, :].reshape(G, P, W), ((0, 0), (0, 0), (0, W)))
    mm = jnp.stack([ext_f[:, :, W - P * j:2 * W - P * j] + ext_r[:, :, (C - 1 - j) * P:(C - 1 - j) * P + W]
                    for j in range(C)], axis=1).reshape(G, W, W)

    def in_mat(d, reverse_powers):
        pr_ = pw_re[d][:, :, :C].transpose(0, 2, 1)
        pi_ = pw_im[d][:, :, :C].transpose(0, 2, 1)
        if reverse_powers:
            pr_, pi_ = pr_[:, ::-1], pi_[:, ::-1]
        re = pr_[:, :, None, :] * bt_re[d][:, None] - pi_[:, :, None, :] * bt_im[d][:, None]
        im = pr_[:, :, None, :] * bt_im[d][:, None] + pi_[:, :, None, :] * bt_re[d][:, None]
        return re, im

    pf_re, pf_im = in_mat(0, True)
    pr_re, pr_im = in_mat(1, False)
    pp = jnp.concatenate([pf_re, pr_re, pf_im, pr_im], axis=-1).reshape(G, W, 4 * SSM_STATE)

    qq = jnp.concatenate([cp_re[0][:, :, 1:], cp_re[1][:, :, :0:-1],
                          -cp_im[0][:, :, 1:], -cp_im[1][:, :, :0:-1]], axis=1).reshape(G, 4 * SSM_STATE, W)
    lam16 = jnp.stack([jnp.concatenate([pw_re[0][:, :, C], pw_re[1][:, :, C]], axis=-1),
                       jnp.concatenate([pw_im[0][:, :, C], pw_im[1][:, :, C]], axis=-1)], axis=0)
    return pp.astype(BF16), mm.astype(BF16), qq.astype(BF16), lam16


def _s5_body(u_ref, pp_ref, mm_ref, qq_ref, lam_ref, y_ref,
             zre_ref, zim_ref, are_ref, aim_ref, bre_ref, bim_ref, *, n_chunks):
    GS = S5_GROUPS_PER_STEP
    NS = 2 * SSM_STATE
    for gi in range(GS):
        z = jnp.dot(u_ref[gi], pp_ref[gi], preferred_element_type=F32)
        zre_ref[:, gi, :] = z[:, :NS]
        zim_ref[:, gi, :] = z[:, NS:]

    ar = lam_ref[0]
    ai = lam_ref[1]
    fwd = lax.broadcasted_iota(jnp.int32, (GS, NS), 1) < SSM_STATE
    sre0 = jnp.where(fwd, zre_ref[n_chunks], 0.0)
    sim0 = jnp.where(fwd, zim_ref[n_chunks], 0.0)

    def scan_step(k, carry):
        sre, sim = carry
        kr = n_chunks - 1 - k
        are_ref[k] = sre
        aim_ref[k] = sim
        bre_ref[kr] = sre
        bim_ref[kr] = sim
        zr = jnp.where(fwd, zre_ref[k], zre_ref[kr])
        zi = jnp.where(fwd, zim_ref[k], zim_ref[kr])
        return ar * sre - ai * sim + zr, ar * sim + ai * sre + zi

    lax.fori_loop(0, n_chunks, scan_step, (sre0, sim0))

    fwd_rows = lax.broadcasted_iota(jnp.int32, (n_chunks, NS), 1) < SSM_STATE
    for gi in range(GS):
        s_re = jnp.where(fwd_rows, are_ref[:, gi, :], bre_ref[:, gi, :])
        s_im = jnp.where(fwd_rows, aim_ref[:, gi, :], bim_ref[:, gi, :])
        scat = jnp.concatenate([s_re, s_im], axis=1).astype(BF16)
        y = (jnp.dot(u_ref[gi, :n_chunks, :], mm_ref[gi], preferred_element_type=F32)
             + jnp.dot(scat, qq_ref[gi], preferred_element_type=F32))
        y_ref[gi] = y.astype(BF16)


def _s5(u_chunks, pp, mm, qq, lam16, n_chunks):
    GS = S5_GROUPS_PER_STEP
    n_rows = u_chunks.shape[1]
    W = S5_CHUNK * SSM_GROUP
    NS = 2 * SSM_STATE
    mat = lambda: pl.BlockSpec((GS, W, W), lambda g: (g, 0, 0))
    return pl.pallas_call(
        functools.partial(_s5_body, n_chunks=n_chunks),
        grid=(SSM_GROUPS // GS,),
        in_specs=[
            pl.BlockSpec((GS, n_rows, W), lambda g: (g, 0, 0)),
            mat(), mat(), mat(),
            pl.BlockSpec((2, GS, NS), lambda g: (0, g, 0)),
        ],
        out_specs=pl.BlockSpec((GS, n_chunks, W), lambda g: (g, 0, 0)),
        out_shape=jax.ShapeDtypeStruct((SSM_GROUPS, n_chunks, W), BF16),
        scratch_shapes=[pltpu.VMEM((n_rows, GS, NS), F32), pltpu.VMEM((n_rows, GS, NS), F32)]
        + [pltpu.VMEM((n_chunks, GS, NS), F32) for _ in range(4)],
        compiler_params=pltpu.CompilerParams(
            dimension_semantics=("arbitrary",), vmem_limit_bytes=VMEM_LIMIT),
        name="s5_chunked",
    )(u_chunks, pp, mm, qq, lam16)


def _mix_out_body(x_ref, att_ref, y_ref, u_ref, d_ref, wglu_ref, bglu_ref, sn_ref, wout_ref,
                  nf_ref, rwh_ref, rwl_ref, sgu_ref, sd_ref, h_ref, hn_ref, sc_ref):
    y = y_ref[...].astype(F32) + d_ref[...] * u_ref[...].astype(F32)
    y = jax.nn.gelu(y)
    gate = jax.nn.sigmoid(jnp.dot(y.astype(BF16), wglu_ref[...], preferred_element_type=F32) + bglu_ref[...])
    s = y * gate
    ms = jnp.mean(s * s, axis=-1, keepdims=True)
    ssm = (s * lax.rsqrt(ms + EPS) * sn_ref[...]).astype(BF16)
    mixed = jnp.concatenate([att_ref[...], ssm], axis=1)
    h = x_ref[...] + jnp.dot(mixed, wout_ref[...], preferred_element_type=F32)
    ms = jnp.mean(h * h, axis=-1, keepdims=True)
    hn = h * lax.rsqrt(ms + EPS) * nf_ref[...]
    hnb = hn.astype(BF16)
    hn_ref[...] = hnb
    hn_lo = (hn - hnb.astype(F32)).astype(BF16)
    nt = (((1,), (1,)), ((), ()))
    logits_t = (lax.dot_general(rwh_ref[...], hnb, nt, preferred_element_type=F32)
                + lax.dot_general(rwl_ref[...], hnb, nt, preferred_element_type=F32)
                + lax.dot_general(rwh_ref[...], hn_lo, nt, preferred_element_type=F32))
    sc_ref[...] = jax.nn.sigmoid(logits_t)
    gu = jnp.dot(hnb, sgu_ref[...], preferred_element_type=F32)
    act = (jax.nn.silu(gu[:, :SHARED_HIDDEN]) * gu[:, SHARED_HIDDEN:]).astype(BF16)
    h_ref[...] = h + jnp.dot(act, sd_ref[...], preferred_element_type=F32)


def _mix_out(x2, att, y, u, d_skip, w_glu, b_glu, ssm_norm, w_out, norm_ffn, router_wt_hi, router_wt_lo,
             sh_gate_up, sh_down):
    seq = x2.shape[0]
    R = MIX_ROW_BLOCK
    res = lambda shape: _resident(shape, lambda i: (0, 0))
    return pl.pallas_call(
        _mix_out_body,
        grid=(seq // R,),
        in_specs=[
            pl.BlockSpec((R, D_MODEL), lambda i: (i, 0)),
            pl.BlockSpec((R, ATT_WIDTH), lambda i: (i, 0)),
            pl.BlockSpec((R, SSM_WIDTH), lambda i: (i, 0)),
            pl.BlockSpec((R, SSM_WIDTH), lambda i: (i, 0)),
            res((1, SSM_WIDTH)), res((SSM_WIDTH, SSM_WIDTH)), res((1, SSM_WIDTH)), res((1, SSM_WIDTH)),
            res((D_MODEL, D_MODEL)), res((1, D_MODEL)), res((N_EXPERTS, D_MODEL)), res((N_EXPERTS, D_MODEL)),
            res((D_MODEL, 2 * SHARED_HIDDEN)), res((SHARED_HIDDEN, D_MODEL)),
        ],
        out_specs=[
            pl.BlockSpec((R, D_MODEL), lambda i: (i, 0)),
            pl.BlockSpec((R, D_MODEL), lambda i: (i, 0)),
            pl.BlockSpec((N_EXPERTS, R), lambda i: (0, i)),
        ],
        out_shape=[
            jax.ShapeDtypeStruct((seq, D_MODEL), F32),
            jax.ShapeDtypeStruct((seq, D_MODEL), BF16),
            jax.ShapeDtypeStruct((N_EXPERTS, seq), F32),
        ],
        compiler_params=pltpu.CompilerParams(
            dimension_semantics=("arbitrary",), vmem_limit_bytes=VMEM_LIMIT),
        name="mix_out_shared",
    )(x2, att, y, u, d_skip, w_glu, b_glu, ssm_norm, w_out, norm_ffn, router_wt_hi, router_wt_lo,
      sh_gate_up, sh_down)


def _route_body(sc_ref, rb_ref, tri_ref, w_ref, slot_ref, cnt_ref):
    scores = sc_ref[...]
    R = scores.shape[1]
    per_group = N_EXPERTS // N_EXPERT_GROUPS
    choice = scores + rb_ref[...]
    c3 = choice.reshape(N_EXPERT_GROUPS, per_group, R)
    within = lax.broadcasted_iota(jnp.int32, c3.shape, 1)
    m1 = jnp.max(c3, axis=1, keepdims=True)
    first = jnp.min(jnp.where(c3 == m1, within, per_group), axis=1, keepdims=True)
    m2 = jnp.max(jnp.where(within == first, -jnp.inf, c3), axis=1, keepdims=True)
    grp = (m1 + m2).reshape(N_EXPERT_GROUPS, R)
    gidx = lax.broadcasted_iota(jnp.int32, grp.shape, 0)
    grank = jnp.zeros(grp.shape, jnp.int32)
    for b in range(N_EXPERT_GROUPS):
        gb = grp[b:b + 1, :]
        grank += ((gb > grp) | ((gb == grp) & (b < gidx))).astype(jnp.int32)
    gmask = grank < TOPK_GROUPS
    emask = jnp.broadcast_to(gmask[:, None, :], c3.shape).reshape(N_EXPERTS, R)
    val = jnp.where(emask, choice, -jnp.inf)
    eidx = lax.broadcasted_iota(jnp.int32, val.shape, 0)
    rank = jnp.zeros(val.shape, jnp.int32)
    for e in range(N_EXPERTS):
        ve = val[e:e + 1, :]
        rank += ((ve > val) | ((ve == val) & (e < eidx))).astype(jnp.int32)
    sel = rank < TOP_K
    w = jnp.where(sel, scores, 0.0)
    w_ref[...] = w / jnp.sum(w, axis=0, keepdims=True) * ROUTED_SCALE
    cum = jnp.dot(jnp.where(sel, 1.0, 0.0).astype(BF16), tri_ref[...], preferred_element_type=F32)
    slot_ref[...] = jnp.where(sel, cum - 1.0, NOT_ROUTED)
    cnt_ref[0] = jnp.broadcast_to(cum[:, R - 1:R], (N_EXPERTS, LANES))


def _route(scores_t, router_bias, tri):
    seq = scores_t.shape[1]
    R = MOE_TOKEN_BLOCK
    n_blk = seq // R
    blk = pl.BlockSpec((N_EXPERTS, R), lambda i: (0, i))
    return pl.pallas_call(
        _route_body,
        grid=(n_blk,),
        in_specs=[blk, _resident((N_EXPERTS, 1), lambda i: (0, 0)), _resident((R, R), lambda i: (0, 0))],
        out_specs=[blk, blk, pl.BlockSpec((1, N_EXPERTS, LANES), lambda i: (i, 0, 0))],
        out_shape=[jax.ShapeDtypeStruct((N_EXPERTS, seq), F32),
                   jax.ShapeDtypeStruct((N_EXPERTS, seq), F32),
                   jax.ShapeDtypeStruct((n_blk, N_EXPERTS, LANES), F32)],
        compiler_params=pltpu.CompilerParams(dimension_semantics=("arbitrary",)),
        name="route",
    )(scores_t, router_bias, tri)


def _moe_tables(cnt):
    n_blk = cnt.shape[0]
    U, RB, SB = MOE_UNIT, MOE_BLOCK_ROWS, MOE_SLOT_BLOCK
    NU = SB // U
    E = N_EXPERTS
    pc = (cnt + U - 1) // U * U
    off = jnp.cumsum(pc, axis=1) - pc
    upc_t = (pc // U).T
    cum_t = jnp.cumsum(upc_t, axis=1)
    units_e = cum_t[:, -1]
    nblk_e = (units_e + NU - 1) // NU
    blk_end = jnp.cumsum(nblk_e)
    blk_start = blk_end - nblk_e
    n_act = blk_end[-1]
    max_blocks = n_blk * (RB // SB) + E
    i = jnp.arange(max_blocks, dtype=jnp.int32)
    active = i < n_act
    count_le = lambda edges, v: jnp.sum((edges <= v[..., None]).astype(jnp.int32), axis=-1)
    last_e = jnp.minimum(count_le(blk_end, n_act - 1), E - 1)
    be = jnp.where(active, jnp.minimum(count_le(blk_end[None, :], i), E - 1), last_e)
    oh_e = be[:, None] == jnp.arange(E, dtype=jnp.int32)[None, :]
    pick_e = lambda v: jnp.sum(jnp.where(oh_e, v[None, :], 0), axis=1)
    pick_e2 = lambda m: jnp.sum(jnp.where(oh_e[:, :, None], m[None, :, :], 0), axis=1)
    bstart_i = pick_e(blk_start)
    first = active & (i == bstart_i)
    local = (i - bstart_i)[:, None] * NU + jnp.arange(NU, dtype=jnp.int32)[None, :]
    valid = active[:, None] & (local < pick_e(units_e)[:, None])
    cum_i = pick_e2(cum_t)
    b_of = jnp.minimum(count_le(cum_i[:, None, :], local), n_blk - 1)
    oh_b = b_of[:, :, None] == jnp.arange(n_blk, dtype=jnp.int32)[None, None, :]
    pick_b = lambda m: jnp.sum(jnp.where(oh_b, m[:, None, :], 0), axis=2)
    seg_start = pick_b(cum_i) - pick_b(pick_e2(upc_t))
    unit = (b_of * RB + pick_b(pick_e2(off.T))) // U + (local - seg_start)
    spare = n_blk * RB // U
    src = jnp.where(valid, unit, spare)
    dst = jnp.where(valid, unit,
                    spare + (1 + i % 2)[:, None] * NU + jnp.arange(NU, dtype=jnp.int32)[None, :])
    short = active & (pick_e(units_e) - (i - bstart_i) * NU <= NU // 2)
    i32 = lambda a: a.astype(jnp.int32)
    flags = i32(first) + 2 * i32(short)
    return (off, pc, i32(src.reshape(-1)), i32(dst.reshape(-1)), i32(be), flags,
            i32(n_act.reshape(1)))


def _split_hi_lo(pos):
    hi = jnp.floor(pos * (1.0 / 64.0))
    return hi.astype(BF16), (pos - 64.0 * hi).astype(BF16)


def _dispatch_body(used_ref, hn_ref, slot_ref, offc_ref, offl_ref, pcl_ref, xs_ref, *, n_blk):
    b = pl.program_id(0)
    RB, TB = MOE_BLOCK_ROWS, MOE_TOKEN_BLOCK
    used = used_ref[jnp.minimum(b, n_blk - 1)]

    @pl.when(b == n_blk)
    def _():
        xs_ref[...] = jnp.zeros(xs_ref.shape, BF16)

    def scatter_rows(rows):
        pos = slot_ref[...] + offc_ref[0]
        hi, lo = _split_hi_lo(pos)
        r = lax.broadcasted_iota(jnp.int32, (rows, N_EXPERTS), 0).astype(F32)
        off = offl_ref[0]
        owner = jnp.where((r >= off) & (r < off + pcl_ref[0]), 1.0, 0.0).astype(BF16)
        p = (64.0 * jnp.dot(owner, hi, preferred_element_type=F32)
             + jnp.dot(owner, lo, preferred_element_type=F32))
        rr = lax.broadcasted_iota(jnp.int32, (rows, TB), 0).astype(F32)
        onehot = jnp.where(jnp.abs(p - rr) < 0.5, 1.0, 0.0).astype(BF16)
        x = hn_ref[...]
        C = MOE_SLOT_BLOCK
        for c in range(rows // C):
            xs_ref[0, c * C:(c + 1) * C, :] = jnp.dot(
                onehot[c * C:(c + 1) * C], x, preferred_element_type=F32).astype(BF16)
        if rows < RB:
            xs_ref[0, rows:, :] = jnp.zeros((RB - rows, D_MODEL), BF16)

    @pl.when((b < n_blk) & (used <= MOE_TYPICAL_ROWS))
    def _():
        scatter_rows(MOE_TYPICAL_ROWS)

    @pl.when((b < n_blk) & (used > MOE_TYPICAL_ROWS))
    def _():
        scatter_rows(RB)


def _dispatch(used, hn, slots, off, pc):
    n_blk = off.shape[0]
    RB, TB = MOE_BLOCK_ROWS, MOE_TOKEN_BLOCK
    clamp = lambda b: jnp.minimum(b, n_blk - 1)
    return pl.pallas_call(
        functools.partial(_dispatch_body, n_blk=n_blk),
        grid=(n_blk + 1,),
        in_specs=[
            pl.BlockSpec(memory_space=pltpu.SMEM),
            pl.BlockSpec((TB, D_MODEL), lambda b: (clamp(b), 0)),
            pl.BlockSpec((N_EXPERTS, TB), lambda b: (0, clamp(b))),
            pl.BlockSpec((1, N_EXPERTS, 1), lambda b: (clamp(b), 0, 0)),
            pl.BlockSpec((1, 1, N_EXPERTS), lambda b: (clamp(b), 0, 0)),
            pl.BlockSpec((1, 1, N_EXPERTS), lambda b: (clamp(b), 0, 0)),
        ],
        out_specs=pl.BlockSpec((1, RB, D_MODEL), lambda b: (b, 0, 0)),
        out_shape=jax.ShapeDtypeStruct((n_blk + 1, RB, D_MODEL), BF16),
        compiler_params=pltpu.CompilerParams(
            dimension_semantics=("arbitrary",), vmem_limit_bytes=VMEM_LIMIT),
        name="moe_dispatch",
    )(used, hn, slots, off[:, :, None], off[:, None, :], pc[:, None, :])


def _expert_body(src_ref, dst_ref, bexp_ref, first_ref, nact_ref,
                 xs_hbm, wg_ref, wu_ref, wd_ref, ys_hbm,
                 xbuf, ybuf, wgub, wdb, sem_in, sem_out):
    del bexp_ref
    i = pl.program_id(0)
    n_act = nact_ref[0]
    cur = lax.rem(i, 2)
    U = MOE_UNIT
    NU = MOE_SLOT_BLOCK // U

    def in_copy(blk, buf, u):
        return pltpu.make_async_copy(xs_hbm.at[src_ref[blk * NU + u]],
                                     xbuf.at[buf, pl.ds(u * U, U)], sem_in.at[buf])

    def out_copy(blk, buf, u):
        return pltpu.make_async_copy(ybuf.at[buf, pl.ds(u * U, U)],
                                     ys_hbm.at[dst_ref[blk * NU + u]], sem_out.at[buf])

    @pl.when(i == 0)
    def _():
        for u in range(NU):
            in_copy(0, 0, u).start(priority=1)

    @pl.when(i < n_act)
    def _():
        @pl.when(i + 1 < n_act)
        def _():
            for u in range(NU):
                in_copy(i + 1, 1 - cur, u).start(priority=1)

        @pl.when(first_ref[i] % 2 == 1)
        def _():
            wgub[:, :EXPERT_HIDDEN] = wg_ref[0].astype(BF16)
            wgub[:, EXPERT_HIDDEN:] = wu_ref[0].astype(BF16)
            wdb[...] = wd_ref[0].astype(BF16)

        for u in range(NU):
            in_copy(i, cur, u).wait()

        @pl.when(i >= 2)
        def _():
            for u in range(NU):
                out_copy(i - 2, cur, u).wait()

        def swiglu_rows(rows):
            x = xbuf[cur, :rows]
            gu = jnp.dot(x, wgub[...], preferred_element_type=F32)
            act = (jax.nn.silu(gu[:, :EXPERT_HIDDEN]) * gu[:, EXPERT_HIDDEN:]).astype(BF16)
            ybuf[cur, :rows] = jnp.dot(act, wdb[...], preferred_element_type=F32).astype(BF16)

        half = MOE_SLOT_BLOCK // 2

        @pl.when(first_ref[i] < 2)
        def _():
            swiglu_rows(MOE_SLOT_BLOCK)

        @pl.when(first_ref[i] >= 2)
        def _():
            swiglu_rows(half)
            ybuf[cur, half:] = jnp.zeros((half, D_MODEL), BF16)

        for u in range(NU):
            out_copy(i, cur, u).start(priority=u % 2)

        @pl.when(i == n_act - 1)
        def _():
            for u in range(NU):
                out_copy(i, cur, u).wait()

            @pl.when(i >= 1)
            def _():
                for u in range(NU):
                    out_copy(i - 1, 1 - cur, u).wait()


def _experts(xs, src, dst, bexp, first, n_act, wg, wu, wd):
    U, SB = MOE_UNIT, MOE_SLOT_BLOCK
    n_units = xs.shape[0] * xs.shape[1] // U
    max_blocks = bexp.shape[0]
    unit_view = lambda a: a.reshape(n_units, U, D_MODEL)
    wspec = lambda shape: pl.BlockSpec((1,) + shape, lambda i, src, dst, bexp, first, nact: (bexp[i], 0, 0))
    grid_spec = pltpu.PrefetchScalarGridSpec(
        num_scalar_prefetch=5,
        grid=(max_blocks,),
        in_specs=[
            pl.BlockSpec(memory_space=pl.ANY),
            wspec((D_MODEL, EXPERT_HIDDEN)), wspec((D_MODEL, EXPERT_HIDDEN)), wspec((EXPERT_HIDDEN, D_MODEL)),
        ],
        out_specs=pl.BlockSpec(memory_space=pl.ANY),
        scratch_shapes=[
            pltpu.VMEM((2, SB, D_MODEL), BF16), pltpu.VMEM((2, SB, D_MODEL), BF16),
            pltpu.VMEM((D_MODEL, 2 * EXPERT_HIDDEN), BF16), pltpu.VMEM((EXPERT_HIDDEN, D_MODEL), BF16),
            pltpu.SemaphoreType.DMA((2,)), pltpu.SemaphoreType.DMA((2,)),
        ],
    )
    ys = pl.pallas_call(
        _expert_body,
        grid_spec=grid_spec,
        out_shape=jax.ShapeDtypeStruct((n_units, U, D_MODEL), BF16),
        input_output_aliases={5: 0},
        compiler_params=pltpu.CompilerParams(
            dimension_semantics=("arbitrary",), vmem_limit_bytes=VMEM_LIMIT),
        name="moe_experts",
    )(src, dst, bexp, first, n_act, unit_view(xs), wg, wu, wd)
    return ys.reshape(xs.shape)


def _combine_body(used_ref, h_ref, ys_ref, slot_ref, w_ref, offl_ref, offc_ref, pcc_ref, o_ref):
    RB, TB = MOE_BLOCK_ROWS, MOE_TOKEN_BLOCK
    used = used_ref[pl.program_id(0)]

    def gather_rows(rows):
        pos = slot_ref[...] + offl_ref[0]
        hi, lo = _split_hi_lo(pos)
        r = lax.broadcasted_iota(jnp.int32, (N_EXPERTS, rows), 1).astype(F32)
        off = offc_ref[0]
        owner = jnp.where((r >= off) & (r < off + pcc_ref[0]), 1.0, 0.0).astype(BF16)
        p = (64.0 * jnp.dot(hi, owner, preferred_element_type=F32)
             + jnp.dot(lo, owner, preferred_element_type=F32))
        wr = jnp.dot(w_ref[...].astype(BF16), owner, preferred_element_type=F32)
        rr = lax.broadcasted_iota(jnp.int32, (TB, rows), 1).astype(F32)
        gather_w = jnp.where(jnp.abs(p - rr) < 0.5, wr, 0.0).astype(BF16)
        o_ref[...] = h_ref[...] + jnp.dot(gather_w, ys_ref[0, :rows, :], preferred_element_type=F32)

    @pl.when(used <= MOE_TYPICAL_ROWS)
    def _():
        gather_rows(MOE_TYPICAL_ROWS)

    @pl.when(used > MOE_TYPICAL_ROWS)
    def _():
        gather_rows(RB)


def _combine(used, h, ys, slots_tok, w_tok, off, pc):
    n_blk = off.shape[0]
    RB, TB = MOE_BLOCK_ROWS, MOE_TOKEN_BLOCK
    seq = h.shape[0]
    return pl.pallas_call(
        _combine_body,
        grid=(n_blk,),
        in_specs=[
            pl.BlockSpec(memory_space=pltpu.SMEM),
            pl.BlockSpec((TB, D_MODEL), lambda b: (b, 0)),
            pl.BlockSpec((1, RB, D_MODEL), lambda b: (b, 0, 0)),
            pl.BlockSpec((TB, N_EXPERTS), lambda b: (b, 0)),
            pl.BlockSpec((TB, N_EXPERTS), lambda b: (b, 0)),
            pl.BlockSpec((1, 1, N_EXPERTS), lambda b: (b, 0, 0)),
            pl.BlockSpec((1, N_EXPERTS, 1), lambda b: (b, 0, 0)),
            pl.BlockSpec((1, N_EXPERTS, 1), lambda b: (b, 0, 0)),
        ],
        out_specs=pl.BlockSpec((TB, D_MODEL), lambda b: (b, 0)),
        out_shape=jax.ShapeDtypeStruct((seq, D_MODEL), F32),
        compiler_params=pltpu.CompilerParams(
            dimension_semantics=("arbitrary",), vmem_limit_bytes=VMEM_LIMIT),
        name="moe_combine",
    )(used, h, ys, slots_tok, w_tok, off[:, None, :], off[:, :, None], pc[:, :, None])


def kernel(x, meta_tokens, rel_bias, norm_mix, w_in, q_norm, k_norm, lam_q1, lam_k1, lam_q2, lam_k2, subln, ssm_a_re, ssm_a_im, ssm_log_step, ssm_b_re, ssm_b_im, ssm_c_re, ssm_c_im, ssm_d, w_glu, b_glu, ssm_norm, w_out, norm_ffn, router_w, router_bias, w_gate, w_up, w_down, shared_gate, shared_up, shared_down):
    batch, seq, d = x.shape
    assert batch == 1 and d == D_MODEL and seq % ROW_BLOCK == 0 and seq % ATT_BLOCK == 0
    assert norm_mix.shape[0] == 1, "single layer"
    x2 = x.reshape(seq, d)
    meta_pad = jnp.zeros((ROW_BLOCK, d), F32).at[:N_META].set(meta_tokens.astype(F32))
    seg = ((jnp.arange(QK_WIDTH)[:, None] // HEAD_DIM == jnp.arange(LANES)[None, :])
           .astype(F32) / HEAD_DIM).astype(BF16)
    qg = jnp.tile(q_norm[0].astype(F32), QK_WIDTH // HEAD_DIM)[None] * (HEAD_DIM ** -0.5)
    kg = jnp.tile(k_norm[0].astype(F32), QK_WIDTH // HEAD_DIM)[None]

    proj, u = _inproj(x2, meta_pad, norm_mix[0][None], w_in[0].astype(BF16), seg, qg, kg)

    score_bound = (BOUND_MARGIN * HEAD_DIM ** 0.5 * jnp.max(jnp.abs(q_norm[0].astype(F32)))
                   * jnp.max(jnp.abs(k_norm[0].astype(F32)))
                   + jnp.max(jnp.abs(rel_bias.astype(F32)))).reshape(1)
    att = _attention(proj, rel_bias.astype(F32), score_bound, lam_q1[0][None], lam_k1[0][None],
                     lam_q2[0][None], lam_k2[0][None], subln[0][None], seq)

    n_rows = proj.shape[0] // S5_CHUNK
    n_chunks = seq // S5_CHUNK
    perm = _lane_regroup_matrix()
    u_chunks = _to_chunks(u.reshape(n_rows, S5_CHUNK, SSM_WIDTH), perm)
    pp, mm, qq, lam16 = _s5_prep(ssm_a_re[0].astype(F32), ssm_a_im[0].astype(F32),
                                 ssm_log_step[0].astype(F32), ssm_b_re[0].astype(F32),
                                 ssm_b_im[0].astype(F32), ssm_c_re[0].astype(F32),
                                 ssm_c_im[0].astype(F32))
    y_chunks = _s5(u_chunks, pp, mm, qq, lam16, n_chunks)
    y = _from_chunks(y_chunks, perm.T).reshape(seq, SSM_WIDTH)

    rw_t = router_w[0].astype(F32).T
    rw_hi = rw_t.astype(BF16)
    rw_lo = (rw_t - rw_hi.astype(F32)).astype(BF16)
    h, hn, scores_t = _mix_out(
        x2, att, y, u, ssm_d[0][None].astype(F32), w_glu[0].astype(BF16), b_glu[0][None].astype(F32),
        ssm_norm[0][None].astype(F32), w_out[0].astype(BF16), norm_ffn[0][None].astype(F32),
        rw_hi, rw_lo,
        jnp.concatenate([shared_gate[0], shared_up[0]], axis=1).astype(BF16), shared_down[0].astype(BF16))

    tb = MOE_TOKEN_BLOCK
    tri = (jnp.arange(tb)[:, None] <= jnp.arange(tb)[None, :]).astype(BF16)
    wts_t, slots_t, cnt = _route(scores_t, router_bias[0].astype(F32)[:, None], tri)
    off, pc, src, dst, bexp, first, n_act = _moe_tables(cnt[:, :, 0].astype(jnp.int32))
    off_f, pc_f = off.astype(F32), pc.astype(F32)
    used = jnp.sum(pc, axis=1).astype(jnp.int32)
    xs = _dispatch(used, hn, slots_t, off_f, pc_f)
    ys = _experts(xs, src, dst, bexp, first, n_act, w_gate[0], w_up[0], w_down[0])
    out = _combine(used, h, ys, slots_t.T, wts_t.T, off_f, pc_f)
    return out.reshape(batch, seq, d)
```
